```python
import math
import jax
import jax.numpy as jnp
from jax import lax
import numpy as np

D_MODEL = 1024
BATCH = 8
SEQ = 4096
DEPTH = 4

GRID_W = 64
CTX_LEN = 256
N_MOD = 9
RMS_EPS = 1e-6
GN_EPS = 1e-6
D_FF = 2816
NA_HEADS = 8
NA_HEAD_DIM = 64
NA_WIDTH = NA_HEADS * NA_HEAD_DIM
WIN_R = 8
WIN_C = 16
HY_WIDTH = D_MODEL - NA_WIDTH
HY_BANDS = 8
HY_EMB = 1 + 2 * HY_BANDS
HY_ORDER = 64
HY_TARGET = 1e-2
HY_FAST_PCT = 0.3
HY_SLOW_PCT = 1.5
EVEN_IN = 3 * NA_WIDTH + 3 * HY_WIDTH
EVEN_CAT = NA_WIDTH + HY_WIDTH
RET_HEADS = 4
RET_KEY_DIM = D_MODEL // RET_HEADS
RET_VAL_DIM = 2 * RET_KEY_DIM
RET_QK = RET_HEADS * RET_KEY_DIM
RET_V = RET_HEADS * RET_VAL_DIM
RET_IN = 2 * RET_QK + 2 * RET_V
RET_CHUNK = 128
ROPE_BASE = 10000.0

kernel_name = "hybrid_na_hyena_retention_dit"


def rms_norm(x, gain):
    xf = x.astype(jnp.float32)
    y = xf * lax.rsqrt(jnp.mean(xf * xf, axis=-1, keepdims=True) + RMS_EPS)
    return (y * gain).astype(x.dtype)


def modulate(x, gain, shift, scale):
    return rms_norm(x, gain) * (1 + scale) + shift


def swiglu(h, w_in, w_out):
    a, b = jnp.split(h @ w_in, 2, axis=-1)
    return (jax.nn.silu(a) * b) @ w_out


def split_heads(t, n_heads):
    b, l, _ = t.shape
    return t.reshape(b, l, n_heads, -1).transpose(0, 2, 1, 3)


def merge_heads(t):
    b, h, l, d = t.shape
    return t.transpose(0, 2, 1, 3).reshape(b, l, h * d)


def axial_rope_tables(length, head_dim):
    t = jnp.arange(length)
    n_freq = head_dim // 4
    inv = ROPE_BASE ** (-jnp.arange(n_freq, dtype=jnp.float32) / n_freq)
    ang_r = (t // GRID_W).astype(jnp.float32)[:, None] * inv
    ang_c = (t % GRID_W).astype(jnp.float32)[:, None] * inv
    ang = jnp.concatenate([ang_r, ang_c], axis=-1)
    return jnp.cos(ang), jnp.sin(ang)


def apply_axial_rope(x, cos, sin):
    nf = x.shape[-1] // 4
    xr1, xr2, xc1, xc2 = jnp.split(x.astype(jnp.float32), 4, axis=-1)
    cr, cc = cos[:, :nf], cos[:, nf:]
    sr, sc = sin[:, :nf], sin[:, nf:]
    out = jnp.concatenate([xr1 * cr - xr2 * sr, xr1 * sr + xr2 * cr,
                           xc1 * cc - xc2 * sc, xc1 * sc + xc2 * cc], axis=-1)
    return out.astype(x.dtype)


def qk_norm(t, gain):
    tf = t.astype(jnp.float32)
    return (tf * lax.rsqrt(jnp.mean(tf * tf, axis=-1, keepdims=True) + RMS_EPS) * gain).astype(t.dtype)


def dense_attention(q, k, v):
    s = jnp.einsum('bhqd,bhkd->bhqk', q, k).astype(jnp.float32) * (q.shape[-1] ** -0.5)
    p = jax.nn.softmax(s, axis=-1).astype(v.dtype)
    return jnp.einsum('bhqk,bhkd->bhqd', p, v)


def neighbourhood_attention(q, k, v, k_ctx, v_ctx, rpb):
    b, h, length, d = q.shape
    rows = length // GRID_W
    wr = min(WIN_R, rows)
    q = q.reshape(b, h, rows, GRID_W, d)
    k = k.reshape(b, h, rows, GRID_W, d)
    v = v.reshape(b, h, rows, GRID_W, d)
    col = jnp.arange(GRID_W)
    col_start = jnp.clip(col - WIN_C // 2, 0, GRID_W - WIN_C)
    col_idx = col_start[:, None] + jnp.arange(WIN_C)[None, :]
    dc_idx = col_idx - col[:, None] + (WIN_C - 1)
    scale = d ** -0.5
    n_win = wr * WIN_C

    def row_block(r):
        rs = jnp.clip(r - wr // 2, 0, rows - wr)
        dr_idx = rs + jnp.arange(wr) - r + (WIN_R - 1)
        bias = rpb[:, dr_idx[None, :, None], dc_idx[:, None, :]].astype(jnp.float32)
        q_r = lax.dynamic_index_in_dim(q, r, axis=2, keepdims=False)
        k_w = jnp.take(lax.dynamic_slice_in_dim(k, rs, wr, axis=2), col_idx, axis=3)
        v_w = jnp.take(lax.dynamic_slice_in_dim(v, rs, wr, axis=2), col_idx, axis=3)
        s_win = jnp.einsum('bhqd,bhrqjd->bhqrj', q_r, k_w).astype(jnp.float32) * scale + bias
        s_ctx = jnp.einsum('bhqd,bhcd->bhqc', q_r, k_ctx).astype(jnp.float32) * scale
        s = jnp.concatenate([s_win.reshape(b, h, GRID_W, n_win), s_ctx], axis=-1)
        p = jax.nn.softmax(s, axis=-1).astype(v.dtype)
        p_win = p[..., :n_win].reshape(b, h, GRID_W, wr, WIN_C)
        return (jnp.einsum('bhqrj,bhrqjd->bhqd', p_win, v_w)
                + jnp.einsum('bhqc,bhcd->bhqd', p[..., n_win:], v_ctx))

    out = lax.map(row_block, jnp.arange(rows))
    return out.transpose(1, 2, 0, 3, 4).reshape(b, h, length, d)


def short_conv(u, w, b):
    up = jnp.pad(u, ((0, 0), (1, 1), (0, 0)))
    return up[:, :-2] * w[0] + up[:, 1:-1] * w[1] + up[:, 2:] * w[2] + b


def hyena_filter(length, fw1, fb1, fw2, fb2, fw3, fb3, fw4, freq):
    t = jnp.linspace(0.0, 1.0, length, dtype=jnp.float32)[:, None]
    w = 2.0 * math.pi * jnp.arange(length, dtype=jnp.float32)[:, None] / length
    bands = jnp.linspace(1e-4, HY_BANDS - 1, HY_BANDS, dtype=jnp.float32)
    emb = jnp.concatenate([t, jnp.cos(bands * w), -jnp.sin(bands * w)], axis=-1)
    h = jnp.sin(freq * (emb @ fw1 + fb1))
    h = jnp.sin(freq * (h @ fw2 + fb2))
    h = jnp.sin(freq * (h @ fw3 + fb3))
    h = (h @ fw4).astype(jnp.float32)
    max_decay = math.log(HY_TARGET) / HY_FAST_PCT
    min_decay = math.log(HY_TARGET) / HY_SLOW_PCT
    deltas = jnp.linspace(min_decay, max_decay, HY_WIDTH, dtype=jnp.float32)
    window = jnp.exp(-t * jnp.abs(deltas))
    h_fwd = h[:, :HY_WIDTH] * window
    h_bwd = h[:, HY_WIDTH:] * window
    return jnp.concatenate([h_fwd[:1] + h_bwd[:1], h_fwd[1:],
                            jnp.zeros((1, HY_WIDTH), jnp.float32), h_bwd[1:][::-1]], axis=0)


def long_conv(u, filt):
    length = u.shape[1]
    uf = jnp.fft.rfft(u.astype(jnp.float32), n=2 * length, axis=1)
    kf = jnp.fft.rfft(filt, n=2 * length, axis=0)
    return jnp.fft.irfft(uf * kf[None], n=2 * length, axis=1)[:, :length]


def hyena_operator(u, conv_w, conv_b, filter_params, d_bias):
    uc = short_conv(u, conv_w, conv_b)
    x0, x1, v = jnp.split(uc, 3, axis=-1)
    z = v * x1
    filt = hyena_filter(u.shape[1], *filter_params)
    y = long_conv(z, filt).astype(z.dtype) + z * d_bias
    return y * x0


def na_hyena_mixer(h_lat, h_ctx, w_in, w_out, q_gain, k_gain, rpb, conv_w, conv_b,
                   filter_params, d_bias, ctx_out):
    p_lat = h_lat @ w_in
    p_ctx = h_ctx @ w_in
    qa_l, ka_l, va_l = [split_heads(t, NA_HEADS) for t in jnp.split(p_lat[..., :3 * NA_WIDTH], 3, axis=-1)]
    qa_c, ka_c, va_c = [split_heads(t, NA_HEADS) for t in jnp.split(p_ctx[..., :3 * NA_WIDTH], 3, axis=-1)]
    k_c = qk_norm(ka_c, k_gain)
    a_lat = merge_heads(neighbourhood_attention(qk_norm(qa_l, q_gain), qk_norm(ka_l, k_gain), va_l,
                                                k_c, va_c, rpb))
    b_lat = hyena_operator(p_lat[..., 3 * NA_WIDTH:], conv_w, conv_b, filter_params, d_bias)
    o_lat = jnp.concatenate([a_lat, b_lat], axis=-1) @ w_out
    if not ctx_out:
        return o_lat, None
    a_ctx = merge_heads(dense_attention(qk_norm(qa_c, q_gain), k_c, va_c))
    b_ctx = hyena_operator(p_ctx[..., 3 * NA_WIDTH:], conv_w, conv_b, filter_params, d_bias)
    o_ctx = jnp.concatenate([a_ctx, b_ctx], axis=-1) @ w_out
    return o_lat, o_ctx


def retention_scan(q, k, v, log_gamma, state0):
    b, h, length, dk = q.shape
    dv = v.shape[-1]
    n = length // RET_CHUNK

    def chunks(a):
        return a.astype(jnp.float32).reshape(b, h, n, RET_CHUNK, a.shape[-1]).transpose(2, 0, 1, 3, 4)

    j = jnp.arange(RET_CHUNK, dtype=jnp.float32)
    diff = j[:, None] - j[None, :]
    lg = log_gamma.astype(jnp.float32)
    dmask = jnp.where(diff[None] >= 0, jnp.exp(lg[:, None, None] * jnp.maximum(diff, 0.0)[None]), 0.0)
    xi = jnp.exp(lg[:, None] * (j + 1.0))
    zeta = jnp.exp(lg[:, None] * (RET_CHUNK - 1.0 - j))
    g_chunk = jnp.exp(lg * RET_CHUNK)

    def step(state, inp):
        qc, kc, vc = inp
        inner = jnp.einsum('bhid,bhjd->bhij', qc, kc) * dmask
        o = (jnp.einsum('bhij,bhje->bhie', inner, vc)
             + jnp.einsum('bhid,bhde->bhie', qc, state) * xi[..., None])
        state = state * g_chunk[:, None, None] + jnp.einsum('bhjd,bhje->bhde', kc * zeta[..., None], vc)
        return state, o

    state, o = lax.scan(step, state0, (chunks(q), chunks(k), chunks(v)))
    return o.transpose(1, 2, 0, 3, 4).reshape(b, h, length, dv), state


def retention_mixer(h_lat, h_ctx, w_in, w_out, logit_f, logit_b, rope_cos, rope_sin, ctx_out):
    def project(hh):
        q, k, v, g = jnp.split(hh @ w_in, [RET_QK, 2 * RET_QK, 2 * RET_QK + RET_V], axis=-1)
        return (split_heads(q, RET_HEADS), split_heads(k, RET_HEADS) * (RET_KEY_DIM ** -0.5),
                split_heads(v, RET_HEADS), g)

    def output(y, g, dtype):
        mu = jnp.mean(y, axis=-1, keepdims=True)
        var = jnp.mean(jnp.square(y - mu), axis=-1, keepdims=True)
        y = merge_heads((y - mu) * lax.rsqrt(var + GN_EPS)).astype(dtype)
        return (jax.nn.silu(g) * y) @ w_out

    q_l, k_l, v_l, g_l = project(h_lat)
    q_l = apply_axial_rope(q_l, rope_cos, rope_sin)
    k_l = apply_axial_rope(k_l, rope_cos, rope_sin)
    q_c, k_c, v_c, g_c = project(h_ctx)
    lg_f = jax.nn.log_sigmoid(logit_f.astype(jnp.float32))
    lg_b = jax.nn.log_sigmoid(logit_b.astype(jnp.float32))
    zero = jnp.zeros((h_lat.shape[0], RET_HEADS, RET_KEY_DIM, RET_VAL_DIM), jnp.float32)
    flip = lambda a: jnp.flip(a, axis=2)
    o_cf, s_f = retention_scan(q_c, k_c, v_c, lg_f, zero)
    o_cb, s_b = retention_scan(flip(q_c), flip(k_c), flip(v_c), lg_b, zero)
    o_lf, _ = retention_scan(q_l, k_l, v_l, lg_f, s_f)
    o_lb, _ = retention_scan(flip(q_l), flip(k_l), flip(v_l), lg_b, s_b)
    o_lat = output(o_lf + flip(o_lb), g_l, h_lat.dtype)
    if not ctx_out:
        return o_lat, None
    o_ctx = output(o_cf + flip(o_cb), g_c, h_ctx.dtype)
    return o_lat, o_ctx


def setup_inputs(seed: int = 0) -> dict:
    key = jax.random.key(seed)
    ks = iter(jax.random.split(key, 40))

    def nrm(shape, scale):
        return scale * jax.random.normal(next(ks), shape, jnp.float32)

    d = D_MODEL
    n_even = (DEPTH + 1) // 2
    n_odd = DEPTH // 2
    gamma = 1.0 - 2.0 ** (-5.0 - np.arange(RET_HEADS))
    logit0 = jnp.asarray(np.log(gamma / (1.0 - gamma)), jnp.float32)
    return {
        "x": nrm((BATCH, SEQ, d), 1.0),
        "c": nrm((BATCH, d), 1.0),
        "ctx": nrm((BATCH, CTX_LEN, d), 1.0),
        "c_ctx": nrm((d,), 1.0),
        "w_mod": nrm((DEPTH, d, N_MOD * d), 0.5 * d ** -0.5),
        "b_mod": nrm((DEPTH, N_MOD * d), 0.02),
        "norm_gain": 1.0 + nrm((DEPTH, 3, d), 0.05),
        "ffn_a_in": nrm((DEPTH, d, 2 * D_FF), d ** -0.5),
        "ffn_a_out": nrm((DEPTH, D_FF, d), D_FF ** -0.5),
        "ffn_b_in": nrm((DEPTH, d, 2 * D_FF), d ** -0.5),
        "ffn_b_out": nrm((DEPTH, D_FF, d), D_FF ** -0.5),
        "even_in": nrm((n_even, d, EVEN_IN), d ** -0.5),
        "even_out": nrm((n_even, EVEN_CAT, d), EVEN_CAT ** -0.5),
        "na_q_gain": 1.0 + nrm((n_even, NA_HEAD_DIM), 0.05),
        "na_k_gain": 1.0 + nrm((n_even, NA_HEAD_DIM), 0.05),
        "na_rpb": nrm((n_even, NA_HEADS, 2 * WIN_R - 1, 2 * WIN_C - 1), 0.1),
        "hy_conv_w": nrm((n_even, 3, 3 * HY_WIDTH), 3 ** -0.5),
        "hy_conv_b": nrm((n_even, 3 * HY_WIDTH), 0.02),
        "hy_fw1": nrm((n_even, HY_EMB, HY_ORDER), HY_EMB ** -0.5),
        "hy_fb1": nrm((n_even, HY_ORDER), 0.1),
        "hy_fw2": nrm((n_even, HY_ORDER, HY_ORDER), HY_ORDER ** -0.5),
        "hy_fb2": nrm((n_even, HY_ORDER), 0.1),
        "hy_fw3": nrm((n_even, HY_ORDER, HY_ORDER), HY_ORDER ** -0.5),
        "hy_fb3": nrm((n_even, HY_ORDER), 0.1),
        "hy_fw4": nrm((n_even, HY_ORDER, 2 * HY_WIDTH), 0.05 * HY_ORDER ** -0.5),
        "hy_freq": 1.0 + nrm((n_even, HY_ORDER), 0.1),
        "hy_bias": nrm((n_even, HY_WIDTH), 0.5),
        "ret_in": nrm((n_odd, d, RET_IN), d ** -0.5),
        "ret_out": nrm((n_odd, RET_V, d), RET_V ** -0.5),
        "ret_logit_f": logit0 + nrm((n_odd, RET_HEADS), 0.1),
        "ret_logit_b": logit0 + nrm((n_odd, RET_HEADS), 0.1),
    }


def reference(x, c, ctx, c_ctx, w_mod, b_mod, norm_gain, ffn_a_in, ffn_a_out, ffn_b_in, ffn_b_out,
              even_in, even_out, na_q_gain, na_k_gain, na_rpb, hy_conv_w, hy_conv_b,
              hy_fw1, hy_fb1, hy_fw2, hy_fb2, hy_fw3, hy_fb3, hy_fw4, hy_freq, hy_bias,
              ret_in, ret_out, ret_logit_f, ret_logit_b):
    b = x.shape[0]
    rope_cos, rope_sin = axial_rope_tables(x.shape[1], RET_KEY_DIM)
    x_lat, x_ctx = x, ctx
    for i in range(DEPTH):
        last = i == DEPTH - 1
        mod_l = (jax.nn.silu(c) @ w_mod[i] + b_mod[i]).reshape(b, N_MOD, 1, D_MODEL)
        mod_c = (jax.nn.silu(c_ctx) @ w_mod[i] + b_mod[i]).reshape(N_MOD, 1, D_MODEL)
        x_lat = x_lat + 0.5 * mod_l[:, 2] * swiglu(modulate(x_lat, norm_gain[i, 0], mod_l[:, 0], mod_l[:, 1]),
                                                 ffn_a_in[i], ffn_a_out[i])
        x_ctx = x_ctx + 0.5 * mod_c[2] * swiglu(modulate(x_ctx, norm_gain[i, 0], mod_c[0], mod_c[1]),
                                              ffn_a_in[i], ffn_a_out[i])
        h_l = modulate(x_lat, norm_gain[i, 1], mod_l[:, 3], mod_l[:, 4])
        h_c = modulate(x_ctx, norm_gain[i, 1], mod_c[3], mod_c[4])
        if i % 2 == 0:
            e = i // 2
            filter_params = (hy_fw1[e], hy_fb1[e], hy_fw2[e], hy_fb2[e], hy_fw3[e], hy_fb3[e], hy_fw4[e], hy_freq[e])
            o_l, o_c = na_hyena_mixer(h_l, h_c, even_in[e], even_out[e], na_q_gain[e], na_k_gain[e], na_rpb[e],
                                      hy_conv_w[e], hy_conv_b[e], filter_params, hy_bias[e], not last)
        else:
            o = i // 2
            o_l, o_c = retention_mixer(h_l, h_c, ret_in[o], ret_out[o], ret_logit_f[o], ret_logit_b[o],
                                       rope_cos, rope_sin, not last)
        x_lat = x_lat + mod_l[:, 5] * o_l
        x_lat = x_lat + 0.5 * mod_l[:, 8] * swiglu(modulate(x_lat, norm_gain[i, 2], mod_l[:, 6], mod_l[:, 7]),
                                                 ffn_b_in[i], ffn_b_out[i])
        if not last:
            x_ctx = x_ctx + mod_c[5] * o_c
            x_ctx = x_ctx + 0.5 * mod_c[8] * swiglu(modulate(x_ctx, norm_gain[i, 2], mod_c[6], mod_c[7]),
                                                  ffn_b_in[i], ffn_b_out[i])
    return x_lat
```

```python
import functools
import math

import ml_dtypes
import numpy as np
import jax
import jax.numpy as jnp
from jax import lax
from jax.experimental import pallas as pl
from jax.experimental.pallas import tpu as pltpu

F32 = jnp.float32
BF16 = jnp.bfloat16

D_MODEL = 1024
SEQ = 4096
DEPTH = 4
GRID_W = 64
CTX_LEN = 256
N_MOD = 9
RMS_EPS = 1e-6
GN_EPS = 1e-6
D_FF = 2816
NA_HEADS = 8
NA_HEAD_DIM = 64
NA_WIDTH = NA_HEADS * NA_HEAD_DIM
WIN_R = 8
WIN_C = 16
HY_WIDTH = D_MODEL - NA_WIDTH
HY_BANDS = 8
HY_TARGET = 1e-2
HY_FAST_PCT = 0.3
HY_SLOW_PCT = 1.5
RET_HEADS = 4
RET_KEY_DIM = D_MODEL // RET_HEADS
RET_VAL_DIM = 2 * RET_KEY_DIM
RET_QK = RET_HEADS * RET_KEY_DIM
RET_V = RET_HEADS * RET_VAL_DIM
RET_CHUNK = 128
ROPE_BASE = 10000.0

LANES = 128
VMEM_LIMIT_BYTES = 56 * 2**20
T_TOK = SEQ + CTX_LEN
TM = CTX_LEN
NT_LAT = SEQ // TM
NT_ALL = T_TOK // TM
MOD_ROWS = 16
FF_CHUNKS = ((0, 1536), (1536, 1280))

NA_RB = 4
NA_KR = NA_RB + WIN_R - 1
GRID_H = SEQ // GRID_W
NA_NQ = NA_RB * GRID_W
NA_NK = NA_KR * GRID_W
NEG_BIG = -1e30

FFT_N = 2 * SEQ
FFT_N1 = 64
FFT_N2 = 128
FFT_N1H = FFT_N1 // 2
FFT_W = 2048
FFT_KB = 4
FFT_NC = 2 * CTX_LEN


def _cparams(n_axes):
    return pltpu.CompilerParams(dimension_semantics=("arbitrary",) * n_axes,
                                vmem_limit_bytes=VMEM_LIMIT_BYTES)


def _bdot(a, b):
    return jnp.dot(a, b, preferred_element_type=F32)


_NT_DIMS = (((1,), (1,)), ((), ()))
_TN_DIMS = (((0,), (0,)), ((), ()))


def _split_hi_lo(m):
    hi = m.astype(BF16)
    lo = (m - hi.astype(F32)).astype(BF16)
    return hi, lo


def _dot3(m_hi, m_lo, d):
    d_hi, d_lo = _split_hi_lo(d)
    return _bdot(m_hi, d_hi) + _bdot(m_lo, d_hi) + _bdot(m_hi, d_lo)


def _modulated(x, gain, mod_ref, shift_row, scale_row):
    ms = jnp.mean(x * x, axis=-1, keepdims=True)
    y = x * lax.rsqrt(ms + RMS_EPS) * gain
    return (y * (1.0 + mod_ref[0, scale_row:scale_row + 1, :])
            + mod_ref[0, shift_row:shift_row + 1, :])


def _silu(a):
    return a * jax.nn.sigmoid(a)


def _mod_kernel(c_ref, w_ref, b_ref, o_ref):
    s = _silu(c_ref[...])
    o_ref[0] = jnp.dot(s, w_ref[0], precision=lax.Precision.HIGHEST,
                       preferred_element_type=F32) + b_ref[0]


def _mod_call(c_rows, w_mod, b_mod):
    depth, d, n = w_mod.shape
    tn = 1024
    return pl.pallas_call(
        _mod_kernel,
        grid=(depth, n // tn),
        in_specs=[pl.BlockSpec((MOD_ROWS, d), lambda i, j: (0, 0)),
                  pl.BlockSpec((1, d, tn), lambda i, j: (i, 0, j)),
                  pl.BlockSpec((1, 1, tn), lambda i, j: (i, 0, j))],
        out_specs=pl.BlockSpec((1, MOD_ROWS, tn), lambda i, j: (i, 0, j)),
        out_shape=jax.ShapeDtypeStruct((depth, MOD_ROWS, n), F32),
        compiler_params=_cparams(2),
        name="mod",
    )(c_rows, w_mod, b_mod.reshape(depth, 1, n))


def _tok_spec(width):
    return pl.BlockSpec((1, TM, width), lambda b, j: (b, j, 0))


def _mod_spec(batch):
    return pl.BlockSpec((1, N_MOD, D_MODEL), lambda b, j: (jnp.where(j == NT_LAT, batch, b), 0, 0))


def _full_spec(shape):
    zeros = (0,) * len(shape)
    return pl.BlockSpec(shape, lambda b, j: zeros)


def _ffn_kernel(x_ref, mod_ref, gain_ref, win_ref, wout_ref, o_ref, *, rows):
    shift_row, scale_row, gate_row = rows
    x = x_ref[0]
    h = _modulated(x, gain_ref[...], mod_ref, shift_row, scale_row).astype(BF16)
    acc = jnp.zeros((TM, D_MODEL), F32)
    for start, size in FF_CHUNKS:
        a = _bdot(h, win_ref[:, start:start + size])
        b = _bdot(h, win_ref[:, D_FF + start:D_FF + start + size])
        g = (_silu(a) * b).astype(BF16)
        acc = acc + _bdot(g, wout_ref[start:start + size, :])
    o_ref[0] = x + (0.5 * mod_ref[0, gate_row:gate_row + 1, :]) * acc


def _ffn_call(xt, mod_i, gain, w_in, w_out, rows, n_tiles):
    batch = xt.shape[0]
    return pl.pallas_call(
        functools.partial(_ffn_kernel, rows=rows),
        grid=(batch, n_tiles),
        in_specs=[_tok_spec(D_MODEL), _mod_spec(batch), _full_spec((1, D_MODEL)),
                  _full_spec((D_MODEL, 2 * D_FF)), _full_spec((D_FF, D_MODEL))],
        out_specs=_tok_spec(D_MODEL),
        out_shape=jax.ShapeDtypeStruct((batch, n_tiles * TM, D_MODEL), F32),
        compiler_params=_cparams(2),
        name="ffn",
    )(xt, mod_i, gain, w_in, w_out)


def _even_in_kernel(x_ref, mod_ref, gain_ref, w_ref, qg_ref, kg_ref, bd_ref,
                    q_ref, k_ref, v_ref, u_ref):
    h = _modulated(x_ref[0], gain_ref[...], mod_ref, 3, 4).astype(BF16)

    def head_norm(t, g):
        hi, lo = _split_hi_lo(t * t)
        ms = _bdot(hi, bd_ref[...]) + _bdot(lo, bd_ref[...])
        return (t * lax.rsqrt(ms + RMS_EPS) * g).astype(BF16)

    q_ref[0] = head_norm(_bdot(h, w_ref[:, 0:NA_WIDTH]), qg_ref[...])
    k_ref[0] = head_norm(_bdot(h, w_ref[:, NA_WIDTH:2 * NA_WIDTH]), kg_ref[...])
    v_ref[0] = _bdot(h, w_ref[:, 2 * NA_WIDTH:3 * NA_WIDTH]).astype(BF16)
    u_ref[0] = _bdot(h, w_ref[:, 3 * NA_WIDTH:])


def _even_in_call(xt, mod_i, gain, w, q_gain, k_gain, bd):
    batch = xt.shape[0]
    n_in = w.shape[1]
    return pl.pallas_call(
        _even_in_kernel,
        grid=(batch, NT_ALL),
        in_specs=[_tok_spec(D_MODEL), _mod_spec(batch), _full_spec((1, D_MODEL)),
                  _full_spec((D_MODEL, n_in)), _full_spec((1, NA_WIDTH)), _full_spec((1, NA_WIDTH)),
                  _full_spec((NA_WIDTH, NA_WIDTH))],
        out_specs=[_tok_spec(NA_WIDTH), _tok_spec(NA_WIDTH), _tok_spec(NA_WIDTH),
                   _tok_spec(3 * HY_WIDTH)],
        out_shape=[jax.ShapeDtypeStruct((batch, T_TOK, NA_WIDTH), BF16)] * 3
        + [jax.ShapeDtypeStruct((batch, T_TOK, 3 * HY_WIDTH), F32)],
        compiler_params=_cparams(2),
        name="even_in",
    )(xt, mod_i, gain, w, q_gain, k_gain, bd)


def _na_kernel(q_ref, k_ref, v_ref, bias_ref, o_ref):
    lane = lax.broadcasted_iota(jnp.int32, (1, LANES), 1)
    first_head = lane < NA_HEAD_DIM
    k_ctx = k_ref[0, SEQ:T_TOK, :]
    v_ctx = v_ref[0, SEQ:T_TOK, :]

    def stack_heads(q):
        zero = jnp.zeros_like(q)
        return jnp.concatenate([jnp.where(first_head, q, zero), jnp.where(first_head, zero, q)], axis=0)

    def attend(qs, scores_and_values):
        s_list = [lax.dot_general(qs, kk, _NT_DIMS, preferred_element_type=F32) if bias is None
                  else lax.dot_general(qs, kk, _NT_DIMS, preferred_element_type=F32) + bias
                  for kk, _, bias in scores_and_values]
        m = functools.reduce(jnp.maximum, [jnp.max(s, axis=-1, keepdims=True) for s in s_list])
        p_list = [jnp.exp(s - m) for s in s_list]
        denom = functools.reduce(jnp.add, [jnp.sum(p, axis=-1, keepdims=True) for p in p_list])
        o = functools.reduce(jnp.add, [_bdot(p.astype(BF16), vv)
                                       for p, (_, vv, _) in zip(p_list, scores_and_values)])
        o = o / denom
        n = qs.shape[0] // 2
        return jnp.where(first_head, o[:n], o[n:]).astype(BF16)

    def block(i, carry):
        r0 = i * NA_RB
        u0 = jnp.clip(r0 - WIN_R // 2, 0, GRID_H - NA_KR)
        pattern = jnp.where(i == 0, 0, jnp.where(i == GRID_H // NA_RB - 1, 2, 1))
        q0 = pl.multiple_of(r0 * GRID_W, NA_NQ)
        k0 = pl.multiple_of(u0 * GRID_W, GRID_W)
        qs = stack_heads(q_ref[0, pl.ds(q0, NA_NQ), :])
        k_win = k_ref[0, pl.ds(k0, NA_NK), :]
        v_win = v_ref[0, pl.ds(k0, NA_NK), :]
        o_ref[0, pl.ds(q0, NA_NQ), :] = attend(
            qs, [(k_win, v_win, bias_ref[pattern, 0]), (k_ctx, v_ctx, None)])
        return carry

    lax.fori_loop(0, GRID_H // NA_RB, block, 0)
    o_ref[0, SEQ:T_TOK, :] = attend(stack_heads(q_ref[0, SEQ:T_TOK, :]), [(k_ctx, v_ctx, None)])


def _na_call(q, k, v, bias):
    batch = q.shape[0]
    n_pairs = NA_WIDTH // LANES
    spec = pl.BlockSpec((1, T_TOK, LANES), lambda b, p: (b, 0, p))
    return pl.pallas_call(
        _na_kernel,
        grid=(batch, n_pairs),
        in_specs=[spec, spec, spec,
                  pl.BlockSpec((3, 1, 2 * NA_NQ, NA_NK), lambda b, p: (0, p, 0, 0))],
        out_specs=spec,
        out_shape=jax.ShapeDtypeStruct((batch, T_TOK, NA_WIDTH), BF16),
        compiler_params=_cparams(2),
        name="na_attn",
    )(q, k, v, bias)


def _na_bias_table(rpb):
    j = np.arange(NA_RB)[:, None, None, None]
    c = np.arange(GRID_W)[None, :, None, None]
    kk = np.arange(NA_KR)[None, None, :, None]
    kc = np.arange(GRID_W)[None, None, None, :]
    cs = np.clip(c - WIN_C // 2, 0, GRID_W - WIN_C)
    col_ok = (kc >= cs) & (kc < cs + WIN_C)
    dc = np.clip(kc - c + WIN_C - 1, 0, 2 * WIN_C - 2)
    shape = (NA_RB, GRID_W, NA_KR, GRID_W)
    tables = []
    for off, rs_rel in ((0, 0 * j), (-(WIN_R // 2), j), (-(NA_KR - NA_RB), NA_KR - WIN_R + 0 * j)):
        row_ok = (kk >= rs_rel) & (kk < rs_rel + WIN_R)
        dr = np.clip(off + kk - j + WIN_R - 1, 0, 2 * WIN_R - 2)
        ok = np.broadcast_to(row_ok & col_ok, shape)
        vals = rpb[:, np.broadcast_to(dr, shape), np.broadcast_to(dc, shape)]
        tables.append(jnp.where(ok[None], vals.astype(F32), NEG_BIG).reshape(NA_HEADS, NA_NQ, NA_NK))
    return jnp.stack(tables).reshape(3, NA_HEADS // 2, 2 * NA_NQ, NA_NK)


def _hy_pre_kernel(u0_ref, u1_ref, uv_ref, w0_ref, w1_ref, wv_ref, b0_ref, b1_ref, bv_ref,
                   zl_ref, xl_ref, zc_ref, xc_ref):
    row = lax.broadcasted_iota(jnp.int32, (TM, LANES), 0)
    zero_row = jnp.zeros((1, LANES), F32)

    def conv(u_ref, w_ref, b_ref, tile):
        s = tile * TM
        cur = u_ref[0, s:s + TM, :]
        seq_start = tile in (0, NT_LAT)
        seq_end = tile in (NT_LAT - 1, NT_LAT)
        prev_row = zero_row if seq_start else u_ref[0, s - 1:s, :]
        next_row = zero_row if seq_end else u_ref[0, s + TM:s + TM + 1, :]
        before = jnp.where(row == 0, prev_row, pltpu.roll(cur, 1, 0))
        after = jnp.where(row == TM - 1, next_row, pltpu.roll(cur, TM - 1, 0))
        return before * w_ref[0:1, :] + cur * w_ref[1:2, :] + after * w_ref[2:3, :] + b_ref[...]

    for tile in range(NT_ALL):
        x0 = conv(u0_ref, w0_ref, b0_ref, tile)
        z = conv(uv_ref, wv_ref, bv_ref, tile) * conv(u1_ref, w1_ref, b1_ref, tile)
        if tile < NT_LAT:
            zl_ref[0, tile * TM:(tile + 1) * TM, :] = z
            xl_ref[0, tile * TM:(tile + 1) * TM, :] = x0
        else:
            zc_ref[0] = z
            xc_ref[0] = x0


def _hy_pre_call(u, conv_w, conv_b):
    batch = u.shape[0]
    nb = HY_WIDTH // LANES
    u_specs = [pl.BlockSpec((1, T_TOK, LANES), lambda b, cb, g=g: (b, 0, g * nb + cb)) for g in range(3)]
    w_specs = [pl.BlockSpec((3, LANES), lambda b, cb, g=g: (0, g * nb + cb)) for g in range(3)]
    b_specs = [pl.BlockSpec((1, LANES), lambda b, cb, g=g: (0, g * nb + cb)) for g in range(3)]
    lat = pl.BlockSpec((1, SEQ, LANES), lambda b, cb: (b, 0, cb))
    ctx = pl.BlockSpec((1, CTX_LEN, LANES), lambda b, cb: (b, 0, cb))
    return pl.pallas_call(
        _hy_pre_kernel,
        grid=(batch, nb),
        in_specs=u_specs + w_specs + b_specs,
        out_specs=[lat, lat, ctx, ctx],
        out_shape=[jax.ShapeDtypeStruct((batch, SEQ, HY_WIDTH), F32)] * 2
        + [jax.ShapeDtypeStruct((batch, CTX_LEN, HY_WIDTH), F32)] * 2,
        compiler_params=_cparams(2),
        name="hy_pre",
    )(u, u, u, conv_w, conv_w, conv_w, conv_b, conv_b, conv_b)


def _complex_block(re, im):
    return np.block([[re, -im], [im, re]])


@functools.lru_cache(maxsize=None)
def _fft_constants():
    k1 = np.arange(FFT_N1)
    n1 = np.arange(FFT_N1H)
    f1 = np.exp(-2j * np.pi * np.outer(k1, n1) / FFT_N1)
    m1 = _complex_block(f1.real, f1.imag)
    n2 = np.arange(FFT_N2)
    tw = np.exp(-2j * np.pi * np.outer(k1, n2) / FFT_N)
    tw = np.stack([tw.real, tw.imag])[..., None] * np.ones((1, 1, 1, LANES))
    f2 = np.exp(-2j * np.pi * np.outer(n2, n2) / FFT_N2)
    m2 = _complex_block(f2.real, f2.imag)
    m2i = _complex_block(f2.real, -f2.imag)
    c1 = np.exp(2j * np.pi * np.outer(n1, k1) / FFT_N1) / FFT_N
    m3 = _complex_block(c1.real, c1.imag)
    kc = np.arange(FFT_NC)
    nc = np.arange(CTX_LEN)
    ang = 2 * np.pi * np.outer(kc, nc) / FFT_NC
    mcf = np.concatenate([np.cos(ang), -np.sin(ang)], axis=0)
    mci = np.concatenate([np.cos(ang.T), -np.sin(ang.T)], axis=1) / FFT_NC

    def hi_lo(m):
        hi = m.astype(ml_dtypes.bfloat16)
        lo = (m - hi.astype(np.float64)).astype(ml_dtypes.bfloat16)
        return hi, lo

    return dict(m1=hi_lo(m1), m2=hi_lo(m2), m2i=hi_lo(m2i), m3=hi_lo(m3), mcf=hi_lo(mcf),
                mci=hi_lo(mci), tw=tw.astype(np.float32))


def _fft_s1_kernel(z_ref, mhi_ref, mlo_ref, a_ref):
    d = z_ref[0].reshape(2 * FFT_N1H, FFT_W)
    a_ref[0] = _dot3(mhi_ref[...], mlo_ref[...], d)


def _fft_s2_kernel(a_ref, tw_ref, g_ref, fhi_ref, flo_ref, ihi_ref, ilo_ref, o_ref):
    reps = HY_WIDTH // LANES
    for kk in range(FFT_KB):
        twr = jnp.concatenate([tw_ref[0, kk]] * reps, axis=1)
        twi = jnp.concatenate([tw_ref[1, kk]] * reps, axis=1)
        ar = a_ref[0, 0, kk]
        ai = a_ref[0, 1, kk]
        d = jnp.concatenate([ar * twr - ai * twi, ar * twi + ai * twr], axis=0)
        x = _dot3(fhi_ref[...], flo_ref[...], d)
        xr, xi = x[:FFT_N2], x[FFT_N2:]
        gr, gi = g_ref[0, kk], g_ref[1, kk]
        y = jnp.concatenate([xr * gr - xi * gi, xr * gi + xi * gr], axis=0)
        b = _dot3(ihi_ref[...], ilo_ref[...], y)
        br, bi = b[:FFT_N2], b[FFT_N2:]
        o_ref[0, 0, kk] = br * twr + bi * twi
        o_ref[0, 1, kk] = bi * twr - br * twi


def _fft_s3_kernel(b_ref, mhi_ref, mlo_ref, z_ref, x0_ref, bias_ref, y_ref):
    y = _dot3(mhi_ref[...], mlo_ref[...], b_ref[0]).reshape(2, FFT_N1H, FFT_W)
    y_ref[0] = ((y + z_ref[0] * bias_ref[...]) * x0_ref[0]).astype(BF16)


def _hy_long_conv_call(z_lat, x0_lat, g_spec, d_bias):
    batch = z_lat.shape[0]
    pairs = batch // 2
    cst = _fft_constants()
    n_lane = FFT_N2 * HY_WIDTH
    zv = z_lat.reshape(pairs, 2, FFT_N1H, n_lane)
    xv = x0_lat.reshape(pairs, 2, FFT_N1H, n_lane)
    pair_spec = pl.BlockSpec((1, 2, FFT_N1H, FFT_W), lambda p, w: (p, 0, 0, w))
    a = pl.pallas_call(
        _fft_s1_kernel,
        grid=(pairs, n_lane // FFT_W),
        in_specs=[pair_spec, _full_spec((2 * FFT_N1, 2 * FFT_N1H)), _full_spec((2 * FFT_N1, 2 * FFT_N1H))],
        out_specs=pl.BlockSpec((1, 2 * FFT_N1, FFT_W), lambda p, w: (p, 0, w)),
        out_shape=jax.ShapeDtypeStruct((pairs, 2 * FFT_N1, n_lane), F32),
        compiler_params=_cparams(2),
        name="hy_fft_s1",
    )(zv, *cst["m1"])
    a = a.reshape(pairs, 2, FFT_N1, FFT_N2, HY_WIDTH)
    blk = pl.BlockSpec((1, 2, FFT_KB, FFT_N2, HY_WIDTH), lambda kb, p: (p, 0, kb, 0, 0))
    sq = (2 * FFT_N2, 2 * FFT_N2)
    b = pl.pallas_call(
        _fft_s2_kernel,
        grid=(FFT_N1 // FFT_KB, pairs),
        in_specs=[blk,
                  pl.BlockSpec((2, FFT_KB, FFT_N2, LANES), lambda kb, p: (0, kb, 0, 0)),
                  pl.BlockSpec((2, FFT_KB, FFT_N2, HY_WIDTH), lambda kb, p: (0, kb, 0, 0)),
                  _full_spec(sq), _full_spec(sq), _full_spec(sq), _full_spec(sq)],
        out_specs=blk,
        out_shape=jax.ShapeDtypeStruct((pairs, 2, FFT_N1, FFT_N2, HY_WIDTH), F32),
        compiler_params=_cparams(2),
        name="hy_fft_s2",
    )(a, cst["tw"], g_spec, *cst["m2"], *cst["m2i"])
    b = b.reshape(pairs, 2 * FFT_N1, n_lane)
    bias = jnp.tile(d_bias.reshape(1, HY_WIDTH), (1, FFT_W // HY_WIDTH))
    y = pl.pallas_call(
        _fft_s3_kernel,
        grid=(pairs, n_lane // FFT_W),
        in_specs=[pl.BlockSpec((1, 2 * FFT_N1, FFT_W), lambda p, w: (p, 0, w)),
                  _full_spec((2 * FFT_N1H, 2 * FFT_N1)), _full_spec((2 * FFT_N1H, 2 * FFT_N1)),
                  pair_spec, pair_spec, _full_spec((1, FFT_W))],
        out_specs=pair_spec,
        out_shape=jax.ShapeDtypeStruct((pairs, 2, FFT_N1H, n_lane), BF16),
        compiler_params=_cparams(2),
        name="hy_fft_s3",
    )(b, *cst["m3"], zv, xv, bias)
    return y.reshape(batch, SEQ, HY_WIDTH)


def _hy_ctx_kernel(z_ref, x0_ref, g_ref, fhi_ref, flo_ref, ihi_ref, ilo_ref, bias_ref, y_ref):
    z = z_ref[0]
    x = _dot3(fhi_ref[...], flo_ref[...], z)
    xr, xi = x[:FFT_NC], x[FFT_NC:]
    gr, gi = g_ref[0], g_ref[1]
    y = jnp.concatenate([xr * gr - xi * gi, xr * gi + xi * gr], axis=0)
    conv = _dot3(ihi_ref[...], ilo_ref[...], y)
    y_ref[0] = ((conv + z * bias_ref[...]) * x0_ref[0]).astype(BF16)


def _hy_ctx_conv_call(z_ctx, x0_ctx, g_spec, d_bias):
    batch = z_ctx.shape[0]
    cst = _fft_constants()
    tok = pl.BlockSpec((1, CTX_LEN, HY_WIDTH), lambda b: (b, 0, 0))

    def full(shape):
        zeros = (0,) * len(shape)
        return pl.BlockSpec(shape, lambda b: zeros)

    return pl.pallas_call(
        _hy_ctx_kernel,
        grid=(batch,),
        in_specs=[tok, tok, full((2, FFT_NC, HY_WIDTH)),
                  full((2 * FFT_NC, CTX_LEN)), full((2 * FFT_NC, CTX_LEN)),
                  full((CTX_LEN, 2 * FFT_NC)), full((CTX_LEN, 2 * FFT_NC)), full((1, HY_WIDTH))],
        out_specs=tok,
        out_shape=jax.ShapeDtypeStruct((batch, CTX_LEN, HY_WIDTH), BF16),
        compiler_params=_cparams(1),
        name="hy_ctx",
    )(z_ctx, x0_ctx, g_spec, *cst["mcf"], *cst["mci"], d_bias.reshape(1, HY_WIDTH))


def _hyena_filter(length, fw1, fb1, fw2, fb2, fw3, fb3, fw4, freq):
    t = jnp.linspace(0.0, 1.0, length, dtype=F32)[:, None]
    w = 2.0 * math.pi * jnp.arange(length, dtype=F32)[:, None] / length
    bands = jnp.linspace(1e-4, HY_BANDS - 1, HY_BANDS, dtype=F32)
    emb = jnp.concatenate([t, jnp.cos(bands * w), -jnp.sin(bands * w)], axis=-1)
    hp = lax.Precision.HIGHEST
    h = jnp.sin(freq * (jnp.dot(emb, fw1, precision=hp) + fb1))
    h = jnp.sin(freq * (jnp.dot(h, fw2, precision=hp) + fb2))
    h = jnp.sin(freq * (jnp.dot(h, fw3, precision=hp) + fb3))
    h = jnp.dot(h, fw4, precision=hp).astype(F32)
    max_decay = math.log(HY_TARGET) / HY_FAST_PCT
    min_decay = math.log(HY_TARGET) / HY_SLOW_PCT
    deltas = jnp.linspace(min_decay, max_decay, HY_WIDTH, dtype=F32)
    window = jnp.exp(-t * jnp.abs(deltas))
    h_fwd = h[:, :HY_WIDTH] * window
    h_bwd = h[:, HY_WIDTH:] * window
    return jnp.concatenate([h_fwd[:1] + h_bwd[:1], h_fwd[1:],
                            jnp.zeros((1, HY_WIDTH), F32), h_bwd[1:][::-1]], axis=0)


def _even_out_kernel(x_ref, mod_ref, a_ref, yl_ref, yc_ref, w_ref, o_ref):
    is_ctx = pl.program_id(1) == NT_LAT
    y = jnp.where(is_ctx, yc_ref[0], yl_ref[0])
    o = _bdot(a_ref[0], w_ref[0:NA_WIDTH, :]) + _bdot(y, w_ref[NA_WIDTH:, :])
    o_ref[0] = x_ref[0] + mod_ref[0, 5:6, :] * o


def _even_out_call(xt, mod_i, a, y_lat, y_ctx, w):
    batch = xt.shape[0]
    return pl.pallas_call(
        _even_out_kernel,
        grid=(batch, NT_ALL),
        in_specs=[_tok_spec(D_MODEL), _mod_spec(batch), _tok_spec(NA_WIDTH),
                  pl.BlockSpec((1, TM, HY_WIDTH), lambda b, j: (b, jnp.minimum(j, NT_LAT - 1), 0)),
                  pl.BlockSpec((1, CTX_LEN, HY_WIDTH), lambda b, j: (b, 0, 0)),
                  _full_spec((D_MODEL, D_MODEL))],
        out_specs=_tok_spec(D_MODEL),
        out_shape=jax.ShapeDtypeStruct((batch, T_TOK, D_MODEL), F32),
        compiler_params=_cparams(2),
        name="even_out",
    )(xt, mod_i, a, y_lat, y_ctx, w)


def _odd_in_kernel(x_ref, mod_ref, gain_ref, w_ref, cos_ref, sin_ref, q_ref, k_ref, v_ref, sg_ref):
    h = _modulated(x_ref[0], gain_ref[...], mod_ref, 3, 4).astype(BF16)
    cos = cos_ref[...]
    sin = sin_ref[...]

    def rope_store(dst_ref, col0, scale):
        p = _bdot(h, w_ref[:, col0:col0 + RET_QK])
        for c in range(RET_QK // LANES):
            t = p[:, c * LANES:(c + 1) * LANES]
            half = (c % 2) * LANES
            r = t * cos[:, half:half + LANES] + pltpu.roll(t, LANES // 2, 1) * sin[:, half:half + LANES]
            dst_ref[0, :, c * LANES:(c + 1) * LANES] = (r * scale).astype(BF16)

    rope_store(q_ref, 0, 1.0)
    rope_store(k_ref, RET_QK, RET_KEY_DIM ** -0.5)
    v_ref[0] = _bdot(h, w_ref[:, 2 * RET_QK:2 * RET_QK + RET_V]).astype(BF16)
    sg_ref[0] = _silu(_bdot(h, w_ref[:, 2 * RET_QK + RET_V:])).astype(BF16)


def _odd_in_call(xt, mod_i, gain, w, cos_t, sin_t):
    batch = xt.shape[0]
    rope_spec = pl.BlockSpec((TM, RET_KEY_DIM), lambda b, j: (j, 0))
    return pl.pallas_call(
        _odd_in_kernel,
        grid=(batch, NT_ALL),
        in_specs=[_tok_spec(D_MODEL), _mod_spec(batch), _full_spec((1, D_MODEL)),
                  _full_spec((D_MODEL, w.shape[1])), rope_spec, rope_spec],
        out_specs=[_tok_spec(RET_QK), _tok_spec(RET_QK), _tok_spec(RET_V), _tok_spec(RET_V)],
        out_shape=[jax.ShapeDtypeStruct((batch, T_TOK, RET_QK), BF16)] * 2
        + [jax.ShapeDtypeStruct((batch, T_TOK, RET_V), BF16)] * 2,
        compiler_params=_cparams(2),
        name="odd_in",
    )(xt, mod_i, gain, w, cos_t, sin_t)


def _rope_tables():
    t = np.arange(SEQ)
    n_freq = RET_KEY_DIM // 4
    inv = ROPE_BASE ** (-jnp.arange(n_freq, dtype=F32) / n_freq)
    ang_r = jnp.asarray(t // GRID_W, F32)[:, None] * inv
    ang_c = jnp.asarray(t % GRID_W, F32)[:, None] * inv
    cr, sr, cc, sc = jnp.cos(ang_r), jnp.sin(ang_r), jnp.cos(ang_c), jnp.sin(ang_c)
    cos_l = jnp.concatenate([cr, cr, cc, cc], axis=-1)
    sin_l = jnp.concatenate([-sr, sr, -sc, sc], axis=-1)
    cos_t = jnp.concatenate([cos_l, jnp.ones((CTX_LEN, RET_KEY_DIM), F32)], axis=0)
    sin_t = jnp.concatenate([sin_l, jnp.zeros((CTX_LEN, RET_KEY_DIM), F32)], axis=0)
    return cos_t, sin_t


def _ret_kernel(lg_ref, q_ref, k_ref, v_ref, sg_ref, y_ref, o_acc, state):
    head = pl.program_id(1)
    n_c = RET_CHUNK
    n_lat = SEQ // n_c
    n_ctx = CTX_LEN // n_c
    n_all = n_lat + n_ctx
    ii = lax.broadcasted_iota(jnp.int32, (n_c, n_c), 0).astype(F32)
    jj = lax.broadcasted_iota(jnp.int32, (n_c, n_c), 1).astype(F32)
    row_k = lax.broadcasted_iota(jnp.int32, (n_c, RET_KEY_DIM), 0).astype(F32)
    row_v = lax.broadcasted_iota(jnp.int32, (n_c, RET_VAL_DIM), 0).astype(F32)

    def scan(backward):
        lg = lg_ref[1 if backward else 0, head]
        if backward:
            diff = jj - ii
            xi = jnp.exp(lg * (n_c - row_v))
            zeta = jnp.exp(lg * row_k)
        else:
            diff = ii - jj
            xi = jnp.exp(lg * (row_v + 1.0))
            zeta = jnp.exp(lg * (n_c - 1.0 - row_k))
        dmask = jnp.where(diff >= 0, jnp.exp(lg * jnp.maximum(diff, 0.0)), 0.0)
        g_chunk = jnp.exp(lg * n_c + jnp.zeros((1, RET_VAL_DIM), F32))
        state[...] = jnp.zeros_like(state)

        def step(s, carry):
            if backward:
                chunk = n_all - 1 - s
            else:
                chunk = jnp.where(s < n_ctx, n_lat + s, s - n_ctx)
            r = pl.multiple_of(chunk * n_c, n_c)
            qc = q_ref[0, pl.ds(r, n_c), :]
            kc = k_ref[0, pl.ds(r, n_c), :]
            vc = v_ref[0, pl.ds(r, n_c), :]
            st = state[...]
            inner = lax.dot_general(qc, kc, _NT_DIMS, preferred_element_type=F32) * dmask
            o = _bdot(inner.astype(BF16), vc) + _bdot(qc, st.astype(BF16)) * xi
            kz = (kc.astype(F32) * zeta).astype(BF16)
            state[...] = st * g_chunk + lax.dot_general(kz, vc, _TN_DIMS, preferred_element_type=F32)
            if backward:
                tot = o_acc[pl.ds(r, n_c), :] + o
                mu = jnp.mean(tot, axis=-1, keepdims=True)
                cen = tot - mu
                var = jnp.mean(cen * cen, axis=-1, keepdims=True)
                yn = cen * lax.rsqrt(var + GN_EPS)
                y_ref[0, pl.ds(r, n_c), :] = (sg_ref[0, pl.ds(r, n_c), :].astype(F32) * yn).astype(BF16)
            else:
                o_acc[pl.ds(r, n_c), :] = o
            return carry

        lax.fori_loop(0, n_all, step, 0)

    scan(False)
    scan(True)


def _ret_call(lg, q, k, v, sg):
    batch = q.shape[0]
    qk_spec = pl.BlockSpec((1, T_TOK, RET_KEY_DIM), lambda b, h: (b, 0, h))
    v_spec = pl.BlockSpec((1, T_TOK, RET_VAL_DIM), lambda b, h: (b, 0, h))
    return pl.pallas_call(
        _ret_kernel,
        grid=(batch, RET_HEADS),
        in_specs=[pl.BlockSpec(memory_space=pltpu.SMEM), qk_spec, qk_spec, v_spec, v_spec],
        out_specs=v_spec,
        out_shape=jax.ShapeDtypeStruct((batch, T_TOK, RET_V), BF16),
        scratch_shapes=[pltpu.VMEM((T_TOK, RET_VAL_DIM), F32),
                        pltpu.VMEM((RET_KEY_DIM, RET_VAL_DIM), F32)],
        compiler_params=_cparams(2),
        name="retention",
    )(lg, q, k, v, sg)


def _odd_out_kernel(x_ref, mod_ref, y_ref, w_ref, o_ref):
    o_ref[0] = x_ref[0] + mod_ref[0, 5:6, :] * _bdot(y_ref[0], w_ref[...])


def _odd_out_call(xt, mod_i, y, w):
    batch = xt.shape[0]
    return pl.pallas_call(
        _odd_out_kernel,
        grid=(batch, NT_ALL),
        in_specs=[_tok_spec(D_MODEL), _mod_spec(batch), _tok_spec(RET_V), _full_spec((RET_V, D_MODEL))],
        out_specs=_tok_spec(D_MODEL),
        out_shape=jax.ShapeDtypeStruct((batch, T_TOK, D_MODEL), F32),
        compiler_params=_cparams(2),
        name="odd_out",
    )(xt, mod_i, y, w)


def _filter_spectra(filter_params):
    g_lat = jnp.fft.fft(_hyena_filter(SEQ, *filter_params), axis=0)
    g_lat = g_lat.reshape(FFT_N2, FFT_N1, HY_WIDTH).transpose(1, 0, 2)
    g_lat = jnp.stack([g_lat.real, g_lat.imag]).astype(F32)
    g_ctx = jnp.fft.fft(_hyena_filter(CTX_LEN, *filter_params), axis=0)
    g_ctx = jnp.stack([g_ctx.real, g_ctx.imag]).astype(F32)
    return g_lat, g_ctx


def _even_mixer(xt, mod_i, gain, w_in, w_out, q_gain, k_gain, rpb, conv_w, conv_b, filter_params, d_bias):
    scale = NA_HEAD_DIM ** -0.5
    qg = (jnp.tile(q_gain, NA_HEADS) * scale).reshape(1, NA_WIDTH)
    kg = jnp.tile(k_gain, NA_HEADS).reshape(1, NA_WIDTH)
    head_of = np.arange(NA_WIDTH) // NA_HEAD_DIM
    bd = jnp.asarray((head_of[:, None] == head_of[None, :]) / NA_HEAD_DIM, BF16)
    q, k, v, u = _even_in_call(xt, mod_i, gain, w_in.astype(BF16), qg, kg, bd)
    a = _na_call(q, k, v, _na_bias_table(rpb))
    z_lat, x0_lat, z_ctx, x0_ctx = _hy_pre_call(u, conv_w, conv_b.reshape(1, -1))
    g_lat, g_ctx = _filter_spectra(filter_params)
    y_lat = _hy_long_conv_call(z_lat, x0_lat, g_lat, d_bias)
    y_ctx = _hy_ctx_conv_call(z_ctx, x0_ctx, g_ctx, d_bias)
    return _even_out_call(xt, mod_i, a, y_lat, y_ctx, w_out.astype(BF16))


def _odd_mixer(xt, mod_i, gain, w_in, w_out, logit_f, logit_b, rope):
    q, k, v, sg = _odd_in_call(xt, mod_i, gain, w_in.astype(BF16), *rope)
    lg = jnp.stack([jax.nn.log_sigmoid(logit_f.astype(F32)), jax.nn.log_sigmoid(logit_b.astype(F32))])
    y = _ret_call(lg, q, k, v, sg)
    return _odd_out_call(xt, mod_i, y, w_out.astype(BF16))


def kernel(x, c, ctx, c_ctx, w_mod, b_mod, norm_gain, ffn_a_in, ffn_a_out, ffn_b_in, ffn_b_out,
           even_in, even_out, na_q_gain, na_k_gain, na_rpb, hy_conv_w, hy_conv_b,
           hy_fw1, hy_fb1, hy_fw2, hy_fb2, hy_fw3, hy_fb3, hy_fw4, hy_freq, hy_bias,
           ret_in, ret_out, ret_logit_f, ret_logit_b):
    batch = x.shape[0]
    assert x.shape == (batch, SEQ, D_MODEL) and ctx.shape == (batch, CTX_LEN, D_MODEL)
    assert batch % 2 == 0 and batch < MOD_ROWS
    xt = jnp.concatenate([x, ctx], axis=1)
    c_rows = jnp.concatenate([c, c_ctx[None], jnp.zeros((MOD_ROWS - batch - 1, D_MODEL), F32)], axis=0)
    mod_all = _mod_call(c_rows, w_mod, b_mod).reshape(DEPTH, MOD_ROWS, N_MOD, D_MODEL)
    rope = _rope_tables()
    for i in range(DEPTH):
        last = i == DEPTH - 1
        mod_i = mod_all[i]
        gains = norm_gain[i].reshape(3, 1, D_MODEL)
        xt = _ffn_call(xt, mod_i, gains[0], ffn_a_in[i].astype(BF16), ffn_a_out[i].astype(BF16),
                       (0, 1, 2), NT_ALL)
        if i % 2 == 0:
            e = i // 2
            filter_params = (hy_fw1[e], hy_fb1[e], hy_fw2[e], hy_fb2[e], hy_fw3[e], hy_fb3[e],
                             hy_fw4[e], hy_freq[e])
            xt = _even_mixer(xt, mod_i, gains[1], even_in[e], even_out[e], na_q_gain[e], na_k_gain[e],
                             na_rpb[e], hy_conv_w[e], hy_conv_b[e], filter_params, hy_bias[e])
        else:
            o = i // 2
            xt = _odd_mixer(xt, mod_i, gains[1], ret_in[o], ret_out[o], ret_logit_f[o], ret_logit_b[o], rope)
        xt = _ffn_call(xt, mod_i, gains[2], ffn_b_in[i].astype(BF16), ffn_b_out[i].astype(BF16),
                       (6, 7, 8), NT_LAT if last else NT_ALL)
    return xt
```

```python
import functools
import math

import ml_dtypes
import numpy as np
import jax
import jax.numpy as jnp
from jax import lax
from jax.experimental import pallas as pl
from jax.experimental.pallas import tpu as pltpu

F32 = jnp.float32
BF16 = jnp.bfloat16

D_MODEL = 1024
SEQ = 4096
DEPTH = 4
GRID_W = 64
CTX_LEN = 256
N_MOD = 9
RMS_EPS = 1e-6
GN_EPS = 1e-6
D_FF = 2816
NA_HEADS = 8
NA_HEAD_DIM = 64
NA_WIDTH = NA_HEADS * NA_HEAD_DIM
WIN_R = 8
WIN_C = 16
HY_WIDTH = D_MODEL - NA_WIDTH
HY_BANDS = 8
HY_TARGET = 1e-2
HY_FAST_PCT = 0.3
HY_SLOW_PCT = 1.5
RET_HEADS = 4
RET_KEY_DIM = D_MODEL // RET_HEADS
RET_VAL_DIM = 2 * RET_KEY_DIM
RET_QK = RET_HEADS * RET_KEY_DIM
RET_V = RET_HEADS * RET_VAL_DIM
RET_CHUNK = 128
ROPE_BASE = 10000.0

LANES = 128
VMEM_LIMIT_BYTES = 56 * 2**20
T_TOK = SEQ + CTX_LEN
TM = CTX_LEN
NT_LAT = SEQ // TM
NT_ALL = T_TOK // TM
MOD_ROWS = 16
FF_CHUNKS = ((0, 1536), (1536, 1280))

NA_RB = 4
NA_KR = NA_RB + WIN_R - 1
GRID_H = SEQ // GRID_W
NA_NQ = NA_RB * GRID_W
NA_NK = NA_KR * GRID_W
NEG_BIG = -1e30

FFT_N = 2 * SEQ
FFT_N1 = 64
FFT_N2 = 128
FFT_N1H = FFT_N1 // 2
FFT_W = 2048
FFT_KB = 4
FFT_NC = 2 * CTX_LEN


def _cparams(n_axes):
    return pltpu.CompilerParams(dimension_semantics=("arbitrary",) * n_axes,
                                vmem_limit_bytes=VMEM_LIMIT_BYTES)


def _bdot(a, b):
    return jnp.dot(a, b, preferred_element_type=F32)


_NT_DIMS = (((1,), (1,)), ((), ()))
_TN_DIMS = (((0,), (0,)), ((), ()))


def _split_hi_lo(m):
    hi = m.astype(BF16)
    lo = (m - hi.astype(F32)).astype(BF16)
    return hi, lo


def _dot3(m_hi, m_lo, d):
    d_hi, d_lo = _split_hi_lo(d)
    return _bdot(m_hi, d_hi) + _bdot(m_lo, d_hi) + _bdot(m_hi, d_lo)


def _modulated(x, gain, mod_ref, shift_row, scale_row):
    ms = jnp.mean(x * x, axis=-1, keepdims=True)
    y = x * lax.rsqrt(ms + RMS_EPS) * gain
    return (y * (1.0 + mod_ref[0, scale_row:scale_row + 1, :])
            + mod_ref[0, shift_row:shift_row + 1, :])


def _silu(a):
    return a * jax.nn.sigmoid(a)


def _mod_kernel(c_ref, w_ref, b_ref, o_ref):
    s = _silu(c_ref[...])
    o_ref[0] = jnp.dot(s, w_ref[0], precision=lax.Precision.HIGHEST,
                       preferred_element_type=F32) + b_ref[0]


def _mod_call(c_rows, w_mod, b_mod):
    depth, d, n = w_mod.shape
    tn = 1024
    return pl.pallas_call(
        _mod_kernel,
        grid=(depth, n // tn),
        in_specs=[pl.BlockSpec((MOD_ROWS, d), lambda i, j: (0, 0)),
                  pl.BlockSpec((1, d, tn), lambda i, j: (i, 0, j)),
                  pl.BlockSpec((1, 1, tn), lambda i, j: (i, 0, j))],
        out_specs=pl.BlockSpec((1, MOD_ROWS, tn), lambda i, j: (i, 0, j)),
        out_shape=jax.ShapeDtypeStruct((depth, MOD_ROWS, n), F32),
        compiler_params=_cparams(2),
        name="mod",
    )(c_rows, w_mod, b_mod.reshape(depth, 1, n))


def _tok_spec(width):
    return pl.BlockSpec((1, TM, width), lambda b, j: (b, j, 0))


def _mod_spec(batch):
    return pl.BlockSpec((1, N_MOD, D_MODEL), lambda b, j: (jnp.where(j == NT_LAT, batch, b), 0, 0))


def _full_spec(shape):
    zeros = (0,) * len(shape)
    return pl.BlockSpec(shape, lambda b, j: zeros)


def _ffn_kernel(x_ref, mod_ref, gain_ref, win_ref, wout_ref, o_ref, *, rows):
    shift_row, scale_row, gate_row = rows
    x = x_ref[0]
    h = _modulated(x, gain_ref[...], mod_ref, shift_row, scale_row).astype(BF16)
    acc = jnp.zeros((TM, D_MODEL), F32)
    for start, size in FF_CHUNKS:
        a = _bdot(h, win_ref[:, start:start + size])
        b = _bdot(h, win_ref[:, D_FF + start:D_FF + start + size])
        g = (_silu(a) * b).astype(BF16)
        acc = acc + _bdot(g, wout_ref[start:start + size, :])
    o_ref[0] = x + (0.5 * mod_ref[0, gate_row:gate_row + 1, :]) * acc


def _ffn_call(xt, mod_i, gain, w_in, w_out, rows, n_tiles):
    batch = xt.shape[0]
    return pl.pallas_call(
        functools.partial(_ffn_kernel, rows=rows),
        grid=(batch, n_tiles),
        in_specs=[_tok_spec(D_MODEL), _mod_spec(batch), _full_spec((1, D_MODEL)),
                  _full_spec((D_MODEL, 2 * D_FF)), _full_spec((D_FF, D_MODEL))],
        out_specs=_tok_spec(D_MODEL),
        out_shape=jax.ShapeDtypeStruct((batch, n_tiles * TM, D_MODEL), F32),
        compiler_params=_cparams(2),
        name="ffn",
    )(xt, mod_i, gain, w_in, w_out)


def _even_in_kernel(x_ref, mod_ref, gain_ref, w_ref, qg_ref, kg_ref, bd_ref,
                    q_ref, k_ref, v_ref, u_ref):
    h = _modulated(x_ref[0], gain_ref[...], mod_ref, 3, 4).astype(BF16)

    def head_norm(t, g):
        hi, lo = _split_hi_lo(t * t)
        ms = _bdot(hi, bd_ref[...]) + _bdot(lo, bd_ref[...])
        return (t * lax.rsqrt(ms + RMS_EPS) * g).astype(BF16)

    q_ref[0] = head_norm(_bdot(h, w_ref[:, 0:NA_WIDTH]), qg_ref[...])
    k_ref[0] = head_norm(_bdot(h, w_ref[:, NA_WIDTH:2 * NA_WIDTH]), kg_ref[...])
    v_ref[0] = _bdot(h, w_ref[:, 2 * NA_WIDTH:3 * NA_WIDTH]).astype(BF16)
    u_ref[0] = _bdot(h, w_ref[:, 3 * NA_WIDTH:])


def _even_in_call(xt, mod_i, gain, w, q_gain, k_gain, bd):
    batch = xt.shape[0]
    n_in = w.shape[1]
    return pl.pallas_call(
        _even_in_kernel,
        grid=(batch, NT_ALL),
        in_specs=[_tok_spec(D_MODEL), _mod_spec(batch), _full_spec((1, D_MODEL)),
                  _full_spec((D_MODEL, n_in)), _full_spec((1, NA_WIDTH)), _full_spec((1, NA_WIDTH)),
                  _full_spec((NA_WIDTH, NA_WIDTH))],
        out_specs=[_tok_spec(NA_WIDTH), _tok_spec(NA_WIDTH), _tok_spec(NA_WIDTH),
                   _tok_spec(3 * HY_WIDTH)],
        out_shape=[jax.ShapeDtypeStruct((batch, T_TOK, NA_WIDTH), BF16)] * 3
        + [jax.ShapeDtypeStruct((batch, T_TOK, 3 * HY_WIDTH), F32)],
        compiler_params=_cparams(2),
        name="even_in",
    )(xt, mod_i, gain, w, q_gain, k_gain, bd)


def _na_kernel(q_ref, k_ref, v_ref, bias_ref, o_ref):
    lane = lax.broadcasted_iota(jnp.int32, (1, LANES), 1)
    first_head = lane < NA_HEAD_DIM
    k_ctx = k_ref[0, SEQ:T_TOK, :]
    v_ctx = v_ref[0, SEQ:T_TOK, :]

    def stack_heads(q):
        zero = jnp.zeros_like(q)
        return jnp.concatenate([jnp.where(first_head, q, zero), jnp.where(first_head, zero, q)], axis=0)

    def attend(qs, scores_and_values):
        s_list = [lax.dot_general(qs, kk, _NT_DIMS, preferred_element_type=F32) if bias is None
                  else lax.dot_general(qs, kk, _NT_DIMS, preferred_element_type=F32) + bias
                  for kk, _, bias in scores_and_values]
        m = functools.reduce(jnp.maximum, [jnp.max(s, axis=-1, keepdims=True) for s in s_list])
        p_list = [jnp.exp(s - m) for s in s_list]
        denom = functools.reduce(jnp.add, [jnp.sum(p, axis=-1, keepdims=True) for p in p_list])
        o = functools.reduce(jnp.add, [_bdot(p.astype(BF16), vv)
                                       for p, (_, vv, _) in zip(p_list, scores_and_values)])
        o = o / denom
        n = qs.shape[0] // 2
        return jnp.where(first_head, o[:n], o[n:]).astype(BF16)

    def block(i, carry):
        r0 = i * NA_RB
        u0 = jnp.clip(r0 - WIN_R // 2, 0, GRID_H - NA_KR)
        pattern = jnp.where(i == 0, 0, jnp.where(i == GRID_H // NA_RB - 1, 2, 1))
        q0 = pl.multiple_of(r0 * GRID_W, NA_NQ)
        k0 = pl.multiple_of(u0 * GRID_W, GRID_W)
        qs = stack_heads(q_ref[0, pl.ds(q0, NA_NQ), :])
        k_win = k_ref[0, pl.ds(k0, NA_NK), :]
        v_win = v_ref[0, pl.ds(k0, NA_NK), :]
        o_ref[0, pl.ds(q0, NA_NQ), :] = attend(
            qs, [(k_win, v_win, bias_ref[pattern, 0]), (k_ctx, v_ctx, None)])
        return carry

    lax.fori_loop(0, GRID_H // NA_RB, block, 0, unroll=2)
    o_ref[0, SEQ:T_TOK, :] = attend(stack_heads(q_ref[0, SEQ:T_TOK, :]), [(k_ctx, v_ctx, None)])


def _na_call(q, k, v, bias):
    batch = q.shape[0]
    n_pairs = NA_WIDTH // LANES
    spec = pl.BlockSpec((1, T_TOK, LANES), lambda b, p: (b, 0, p))
    return pl.pallas_call(
        _na_kernel,
        grid=(batch, n_pairs),
        in_specs=[spec, spec, spec,
                  pl.BlockSpec((3, 1, 2 * NA_NQ, NA_NK), lambda b, p: (0, p, 0, 0))],
        out_specs=spec,
        out_shape=jax.ShapeDtypeStruct((batch, T_TOK, NA_WIDTH), BF16),
        compiler_params=_cparams(2),
        name="na_attn",
    )(q, k, v, bias)


def _na_bias_table(rpb):
    j = np.arange(NA_RB)[:, None, None, None]
    c = np.arange(GRID_W)[None, :, None, None]
    kk = np.arange(NA_KR)[None, None, :, None]
    kc = np.arange(GRID_W)[None, None, None, :]
    cs = np.clip(c - WIN_C // 2, 0, GRID_W - WIN_C)
    col_ok = (kc >= cs) & (kc < cs + WIN_C)
    dc = np.clip(kc - c + WIN_C - 1, 0, 2 * WIN_C - 2)[0, :, 0, :]
    col_sel = (dc[..., None] == np.arange(2 * WIN_C - 1)).astype(np.float32)
    shape = (NA_RB, GRID_W, NA_KR, GRID_W)
    oks, row_sels = [], []
    for off, rs_rel in ((0, 0 * j), (-(WIN_R // 2), j), (-(NA_KR - NA_RB), NA_KR - WIN_R + 0 * j)):
        row_ok = (kk >= rs_rel) & (kk < rs_rel + WIN_R)
        dr = np.clip(off + kk - j + WIN_R - 1, 0, 2 * WIN_R - 2)[:, 0, :, 0]
        row_sels.append((dr[..., None] == np.arange(2 * WIN_R - 1)).astype(np.float32))
        oks.append(np.broadcast_to(row_ok & col_ok, shape))
    vals = jnp.einsum("pjkr,hrc,qmc->phjqkm", np.stack(row_sels), rpb.astype(F32), col_sel,
                      precision=lax.Precision.HIGHEST)
    table = jnp.where(np.stack(oks)[:, None], vals, NEG_BIG)
    return table.reshape(3, NA_HEADS // 2, 2 * NA_NQ, NA_NK)


def _hy_pre_kernel(u0_ref, u1_ref, uv_ref, w0_ref, w1_ref, wv_ref, b0_ref, b1_ref, bv_ref,
                   zl_ref, xl_ref, zc_ref, xc_ref):
    row = lax.broadcasted_iota(jnp.int32, (TM, LANES), 0)
    zero_row = jnp.zeros((1, LANES), F32)

    def conv(u_ref, w_ref, b_ref, tile):
        s = tile * TM
        cur = u_ref[0, s:s + TM, :]
        seq_start = tile in (0, NT_LAT)
        seq_end = tile in (NT_LAT - 1, NT_LAT)
        prev_row = zero_row if seq_start else u_ref[0, s - 1:s, :]
        next_row = zero_row if seq_end else u_ref[0, s + TM:s + TM + 1, :]
        before = jnp.where(row == 0, prev_row, pltpu.roll(cur, 1, 0))
        after = jnp.where(row == TM - 1, next_row, pltpu.roll(cur, TM - 1, 0))
        return before * w_ref[0:1, :] + cur * w_ref[1:2, :] + after * w_ref[2:3, :] + b_ref[...]

    for tile in range(NT_ALL):
        x0 = conv(u0_ref, w0_ref, b0_ref, tile)
        z = conv(uv_ref, wv_ref, bv_ref, tile) * conv(u1_ref, w1_ref, b1_ref, tile)
        if tile < NT_LAT:
            zl_ref[0, tile * TM:(tile + 1) * TM, :] = z
            xl_ref[0, tile * TM:(tile + 1) * TM, :] = x0
        else:
            zc_ref[0] = z
            xc_ref[0] = x0


def _hy_pre_call(u, conv_w, conv_b):
    batch = u.shape[0]
    nb = HY_WIDTH // LANES
    u_specs = [pl.BlockSpec((1, T_TOK, LANES), lambda b, cb, g=g: (b, 0, g * nb + cb)) for g in range(3)]
    w_specs = [pl.BlockSpec((3, LANES), lambda b, cb, g=g: (0, g * nb + cb)) for g in range(3)]
    b_specs = [pl.BlockSpec((1, LANES), lambda b, cb, g=g: (0, g * nb + cb)) for g in range(3)]
    lat = pl.BlockSpec((1, SEQ, LANES), lambda b, cb: (b, 0, cb))
    ctx = pl.BlockSpec((1, CTX_LEN, LANES), lambda b, cb: (b, 0, cb))
    return pl.pallas_call(
        _hy_pre_kernel,
        grid=(batch, nb),
        in_specs=u_specs + w_specs + b_specs,
        out_specs=[lat, lat, ctx, ctx],
        out_shape=[jax.ShapeDtypeStruct((batch, SEQ, HY_WIDTH), F32)] * 2
        + [jax.ShapeDtypeStruct((batch, CTX_LEN, HY_WIDTH), F32)] * 2,
        compiler_params=_cparams(2),
        name="hy_pre",
    )(u, u, u, conv_w, conv_w, conv_w, conv_b, conv_b, conv_b)


def _complex_block(re, im):
    return np.block([[re, -im], [im, re]])


@functools.lru_cache(maxsize=None)
def _fft_constants():
    k1 = np.arange(FFT_N1)
    n1 = np.arange(FFT_N1H)
    f1 = np.exp(-2j * np.pi * np.outer(k1, n1) / FFT_N1)
    m1 = _complex_block(f1.real, f1.imag)
    f1_full = np.exp(-2j * np.pi * np.outer(k1, k1) / FFT_N1)
    m1r = np.concatenate([f1_full.real, f1_full.imag], axis=0)
    n2 = np.arange(FFT_N2)
    tw = np.exp(-2j * np.pi * np.outer(k1, n2) / FFT_N)
    tw = np.stack([tw.real, tw.imag])[..., None] * np.ones((1, 1, 1, LANES))
    f2 = np.exp(-2j * np.pi * np.outer(n2, n2) / FFT_N2)
    m2 = _complex_block(f2.real, f2.imag)
    m2i = _complex_block(f2.real, -f2.imag)
    c1 = np.exp(2j * np.pi * np.outer(n1, k1) / FFT_N1) / FFT_N
    m3 = _complex_block(c1.real, c1.imag)
    kc = np.arange(FFT_NC)
    nc = np.arange(CTX_LEN)
    ang = 2 * np.pi * np.outer(kc, nc) / FFT_NC
    mcf = np.concatenate([np.cos(ang), -np.sin(ang)], axis=0)
    ang_full = 2 * np.pi * np.outer(kc, kc) / FFT_NC
    mcf_full = np.concatenate([np.cos(ang_full), -np.sin(ang_full)], axis=0)
    mci = np.concatenate([np.cos(ang.T), -np.sin(ang.T)], axis=1) / FFT_NC

    def hi_lo(m):
        hi = m.astype(ml_dtypes.bfloat16)
        lo = (m - hi.astype(np.float64)).astype(ml_dtypes.bfloat16)
        return hi, lo

    return dict(m1=hi_lo(m1), m1r=hi_lo(m1r), m2=hi_lo(m2), m2i=hi_lo(m2i), m3=hi_lo(m3),
                mcf=hi_lo(mcf), mcf_full=hi_lo(mcf_full), mci=hi_lo(mci), tw=tw.astype(np.float32))


def _fft_s1_kernel(z_ref, mhi_ref, mlo_ref, a_ref):
    d = z_ref[0].reshape(-1, FFT_W)
    a_ref[0] = _dot3(mhi_ref[...], mlo_ref[...], d)


def _fft_s2_kernel(a_ref, tw_ref, g_ref, fhi_ref, flo_ref, ihi_ref, ilo_ref, o_ref):
    reps = HY_WIDTH // LANES
    for kk in range(FFT_KB):
        twr = jnp.concatenate([tw_ref[0, kk]] * reps, axis=1)
        twi = jnp.concatenate([tw_ref[1, kk]] * reps, axis=1)
        ar = a_ref[0, 0, kk]
        ai = a_ref[0, 1, kk]
        d = jnp.concatenate([ar * twr - ai * twi, ar * twi + ai * twr], axis=0)
        x = _dot3(fhi_ref[...], flo_ref[...], d)
        xr, xi = x[:FFT_N2], x[FFT_N2:]
        gr, gi = g_ref[0, kk], g_ref[1, kk]
        y = jnp.concatenate([xr * gr - xi * gi, xr * gi + xi * gr], axis=0)
        b = _dot3(ihi_ref[...], ilo_ref[...], y)
        br, bi = b[:FFT_N2], b[FFT_N2:]
        o_ref[0, 0, kk] = br * twr + bi * twi
        o_ref[0, 1, kk] = bi * twr - br * twi


def _fft_s3_kernel(b_ref, mhi_ref, mlo_ref, z_ref, x0_ref, bias_ref, y_ref):
    y = _dot3(mhi_ref[...], mlo_ref[...], b_ref[0]).reshape(2, FFT_N1H, FFT_W)
    y_ref[0] = ((y + z_ref[0] * bias_ref[...]) * x0_ref[0]).astype(BF16)


def _hy_long_conv_call(z_lat, x0_lat, g_spec, d_bias):
    batch = z_lat.shape[0]
    pairs = batch // 2
    cst = _fft_constants()
    n_lane = FFT_N2 * HY_WIDTH
    zv = z_lat.reshape(pairs, 2, FFT_N1H, n_lane)
    xv = x0_lat.reshape(pairs, 2, FFT_N1H, n_lane)
    pair_spec = pl.BlockSpec((1, 2, FFT_N1H, FFT_W), lambda p, w: (p, 0, 0, w))
    a = pl.pallas_call(
        _fft_s1_kernel,
        grid=(pairs, n_lane // FFT_W),
        in_specs=[pair_spec, _full_spec((2 * FFT_N1, 2 * FFT_N1H)), _full_spec((2 * FFT_N1, 2 * FFT_N1H))],
        out_specs=pl.BlockSpec((1, 2 * FFT_N1, FFT_W), lambda p, w: (p, 0, w)),
        out_shape=jax.ShapeDtypeStruct((pairs, 2 * FFT_N1, n_lane), F32),
        compiler_params=_cparams(2),
        name="hy_fft_s1",
    )(zv, *cst["m1"])
    a = a.reshape(pairs, 2, FFT_N1, FFT_N2, HY_WIDTH)
    blk = pl.BlockSpec((1, 2, FFT_KB, FFT_N2, HY_WIDTH), lambda kb, p: (p, 0, kb, 0, 0))
    sq = (2 * FFT_N2, 2 * FFT_N2)
    b = pl.pallas_call(
        _fft_s2_kernel,
        grid=(FFT_N1 // FFT_KB, pairs),
        in_specs=[blk,
                  pl.BlockSpec((2, FFT_KB, FFT_N2, LANES), lambda kb, p: (0, kb, 0, 0)),
                  pl.BlockSpec((2, FFT_KB, FFT_N2, HY_WIDTH), lambda kb, p: (0, kb, 0, 0)),
                  _full_spec(sq), _full_spec(sq), _full_spec(sq), _full_spec(sq)],
        out_specs=blk,
        out_shape=jax.ShapeDtypeStruct((pairs, 2, FFT_N1, FFT_N2, HY_WIDTH), F32),
        compiler_params=_cparams(2),
        name="hy_fft_s2",
    )(a, cst["tw"], g_spec, *cst["m2"], *cst["m2i"])
    b = b.reshape(pairs, 2 * FFT_N1, n_lane)
    bias = jnp.tile(d_bias.reshape(1, HY_WIDTH), (1, FFT_W // HY_WIDTH))
    y = pl.pallas_call(
        _fft_s3_kernel,
        grid=(pairs, n_lane // FFT_W),
        in_specs=[pl.BlockSpec((1, 2 * FFT_N1, FFT_W), lambda p, w: (p, 0, w)),
                  _full_spec((2 * FFT_N1H, 2 * FFT_N1)), _full_spec((2 * FFT_N1H, 2 * FFT_N1)),
                  pair_spec, pair_spec, _full_spec((1, FFT_W))],
        out_specs=pair_spec,
        out_shape=jax.ShapeDtypeStruct((pairs, 2, FFT_N1H, n_lane), BF16),
        compiler_params=_cparams(2),
        name="hy_fft_s3",
    )(b, *cst["m3"], zv, xv, bias)
    return y.reshape(batch, SEQ, HY_WIDTH)


def _hy_ctx_kernel(z_ref, x0_ref, g_ref, fhi_ref, flo_ref, ihi_ref, ilo_ref, bias_ref, y_ref):
    z = z_ref[0]
    x = _dot3(fhi_ref[...], flo_ref[...], z)
    xr, xi = x[:FFT_NC], x[FFT_NC:]
    gr, gi = g_ref[0], g_ref[1]
    y = jnp.concatenate([xr * gr - xi * gi, xr * gi + xi * gr], axis=0)
    conv = _dot3(ihi_ref[...], ilo_ref[...], y)
    y_ref[0] = ((conv + z * bias_ref[...]) * x0_ref[0]).astype(BF16)


def _hy_ctx_conv_call(z_ctx, x0_ctx, g_spec, d_bias):
    batch = z_ctx.shape[0]
    cst = _fft_constants()
    tok = pl.BlockSpec((1, CTX_LEN, HY_WIDTH), lambda b: (b, 0, 0))

    def full(shape):
        zeros = (0,) * len(shape)
        return pl.BlockSpec(shape, lambda b: zeros)

    return pl.pallas_call(
        _hy_ctx_kernel,
        grid=(batch,),
        in_specs=[tok, tok, full((2, FFT_NC, HY_WIDTH)),
                  full((2 * FFT_NC, CTX_LEN)), full((2 * FFT_NC, CTX_LEN)),
                  full((CTX_LEN, 2 * FFT_NC)), full((CTX_LEN, 2 * FFT_NC)), full((1, HY_WIDTH))],
        out_specs=tok,
        out_shape=jax.ShapeDtypeStruct((batch, CTX_LEN, HY_WIDTH), BF16),
        compiler_params=_cparams(1),
        name="hy_ctx",
    )(z_ctx, x0_ctx, g_spec, *cst["mcf"], *cst["mci"], d_bias.reshape(1, HY_WIDTH))


def _hyena_filter(length, fw1, fb1, fw2, fb2, fw3, fb3, fw4, freq):
    t = jnp.linspace(0.0, 1.0, length, dtype=F32)[:, None]
    w = 2.0 * math.pi * jnp.arange(length, dtype=F32)[:, None] / length
    bands = jnp.linspace(1e-4, HY_BANDS - 1, HY_BANDS, dtype=F32)
    emb = jnp.concatenate([t, jnp.cos(bands * w), -jnp.sin(bands * w)], axis=-1)
    hp = lax.Precision.HIGHEST
    h = jnp.sin(freq * (jnp.dot(emb, fw1, precision=hp) + fb1))
    h = jnp.sin(freq * (jnp.dot(h, fw2, precision=hp) + fb2))
    h = jnp.sin(freq * (jnp.dot(h, fw3, precision=hp) + fb3))
    h = jnp.dot(h, fw4, precision=hp).astype(F32)
    max_decay = math.log(HY_TARGET) / HY_FAST_PCT
    min_decay = math.log(HY_TARGET) / HY_SLOW_PCT
    deltas = jnp.linspace(min_decay, max_decay, HY_WIDTH, dtype=F32)
    window = jnp.exp(-t * jnp.abs(deltas))
    h_fwd = h[:, :HY_WIDTH] * window
    h_bwd = h[:, HY_WIDTH:] * window
    return jnp.concatenate([h_fwd[:1] + h_bwd[:1], h_fwd[1:],
                            jnp.zeros((1, HY_WIDTH), F32), h_bwd[1:][::-1]], axis=0)


def _even_out_kernel(x_ref, mod_ref, a_ref, yl_ref, yc_ref, w_ref, o_ref):
    is_ctx = pl.program_id(1) == NT_LAT
    y = jnp.where(is_ctx, yc_ref[0], yl_ref[0])
    o = _bdot(a_ref[0], w_ref[0:NA_WIDTH, :]) + _bdot(y, w_ref[NA_WIDTH:, :])
    o_ref[0] = x_ref[0] + mod_ref[0, 5:6, :] * o


def _even_out_call(xt, mod_i, a, y_lat, y_ctx, w):
    batch = xt.shape[0]
    return pl.pallas_call(
        _even_out_kernel,
        grid=(batch, NT_ALL),
        in_specs=[_tok_spec(D_MODEL), _mod_spec(batch), _tok_spec(NA_WIDTH),
                  pl.BlockSpec((1, TM, HY_WIDTH), lambda b, j: (b, jnp.minimum(j, NT_LAT - 1), 0)),
                  pl.BlockSpec((1, CTX_LEN, HY_WIDTH), lambda b, j: (b, 0, 0)),
                  _full_spec((D_MODEL, D_MODEL))],
        out_specs=_tok_spec(D_MODEL),
        out_shape=jax.ShapeDtypeStruct((batch, T_TOK, D_MODEL), F32),
        compiler_params=_cparams(2),
        name="even_out",
    )(xt, mod_i, a, y_lat, y_ctx, w)


def _odd_in_kernel(x_ref, mod_ref, gain_ref, w_ref, cos_ref, sin_ref, q_ref, k_ref, v_ref, sg_ref):
    h = _modulated(x_ref[0], gain_ref[...], mod_ref, 3, 4).astype(BF16)
    cos = cos_ref[...]
    sin = sin_ref[...]

    def rope_store(dst_ref, col0, scale):
        p = _bdot(h, w_ref[:, col0:col0 + RET_QK])
        for c in range(RET_QK // LANES):
            t = p[:, c * LANES:(c + 1) * LANES]
            half = (c % 2) * LANES
            r = t * cos[:, half:half + LANES] + pltpu.roll(t, LANES // 2, 1) * sin[:, half:half + LANES]
            dst_ref[0, :, c * LANES:(c + 1) * LANES] = (r * scale).astype(BF16)

    rope_store(q_ref, 0, 1.0)
    rope_store(k_ref, RET_QK, RET_KEY_DIM ** -0.5)
    v_ref[0] = _bdot(h, w_ref[:, 2 * RET_QK:2 * RET_QK + RET_V]).astype(BF16)
    sg_ref[0] = _silu(_bdot(h, w_ref[:, 2 * RET_QK + RET_V:])).astype(BF16)


def _odd_in_call(xt, mod_i, gain, w, cos_t, sin_t):
    batch = xt.shape[0]
    rope_spec = pl.BlockSpec((TM, RET_KEY_DIM), lambda b, j: (j, 0))
    return pl.pallas_call(
        _odd_in_kernel,
        grid=(batch, NT_ALL),
        in_specs=[_tok_spec(D_MODEL), _mod_spec(batch), _full_spec((1, D_MODEL)),
                  _full_spec((D_MODEL, w.shape[1])), rope_spec, rope_spec],
        out_specs=[_tok_spec(RET_QK), _tok_spec(RET_QK), _tok_spec(RET_V), _tok_spec(RET_V)],
        out_shape=[jax.ShapeDtypeStruct((batch, T_TOK, RET_QK), BF16)] * 2
        + [jax.ShapeDtypeStruct((batch, T_TOK, RET_V), BF16)] * 2,
        compiler_params=_cparams(2),
        name="odd_in",
    )(xt, mod_i, gain, w, cos_t, sin_t)


def _rope_tables():
    t = np.arange(SEQ)
    n_freq = RET_KEY_DIM // 4
    inv = ROPE_BASE ** (-jnp.arange(n_freq, dtype=F32) / n_freq)
    ang_r = jnp.asarray(t // GRID_W, F32)[:, None] * inv
    ang_c = jnp.asarray(t % GRID_W, F32)[:, None] * inv
    cr, sr, cc, sc = jnp.cos(ang_r), jnp.sin(ang_r), jnp.cos(ang_c), jnp.sin(ang_c)
    cos_l = jnp.concatenate([cr, cr, cc, cc], axis=-1)
    sin_l = jnp.concatenate([-sr, sr, -sc, sc], axis=-1)
    cos_t = jnp.concatenate([cos_l, jnp.ones((CTX_LEN, RET_KEY_DIM), F32)], axis=0)
    sin_t = jnp.concatenate([sin_l, jnp.zeros((CTX_LEN, RET_KEY_DIM), F32)], axis=0)
    return cos_t, sin_t


def _ret_kernel(lg_ref, q_ref, k_ref, v_ref, sg_ref, y_ref, o_acc, state_f, state_b):
    head = pl.program_id(1)
    n_c = RET_CHUNK
    n_lat = SEQ // n_c
    assert CTX_LEN == 2 * n_c and n_lat % 2 == 0
    ii = lax.broadcasted_iota(jnp.int32, (n_c, n_c), 0).astype(F32)
    jj = lax.broadcasted_iota(jnp.int32, (n_c, n_c), 1).astype(F32)
    row_k = lax.broadcasted_iota(jnp.int32, (n_c, RET_KEY_DIM), 0).astype(F32)
    row_v = lax.broadcasted_iota(jnp.int32, (n_c, RET_VAL_DIM), 0).astype(F32)

    def decays(backward):
        lg = lg_ref[1 if backward else 0, head]
        if backward:
            diff = jj - ii
            xi = jnp.exp(lg * (n_c - row_v))
            zeta = jnp.exp(lg * row_k)
        else:
            diff = ii - jj
            xi = jnp.exp(lg * (row_v + 1.0))
            zeta = jnp.exp(lg * (n_c - 1.0 - row_k))
        dmask = jnp.where(diff >= 0, jnp.exp(lg * jnp.maximum(diff, 0.0)), 0.0)
        g_chunk = jnp.exp(lg * n_c + jnp.zeros((1, RET_VAL_DIM), F32))
        return dmask, xi, zeta, g_chunk

    def advance(chunk, state, consts, finalize):
        dmask, xi, zeta, g_chunk = consts
        r = chunk * n_c if isinstance(chunk, int) else pl.multiple_of(chunk * n_c, n_c)
        qc = q_ref[0, pl.ds(r, n_c), :]
        kc = k_ref[0, pl.ds(r, n_c), :]
        vc = v_ref[0, pl.ds(r, n_c), :]
        st = state[...]
        inner = lax.dot_general(qc, kc, _NT_DIMS, preferred_element_type=F32) * dmask
        o = _bdot(inner.astype(BF16), vc) + _bdot(qc, st.astype(BF16)) * xi
        kz = (kc.astype(F32) * zeta).astype(BF16)
        state[...] = st * g_chunk + lax.dot_general(kz, vc, _TN_DIMS, preferred_element_type=F32)
        if finalize:
            tot = o_acc[pl.ds(r, n_c), :] + o
            mu = jnp.mean(tot, axis=-1, keepdims=True)
            cen = tot - mu
            var = jnp.mean(cen * cen, axis=-1, keepdims=True)
            yn = cen * lax.rsqrt(var + GN_EPS)
            y_ref[0, pl.ds(r, n_c), :] = (sg_ref[0, pl.ds(r, n_c), :].astype(F32) * yn).astype(BF16)
        else:
            o_acc[pl.ds(r, n_c), :] = o

    consts_f = decays(False)
    consts_b = decays(True)
    state_f[...] = jnp.zeros_like(state_f)
    state_b[...] = jnp.zeros_like(state_b)

    def both(chunk_f, chunk_b, finalize):
        advance(chunk_f, state_f, consts_f, finalize)
        advance(chunk_b, state_b, consts_b, finalize)

    both(n_lat, n_lat + 1, False)
    both(n_lat + 1, n_lat, True)

    def first_half(s, carry):
        both(s, n_lat - 1 - s, False)
        return carry

    def second_half(s, carry):
        both(s, n_lat - 1 - s, True)
        return carry

    lax.fori_loop(0, n_lat // 2, first_half, 0)
    lax.fori_loop(n_lat // 2, n_lat, second_half, 0)


def _ret_call(lg, q, k, v, sg):
    batch = q.shape[0]
    qk_spec = pl.BlockSpec((1, T_TOK, RET_KEY_DIM), lambda b, h: (b, 0, h))
    v_spec = pl.BlockSpec((1, T_TOK, RET_VAL_DIM), lambda b, h: (b, 0, h))
    return pl.pallas_call(
        _ret_kernel,
        grid=(batch, RET_HEADS),
        in_specs=[pl.BlockSpec(memory_space=pltpu.SMEM), qk_spec, qk_spec, v_spec, v_spec],
        out_specs=v_spec,
        out_shape=jax.ShapeDtypeStruct((batch, T_TOK, RET_V), BF16),
        scratch_shapes=[pltpu.VMEM((T_TOK, RET_VAL_DIM), F32),
                        pltpu.VMEM((RET_KEY_DIM, RET_VAL_DIM), F32),
                        pltpu.VMEM((RET_KEY_DIM, RET_VAL_DIM), F32)],
        compiler_params=_cparams(2),
        name="retention",
    )(lg, q, k, v, sg)


def _odd_out_kernel(x_ref, mod_ref, y_ref, w_ref, o_ref):
    o_ref[0] = x_ref[0] + mod_ref[0, 5:6, :] * _bdot(y_ref[0], w_ref[...])


def _odd_out_call(xt, mod_i, y, w):
    batch = xt.shape[0]
    return pl.pallas_call(
        _odd_out_kernel,
        grid=(batch, NT_ALL),
        in_specs=[_tok_spec(D_MODEL), _mod_spec(batch), _tok_spec(RET_V), _full_spec((RET_V, D_MODEL))],
        out_specs=_tok_spec(D_MODEL),
        out_shape=jax.ShapeDtypeStruct((batch, T_TOK, D_MODEL), F32),
        compiler_params=_cparams(2),
        name="odd_out",
    )(xt, mod_i, y, w)


def _fft_s2_fwd_kernel(a_ref, tw_ref, fhi_ref, flo_ref, g_ref):
    reps = HY_WIDTH // LANES
    for kk in range(FFT_KB):
        twr = jnp.concatenate([tw_ref[0, kk]] * reps, axis=1)
        twi = jnp.concatenate([tw_ref[1, kk]] * reps, axis=1)
        ar = a_ref[0, kk]
        ai = a_ref[1, kk]
        d = jnp.concatenate([ar * twr - ai * twi, ar * twi + ai * twr], axis=0)
        x = _dot3(fhi_ref[...], flo_ref[...], d)
        g_ref[0, kk] = x[:FFT_N2]
        g_ref[1, kk] = x[FFT_N2:]


def _dft_dense_kernel(x_ref, mhi_ref, mlo_ref, o_ref):
    o_ref[...] = _dot3(mhi_ref[...], mlo_ref[...], x_ref[...])


def _filter_spectra(filter_params):
    cst = _fft_constants()
    n_lane = FFT_N2 * HY_WIDTH
    filt = _hyena_filter(SEQ, *filter_params).reshape(1, 1, FFT_N1, n_lane)
    a = pl.pallas_call(
        _fft_s1_kernel,
        grid=(1, n_lane // FFT_W),
        in_specs=[pl.BlockSpec((1, 1, FFT_N1, FFT_W), lambda p, w: (p, 0, 0, w)),
                  _full_spec((2 * FFT_N1, FFT_N1)), _full_spec((2 * FFT_N1, FFT_N1))],
        out_specs=pl.BlockSpec((1, 2 * FFT_N1, FFT_W), lambda p, w: (p, 0, w)),
        out_shape=jax.ShapeDtypeStruct((1, 2 * FFT_N1, n_lane), F32),
        compiler_params=_cparams(2),
        name="filt_fft_s1",
    )(filt, *cst["m1r"])
    a = a.reshape(2, FFT_N1, FFT_N2, HY_WIDTH)
    blk = pl.BlockSpec((2, FFT_KB, FFT_N2, HY_WIDTH), lambda kb: (0, kb, 0, 0))
    sq = (2 * FFT_N2, 2 * FFT_N2)
    g_lat = pl.pallas_call(
        _fft_s2_fwd_kernel,
        grid=(FFT_N1 // FFT_KB,),
        in_specs=[blk, pl.BlockSpec((2, FFT_KB, FFT_N2, LANES), lambda kb: (0, kb, 0, 0)),
                  pl.BlockSpec(sq, lambda kb: (0, 0)), pl.BlockSpec(sq, lambda kb: (0, 0))],
        out_specs=blk,
        out_shape=jax.ShapeDtypeStruct((2, FFT_N1, FFT_N2, HY_WIDTH), F32),
        compiler_params=_cparams(1),
        name="filt_fft_s2",
    )(a, cst["tw"], *cst["m2"])
    filt_ctx = _hyena_filter(CTX_LEN, *filter_params)
    g_ctx = pl.pallas_call(
        _dft_dense_kernel,
        out_shape=jax.ShapeDtypeStruct((2 * FFT_NC, HY_WIDTH), F32),
        compiler_params=pltpu.CompilerParams(vmem_limit_bytes=VMEM_LIMIT_BYTES),
        name="filt_dft_ctx",
    )(filt_ctx, *cst["mcf_full"])
    return g_lat, g_ctx.reshape(2, FFT_NC, HY_WIDTH)


def _even_mixer(xt, mod_i, gain, w_in, w_out, q_gain, k_gain, rpb, conv_w, conv_b, filter_params, d_bias):
    scale = NA_HEAD_DIM ** -0.5
    qg = (jnp.tile(q_gain, NA_HEADS) * scale).reshape(1, NA_WIDTH)
    kg = jnp.tile(k_gain, NA_HEADS).reshape(1, NA_WIDTH)
    head_of = np.arange(NA_WIDTH) // NA_HEAD_DIM
    bd = jnp.asarray((head_of[:, None] == head_of[None, :]) / NA_HEAD_DIM, BF16)
    q, k, v, u = _even_in_call(xt, mod_i, gain, w_in.astype(BF16), qg, kg, bd)
    a = _na_call(q, k, v, _na_bias_table(rpb))
    z_lat, x0_lat, z_ctx, x0_ctx = _hy_pre_call(u, conv_w, conv_b.reshape(1, -1))
    g_lat, g_ctx = _filter_spectra(filter_params)
    y_lat = _hy_long_conv_call(z_lat, x0_lat, g_lat, d_bias)
    y_ctx = _hy_ctx_conv_call(z_ctx, x0_ctx, g_ctx, d_bias)
    return _even_out_call(xt, mod_i, a, y_lat, y_ctx, w_out.astype(BF16))


def _odd_mixer(xt, mod_i, gain, w_in, w_out, logit_f, logit_b, rope):
    q, k, v, sg = _odd_in_call(xt, mod_i, gain, w_in.astype(BF16), *rope)
    lg = jnp.stack([jax.nn.log_sigmoid(logit_f.astype(F32)), jax.nn.log_sigmoid(logit_b.astype(F32))])
    y = _ret_call(lg, q, k, v, sg)
    return _odd_out_call(xt, mod_i, y, w_out.astype(BF16))


def kernel(x, c, ctx, c_ctx, w_mod, b_mod, norm_gain, ffn_a_in, ffn_a_out, ffn_b_in, ffn_b_out,
           even_in, even_out, na_q_gain, na_k_gain, na_rpb, hy_conv_w, hy_conv_b,
           hy_fw1, hy_fb1, hy_fw2, hy_fb2, hy_fw3, hy_fb3, hy_fw4, hy_freq, hy_bias,
           ret_in, ret_out, ret_logit_f, ret_logit_b):
    batch = x.shape[0]
    assert x.shape == (batch, SEQ, D_MODEL) and ctx.shape == (batch, CTX_LEN, D_MODEL)
    assert batch % 2 == 0 and batch < MOD_ROWS
    xt = jnp.concatenate([x, ctx], axis=1)
    c_rows = jnp.concatenate([c, c_ctx[None], jnp.zeros((MOD_ROWS - batch - 1, D_MODEL), F32)], axis=0)
    mod_all = _mod_call(c_rows, w_mod, b_mod).reshape(DEPTH, MOD_ROWS, N_MOD, D_MODEL)
    rope = _rope_tables()
    for i in range(DEPTH):
        last = i == DEPTH - 1
        mod_i = mod_all[i]
        gains = norm_gain[i].reshape(3, 1, D_MODEL)
        xt = _ffn_call(xt, mod_i, gains[0], ffn_a_in[i].astype(BF16), ffn_a_out[i].astype(BF16),
                       (0, 1, 2), NT_ALL)
        if i % 2 == 0:
            e = i // 2
            filter_params = (hy_fw1[e], hy_fb1[e], hy_fw2[e], hy_fb2[e], hy_fw3[e], hy_fb3[e],
                             hy_fw4[e], hy_freq[e])
            xt = _even_mixer(xt, mod_i, gains[1], even_in[e], even_out[e], na_q_gain[e], na_k_gain[e],
                             na_rpb[e], hy_conv_w[e], hy_conv_b[e], filter_params, hy_bias[e])
        else:
            o = i // 2
            xt = _odd_mixer(xt, mod_i, gains[1], ret_in[o], ret_out[o], ret_logit_f[o], ret_logit_b[o], rope)
        xt = _ffn_call(xt, mod_i, gains[2], ffn_b_in[i].astype(BF16), ffn_b_out[i].astype(BF16),
                       (6, 7, 8), NT_LAT if last else NT_ALL)
    return xt
```

```python
import functools
import math

import ml_dtypes
import numpy as np
import jax
import jax.numpy as jnp
from jax import lax
from jax.experimental import pallas as pl
from jax.experimental.pallas import tpu as pltpu

F32 = jnp.float32
BF16 = jnp.bfloat16

D_MODEL = 1024
SEQ = 4096
DEPTH = 4
GRID_W = 64
CTX_LEN = 256
N_MOD = 9
RMS_EPS = 1e-6
GN_EPS = 1e-6
D_FF = 2816
NA_HEADS = 8
NA_HEAD_DIM = 64
NA_WIDTH = NA_HEADS * NA_HEAD_DIM
WIN_R = 8
WIN_C = 16
HY_WIDTH = D_MODEL - NA_WIDTH
HY_BANDS = 8
HY_TARGET = 1e-2
HY_FAST_PCT = 0.3
HY_SLOW_PCT = 1.5
RET_HEADS = 4
RET_KEY_DIM = D_MODEL // RET_HEADS
RET_VAL_DIM = 2 * RET_KEY_DIM
RET_QK = RET_HEADS * RET_KEY_DIM
RET_V = RET_HEADS * RET_VAL_DIM
ROPE_BASE = 10000.0

LANES = 128
VMEM_LIMIT_BYTES = 56 * 2**20
T_TOK = SEQ + CTX_LEN
TM = CTX_LEN
NT_LAT = SEQ // TM
NT_ALL = T_TOK // TM
MOD_ROWS = 16
FF_CHUNKS = ((0, 1536), (1536, 1280))
RET_BLOCK = 256

NA_RB = 4
NA_KR = NA_RB + WIN_R - 1
GRID_H = SEQ // GRID_W
NA_NQ = NA_RB * GRID_W
NA_NK = NA_KR * GRID_W
NEG_BIG = -1e30

FFT_N = 2 * SEQ
FFT_N1 = 64
FFT_N2 = 128
FFT_N1H = FFT_N1 // 2
FFT_W = 2048
FFT_KB = 4
FFT_NC = 2 * CTX_LEN


def _cparams(n_axes):
    return pltpu.CompilerParams(dimension_semantics=("arbitrary",) * n_axes,
                                vmem_limit_bytes=VMEM_LIMIT_BYTES)


def _bdot(a, b):
    return jnp.dot(a, b, preferred_element_type=F32)


_NT_DIMS = (((1,), (1,)), ((), ()))
_TN_DIMS = (((0,), (0,)), ((), ()))


def _split_hi_lo(m):
    hi = m.astype(BF16)
    lo = (m - hi.astype(F32)).astype(BF16)
    return hi, lo


def _dot3(m_hi, m_lo, d):
    d_hi, d_lo = _split_hi_lo(d)
    return _bdot(m_hi, d_hi) + _bdot(m_lo, d_hi) + _bdot(m_hi, d_lo)


def _modulated(x, gain, mod_ref, shift_row, scale_row):
    ms = jnp.mean(x * x, axis=-1, keepdims=True)
    y = x * lax.rsqrt(ms + RMS_EPS) * gain
    return (y * (1.0 + mod_ref[0, scale_row:scale_row + 1, :])
            + mod_ref[0, shift_row:shift_row + 1, :])


def _silu(a):
    return a * jax.nn.sigmoid(a)


def _mod_kernel(c_ref, w_ref, b_ref, o_ref):
    s = _silu(c_ref[...])
    o_ref[0] = jnp.dot(s, w_ref[0], precision=lax.Precision.HIGHEST,
                       preferred_element_type=F32) + b_ref[0]


def _mod_call(c_rows, w_mod, b_mod):
    depth, d, n = w_mod.shape
    tn = 1024
    return pl.pallas_call(
        _mod_kernel,
        grid=(depth, n // tn),
        in_specs=[pl.BlockSpec((MOD_ROWS, d), lambda i, j: (0, 0)),
                  pl.BlockSpec((1, d, tn), lambda i, j: (i, 0, j)),
                  pl.BlockSpec((1, 1, tn), lambda i, j: (i, 0, j))],
        out_specs=pl.BlockSpec((1, MOD_ROWS, tn), lambda i, j: (i, 0, j)),
        out_shape=jax.ShapeDtypeStruct((depth, MOD_ROWS, n), F32),
        compiler_params=_cparams(2),
        name="mod",
    )(c_rows, w_mod, b_mod.reshape(depth, 1, n))


def _tok_spec(width):
    return pl.BlockSpec((1, TM, width), lambda b, j: (b, j, 0))


def _mod_spec(batch):
    return pl.BlockSpec((1, N_MOD, D_MODEL), lambda b, j: (jnp.where(j == NT_LAT, batch, b), 0, 0))


def _full_spec(shape):
    zeros = (0,) * len(shape)
    return pl.BlockSpec(shape, lambda b, j: zeros)


def _ffn_kernel(x_ref, mod_ref, gain_ref, win_ref, wout_ref, o_ref, *, rows):
    shift_row, scale_row, gate_row = rows
    x = x_ref[0]
    h = _modulated(x, gain_ref[...], mod_ref, shift_row, scale_row).astype(BF16)
    acc = jnp.zeros((TM, D_MODEL), F32)
    for start, size in FF_CHUNKS:
        a = _bdot(h, win_ref[:, start:start + size])
        b = _bdot(h, win_ref[:, D_FF + start:D_FF + start + size])
        g = (_silu(a) * b).astype(BF16)
        acc = acc + _bdot(g, wout_ref[start:start + size, :])
    o_ref[0] = x + (0.5 * mod_ref[0, gate_row:gate_row + 1, :]) * acc


def _ffn_call(xt, mod_i, gain, w_in, w_out, rows, n_tiles):
    batch = xt.shape[0]
    return pl.pallas_call(
        functools.partial(_ffn_kernel, rows=rows),
        grid=(batch, n_tiles),
        in_specs=[_tok_spec(D_MODEL), _mod_spec(batch), _full_spec((1, D_MODEL)),
                  _full_spec((D_MODEL, 2 * D_FF)), _full_spec((D_FF, D_MODEL))],
        out_specs=_tok_spec(D_MODEL),
        out_shape=jax.ShapeDtypeStruct((batch, n_tiles * TM, D_MODEL), F32),
        compiler_params=_cparams(2),
        name="ffn",
    )(xt, mod_i, gain, w_in, w_out)


def _even_in_kernel(x_ref, mod_ref, gain_ref, w_ref, qg_ref, kg_ref, bd_ref,
                    q_ref, k_ref, v_ref, u_ref):
    h = _modulated(x_ref[0], gain_ref[...], mod_ref, 3, 4).astype(BF16)

    def head_norm(t, g):
        ms = _bdot((t * t).astype(BF16), bd_ref[...])
        return (t * lax.rsqrt(ms + RMS_EPS) * g).astype(BF16)

    q_ref[0] = head_norm(_bdot(h, w_ref[:, 0:NA_WIDTH]), qg_ref[...])
    k_ref[0] = head_norm(_bdot(h, w_ref[:, NA_WIDTH:2 * NA_WIDTH]), kg_ref[...])
    v_ref[0] = _bdot(h, w_ref[:, 2 * NA_WIDTH:3 * NA_WIDTH]).astype(BF16)
    u_ref[0] = _bdot(h, w_ref[:, 3 * NA_WIDTH:])


def _even_in_call(xt, mod_i, gain, w, q_gain, k_gain, bd):
    batch = xt.shape[0]
    n_in = w.shape[1]
    return pl.pallas_call(
        _even_in_kernel,
        grid=(batch, NT_ALL),
        in_specs=[_tok_spec(D_MODEL), _mod_spec(batch), _full_spec((1, D_MODEL)),
                  _full_spec((D_MODEL, n_in)), _full_spec((1, NA_WIDTH)), _full_spec((1, NA_WIDTH)),
                  _full_spec((NA_WIDTH, NA_WIDTH))],
        out_specs=[_tok_spec(NA_WIDTH), _tok_spec(NA_WIDTH), _tok_spec(NA_WIDTH),
                   _tok_spec(3 * HY_WIDTH)],
        out_shape=[jax.ShapeDtypeStruct((batch, T_TOK, NA_WIDTH), BF16)] * 3
        + [jax.ShapeDtypeStruct((batch, T_TOK, 3 * HY_WIDTH), F32)],
        compiler_params=_cparams(2),
        name="even_in",
    )(xt, mod_i, gain, w, q_gain, k_gain, bd)


def _na_kernel(q_ref, k_ref, v_ref, bias_ref, o_ref):
    lane = lax.broadcasted_iota(jnp.int32, (1, LANES), 1)
    first_head = lane < NA_HEAD_DIM
    k_ctx = k_ref[0, SEQ:T_TOK, :]
    v_ctx = v_ref[0, SEQ:T_TOK, :]

    def stack_heads(q):
        zero = jnp.zeros_like(q)
        return jnp.concatenate([jnp.where(first_head, q, zero), jnp.where(first_head, zero, q)], axis=0)

    def attend(qs, scores_and_values):
        s_list = [lax.dot_general(qs, kk, _NT_DIMS, preferred_element_type=F32) if bias is None
                  else lax.dot_general(qs, kk, _NT_DIMS, preferred_element_type=F32) + bias
                  for kk, _, bias in scores_and_values]
        m = functools.reduce(jnp.maximum, [jnp.max(s, axis=-1, keepdims=True) for s in s_list])
        p_list = [jnp.exp(s - m) for s in s_list]
        denom = functools.reduce(jnp.add, [jnp.sum(p, axis=-1, keepdims=True) for p in p_list])
        o = functools.reduce(jnp.add, [_bdot(p.astype(BF16), vv)
                                       for p, (_, vv, _) in zip(p_list, scores_and_values)])
        o = o / denom
        n = qs.shape[0] // 2
        return jnp.where(first_head, o[:n], o[n:]).astype(BF16)

    def block(i, carry):
        r0 = i * NA_RB
        u0 = jnp.clip(r0 - WIN_R // 2, 0, GRID_H - NA_KR)
        pattern = jnp.where(i == 0, 0, jnp.where(i == GRID_H // NA_RB - 1, 2, 1))
        q0 = pl.multiple_of(r0 * GRID_W, NA_NQ)
        k0 = pl.multiple_of(u0 * GRID_W, GRID_W)
        qs = stack_heads(q_ref[0, pl.ds(q0, NA_NQ), :])
        k_win = k_ref[0, pl.ds(k0, NA_NK), :]
        v_win = v_ref[0, pl.ds(k0, NA_NK), :]
        o_ref[0, pl.ds(q0, NA_NQ), :] = attend(
            qs, [(k_win, v_win, bias_ref[pattern, 0]), (k_ctx, v_ctx, None)])
        return carry

    lax.fori_loop(0, GRID_H // NA_RB, block, 0, unroll=2)
    o_ref[0, SEQ:T_TOK, :] = attend(stack_heads(q_ref[0, SEQ:T_TOK, :]), [(k_ctx, v_ctx, None)])


def _na_call(q, k, v, bias):
    batch = q.shape[0]
    n_pairs = NA_WIDTH // LANES
    spec = pl.BlockSpec((1, T_TOK, LANES), lambda b, p: (b, 0, p))
    return pl.pallas_call(
        _na_kernel,
        grid=(batch, n_pairs),
        in_specs=[spec, spec, spec,
                  pl.BlockSpec((3, 1, 2 * NA_NQ, NA_NK), lambda b, p: (0, p, 0, 0))],
        out_specs=spec,
        out_shape=jax.ShapeDtypeStruct((batch, T_TOK, NA_WIDTH), BF16),
        compiler_params=_cparams(2),
        name="na_attn",
    )(q, k, v, bias)


def _na_bias_table(rpb):
    j = np.arange(NA_RB)[:, None, None, None]
    c = np.arange(GRID_W)[None, :, None, None]
    kk = np.arange(NA_KR)[None, None, :, None]
    kc = np.arange(GRID_W)[None, None, None, :]
    cs = np.clip(c - WIN_C // 2, 0, GRID_W - WIN_C)
    col_ok = (kc >= cs) & (kc < cs + WIN_C)
    dc = np.clip(kc - c + WIN_C - 1, 0, 2 * WIN_C - 2)[0, :, 0, :]
    col_sel = (dc[..., None] == np.arange(2 * WIN_C - 1)).astype(np.float32)
    shape = (NA_RB, GRID_W, NA_KR, GRID_W)
    oks, row_sels = [], []
    for off, rs_rel in ((0, 0 * j), (-(WIN_R // 2), j), (-(NA_KR - NA_RB), NA_KR - WIN_R + 0 * j)):
        row_ok = (kk >= rs_rel) & (kk < rs_rel + WIN_R)
        dr = np.clip(off + kk - j + WIN_R - 1, 0, 2 * WIN_R - 2)[:, 0, :, 0]
        row_sels.append((dr[..., None] == np.arange(2 * WIN_R - 1)).astype(np.float32))
        oks.append(np.broadcast_to(row_ok & col_ok, shape))
    vals = jnp.einsum("pjkr,hrc,qmc->phjqkm", np.stack(row_sels), rpb.astype(F32), col_sel,
                      precision=lax.Precision.HIGHEST)
    table = jnp.where(np.stack(oks)[:, None], vals, NEG_BIG)
    return table.reshape(3, NA_HEADS // 2, 2 * NA_NQ, NA_NK)


def _hy_pre_kernel(u0_ref, u1_ref, uv_ref, w0_ref, w1_ref, wv_ref, b0_ref, b1_ref, bv_ref,
                   zl_ref, xl_ref, zc_ref, xc_ref):
    row = lax.broadcasted_iota(jnp.int32, (TM, LANES), 0)
    zero_row = jnp.zeros((1, LANES), F32)

    def conv(u_ref, w_ref, b_ref, tile):
        s = tile * TM
        cur = u_ref[0, s:s + TM, :]
        seq_start = tile in (0, NT_LAT)
        seq_end = tile in (NT_LAT - 1, NT_LAT)
        prev_row = zero_row if seq_start else u_ref[0, s - 1:s, :]
        next_row = zero_row if seq_end else u_ref[0, s + TM:s + TM + 1, :]
        before = jnp.where(row == 0, prev_row, pltpu.roll(cur, 1, 0))
        after = jnp.where(row == TM - 1, next_row, pltpu.roll(cur, TM - 1, 0))
        return before * w_ref[0:1, :] + cur * w_ref[1:2, :] + after * w_ref[2:3, :] + b_ref[...]

    for tile in range(NT_ALL):
        x0 = conv(u0_ref, w0_ref, b0_ref, tile)
        z = conv(uv_ref, wv_ref, bv_ref, tile) * conv(u1_ref, w1_ref, b1_ref, tile)
        if tile < NT_LAT:
            zl_ref[0, tile * TM:(tile + 1) * TM, :] = z
            xl_ref[0, tile * TM:(tile + 1) * TM, :] = x0
        else:
            zc_ref[0] = z
            xc_ref[0] = x0


def _hy_pre_call(u, conv_w, conv_b):
    batch = u.shape[0]
    nb = HY_WIDTH // LANES
    u_specs = [pl.BlockSpec((1, T_TOK, LANES), lambda b, cb, g=g: (b, 0, g * nb + cb)) for g in range(3)]
    w_specs = [pl.BlockSpec((3, LANES), lambda b, cb, g=g: (0, g * nb + cb)) for g in range(3)]
    b_specs = [pl.BlockSpec((1, LANES), lambda b, cb, g=g: (0, g * nb + cb)) for g in range(3)]
    lat = pl.BlockSpec((1, SEQ, LANES), lambda b, cb: (b, 0, cb))
    ctx = pl.BlockSpec((1, CTX_LEN, LANES), lambda b, cb: (b, 0, cb))
    return pl.pallas_call(
        _hy_pre_kernel,
        grid=(batch, nb),
        in_specs=u_specs + w_specs + b_specs,
        out_specs=[lat, lat, ctx, ctx],
        out_shape=[jax.ShapeDtypeStruct((batch, SEQ, HY_WIDTH), F32)] * 2
        + [jax.ShapeDtypeStruct((batch, CTX_LEN, HY_WIDTH), F32)] * 2,
        compiler_params=_cparams(2),
        name="hy_pre",
    )(u, u, u, conv_w, conv_w, conv_w, conv_b, conv_b, conv_b)


def _complex_block(re, im):
    return np.block([[re, -im], [im, re]])


@functools.lru_cache(maxsize=None)
def _fft_constants():
    k1 = np.arange(FFT_N1)
    n1 = np.arange(FFT_N1H)
    f1 = np.exp(-2j * np.pi * np.outer(k1, n1) / FFT_N1)
    m1 = _complex_block(f1.real, f1.imag)
    f1_full = np.exp(-2j * np.pi * np.outer(k1, k1) / FFT_N1)
    m1r = np.concatenate([f1_full.real, f1_full.imag], axis=0)
    n2 = np.arange(FFT_N2)
    tw = np.exp(-2j * np.pi * np.outer(k1, n2) / FFT_N)
    tw = np.stack([tw.real, tw.imag])[..., None] * np.ones((1, 1, 1, LANES))
    f2 = np.exp(-2j * np.pi * np.outer(n2, n2) / FFT_N2)
    m2 = _complex_block(f2.real, f2.imag)
    m2i = _complex_block(f2.real, -f2.imag)
    c1 = np.exp(2j * np.pi * np.outer(n1, k1) / FFT_N1) / FFT_N
    m3 = _complex_block(c1.real, c1.imag)
    kc = np.arange(FFT_NC)
    nc = np.arange(CTX_LEN)
    ang = 2 * np.pi * np.outer(kc, nc) / FFT_NC
    mcf = np.concatenate([np.cos(ang), -np.sin(ang)], axis=0)
    ang_full = 2 * np.pi * np.outer(kc, kc) / FFT_NC
    mcf_full = np.concatenate([np.cos(ang_full), -np.sin(ang_full)], axis=0)
    mci = np.concatenate([np.cos(ang.T), -np.sin(ang.T)], axis=1) / FFT_NC

    def hi_lo(m):
        hi = m.astype(ml_dtypes.bfloat16)
        lo = (m - hi.astype(np.float64)).astype(ml_dtypes.bfloat16)
        return hi, lo

    return dict(m1=hi_lo(m1), m1r=hi_lo(m1r), m2=hi_lo(m2), m2i=hi_lo(m2i), m3=hi_lo(m3),
                mcf=hi_lo(mcf), mcf_full=hi_lo(mcf_full), mci=hi_lo(mci), tw=tw.astype(np.float32))


def _fft_s1_kernel(z_ref, mhi_ref, mlo_ref, a_ref):
    d = z_ref[0].reshape(-1, FFT_W)
    a_ref[0] = _dot3(mhi_ref[...], mlo_ref[...], d)


def _fft_s2_kernel(a_ref, tw_ref, g_ref, fhi_ref, flo_ref, ihi_ref, ilo_ref, o_ref):
    reps = HY_WIDTH // LANES
    for kk in range(FFT_KB):
        twr = jnp.concatenate([tw_ref[0, kk]] * reps, axis=1)
        twi = jnp.concatenate([tw_ref[1, kk]] * reps, axis=1)
        ar = a_ref[0, 0, kk]
        ai = a_ref[0, 1, kk]
        d = jnp.concatenate([ar * twr - ai * twi, ar * twi + ai * twr], axis=0)
        x = _dot3(fhi_ref[...], flo_ref[...], d)
        xr, xi = x[:FFT_N2], x[FFT_N2:]
        gr, gi = g_ref[0, kk], g_ref[1, kk]
        y = jnp.concatenate([xr * gr - xi * gi, xr * gi + xi * gr], axis=0)
        b = _dot3(ihi_ref[...], ilo_ref[...], y)
        br, bi = b[:FFT_N2], b[FFT_N2:]
        o_ref[0, 0, kk] = br * twr + bi * twi
        o_ref[0, 1, kk] = bi * twr - br * twi


def _fft_s3_kernel(b_ref, mhi_ref, mlo_ref, z_ref, x0_ref, bias_ref, y_ref):
    y = _dot3(mhi_ref[...], mlo_ref[...], b_ref[0]).reshape(2, FFT_N1H, FFT_W)
    y_ref[0] = ((y + z_ref[0] * bias_ref[...]) * x0_ref[0]).astype(BF16)


def _hy_long_conv_call(z_lat, x0_lat, g_spec, d_bias):
    batch = z_lat.shape[0]
    pairs = batch // 2
    cst = _fft_constants()
    n_lane = FFT_N2 * HY_WIDTH
    zv = z_lat.reshape(pairs, 2, FFT_N1H, n_lane)
    xv = x0_lat.reshape(pairs, 2, FFT_N1H, n_lane)
    pair_spec = pl.BlockSpec((1, 2, FFT_N1H, FFT_W), lambda p, w: (p, 0, 0, w))
    a = pl.pallas_call(
        _fft_s1_kernel,
        grid=(pairs, n_lane // FFT_W),
        in_specs=[pair_spec, _full_spec((2 * FFT_N1, 2 * FFT_N1H)), _full_spec((2 * FFT_N1, 2 * FFT_N1H))],
        out_specs=pl.BlockSpec((1, 2 * FFT_N1, FFT_W), lambda p, w: (p, 0, w)),
        out_shape=jax.ShapeDtypeStruct((pairs, 2 * FFT_N1, n_lane), F32),
        compiler_params=_cparams(2),
        name="hy_fft_s1",
    )(zv, *cst["m1"])
    a = a.reshape(pairs, 2, FFT_N1, FFT_N2, HY_WIDTH)
    blk = pl.BlockSpec((1, 2, FFT_KB, FFT_N2, HY_WIDTH), lambda kb, p: (p, 0, kb, 0, 0))
    sq = (2 * FFT_N2, 2 * FFT_N2)
    b = pl.pallas_call(
        _fft_s2_kernel,
        grid=(FFT_N1 // FFT_KB, pairs),
        in_specs=[blk,
                  pl.BlockSpec((2, FFT_KB, FFT_N2, LANES), lambda kb, p: (0, kb, 0, 0)),
                  pl.BlockSpec((2, FFT_KB, FFT_N2, HY_WIDTH), lambda kb, p: (0, kb, 0, 0)),
                  _full_spec(sq), _full_spec(sq), _full_spec(sq), _full_spec(sq)],
        out_specs=blk,
        out_shape=jax.ShapeDtypeStruct((pairs, 2, FFT_N1, FFT_N2, HY_WIDTH), F32),
        compiler_params=_cparams(2),
        name="hy_fft_s2",
    )(a, cst["tw"], g_spec, *cst["m2"], *cst["m2i"])
    b = b.reshape(pairs, 2 * FFT_N1, n_lane)
    bias = jnp.tile(d_bias.reshape(1, HY_WIDTH), (1, FFT_W // HY_WIDTH))
    y = pl.pallas_call(
        _fft_s3_kernel,
        grid=(pairs, n_lane // FFT_W),
        in_specs=[pl.BlockSpec((1, 2 * FFT_N1, FFT_W), lambda p, w: (p, 0, w)),
                  _full_spec((2 * FFT_N1H, 2 * FFT_N1)), _full_spec((2 * FFT_N1H, 2 * FFT_N1)),
                  pair_spec, pair_spec, _full_spec((1, FFT_W))],
        out_specs=pair_spec,
        out_shape=jax.ShapeDtypeStruct((pairs, 2, FFT_N1H, n_lane), BF16),
        compiler_params=_cparams(2),
        name="hy_fft_s3",
    )(b, *cst["m3"], zv, xv, bias)
    return y.reshape(batch, SEQ, HY_WIDTH)


def _hy_ctx_kernel(z_ref, x0_ref, g_ref, fhi_ref, flo_ref, ihi_ref, ilo_ref, bias_ref, y_ref):
    z = z_ref[0]
    x = _dot3(fhi_ref[...], flo_ref[...], z)
    xr, xi = x[:FFT_NC], x[FFT_NC:]
    gr, gi = g_ref[0], g_ref[1]
    y = jnp.concatenate([xr * gr - xi * gi, xr * gi + xi * gr], axis=0)
    conv = _dot3(ihi_ref[...], ilo_ref[...], y)
    y_ref[0] = ((conv + z * bias_ref[...]) * x0_ref[0]).astype(BF16)


def _hy_ctx_conv_call(z_ctx, x0_ctx, g_spec, d_bias):
    batch = z_ctx.shape[0]
    cst = _fft_constants()
    tok = pl.BlockSpec((1, CTX_LEN, HY_WIDTH), lambda b: (b, 0, 0))

    def full(shape):
        zeros = (0,) * len(shape)
        return pl.BlockSpec(shape, lambda b: zeros)

    return pl.pallas_call(
        _hy_ctx_kernel,
        grid=(batch,),
        in_specs=[tok, tok, full((2, FFT_NC, HY_WIDTH)),
                  full((2 * FFT_NC, CTX_LEN)), full((2 * FFT_NC, CTX_LEN)),
                  full((CTX_LEN, 2 * FFT_NC)), full((CTX_LEN, 2 * FFT_NC)), full((1, HY_WIDTH))],
        out_specs=tok,
        out_shape=jax.ShapeDtypeStruct((batch, CTX_LEN, HY_WIDTH), BF16),
        compiler_params=_cparams(1),
        name="hy_ctx",
    )(z_ctx, x0_ctx, g_spec, *cst["mcf"], *cst["mci"], d_bias.reshape(1, HY_WIDTH))


def _hyena_filter(length, fw1, fb1, fw2, fb2, fw3, fb3, fw4, freq):
    t = jnp.linspace(0.0, 1.0, length, dtype=F32)[:, None]
    w = 2.0 * math.pi * jnp.arange(length, dtype=F32)[:, None] / length
    bands = jnp.linspace(1e-4, HY_BANDS - 1, HY_BANDS, dtype=F32)
    emb = jnp.concatenate([t, jnp.cos(bands * w), -jnp.sin(bands * w)], axis=-1)
    emb = jnp.concatenate([emb, emb[::-1]], axis=0)
    hp = lax.Precision.HIGHEST
    h = jnp.sin(freq * (jnp.dot(emb, fw1, precision=hp) + fb1))
    h = jnp.sin(freq * (jnp.dot(h, fw2, precision=hp) + fb2))
    h = jnp.sin(freq * (jnp.dot(h, fw3, precision=hp) + fb3))
    max_decay = math.log(HY_TARGET) / HY_FAST_PCT
    min_decay = math.log(HY_TARGET) / HY_SLOW_PCT
    deltas = jnp.linspace(min_decay, max_decay, HY_WIDTH, dtype=F32)
    h_fwd = jnp.dot(h[:length], fw4[:, :HY_WIDTH], precision=hp).astype(F32) * jnp.exp(-t * jnp.abs(deltas))
    h_bwd_rev = (jnp.dot(h[length:], fw4[:, HY_WIDTH:], precision=hp).astype(F32)
                 * jnp.exp(-t[::-1] * jnp.abs(deltas)))
    return jnp.concatenate([h_fwd[:1] + h_bwd_rev[length - 1:], h_fwd[1:],
                            jnp.zeros((1, HY_WIDTH), F32), h_bwd_rev[:length - 1]], axis=0)


def _even_out_kernel(x_ref, mod_ref, a_ref, yl_ref, yc_ref, w_ref, o_ref):
    is_ctx = pl.program_id(1) == NT_LAT
    y = jnp.where(is_ctx, yc_ref[0], yl_ref[0])
    o = _bdot(a_ref[0], w_ref[0:NA_WIDTH, :]) + _bdot(y, w_ref[NA_WIDTH:, :])
    o_ref[0] = x_ref[0] + mod_ref[0, 5:6, :] * o


def _even_out_call(xt, mod_i, a, y_lat, y_ctx, w):
    batch = xt.shape[0]
    return pl.pallas_call(
        _even_out_kernel,
        grid=(batch, NT_ALL),
        in_specs=[_tok_spec(D_MODEL), _mod_spec(batch), _tok_spec(NA_WIDTH),
                  pl.BlockSpec((1, TM, HY_WIDTH), lambda b, j: (b, jnp.minimum(j, NT_LAT - 1), 0)),
                  pl.BlockSpec((1, CTX_LEN, HY_WIDTH), lambda b, j: (b, 0, 0)),
                  _full_spec((D_MODEL, D_MODEL))],
        out_specs=_tok_spec(D_MODEL),
        out_shape=jax.ShapeDtypeStruct((batch, T_TOK, D_MODEL), F32),
        compiler_params=_cparams(2),
        name="even_out",
    )(xt, mod_i, a, y_lat, y_ctx, w)


def _odd_in_kernel(x_ref, mod_ref, gain_ref, w_ref, cos_ref, sin_ref, q_ref, k_ref, v_ref, sg_ref):
    h = _modulated(x_ref[0], gain_ref[...], mod_ref, 3, 4).astype(BF16)
    cos = cos_ref[...]
    sin = sin_ref[...]

    def rope_store(dst_ref, col0, scale):
        p = _bdot(h, w_ref[:, col0:col0 + RET_QK])
        for c in range(RET_QK // LANES):
            t = p[:, c * LANES:(c + 1) * LANES]
            half = (c % 2) * LANES
            r = t * cos[:, half:half + LANES] + pltpu.roll(t, LANES // 2, 1) * sin[:, half:half + LANES]
            dst_ref[0, :, c * LANES:(c + 1) * LANES] = (r * scale).astype(BF16)

    rope_store(q_ref, 0, 1.0)
    rope_store(k_ref, RET_QK, RET_KEY_DIM ** -0.5)
    v_ref[0] = _bdot(h, w_ref[:, 2 * RET_QK:2 * RET_QK + RET_V]).astype(BF16)
    sg_ref[0] = _silu(_bdot(h, w_ref[:, 2 * RET_QK + RET_V:])).astype(BF16)


def _odd_in_call(xt, mod_i, gain, w, cos_t, sin_t):
    batch = xt.shape[0]
    rope_spec = pl.BlockSpec((TM, RET_KEY_DIM), lambda b, j: (j, 0))
    return pl.pallas_call(
        _odd_in_kernel,
        grid=(batch, NT_ALL),
        in_specs=[_tok_spec(D_MODEL), _mod_spec(batch), _full_spec((1, D_MODEL)),
                  _full_spec((D_MODEL, w.shape[1])), rope_spec, rope_spec],
        out_specs=[_tok_spec(RET_QK), _tok_spec(RET_QK), _tok_spec(RET_V), _tok_spec(RET_V)],
        out_shape=[jax.ShapeDtypeStruct((batch, T_TOK, RET_QK), BF16)] * 2
        + [jax.ShapeDtypeStruct((batch, T_TOK, RET_V), BF16)] * 2,
        compiler_params=_cparams(2),
        name="odd_in",
    )(xt, mod_i, gain, w, cos_t, sin_t)


def _rope_tables():
    t = np.arange(SEQ)
    n_freq = RET_KEY_DIM // 4
    inv = ROPE_BASE ** (-jnp.arange(n_freq, dtype=F32) / n_freq)
    ang_r = jnp.asarray(t // GRID_W, F32)[:, None] * inv
    ang_c = jnp.asarray(t % GRID_W, F32)[:, None] * inv
    cr, sr, cc, sc = jnp.cos(ang_r), jnp.sin(ang_r), jnp.cos(ang_c), jnp.sin(ang_c)
    cos_l = jnp.concatenate([cr, cr, cc, cc], axis=-1)
    sin_l = jnp.concatenate([-sr, sr, -sc, sc], axis=-1)
    cos_t = jnp.concatenate([cos_l, jnp.ones((CTX_LEN, RET_KEY_DIM), F32)], axis=0)
    sin_t = jnp.concatenate([sin_l, jnp.zeros((CTX_LEN, RET_KEY_DIM), F32)], axis=0)
    return cos_t, sin_t


def _ret_kernel(lg_ref, q_ref, k_ref, v_ref, sg_ref, y_ref, o_acc, state_f, state_b):
    head = pl.program_id(1)
    n_c = RET_BLOCK
    n_lat = SEQ // n_c
    assert CTX_LEN == n_c and n_lat % 2 == 0
    ii = lax.broadcasted_iota(jnp.int32, (n_c, n_c), 0).astype(F32)
    jj = lax.broadcasted_iota(jnp.int32, (n_c, n_c), 1).astype(F32)
    row_k = lax.broadcasted_iota(jnp.int32, (n_c, RET_KEY_DIM), 0).astype(F32)
    row_v = lax.broadcasted_iota(jnp.int32, (n_c, RET_VAL_DIM), 0).astype(F32)

    def decays(backward):
        lg = lg_ref[1 if backward else 0, head]
        if backward:
            diff = jj - ii
            xi = jnp.exp(lg * (n_c - row_v))
            zeta = jnp.exp(lg * row_k)
        else:
            diff = ii - jj
            xi = jnp.exp(lg * (row_v + 1.0))
            zeta = jnp.exp(lg * (n_c - 1.0 - row_k))
        dmask = jnp.where(diff >= 0, jnp.exp(lg * jnp.maximum(diff, 0.0)), 0.0)
        g_chunk = jnp.exp(lg * n_c + jnp.zeros((1, RET_VAL_DIM), F32))
        return dmask, xi, zeta, g_chunk

    def advance(chunk, state, consts, finalize):
        dmask, xi, zeta, g_chunk = consts
        r = chunk * n_c if isinstance(chunk, int) else pl.multiple_of(chunk * n_c, n_c)
        qc = q_ref[0, pl.ds(r, n_c), :]
        kc = k_ref[0, pl.ds(r, n_c), :]
        vc = v_ref[0, pl.ds(r, n_c), :]
        st = state[...]
        inner = lax.dot_general(qc, kc, _NT_DIMS, preferred_element_type=F32) * dmask
        o = _bdot(inner.astype(BF16), vc) + _bdot(qc, st.astype(BF16)) * xi
        kz = (kc.astype(F32) * zeta).astype(BF16)
        state[...] = st * g_chunk + lax.dot_general(kz, vc, _TN_DIMS, preferred_element_type=F32)
        if finalize:
            tot = o_acc[pl.ds(r, n_c), :] + o
            mu = jnp.mean(tot, axis=-1, keepdims=True)
            cen = tot - mu
            var = jnp.mean(cen * cen, axis=-1, keepdims=True)
            yn = cen * lax.rsqrt(var + GN_EPS)
            y_ref[0, pl.ds(r, n_c), :] = (sg_ref[0, pl.ds(r, n_c), :].astype(F32) * yn).astype(BF16)
        else:
            o_acc[pl.ds(r, n_c), :] = o

    consts_f = decays(False)
    consts_b = decays(True)
    state_f[...] = jnp.zeros_like(state_f)
    state_b[...] = jnp.zeros_like(state_b)

    def both(chunk_f, chunk_b, finalize):
        advance(chunk_f, state_f, consts_f, finalize)
        advance(chunk_b, state_b, consts_b, finalize)

    advance(n_lat, state_f, consts_f, False)
    advance(n_lat, state_b, consts_b, True)

    def first_half(s, carry):
        both(s, n_lat - 1 - s, False)
        return carry

    def second_half(s, carry):
        both(s, n_lat - 1 - s, True)
        return carry

    lax.fori_loop(0, n_lat // 2, first_half, 0, unroll=2)
    lax.fori_loop(n_lat // 2, n_lat, second_half, 0, unroll=2)


def _ret_call(lg, q, k, v, sg):
    batch = q.shape[0]
    qk_spec = pl.BlockSpec((1, T_TOK, RET_KEY_DIM), lambda b, h: (b, 0, h))
    v_spec = pl.BlockSpec((1, T_TOK, RET_VAL_DIM), lambda b, h: (b, 0, h))
    return pl.pallas_call(
        _ret_kernel,
        grid=(batch, RET_HEADS),
        in_specs=[pl.BlockSpec(memory_space=pltpu.SMEM), qk_spec, qk_spec, v_spec, v_spec],
        out_specs=v_spec,
        out_shape=jax.ShapeDtypeStruct((batch, T_TOK, RET_V), BF16),
        scratch_shapes=[pltpu.VMEM((T_TOK, RET_VAL_DIM), F32),
                        pltpu.VMEM((RET_KEY_DIM, RET_VAL_DIM), F32),
                        pltpu.VMEM((RET_KEY_DIM, RET_VAL_DIM), F32)],
        compiler_params=_cparams(2),
        name="retention",
    )(lg, q, k, v, sg)


def _odd_out_kernel(x_ref, mod_ref, y_ref, w_ref, o_ref):
    o_ref[0] = x_ref[0] + mod_ref[0, 5:6, :] * _bdot(y_ref[0], w_ref[...])


def _odd_out_call(xt, mod_i, y, w):
    batch = xt.shape[0]
    return pl.pallas_call(
        _odd_out_kernel,
        grid=(batch, NT_ALL),
        in_specs=[_tok_spec(D_MODEL), _mod_spec(batch), _tok_spec(RET_V), _full_spec((RET_V, D_MODEL))],
        out_specs=_tok_spec(D_MODEL),
        out_shape=jax.ShapeDtypeStruct((batch, T_TOK, D_MODEL), F32),
        compiler_params=_cparams(2),
        name="odd_out",
    )(xt, mod_i, y, w)


def _fft_s2_fwd_kernel(a_ref, tw_ref, fhi_ref, flo_ref, g_ref):
    reps = HY_WIDTH // LANES
    for kk in range(FFT_KB):
        twr = jnp.concatenate([tw_ref[0, kk]] * reps, axis=1)
        twi = jnp.concatenate([tw_ref[1, kk]] * reps, axis=1)
        ar = a_ref[0, kk]
        ai = a_ref[1, kk]
        d = jnp.concatenate([ar * twr - ai * twi, ar * twi + ai * twr], axis=0)
        x = _dot3(fhi_ref[...], flo_ref[...], d)
        g_ref[0, kk] = x[:FFT_N2]
        g_ref[1, kk] = x[FFT_N2:]


def _dft_dense_kernel(x_ref, mhi_ref, mlo_ref, o_ref):
    o_ref[...] = _dot3(mhi_ref[...], mlo_ref[...], x_ref[...])


def _filter_spectra(filter_params):
    cst = _fft_constants()
    n_lane = FFT_N2 * HY_WIDTH
    filt = _hyena_filter(SEQ, *filter_params).reshape(1, 1, FFT_N1, n_lane)
    a = pl.pallas_call(
        _fft_s1_kernel,
        grid=(1, n_lane // FFT_W),
        in_specs=[pl.BlockSpec((1, 1, FFT_N1, FFT_W), lambda p, w: (p, 0, 0, w)),
                  _full_spec((2 * FFT_N1, FFT_N1)), _full_spec((2 * FFT_N1, FFT_N1))],
        out_specs=pl.BlockSpec((1, 2 * FFT_N1, FFT_W), lambda p, w: (p, 0, w)),
        out_shape=jax.ShapeDtypeStruct((1, 2 * FFT_N1, n_lane), F32),
        compiler_params=_cparams(2),
        name="filt_fft_s1",
    )(filt, *cst["m1r"])
    a = a.reshape(2, FFT_N1, FFT_N2, HY_WIDTH)
    blk = pl.BlockSpec((2, FFT_KB, FFT_N2, HY_WIDTH), lambda kb: (0, kb, 0, 0))
    sq = (2 * FFT_N2, 2 * FFT_N2)
    g_lat = pl.pallas_call(
        _fft_s2_fwd_kernel,
        grid=(FFT_N1 // FFT_KB,),
        in_specs=[blk, pl.BlockSpec((2, FFT_KB, FFT_N2, LANES), lambda kb: (0, kb, 0, 0)),
                  pl.BlockSpec(sq, lambda kb: (0, 0)), pl.BlockSpec(sq, lambda kb: (0, 0))],
        out_specs=blk,
        out_shape=jax.ShapeDtypeStruct((2, FFT_N1, FFT_N2, HY_WIDTH), F32),
        compiler_params=_cparams(1),
        name="filt_fft_s2",
    )(a, cst["tw"], *cst["m2"])
    filt_ctx = _hyena_filter(CTX_LEN, *filter_params)
    g_ctx = pl.pallas_call(
        _dft_dense_kernel,
        out_shape=jax.ShapeDtypeStruct((2 * FFT_NC, HY_WIDTH), F32),
        compiler_params=pltpu.CompilerParams(vmem_limit_bytes=VMEM_LIMIT_BYTES),
        name="filt_dft_ctx",
    )(filt_ctx, *cst["mcf_full"])
    return g_lat, g_ctx.reshape(2, FFT_NC, HY_WIDTH)


def _even_mixer(xt, mod_i, gain, w_in, w_out, q_gain, k_gain, rpb, conv_w, conv_b, filter_params, d_bias):
    scale = NA_HEAD_DIM ** -0.5
    qg = (jnp.tile(q_gain, NA_HEADS) * scale).reshape(1, NA_WIDTH)
    kg = jnp.tile(k_gain, NA_HEADS).reshape(1, NA_WIDTH)
    head_of = np.arange(NA_WIDTH) // NA_HEAD_DIM
    bd = jnp.asarray((head_of[:, None] == head_of[None, :]) / NA_HEAD_DIM, BF16)
    q, k, v, u = _even_in_call(xt, mod_i, gain, w_in.astype(BF16), qg, kg, bd)
    a = _na_call(q, k, v, _na_bias_table(rpb))
    z_lat, x0_lat, z_ctx, x0_ctx = _hy_pre_call(u, conv_w, conv_b.reshape(1, -1))
    g_lat, g_ctx = _filter_spectra(filter_params)
    y_lat = _hy_long_conv_call(z_lat, x0_lat, g_lat, d_bias)
    y_ctx = _hy_ctx_conv_call(z_ctx, x0_ctx, g_ctx, d_bias)
    return _even_out_call(xt, mod_i, a, y_lat, y_ctx, w_out.astype(BF16))


def _odd_mixer(xt, mod_i, gain, w_in, w_out, logit_f, logit_b, rope):
    q, k, v, sg = _odd_in_call(xt, mod_i, gain, w_in.astype(BF16), *rope)
    lg = jnp.stack([jax.nn.log_sigmoid(logit_f.astype(F32)), jax.nn.log_sigmoid(logit_b.astype(F32))])
    y = _ret_call(lg, q, k, v, sg)
    return _odd_out_call(xt, mod_i, y, w_out.astype(BF16))


def kernel(x, c, ctx, c_ctx, w_mod, b_mod, norm_gain, ffn_a_in, ffn_a_out, ffn_b_in, ffn_b_out,
           even_in, even_out, na_q_gain, na_k_gain, na_rpb, hy_conv_w, hy_conv_b,
           hy_fw1, hy_fb1, hy_fw2, hy_fb2, hy_fw3, hy_fb3, hy_fw4, hy_freq, hy_bias,
           ret_in, ret_out, ret_logit_f, ret_logit_b):
    batch = x.shape[0]
    assert x.shape == (batch, SEQ, D_MODEL) and ctx.shape == (batch, CTX_LEN, D_MODEL)
    assert batch % 2 == 0 and batch < MOD_ROWS
    xt = jnp.concatenate([x, ctx], axis=1)
    c_rows = jnp.concatenate([c, c_ctx[None], jnp.zeros((MOD_ROWS - batch - 1, D_MODEL), F32)], axis=0)
    mod_all = _mod_call(c_rows, w_mod, b_mod).reshape(DEPTH, MOD_ROWS, N_MOD, D_MODEL)
    rope = _rope_tables()
    for i in range(DEPTH):
        last = i == DEPTH - 1
        mod_i = mod_all[i]
        gains = norm_gain[i].reshape(3, 1, D_MODEL)
        xt = _ffn_call(xt, mod_i, gains[0], ffn_a_in[i].astype(BF16), ffn_a_out[i].astype(BF16),
                       (0, 1, 2), NT_ALL)
        if i % 2 == 0:
            e = i // 2
            filter_params = (hy_fw1[e], hy_fb1[e], hy_fw2[e], hy_fb2[e], hy_fw3[e], hy_fb3[e],
                             hy_fw4[e], hy_freq[e])
            xt = _even_mixer(xt, mod_i, gains[1], even_in[e], even_out[e], na_q_gain[e], na_k_gain[e],
                             na_rpb[e], hy_conv_w[e], hy_conv_b[e], filter_params, hy_bias[e])
        else:
            o = i // 2
            xt = _odd_mixer(xt, mod_i, gains[1], ret_in[o], ret_out[o], ret_logit_f[o], ret_logit_b[o], rope)
        xt = _ffn_call(xt, mod_i, gains[2], ffn_b_in[i].astype(BF16), ffn_b_out[i].astype(BF16),
                       (6, 7, 8), NT_LAT if last else NT_ALL)
    return xt
```

```python
import functools
import math

import ml_dtypes
import numpy as np
import jax
import jax.numpy as jnp
from jax import lax
from jax.experimental import pallas as pl
from jax.experimental.pallas import tpu as pltpu

F32 = jnp.float32
BF16 = jnp.bfloat16

D_MODEL = 1024
SEQ = 4096
DEPTH = 4
GRID_W = 64
CTX_LEN = 256
N_MOD = 9
RMS_EPS = 1e-6
GN_EPS = 1e-6
D_FF = 2816
NA_HEADS = 8
NA_HEAD_DIM = 64
NA_WIDTH = NA_HEADS * NA_HEAD_DIM
WIN_R = 8
WIN_C = 16
HY_WIDTH = D_MODEL - NA_WIDTH
HY_BANDS = 8
HY_TARGET = 1e-2
HY_FAST_PCT = 0.3
HY_SLOW_PCT = 1.5
RET_HEADS = 4
RET_KEY_DIM = D_MODEL // RET_HEADS
RET_VAL_DIM = 2 * RET_KEY_DIM
RET_QK = RET_HEADS * RET_KEY_DIM
RET_V = RET_HEADS * RET_VAL_DIM
ROPE_BASE = 10000.0

LANES = 128
VMEM_LIMIT_BYTES = 56 * 2**20
T_TOK = SEQ + CTX_LEN
TM = CTX_LEN
NT_LAT = SEQ // TM
NT_ALL = T_TOK // TM
MOD_ROWS = 16
FF_CHUNKS = ((0, 1536), (1536, 1280))
RET_BLOCK = 256

NA_RB = 4
NA_KR = NA_RB + WIN_R - 1
GRID_H = SEQ // GRID_W
NA_NQ = NA_RB * GRID_W
NA_NK = NA_KR * GRID_W
NEG_BIG = -1e30

FFT_N = 2 * SEQ
FFT_N1 = 64
FFT_N2 = 128
FFT_N1H = FFT_N1 // 2
FFT_W = 2048
FFT_KB = 4
FFT_PITCH = FFT_N2 + 8
FFT_NC = 2 * CTX_LEN


def _cparams(n_axes):
    return pltpu.CompilerParams(dimension_semantics=("arbitrary",) * n_axes,
                                vmem_limit_bytes=VMEM_LIMIT_BYTES)


def _bdot(a, b):
    return jnp.dot(a, b, preferred_element_type=F32)


_NT_DIMS = (((1,), (1,)), ((), ()))
_TN_DIMS = (((0,), (0,)), ((), ()))


def _split_hi_lo(m):
    hi = m.astype(BF16)
    lo = (m - hi.astype(F32)).astype(BF16)
    return hi, lo


def _dot3(m_hi, m_lo, d):
    d_hi, d_lo = _split_hi_lo(d)
    return _bdot(m_hi, d_hi) + _bdot(m_lo, d_hi) + _bdot(m_hi, d_lo)


def _modulated(x, gain, mod_ref, shift_row, scale_row):
    ms = jnp.mean(x * x, axis=-1, keepdims=True)
    y = x * lax.rsqrt(ms + RMS_EPS) * gain
    return (y * (1.0 + mod_ref[0, scale_row:scale_row + 1, :])
            + mod_ref[0, shift_row:shift_row + 1, :])


def _silu(a):
    return a * jax.nn.sigmoid(a)


def _mod_kernel(c_ref, w_ref, b_ref, o_ref):
    s = _silu(c_ref[...])
    o_ref[0] = jnp.dot(s, w_ref[0], precision=lax.Precision.HIGHEST,
                       preferred_element_type=F32) + b_ref[0]


def _mod_call(c_rows, w_mod, b_mod):
    depth, d, n = w_mod.shape
    tn = 1024
    return pl.pallas_call(
        _mod_kernel,
        grid=(depth, n // tn),
        in_specs=[pl.BlockSpec((MOD_ROWS, d), lambda i, j: (0, 0)),
                  pl.BlockSpec((1, d, tn), lambda i, j: (i, 0, j)),
                  pl.BlockSpec((1, 1, tn), lambda i, j: (i, 0, j))],
        out_specs=pl.BlockSpec((1, MOD_ROWS, tn), lambda i, j: (i, 0, j)),
        out_shape=jax.ShapeDtypeStruct((depth, MOD_ROWS, n), F32),
        compiler_params=_cparams(2),
        name="mod",
    )(c_rows, w_mod, b_mod.reshape(depth, 1, n))


def _tok_spec(width):
    return pl.BlockSpec((1, TM, width), lambda b, j: (b, j, 0))


def _mod_spec(batch):
    return pl.BlockSpec((1, N_MOD, D_MODEL), lambda b, j: (jnp.where(j == NT_LAT, batch, b), 0, 0))


def _full_spec(shape):
    zeros = (0,) * len(shape)
    return pl.BlockSpec(shape, lambda b, j: zeros)


def _ffn_kernel(x_ref, mod_ref, gain_ref, win_ref, wout_ref, o_ref, *, rows):
    _ffn_tile(x_ref[0], mod_ref, gain_ref, win_ref, wout_ref, o_ref, rows)


def _ffn_split_kernel(xl_ref, xc_ref, mod_ref, gain_ref, win_ref, wout_ref, o_ref, *, rows):
    x = jnp.where(pl.program_id(1) == NT_LAT, xc_ref[0], xl_ref[0])
    _ffn_tile(x, mod_ref, gain_ref, win_ref, wout_ref, o_ref, rows)


def _ffn_tile(x, mod_ref, gain_ref, win_ref, wout_ref, o_ref, rows):
    shift_row, scale_row, gate_row = rows
    h = _modulated(x, gain_ref[...], mod_ref, shift_row, scale_row).astype(BF16)
    acc = jnp.zeros((TM, D_MODEL), F32)
    for start, size in FF_CHUNKS:
        a = _bdot(h, win_ref[:, start:start + size])
        b = _bdot(h, win_ref[:, D_FF + start:D_FF + start + size])
        g = (_silu(a) * b).astype(BF16)
        acc = acc + _bdot(g, wout_ref[start:start + size, :])
    o_ref[0] = x + (0.5 * mod_ref[0, gate_row:gate_row + 1, :]) * acc


def _ffn_call(xt, mod_i, gain, w_in, w_out, rows, n_tiles):
    batch = xt.shape[0]
    return pl.pallas_call(
        functools.partial(_ffn_kernel, rows=rows),
        grid=(batch, n_tiles),
        in_specs=[_tok_spec(D_MODEL), _mod_spec(batch), _full_spec((1, D_MODEL)),
                  _full_spec((D_MODEL, 2 * D_FF)), _full_spec((D_FF, D_MODEL))],
        out_specs=_tok_spec(D_MODEL),
        out_shape=jax.ShapeDtypeStruct((batch, n_tiles * TM, D_MODEL), F32),
        compiler_params=_cparams(2),
        name="ffn",
    )(xt, mod_i, gain, w_in, w_out)


def _ffn_split_call(x, ctx, mod_i, gain, w_in, w_out, rows):
    batch = x.shape[0]
    return pl.pallas_call(
        functools.partial(_ffn_split_kernel, rows=rows),
        grid=(batch, NT_ALL),
        in_specs=[pl.BlockSpec((1, TM, D_MODEL), lambda b, j: (b, jnp.minimum(j, NT_LAT - 1), 0)),
                  pl.BlockSpec((1, CTX_LEN, D_MODEL), lambda b, j: (b, 0, 0)),
                  _mod_spec(batch), _full_spec((1, D_MODEL)),
                  _full_spec((D_MODEL, 2 * D_FF)), _full_spec((D_FF, D_MODEL))],
        out_specs=_tok_spec(D_MODEL),
        out_shape=jax.ShapeDtypeStruct((batch, T_TOK, D_MODEL), F32),
        compiler_params=_cparams(2),
        name="ffn_split",
    )(x, ctx, mod_i, gain, w_in, w_out)


def _even_in_kernel(x_ref, mod_ref, gain_ref, w_ref, qg_ref, kg_ref, bd_ref,
                    q_ref, k_ref, v_ref, u_ref):
    h = _modulated(x_ref[0], gain_ref[...], mod_ref, 3, 4).astype(BF16)

    def head_norm(t, g):
        ms = _bdot((t * t).astype(BF16), bd_ref[...])
        return (t * lax.rsqrt(ms + RMS_EPS) * g).astype(BF16)

    q_ref[0] = head_norm(_bdot(h, w_ref[:, 0:NA_WIDTH]), qg_ref[...])
    k_ref[0] = head_norm(_bdot(h, w_ref[:, NA_WIDTH:2 * NA_WIDTH]), kg_ref[...])
    v_ref[0] = _bdot(h, w_ref[:, 2 * NA_WIDTH:3 * NA_WIDTH]).astype(BF16)
    u_ref[0] = _bdot(h, w_ref[:, 3 * NA_WIDTH:])


def _even_in_call(xt, mod_i, gain, w, q_gain, k_gain, bd):
    batch = xt.shape[0]
    n_in = w.shape[1]
    return pl.pallas_call(
        _even_in_kernel,
        grid=(batch, NT_ALL),
        in_specs=[_tok_spec(D_MODEL), _mod_spec(batch), _full_spec((1, D_MODEL)),
                  _full_spec((D_MODEL, n_in)), _full_spec((1, NA_WIDTH)), _full_spec((1, NA_WIDTH)),
                  _full_spec((NA_WIDTH, NA_WIDTH))],
        out_specs=[_tok_spec(NA_WIDTH), _tok_spec(NA_WIDTH), _tok_spec(NA_WIDTH),
                   _tok_spec(3 * HY_WIDTH)],
        out_shape=[jax.ShapeDtypeStruct((batch, T_TOK, NA_WIDTH), BF16)] * 3
        + [jax.ShapeDtypeStruct((batch, T_TOK, 3 * HY_WIDTH), F32)],
        compiler_params=_cparams(2),
        name="even_in",
    )(xt, mod_i, gain, w, q_gain, k_gain, bd)


def _na_kernel(q_ref, k_ref, v_ref, bias_ref, o_ref):
    lane = lax.broadcasted_iota(jnp.int32, (1, LANES), 1)
    first_head = lane < NA_HEAD_DIM
    k_ctx = k_ref[0, SEQ:T_TOK, :]
    v_ctx = v_ref[0, SEQ:T_TOK, :]

    def stack_heads(q):
        zero = jnp.zeros_like(q)
        return jnp.concatenate([jnp.where(first_head, q, zero), jnp.where(first_head, zero, q)], axis=0)

    def attend(qs, scores_and_values):
        s_list = [lax.dot_general(qs, kk, _NT_DIMS, preferred_element_type=F32) if bias is None
                  else lax.dot_general(qs, kk, _NT_DIMS, preferred_element_type=F32) + bias
                  for kk, _, bias in scores_and_values]
        m = functools.reduce(jnp.maximum, [jnp.max(s, axis=-1, keepdims=True) for s in s_list])
        p_list = [jnp.exp(s - m) for s in s_list]
        denom = functools.reduce(jnp.add, [jnp.sum(p, axis=-1, keepdims=True) for p in p_list])
        o = functools.reduce(jnp.add, [_bdot(p.astype(BF16), vv)
                                       for p, (_, vv, _) in zip(p_list, scores_and_values)])
        o = o / denom
        n = qs.shape[0] // 2
        return jnp.where(first_head, o[:n], o[n:]).astype(BF16)

    def block(i, carry):
        r0 = i * NA_RB
        u0 = jnp.clip(r0 - WIN_R // 2, 0, GRID_H - NA_KR)
        pattern = jnp.where(i == 0, 0, jnp.where(i == GRID_H // NA_RB - 1, 2, 1))
        q0 = pl.multiple_of(r0 * GRID_W, NA_NQ)
        k0 = pl.multiple_of(u0 * GRID_W, GRID_W)
        qs = stack_heads(q_ref[0, pl.ds(q0, NA_NQ), :])
        k_win = k_ref[0, pl.ds(k0, NA_NK), :]
        v_win = v_ref[0, pl.ds(k0, NA_NK), :]
        o_ref[0, pl.ds(q0, NA_NQ), :] = attend(
            qs, [(k_win, v_win, bias_ref[pattern, 0]), (k_ctx, v_ctx, None)])
        return carry

    lax.fori_loop(0, GRID_H // NA_RB, block, 0, unroll=2)
    o_ref[0, SEQ:T_TOK, :] = attend(stack_heads(q_ref[0, SEQ:T_TOK, :]), [(k_ctx, v_ctx, None)])


def _na_call(q, k, v, bias):
    batch = q.shape[0]
    n_pairs = NA_WIDTH // LANES
    spec = pl.BlockSpec((1, T_TOK, LANES), lambda b, p: (b, 0, p))
    return pl.pallas_call(
        _na_kernel,
        grid=(batch, n_pairs),
        in_specs=[spec, spec, spec,
                  pl.BlockSpec((3, 1, 2 * NA_NQ, NA_NK), lambda b, p: (0, p, 0, 0))],
        out_specs=spec,
        out_shape=jax.ShapeDtypeStruct((batch, T_TOK, NA_WIDTH), BF16),
        compiler_params=_cparams(2),
        name="na_attn",
    )(q, k, v, bias)


def _na_bias_table(rpb):
    j = np.arange(NA_RB)[:, None, None, None]
    c = np.arange(GRID_W)[None, :, None, None]
    kk = np.arange(NA_KR)[None, None, :, None]
    kc = np.arange(GRID_W)[None, None, None, :]
    cs = np.clip(c - WIN_C // 2, 0, GRID_W - WIN_C)
    col_ok = (kc >= cs) & (kc < cs + WIN_C)
    dc = np.clip(kc - c + WIN_C - 1, 0, 2 * WIN_C - 2)[0, :, 0, :]
    col_sel = (dc[..., None] == np.arange(2 * WIN_C - 1)).astype(np.float32)
    shape = (NA_RB, GRID_W, NA_KR, GRID_W)
    oks, row_sels = [], []
    for off, rs_rel in ((0, 0 * j), (-(WIN_R // 2), j), (-(NA_KR - NA_RB), NA_KR - WIN_R + 0 * j)):
        row_ok = (kk >= rs_rel) & (kk < rs_rel + WIN_R)
        dr = np.clip(off + kk - j + WIN_R - 1, 0, 2 * WIN_R - 2)[:, 0, :, 0]
        row_sels.append((dr[..., None] == np.arange(2 * WIN_R - 1)).astype(np.float32))
        oks.append(np.broadcast_to(row_ok & col_ok, shape))
    vals = jnp.einsum("pjkr,hrc,qmc->phjqkm", np.stack(row_sels), rpb.astype(F32), col_sel,
                      precision=lax.Precision.HIGHEST)
    table = jnp.where(np.stack(oks)[:, None], vals, NEG_BIG)
    return table.reshape(3, NA_HEADS // 2, 2 * NA_NQ, NA_NK)


def _hy_pre_kernel(u0_ref, u1_ref, uv_ref, w0_ref, w1_ref, wv_ref, b0_ref, b1_ref, bv_ref,
                   zl_ref, xl_ref, zc_ref, xc_ref):
    row = lax.broadcasted_iota(jnp.int32, (TM, LANES), 0)
    zero_row = jnp.zeros((1, LANES), F32)

    def conv(u_ref, w_ref, b_ref, tile):
        s = tile * TM
        cur = u_ref[0, s:s + TM, :]
        seq_start = tile in (0, NT_LAT)
        seq_end = tile in (NT_LAT - 1, NT_LAT)
        prev_row = zero_row if seq_start else u_ref[0, s - 1:s, :]
        next_row = zero_row if seq_end else u_ref[0, s + TM:s + TM + 1, :]
        before = jnp.where(row == 0, prev_row, pltpu.roll(cur, 1, 0))
        after = jnp.where(row == TM - 1, next_row, pltpu.roll(cur, TM - 1, 0))
        return before * w_ref[0:1, :] + cur * w_ref[1:2, :] + after * w_ref[2:3, :] + b_ref[...]

    for tile in range(NT_ALL):
        x0 = conv(u0_ref, w0_ref, b0_ref, tile)
        z = conv(uv_ref, wv_ref, bv_ref, tile) * conv(u1_ref, w1_ref, b1_ref, tile)
        if tile < NT_LAT:
            for half in range(TM // FFT_N2):
                base = (tile * (TM // FFT_N2) + half) * FFT_PITCH
                zl_ref[0, base:base + FFT_N2, :] = z[half * FFT_N2:(half + 1) * FFT_N2]
                xl_ref[0, base:base + FFT_N2, :] = x0[half * FFT_N2:(half + 1) * FFT_N2]
                pad = jnp.zeros((FFT_PITCH - FFT_N2, LANES), F32)
                zl_ref[0, base + FFT_N2:base + FFT_PITCH, :] = pad
                xl_ref[0, base + FFT_N2:base + FFT_PITCH, :] = pad
        else:
            zc_ref[0] = z
            xc_ref[0] = x0


def _hy_pre_call(u, conv_w, conv_b):
    batch = u.shape[0]
    nb = HY_WIDTH // LANES
    u_specs = [pl.BlockSpec((1, T_TOK, LANES), lambda b, cb, g=g: (b, 0, g * nb + cb)) for g in range(3)]
    w_specs = [pl.BlockSpec((3, LANES), lambda b, cb, g=g: (0, g * nb + cb)) for g in range(3)]
    b_specs = [pl.BlockSpec((1, LANES), lambda b, cb, g=g: (0, g * nb + cb)) for g in range(3)]
    lat = pl.BlockSpec((1, FFT_N1H * FFT_PITCH, LANES), lambda b, cb: (b, 0, cb))
    ctx = pl.BlockSpec((1, CTX_LEN, LANES), lambda b, cb: (b, 0, cb))
    return pl.pallas_call(
        _hy_pre_kernel,
        grid=(batch, nb),
        in_specs=u_specs + w_specs + b_specs,
        out_specs=[lat, lat, ctx, ctx],
        out_shape=[jax.ShapeDtypeStruct((batch, FFT_N1H * FFT_PITCH, HY_WIDTH), F32)] * 2
        + [jax.ShapeDtypeStruct((batch, CTX_LEN, HY_WIDTH), F32)] * 2,
        compiler_params=_cparams(2),
        name="hy_pre",
    )(u, u, u, conv_w, conv_w, conv_w, conv_b, conv_b, conv_b)


def _complex_block(re, im):
    return np.block([[re, -im], [im, re]])


@functools.lru_cache(maxsize=None)
def _fft_constants():
    k1 = np.arange(FFT_N1)
    n1 = np.arange(FFT_N1H)
    f1 = np.exp(-2j * np.pi * np.outer(k1, n1) / FFT_N1)
    m1 = _complex_block(f1.real, f1.imag)
    f1_full = np.exp(-2j * np.pi * np.outer(k1, k1) / FFT_N1)
    m1r = np.concatenate([f1_full.real, f1_full.imag], axis=0)
    n2 = np.arange(FFT_N2)
    tw = np.exp(-2j * np.pi * np.outer(k1, n2) / FFT_N)
    tw = np.stack([tw.real, tw.imag])[..., None] * np.ones((1, 1, 1, LANES))
    f2 = np.exp(-2j * np.pi * np.outer(n2, n2) / FFT_N2)
    m2 = _complex_block(f2.real, f2.imag)
    m2i = _complex_block(f2.real, -f2.imag)
    c1 = np.exp(2j * np.pi * np.outer(n1, k1) / FFT_N1) / FFT_N
    m3 = _complex_block(c1.real, c1.imag)
    kc = np.arange(FFT_NC)
    nc = np.arange(CTX_LEN)
    ang = 2 * np.pi * np.outer(kc, nc) / FFT_NC
    mcf = np.concatenate([np.cos(ang), -np.sin(ang)], axis=0)
    ang_full = 2 * np.pi * np.outer(kc, kc) / FFT_NC
    mcf_full = np.concatenate([np.cos(ang_full), -np.sin(ang_full)], axis=0)
    mci = np.concatenate([np.cos(ang.T), -np.sin(ang.T)], axis=1) / FFT_NC

    def hi_lo(m):
        hi = m.astype(ml_dtypes.bfloat16)
        lo = (m - hi.astype(np.float64)).astype(ml_dtypes.bfloat16)
        return hi, lo

    return dict(m1=hi_lo(m1), m1r=hi_lo(m1r), m2=hi_lo(m2), m2i=hi_lo(m2i), m3=hi_lo(m3),
                mcf=hi_lo(mcf), mcf_full=hi_lo(mcf_full), mci=hi_lo(mci), tw=tw.astype(np.float32))


def _fft_s1_flat_kernel(z_ref, mhi_ref, mlo_ref, a_ref):
    d = z_ref[0].reshape(-1, FFT_W)
    a_ref[0] = _dot3(mhi_ref[...], mlo_ref[...], d)


def _slab_rows(n2, n_slabs):
    return pl.ds(n2, n_slabs, stride=FFT_PITCH)


def _fft_s1_kernel(z_ref, mhi_ref, mlo_ref, a_ref):
    def step(i, carry):
        cols = []
        for dn in range(2):
            rows = _slab_rows(2 * i + dn, FFT_N1H)
            cols.append(jnp.concatenate([z_ref[0, 0, rows, :], z_ref[0, 1, rows, :]], axis=0))
        r = _dot3(mhi_ref[...], mlo_ref[...], jnp.concatenate(cols, axis=1))
        for dn in range(2):
            rows = _slab_rows(2 * i + dn, FFT_N1)
            a_ref[0, 0, rows, :] = r[:FFT_N1, dn * LANES:(dn + 1) * LANES]
            a_ref[0, 1, rows, :] = r[FFT_N1:, dn * LANES:(dn + 1) * LANES]
        return carry

    lax.fori_loop(0, FFT_N2 // 2, step, 0)
    for pad_row in range(FFT_N2, FFT_PITCH):
        for part in range(2):
            a_ref[0, part, _slab_rows(pad_row, FFT_N1), :] = jnp.zeros((FFT_N1, LANES), F32)


def _fft_s2_kernel(a_ref, tw_ref, g_ref, fhi_ref, flo_ref, ihi_ref, ilo_ref, o_ref):
    reps = HY_WIDTH // LANES
    pad = jnp.zeros((FFT_PITCH - FFT_N2, HY_WIDTH), F32)
    for kk in range(FFT_KB):
        twr = jnp.concatenate([tw_ref[0, kk]] * reps, axis=1)
        twi = jnp.concatenate([tw_ref[1, kk]] * reps, axis=1)
        ar = a_ref[0, 0, kk, 0:FFT_N2, :]
        ai = a_ref[0, 1, kk, 0:FFT_N2, :]
        d = jnp.concatenate([ar * twr - ai * twi, ar * twi + ai * twr], axis=0)
        x = _dot3(fhi_ref[...], flo_ref[...], d)
        xr, xi = x[:FFT_N2], x[FFT_N2:]
        gr, gi = g_ref[0, kk], g_ref[1, kk]
        y = jnp.concatenate([xr * gr - xi * gi, xr * gi + xi * gr], axis=0)
        b = _dot3(ihi_ref[...], ilo_ref[...], y)
        br, bi = b[:FFT_N2], b[FFT_N2:]
        o_ref[0, 0, kk, 0:FFT_N2, :] = br * twr + bi * twi
        o_ref[0, 1, kk, 0:FFT_N2, :] = bi * twr - br * twi
        o_ref[0, 0, kk, FFT_N2:FFT_PITCH, :] = pad
        o_ref[0, 1, kk, FFT_N2:FFT_PITCH, :] = pad


def _fft_s3_kernel(b_ref, mhi_ref, mlo_ref, z_ref, x0_ref, bias_ref, y_ref):
    def step(i, carry):
        cols = []
        for dn in range(2):
            rows = _slab_rows(2 * i + dn, FFT_N1)
            cols.append(jnp.concatenate([b_ref[0, 0, rows, :], b_ref[0, 1, rows, :]], axis=0))
        y = _dot3(mhi_ref[...], mlo_ref[...], jnp.concatenate(cols, axis=1))
        for dn in range(2):
            rows = _slab_rows(2 * i + dn, FFT_N1H)
            for s in range(2):
                conv = y[s * FFT_N1H:(s + 1) * FFT_N1H, dn * LANES:(dn + 1) * LANES]
                y_ref[0, s, rows, :] = (conv + z_ref[0, s, rows, :] * bias_ref[...]) * x0_ref[0, s, rows, :]
        return carry

    lax.fori_loop(0, FFT_N2 // 2, step, 0)
    for pad_row in range(FFT_N2, FFT_PITCH):
        for s in range(2):
            y_ref[0, s, _slab_rows(pad_row, FFT_N1H), :] = jnp.zeros((FFT_N1H, LANES), F32)


def _hy_long_conv_call(z_lat, x0_lat, g_spec, d_bias):
    batch = z_lat.shape[0]
    pairs = batch // 2
    cst = _fft_constants()
    n_cb = HY_WIDTH // LANES
    rows_z = FFT_N1H * FFT_PITCH
    rows_a = FFT_N1 * FFT_PITCH
    zv = z_lat.reshape(pairs, 2, rows_z, HY_WIDTH)
    xv = x0_lat.reshape(pairs, 2, rows_z, HY_WIDTH)
    z_spec = pl.BlockSpec((1, 2, rows_z, LANES), lambda p, cb: (p, 0, 0, cb))
    a_spec = pl.BlockSpec((1, 2, rows_a, LANES), lambda p, cb: (p, 0, 0, cb))
    a = pl.pallas_call(
        _fft_s1_kernel,
        grid=(pairs, n_cb),
        in_specs=[z_spec, _full_spec((2 * FFT_N1, 2 * FFT_N1H)), _full_spec((2 * FFT_N1, 2 * FFT_N1H))],
        out_specs=a_spec,
        out_shape=jax.ShapeDtypeStruct((pairs, 2, rows_a, HY_WIDTH), F32),
        compiler_params=_cparams(2),
        name="hy_fft_s1",
    )(zv, *cst["m1"])
    a = a.reshape(pairs, 2, FFT_N1, FFT_PITCH, HY_WIDTH)
    blk = pl.BlockSpec((1, 2, FFT_KB, FFT_PITCH, HY_WIDTH), lambda kb, p: (p, 0, kb, 0, 0))
    sq = (2 * FFT_N2, 2 * FFT_N2)
    b = pl.pallas_call(
        _fft_s2_kernel,
        grid=(FFT_N1 // FFT_KB, pairs),
        in_specs=[blk,
                  pl.BlockSpec((2, FFT_KB, FFT_N2, LANES), lambda kb, p: (0, kb, 0, 0)),
                  pl.BlockSpec((2, FFT_KB, FFT_N2, HY_WIDTH), lambda kb, p: (0, kb, 0, 0)),
                  _full_spec(sq), _full_spec(sq), _full_spec(sq), _full_spec(sq)],
        out_specs=blk,
        out_shape=jax.ShapeDtypeStruct((pairs, 2, FFT_N1, FFT_PITCH, HY_WIDTH), F32),
        compiler_params=_cparams(2),
        name="hy_fft_s2",
    )(a, cst["tw"], g_spec, *cst["m2"], *cst["m2i"])
    b = b.reshape(pairs, 2, rows_a, HY_WIDTH)
    y = pl.pallas_call(
        _fft_s3_kernel,
        grid=(pairs, n_cb),
        in_specs=[a_spec, _full_spec((2 * FFT_N1H, 2 * FFT_N1)), _full_spec((2 * FFT_N1H, 2 * FFT_N1)),
                  z_spec, z_spec, pl.BlockSpec((1, LANES), lambda p, cb: (0, cb))],
        out_specs=z_spec,
        out_shape=jax.ShapeDtypeStruct((pairs, 2, rows_z, HY_WIDTH), F32),
        compiler_params=_cparams(2),
        name="hy_fft_s3",
    )(b, *cst["m3"], zv, xv, d_bias.reshape(1, HY_WIDTH))
    return y.reshape(batch, rows_z, HY_WIDTH)


def _hy_ctx_kernel(z_ref, x0_ref, g_ref, fhi_ref, flo_ref, ihi_ref, ilo_ref, bias_ref, y_ref):
    z = z_ref[0]
    x = _dot3(fhi_ref[...], flo_ref[...], z)
    xr, xi = x[:FFT_NC], x[FFT_NC:]
    gr, gi = g_ref[0], g_ref[1]
    y = jnp.concatenate([xr * gr - xi * gi, xr * gi + xi * gr], axis=0)
    conv = _dot3(ihi_ref[...], ilo_ref[...], y)
    y_ref[0] = ((conv + z * bias_ref[...]) * x0_ref[0]).astype(BF16)


def _hy_ctx_conv_call(z_ctx, x0_ctx, g_spec, d_bias):
    batch = z_ctx.shape[0]
    cst = _fft_constants()
    tok = pl.BlockSpec((1, CTX_LEN, HY_WIDTH), lambda b: (b, 0, 0))

    def full(shape):
        zeros = (0,) * len(shape)
        return pl.BlockSpec(shape, lambda b: zeros)

    return pl.pallas_call(
        _hy_ctx_kernel,
        grid=(batch,),
        in_specs=[tok, tok, full((2, FFT_NC, HY_WIDTH)),
                  full((2 * FFT_NC, CTX_LEN)), full((2 * FFT_NC, CTX_LEN)),
                  full((CTX_LEN, 2 * FFT_NC)), full((CTX_LEN, 2 * FFT_NC)), full((1, HY_WIDTH))],
        out_specs=tok,
        out_shape=jax.ShapeDtypeStruct((batch, CTX_LEN, HY_WIDTH), BF16),
        compiler_params=_cparams(1),
        name="hy_ctx",
    )(z_ctx, x0_ctx, g_spec, *cst["mcf"], *cst["mci"], d_bias.reshape(1, HY_WIDTH))


def _hyena_filter(length, fw1, fb1, fw2, fb2, fw3, fb3, fw4, freq):
    t = jnp.linspace(0.0, 1.0, length, dtype=F32)[:, None]
    w = 2.0 * math.pi * jnp.arange(length, dtype=F32)[:, None] / length
    bands = jnp.linspace(1e-4, HY_BANDS - 1, HY_BANDS, dtype=F32)
    emb = jnp.concatenate([t, jnp.cos(bands * w), -jnp.sin(bands * w)], axis=-1)
    emb = jnp.concatenate([emb, emb[::-1]], axis=0)
    hp = lax.Precision.HIGHEST
    h = jnp.sin(freq * (jnp.dot(emb, fw1, precision=hp) + fb1))
    h = jnp.sin(freq * (jnp.dot(h, fw2, precision=hp) + fb2))
    h = jnp.sin(freq * (jnp.dot(h, fw3, precision=hp) + fb3))
    max_decay = math.log(HY_TARGET) / HY_FAST_PCT
    min_decay = math.log(HY_TARGET) / HY_SLOW_PCT
    deltas = jnp.linspace(min_decay, max_decay, HY_WIDTH, dtype=F32)
    h_fwd = jnp.dot(h[:length], fw4[:, :HY_WIDTH], precision=hp).astype(F32) * jnp.exp(-t * jnp.abs(deltas))
    h_bwd_rev = (jnp.dot(h[length:], fw4[:, HY_WIDTH:], precision=hp).astype(F32)
                 * jnp.exp(-t[::-1] * jnp.abs(deltas)))
    return jnp.concatenate([h_fwd[:1] + h_bwd_rev[length - 1:], h_fwd[1:],
                            jnp.zeros((1, HY_WIDTH), F32), h_bwd_rev[:length - 1]], axis=0)


def _even_out_kernel(x_ref, mod_ref, a_ref, yl_ref, yc_ref, w_ref, o_ref):
    is_ctx = pl.program_id(1) == NT_LAT
    y_lat = jnp.concatenate([yl_ref[0, s, 0:FFT_N2, :] for s in range(TM // FFT_N2)], axis=0)
    y = jnp.where(is_ctx, yc_ref[0], y_lat).astype(BF16)
    o = _bdot(a_ref[0], w_ref[0:NA_WIDTH, :]) + _bdot(y, w_ref[NA_WIDTH:, :])
    o_ref[0] = x_ref[0] + mod_ref[0, 5:6, :] * o


def _even_out_call(xt, mod_i, a, y_lat, y_ctx, w):
    batch = xt.shape[0]
    return pl.pallas_call(
        _even_out_kernel,
        grid=(batch, NT_ALL),
        in_specs=[_tok_spec(D_MODEL), _mod_spec(batch), _tok_spec(NA_WIDTH),
                  pl.BlockSpec((1, TM // FFT_N2, FFT_PITCH, HY_WIDTH),
                               lambda b, j: (b, jnp.minimum(j, NT_LAT - 1), 0, 0)),
                  pl.BlockSpec((1, CTX_LEN, HY_WIDTH), lambda b, j: (b, 0, 0)),
                  _full_spec((D_MODEL, D_MODEL))],
        out_specs=_tok_spec(D_MODEL),
        out_shape=jax.ShapeDtypeStruct((batch, T_TOK, D_MODEL), F32),
        compiler_params=_cparams(2),
        name="even_out",
    )(xt, mod_i, a, y_lat.reshape(batch, FFT_N1H, FFT_PITCH, HY_WIDTH), y_ctx, w)


def _odd_in_kernel(x_ref, mod_ref, gain_ref, w_ref, cos_ref, sin_ref, q_ref, k_ref, v_ref, sg_ref):
    h = _modulated(x_ref[0], gain_ref[...], mod_ref, 3, 4).astype(BF16)
    cos = cos_ref[...]
    sin = sin_ref[...]

    def rope_store(dst_ref, col0, scale):
        p = _bdot(h, w_ref[:, col0:col0 + RET_QK])
        for c in range(RET_QK // LANES):
            t = p[:, c * LANES:(c + 1) * LANES]
            half = (c % 2) * LANES
            r = t * cos[:, half:half + LANES] + pltpu.roll(t, LANES // 2, 1) * sin[:, half:half + LANES]
            dst_ref[0, :, c * LANES:(c + 1) * LANES] = (r * scale).astype(BF16)

    rope_store(q_ref, 0, 1.0)
    rope_store(k_ref, RET_QK, RET_KEY_DIM ** -0.5)
    v_ref[0] = _bdot(h, w_ref[:, 2 * RET_QK:2 * RET_QK + RET_V]).astype(BF16)
    sg_ref[0] = _silu(_bdot(h, w_ref[:, 2 * RET_QK + RET_V:])).astype(BF16)


def _odd_in_call(xt, mod_i, gain, w, cos_t, sin_t):
    batch = xt.shape[0]
    rope_spec = pl.BlockSpec((TM, RET_KEY_DIM), lambda b, j: (j, 0))
    return pl.pallas_call(
        _odd_in_kernel,
        grid=(batch, NT_ALL),
        in_specs=[_tok_spec(D_MODEL), _mod_spec(batch), _full_spec((1, D_MODEL)),
                  _full_spec((D_MODEL, w.shape[1])), rope_spec, rope_spec],
        out_specs=[_tok_spec(RET_QK), _tok_spec(RET_QK), _tok_spec(RET_V), _tok_spec(RET_V)],
        out_shape=[jax.ShapeDtypeStruct((batch, T_TOK, RET_QK), BF16)] * 2
        + [jax.ShapeDtypeStruct((batch, T_TOK, RET_V), BF16)] * 2,
        compiler_params=_cparams(2),
        name="odd_in",
    )(xt, mod_i, gain, w, cos_t, sin_t)


def _rope_tables():
    t = np.arange(SEQ)
    n_freq = RET_KEY_DIM // 4
    inv = ROPE_BASE ** (-jnp.arange(n_freq, dtype=F32) / n_freq)
    ang_r = jnp.asarray(t // GRID_W, F32)[:, None] * inv
    ang_c = jnp.asarray(t % GRID_W, F32)[:, None] * inv
    cr, sr, cc, sc = jnp.cos(ang_r), jnp.sin(ang_r), jnp.cos(ang_c), jnp.sin(ang_c)
    cos_l = jnp.concatenate([cr, cr, cc, cc], axis=-1)
    sin_l = jnp.concatenate([-sr, sr, -sc, sc], axis=-1)
    cos_t = jnp.concatenate([cos_l, jnp.ones((CTX_LEN, RET_KEY_DIM), F32)], axis=0)
    sin_t = jnp.concatenate([sin_l, jnp.zeros((CTX_LEN, RET_KEY_DIM), F32)], axis=0)
    return cos_t, sin_t


def _ret_kernel(lg_ref, q_ref, k_ref, v_ref, sg_ref, y_ref, o_acc, state_f, state_b):
    head = pl.program_id(1)
    n_c = RET_BLOCK
    n_lat = SEQ // n_c
    assert CTX_LEN == n_c and n_lat % 2 == 0
    ii = lax.broadcasted_iota(jnp.int32, (n_c, n_c), 0).astype(F32)
    jj = lax.broadcasted_iota(jnp.int32, (n_c, n_c), 1).astype(F32)
    row_k = lax.broadcasted_iota(jnp.int32, (n_c, RET_KEY_DIM), 0).astype(F32)
    row_v = lax.broadcasted_iota(jnp.int32, (n_c, RET_VAL_DIM), 0).astype(F32)

    def decays(backward):
        lg = lg_ref[1 if backward else 0, head]
        if backward:
            diff = jj - ii
            xi = jnp.exp(lg * (n_c - row_v))
            zeta = jnp.exp(lg * row_k)
        else:
            diff = ii - jj
            xi = jnp.exp(lg * (row_v + 1.0))
            zeta = jnp.exp(lg * (n_c - 1.0 - row_k))
        dmask = jnp.where(diff >= 0, jnp.exp(lg * jnp.maximum(diff, 0.0)), 0.0)
        g_chunk = jnp.exp(lg * n_c + jnp.zeros((1, RET_VAL_DIM), F32))
        return dmask, xi, zeta, g_chunk

    def advance(chunk, state, consts, finalize):
        dmask, xi, zeta, g_chunk = consts
        r = chunk * n_c if isinstance(chunk, int) else pl.multiple_of(chunk * n_c, n_c)
        qc = q_ref[0, pl.ds(r, n_c), :]
        kc = k_ref[0, pl.ds(r, n_c), :]
        vc = v_ref[0, pl.ds(r, n_c), :]
        st = state[...]
        inner = lax.dot_general(qc, kc, _NT_DIMS, preferred_element_type=F32) * dmask
        o = _bdot(inner.astype(BF16), vc) + _bdot(qc, st.astype(BF16)) * xi
        kz = (kc.astype(F32) * zeta).astype(BF16)
        state[...] = st * g_chunk + lax.dot_general(kz, vc, _TN_DIMS, preferred_element_type=F32)
        if finalize:
            tot = o_acc[pl.ds(r, n_c), :] + o
            mu = jnp.mean(tot, axis=-1, keepdims=True)
            cen = tot - mu
            var = jnp.mean(cen * cen, axis=-1, keepdims=True)
            yn = cen * lax.rsqrt(var + GN_EPS)
            y_ref[0, pl.ds(r, n_c), :] = (sg_ref[0, pl.ds(r, n_c), :].astype(F32) * yn).astype(BF16)
        else:
            o_acc[pl.ds(r, n_c), :] = o

    consts_f = decays(False)
    consts_b = decays(True)
    state_f[...] = jnp.zeros_like(state_f)
    state_b[...] = jnp.zeros_like(state_b)

    def both(chunk_f, chunk_b, finalize):
        advance(chunk_f, state_f, consts_f, finalize)
        advance(chunk_b, state_b, consts_b, finalize)

    advance(n_lat, state_f, consts_f, False)
    advance(n_lat, state_b, consts_b, True)

    def first_half(s, carry):
        both(s, n_lat - 1 - s, False)
        return carry

    def second_half(s, carry):
        both(s, n_lat - 1 - s, True)
        return carry

    lax.fori_loop(0, n_lat // 2, first_half, 0, unroll=2)
    lax.fori_loop(n_lat // 2, n_lat, second_half, 0, unroll=2)


def _ret_call(lg, q, k, v, sg):
    batch = q.shape[0]
    qk_spec = pl.BlockSpec((1, T_TOK, RET_KEY_DIM), lambda b, h: (b, 0, h))
    v_spec = pl.BlockSpec((1, T_TOK, RET_VAL_DIM), lambda b, h: (b, 0, h))
    return pl.pallas_call(
        _ret_kernel,
        grid=(batch, RET_HEADS),
        in_specs=[pl.BlockSpec(memory_space=pltpu.SMEM), qk_spec, qk_spec, v_spec, v_spec],
        out_specs=v_spec,
        out_shape=jax.ShapeDtypeStruct((batch, T_TOK, RET_V), BF16),
        scratch_shapes=[pltpu.VMEM((T_TOK, RET_VAL_DIM), F32),
                        pltpu.VMEM((RET_KEY_DIM, RET_VAL_DIM), F32),
                        pltpu.VMEM((RET_KEY_DIM, RET_VAL_DIM), F32)],
        compiler_params=_cparams(2),
        name="retention",
    )(lg, q, k, v, sg)


def _odd_out_kernel(x_ref, mod_ref, y_ref, w_ref, o_ref):
    o_ref[0] = x_ref[0] + mod_ref[0, 5:6, :] * _bdot(y_ref[0], w_ref[...])


def _odd_out_call(xt, mod_i, y, w):
    batch = xt.shape[0]
    return pl.pallas_call(
        _odd_out_kernel,
        grid=(batch, NT_ALL),
        in_specs=[_tok_spec(D_MODEL), _mod_spec(batch), _tok_spec(RET_V), _full_spec((RET_V, D_MODEL))],
        out_specs=_tok_spec(D_MODEL),
        out_shape=jax.ShapeDtypeStruct((batch, T_TOK, D_MODEL), F32),
        compiler_params=_cparams(2),
        name="odd_out",
    )(xt, mod_i, y, w)


def _fft_s2_fwd_kernel(a_ref, tw_ref, fhi_ref, flo_ref, g_ref):
    reps = HY_WIDTH // LANES
    for kk in range(FFT_KB):
        twr = jnp.concatenate([tw_ref[0, kk]] * reps, axis=1)
        twi = jnp.concatenate([tw_ref[1, kk]] * reps, axis=1)
        ar = a_ref[0, kk]
        ai = a_ref[1, kk]
        d = jnp.concatenate([ar * twr - ai * twi, ar * twi + ai * twr], axis=0)
        x = _dot3(fhi_ref[...], flo_ref[...], d)
        g_ref[0, kk] = x[:FFT_N2]
        g_ref[1, kk] = x[FFT_N2:]


def _dft_dense_kernel(x_ref, mhi_ref, mlo_ref, o_ref):
    o_ref[...] = _dot3(mhi_ref[...], mlo_ref[...], x_ref[...])


def _filter_spectra(filter_params):
    cst = _fft_constants()
    n_lane = FFT_N2 * HY_WIDTH
    filt = _hyena_filter(SEQ, *filter_params).reshape(1, 1, FFT_N1, n_lane)
    a = pl.pallas_call(
        _fft_s1_flat_kernel,
        grid=(1, n_lane // FFT_W),
        in_specs=[pl.BlockSpec((1, 1, FFT_N1, FFT_W), lambda p, w: (p, 0, 0, w)),
                  _full_spec((2 * FFT_N1, FFT_N1)), _full_spec((2 * FFT_N1, FFT_N1))],
        out_specs=pl.BlockSpec((1, 2 * FFT_N1, FFT_W), lambda p, w: (p, 0, w)),
        out_shape=jax.ShapeDtypeStruct((1, 2 * FFT_N1, n_lane), F32),
        compiler_params=_cparams(2),
        name="filt_fft_s1",
    )(filt, *cst["m1r"])
    a = a.reshape(2, FFT_N1, FFT_N2, HY_WIDTH)
    blk = pl.BlockSpec((2, FFT_KB, FFT_N2, HY_WIDTH), lambda kb: (0, kb, 0, 0))
    sq = (2 * FFT_N2, 2 * FFT_N2)
    g_lat = pl.pallas_call(
        _fft_s2_fwd_kernel,
        grid=(FFT_N1 // FFT_KB,),
        in_specs=[blk, pl.BlockSpec((2, FFT_KB, FFT_N2, LANES), lambda kb: (0, kb, 0, 0)),
                  pl.BlockSpec(sq, lambda kb: (0, 0)), pl.BlockSpec(sq, lambda kb: (0, 0))],
        out_specs=blk,
        out_shape=jax.ShapeDtypeStruct((2, FFT_N1, FFT_N2, HY_WIDTH), F32),
        compiler_params=_cparams(1),
        name="filt_fft_s2",
    )(a, cst["tw"], *cst["m2"])
    filt_ctx = _hyena_filter(CTX_LEN, *filter_params)
    g_ctx = pl.pallas_call(
        _dft_dense_kernel,
        out_shape=jax.ShapeDtypeStruct((2 * FFT_NC, HY_WIDTH), F32),
        compiler_params=pltpu.CompilerParams(vmem_limit_bytes=VMEM_LIMIT_BYTES),
        name="filt_dft_ctx",
    )(filt_ctx, *cst["mcf_full"])
    return g_lat, g_ctx.reshape(2, FFT_NC, HY_WIDTH)


def _even_mixer(xt, mod_i, gain, w_in, w_out, q_gain, k_gain, rpb, conv_w, conv_b, filter_params, d_bias):
    scale = NA_HEAD_DIM ** -0.5
    qg = (jnp.tile(q_gain, NA_HEADS) * scale).reshape(1, NA_WIDTH)
    kg = jnp.tile(k_gain, NA_HEADS).reshape(1, NA_WIDTH)
    head_of = np.arange(NA_WIDTH) // NA_HEAD_DIM
    bd = jnp.asarray((head_of[:, None] == head_of[None, :]) / NA_HEAD_DIM, BF16)
    q, k, v, u = _even_in_call(xt, mod_i, gain, w_in.astype(BF16), qg, kg, bd)
    a = _na_call(q, k, v, _na_bias_table(rpb))
    z_lat, x0_lat, z_ctx, x0_ctx = _hy_pre_call(u, conv_w, conv_b.reshape(1, -1))
    g_lat, g_ctx = _filter_spectra(filter_params)
    y_lat = _hy_long_conv_call(z_lat, x0_lat, g_lat, d_bias)
    y_ctx = _hy_ctx_conv_call(z_ctx, x0_ctx, g_ctx, d_bias)
    return _even_out_call(xt, mod_i, a, y_lat, y_ctx, w_out.astype(BF16))


def _odd_mixer(xt, mod_i, gain, w_in, w_out, logit_f, logit_b, rope):
    q, k, v, sg = _odd_in_call(xt, mod_i, gain, w_in.astype(BF16), *rope)
    lg = jnp.stack([jax.nn.log_sigmoid(logit_f.astype(F32)), jax.nn.log_sigmoid(logit_b.astype(F32))])
    y = _ret_call(lg, q, k, v, sg)
    return _odd_out_call(xt, mod_i, y, w_out.astype(BF16))


def kernel(x, c, ctx, c_ctx, w_mod, b_mod, norm_gain, ffn_a_in, ffn_a_out, ffn_b_in, ffn_b_out,
           even_in, even_out, na_q_gain, na_k_gain, na_rpb, hy_conv_w, hy_conv_b,
           hy_fw1, hy_fb1, hy_fw2, hy_fb2, hy_fw3, hy_fb3, hy_fw4, hy_freq, hy_bias,
           ret_in, ret_out, ret_logit_f, ret_logit_b):
    batch = x.shape[0]
    assert x.shape == (batch, SEQ, D_MODEL) and ctx.shape == (batch, CTX_LEN, D_MODEL)
    assert batch % 2 == 0 and batch < MOD_ROWS
    c_rows = jnp.concatenate([c, c_ctx[None], jnp.zeros((MOD_ROWS - batch - 1, D_MODEL), F32)], axis=0)
    mod_all = _mod_call(c_rows, w_mod, b_mod).reshape(DEPTH, MOD_ROWS, N_MOD, D_MODEL)
    rope = _rope_tables()
    for i in range(DEPTH):
        last = i == DEPTH - 1
        mod_i = mod_all[i]
        gains = norm_gain[i].reshape(3, 1, D_MODEL)
        w_a = (ffn_a_in[i].astype(BF16), ffn_a_out[i].astype(BF16))
        if i == 0:
            xt = _ffn_split_call(x, ctx, mod_i, gains[0], *w_a, (0, 1, 2))
        else:
            xt = _ffn_call(xt, mod_i, gains[0], *w_a, (0, 1, 2), NT_ALL)
        if i % 2 == 0:
            e = i // 2
            filter_params = (hy_fw1[e], hy_fb1[e], hy_fw2[e], hy_fb2[e], hy_fw3[e], hy_fb3[e],
                             hy_fw4[e], hy_freq[e])
            xt = _even_mixer(xt, mod_i, gains[1], even_in[e], even_out[e], na_q_gain[e], na_k_gain[e],
                             na_rpb[e], hy_conv_w[e], hy_conv_b[e], filter_params, hy_bias[e])
        else:
            o = i // 2
            xt = _odd_mixer(xt, mod_i, gains[1], ret_in[o], ret_out[o], ret_logit_f[o], ret_logit_b[o], rope)
        xt = _ffn_call(xt, mod_i, gains[2], ffn_b_in[i].astype(BF16), ffn_b_out[i].astype(BF16),
                       (6, 7, 8), NT_LAT if last else NT_ALL)
    return xt
```

```python
import functools
import math

import ml_dtypes
import numpy as np
import jax
import jax.numpy as jnp
from jax import lax
from jax.experimental import pallas as pl
from jax.experimental.pallas import tpu as pltpu

F32 = jnp.float32
BF16 = jnp.bfloat16

D_MODEL = 1024
SEQ = 4096
DEPTH = 4
GRID_W = 64
CTX_LEN = 256
N_MOD = 9
RMS_EPS = 1e-6
GN_EPS = 1e-6
D_FF = 2816
NA_HEADS = 8
NA_HEAD_DIM = 64
NA_WIDTH = NA_HEADS * NA_HEAD_DIM
WIN_R = 8
WIN_C = 16
HY_WIDTH = D_MODEL - NA_WIDTH
HY_BANDS = 8
HY_TARGET = 1e-2
HY_FAST_PCT = 0.3
HY_SLOW_PCT = 1.5
RET_HEADS = 4
RET_KEY_DIM = D_MODEL // RET_HEADS
RET_VAL_DIM = 2 * RET_KEY_DIM
RET_QK = RET_HEADS * RET_KEY_DIM
RET_V = RET_HEADS * RET_VAL_DIM
ROPE_BASE = 10000.0

LANES = 128
VMEM_LIMIT_BYTES = 56 * 2**20
T_TOK = SEQ + CTX_LEN
TM = CTX_LEN
NT_LAT = SEQ // TM
NT_ALL = T_TOK // TM
MOD_ROWS = 16
FF_CHUNKS = ((0, 1536), (1536, 1280))
RET_BLOCK = 256

NA_RB = 4
NA_KR = NA_RB + WIN_R - 1
GRID_H = SEQ // GRID_W
NA_NQ = NA_RB * GRID_W
NA_NK = NA_KR * GRID_W
NEG_BIG = -1e30

FFT_N = 2 * SEQ
FFT_N1 = 64
FFT_N2 = 128
FFT_N1H = FFT_N1 // 2
FFT_W = 2048
FFT_KB = 4
FFT_PITCH = FFT_N2 + 8
FFT_NC = 2 * CTX_LEN


def _cparams(n_axes):
    return pltpu.CompilerParams(dimension_semantics=("arbitrary",) * n_axes,
                                vmem_limit_bytes=VMEM_LIMIT_BYTES)


def _bdot(a, b):
    return jnp.dot(a, b, preferred_element_type=F32)


_NT_DIMS = (((1,), (1,)), ((), ()))
_TN_DIMS = (((0,), (0,)), ((), ()))


def _split_hi_lo(m):
    hi = m.astype(BF16)
    lo = (m - hi.astype(F32)).astype(BF16)
    return hi, lo


def _dot3(m_hi, m_lo, d):
    d_hi, d_lo = _split_hi_lo(d)
    return _bdot(m_hi, d_hi) + _bdot(m_lo, d_hi) + _bdot(m_hi, d_lo)


def _modulated(x, gain, mod_ref, shift_row, scale_row):
    ms = jnp.mean(x * x, axis=-1, keepdims=True)
    y = x * lax.rsqrt(ms + RMS_EPS) * gain
    return (y * (1.0 + mod_ref[0, scale_row:scale_row + 1, :])
            + mod_ref[0, shift_row:shift_row + 1, :])


def _silu(a):
    return a * jax.nn.sigmoid(a)


def _mod_kernel(c_ref, w_ref, b_ref, o_ref):
    s = _silu(c_ref[...])
    o_ref[0] = jnp.dot(s, w_ref[0], precision=lax.Precision.HIGHEST,
                       preferred_element_type=F32) + b_ref[0]


def _mod_call(c_rows, w_mod, b_mod):
    depth, d, n = w_mod.shape
    tn = 1024
    return pl.pallas_call(
        _mod_kernel,
        grid=(depth, n // tn),
        in_specs=[pl.BlockSpec((MOD_ROWS, d), lambda i, j: (0, 0)),
                  pl.BlockSpec((1, d, tn), lambda i, j: (i, 0, j)),
                  pl.BlockSpec((1, 1, tn), lambda i, j: (i, 0, j))],
        out_specs=pl.BlockSpec((1, MOD_ROWS, tn), lambda i, j: (i, 0, j)),
        out_shape=jax.ShapeDtypeStruct((depth, MOD_ROWS, n), F32),
        compiler_params=_cparams(2),
        name="mod",
    )(c_rows, w_mod, b_mod.reshape(depth, 1, n))


def _tok_spec(width):
    return pl.BlockSpec((1, TM, width), lambda b, j: (b, j, 0))


def _mod_spec(batch):
    return pl.BlockSpec((1, N_MOD, D_MODEL), lambda b, j: (jnp.where(j == NT_LAT, batch, b), 0, 0))


def _full_spec(shape):
    zeros = (0,) * len(shape)
    return pl.BlockSpec(shape, lambda b, j: zeros)


_FFN_SOURCES = {"stream": 1, "split": 2, "even": 5, "odd": 3}


def _ffn_kernel(*refs, rows, source, n_tiles, n_all):
    shift_row, scale_row, gate_row = rows
    n_src = _FFN_SOURCES[source]
    src = refs[:n_src]
    modn_ref, modc_ref, gain_ref, win_ref, wout_ref, o_ref, h_a, x_a, h_b, x_b = refs[n_src:]
    step = pl.program_id(0)
    next_is_ctx = jnp.minimum(step, n_all - 1) % n_tiles == NT_LAT

    def next_tile():
        if source == "stream":
            return src[0][0]
        if source == "split":
            return jnp.where(next_is_ctx, src[1][0], src[0][0])
        if source == "even":
            x_ref, a_ref, yl_ref, yc_ref, wo_ref = src
            y_lat = jnp.concatenate([yl_ref[0, s, 0:FFT_N2, :] for s in range(TM // FFT_N2)], axis=0)
            y = jnp.where(next_is_ctx, yc_ref[0], y_lat).astype(BF16)
            o = _bdot(a_ref[0], wo_ref[0:NA_WIDTH, :]) + _bdot(y, wo_ref[NA_WIDTH:, :])
            return x_ref[0] + modn_ref[0, 5:6, :] * o
        x_ref, y_ref, wo_ref = src
        return x_ref[0] + modn_ref[0, 5:6, :] * _bdot(y_ref[0], wo_ref[...])

    def body(h_read, x_read, h_write, x_write):
        x_next = next_tile()
        h_write[...] = _modulated(x_next, gain_ref[...], modn_ref, shift_row, scale_row).astype(BF16)
        x_write[...] = x_next
        h = h_read[...]
        acc = jnp.zeros((TM, D_MODEL), F32)
        for start, size in FF_CHUNKS:
            a = _bdot(h, win_ref[:, start:start + size])
            b = _bdot(h, win_ref[:, D_FF + start:D_FF + start + size])
            g = (_silu(a) * b).astype(BF16)
            acc = acc + _bdot(g, wout_ref[start:start + size, :])
        o_ref[0] = x_read[...] + (0.5 * modc_ref[0, gate_row:gate_row + 1, :]) * acc

    @pl.when(step == 0)
    def _():
        h_b[...] = jnp.zeros_like(h_b)
        x_b[...] = jnp.zeros_like(x_b)

    @pl.when(step % 2 == 0)
    def _():
        body(h_b, x_b, h_a, x_a)

    @pl.when(step % 2 == 1)
    def _():
        body(h_a, x_a, h_b, x_b)


def _ffn_call(source, srcs, mod_i, gain, w_in, w_out, rows, n_tiles, batch):
    n_all = batch * n_tiles

    def next_tile(s):
        t = jnp.minimum(s, n_all - 1)
        return t // n_tiles, t % n_tiles

    def cur_tile(s):
        t = jnp.maximum(s - 1, 0)
        return t // n_tiles, t % n_tiles

    def mod_spec(tile_of):
        def index(s):
            b, j = tile_of(s)
            return jnp.where(j == NT_LAT, batch, b), 0, 0
        return pl.BlockSpec((1, N_MOD, D_MODEL), index)

    def tok(width):
        return pl.BlockSpec((1, TM, width), lambda s: (*next_tile(s), 0))

    def lat_only(block, n_trailing):
        def index(s):
            b, j = next_tile(s)
            return (b, jnp.minimum(j, NT_LAT - 1)) + (0,) * n_trailing
        return pl.BlockSpec(block, index)

    def per_sample(block):
        return pl.BlockSpec(block, lambda s: (next_tile(s)[0],) + (0,) * (len(block) - 1))

    def full(shape):
        zeros = (0,) * len(shape)
        return pl.BlockSpec(shape, lambda s: zeros)

    if source == "stream":
        src_specs = [tok(D_MODEL)]
    elif source == "split":
        src_specs = [lat_only((1, TM, D_MODEL), 1), per_sample((1, CTX_LEN, D_MODEL))]
    elif source == "even":
        src_specs = [tok(D_MODEL), tok(NA_WIDTH), lat_only((1, TM // FFT_N2, FFT_PITCH, HY_WIDTH), 2),
                     per_sample((1, CTX_LEN, HY_WIDTH)), full((D_MODEL, D_MODEL))]
    else:
        src_specs = [tok(D_MODEL), tok(RET_V), full((RET_V, D_MODEL))]
    return pl.pallas_call(
        functools.partial(_ffn_kernel, rows=rows, source=source, n_tiles=n_tiles, n_all=n_all),
        grid=(n_all + 1,),
        in_specs=src_specs + [mod_spec(next_tile), mod_spec(cur_tile), full((1, D_MODEL)),
                              full((D_MODEL, 2 * D_FF)), full((D_FF, D_MODEL))],
        out_specs=pl.BlockSpec((1, TM, D_MODEL), lambda s: (*cur_tile(s), 0)),
        out_shape=jax.ShapeDtypeStruct((batch, n_tiles * TM, D_MODEL), F32),
        scratch_shapes=[pltpu.VMEM((TM, D_MODEL), BF16), pltpu.VMEM((TM, D_MODEL), F32),
                        pltpu.VMEM((TM, D_MODEL), BF16), pltpu.VMEM((TM, D_MODEL), F32)],
        compiler_params=_cparams(1),
        name="ffn_" + source,
    )(*srcs, mod_i, mod_i, gain, w_in, w_out)


def _even_in_kernel(x_ref, mod_ref, gain_ref, w_ref, qg_ref, kg_ref, bd_ref,
                    q_ref, k_ref, v_ref, u_ref):
    h = _modulated(x_ref[0], gain_ref[...], mod_ref, 3, 4).astype(BF16)

    def head_norm(t, g):
        ms = _bdot((t * t).astype(BF16), bd_ref[...])
        return (t * lax.rsqrt(ms + RMS_EPS) * g).astype(BF16)

    q_ref[0] = head_norm(_bdot(h, w_ref[:, 0:NA_WIDTH]), qg_ref[...])
    k_ref[0] = head_norm(_bdot(h, w_ref[:, NA_WIDTH:2 * NA_WIDTH]), kg_ref[...])
    v_ref[0] = _bdot(h, w_ref[:, 2 * NA_WIDTH:3 * NA_WIDTH]).astype(BF16)
    u_ref[0] = _bdot(h, w_ref[:, 3 * NA_WIDTH:])


def _even_in_call(xt, mod_i, gain, w, q_gain, k_gain, bd):
    batch = xt.shape[0]
    n_in = w.shape[1]
    return pl.pallas_call(
        _even_in_kernel,
        grid=(batch, NT_ALL),
        in_specs=[_tok_spec(D_MODEL), _mod_spec(batch), _full_spec((1, D_MODEL)),
                  _full_spec((D_MODEL, n_in)), _full_spec((1, NA_WIDTH)), _full_spec((1, NA_WIDTH)),
                  _full_spec((NA_WIDTH, NA_WIDTH))],
        out_specs=[_tok_spec(NA_WIDTH), _tok_spec(NA_WIDTH), _tok_spec(NA_WIDTH),
                   _tok_spec(3 * HY_WIDTH)],
        out_shape=[jax.ShapeDtypeStruct((batch, T_TOK, NA_WIDTH), BF16)] * 3
        + [jax.ShapeDtypeStruct((batch, T_TOK, 3 * HY_WIDTH), F32)],
        compiler_params=_cparams(2),
        name="even_in",
    )(xt, mod_i, gain, w, q_gain, k_gain, bd)


def _na_kernel(q_ref, k_ref, v_ref, bias_ref, o_ref):
    lane = lax.broadcasted_iota(jnp.int32, (1, LANES), 1)
    first_head = lane < NA_HEAD_DIM
    k_ctx = k_ref[0, SEQ:T_TOK, :]
    v_ctx = v_ref[0, SEQ:T_TOK, :]

    def stack_heads(q):
        zero = jnp.zeros_like(q)
        return jnp.concatenate([jnp.where(first_head, q, zero), jnp.where(first_head, zero, q)], axis=0)

    def attend(qs, scores_and_values):
        s_list = [lax.dot_general(qs, kk, _NT_DIMS, preferred_element_type=F32) if bias is None
                  else lax.dot_general(qs, kk, _NT_DIMS, preferred_element_type=F32) + bias
                  for kk, _, bias in scores_and_values]
        m = functools.reduce(jnp.maximum, [jnp.max(s, axis=-1, keepdims=True) for s in s_list])
        o = functools.reduce(jnp.add, [
            _bdot(jnp.exp(s - m).astype(BF16),
                  jnp.concatenate([vv, jnp.ones((vv.shape[0], LANES), BF16)], axis=1))
            for s, (_, vv, _) in zip(s_list, scores_and_values)])
        o = o[:, :LANES] / o[:, LANES:]
        n = qs.shape[0] // 2
        return jnp.where(first_head, o[:n], o[n:]).astype(BF16)

    def block(i, carry):
        r0 = i * NA_RB
        u0 = jnp.clip(r0 - WIN_R // 2, 0, GRID_H - NA_KR)
        pattern = jnp.where(i == 0, 0, jnp.where(i == GRID_H // NA_RB - 1, 2, 1))
        q0 = pl.multiple_of(r0 * GRID_W, NA_NQ)
        k0 = pl.multiple_of(u0 * GRID_W, GRID_W)
        qs = stack_heads(q_ref[0, pl.ds(q0, NA_NQ), :])
        k_win = k_ref[0, pl.ds(k0, NA_NK), :]
        v_win = v_ref[0, pl.ds(k0, NA_NK), :]
        o_ref[0, pl.ds(q0, NA_NQ), :] = attend(
            qs, [(k_win, v_win, bias_ref[pattern, 0]), (k_ctx, v_ctx, None)])
        return carry

    lax.fori_loop(0, GRID_H // NA_RB, block, 0, unroll=4)
    o_ref[0, SEQ:T_TOK, :] = attend(stack_heads(q_ref[0, SEQ:T_TOK, :]), [(k_ctx, v_ctx, None)])


def _na_call(q, k, v, bias):
    batch = q.shape[0]
    n_pairs = NA_WIDTH // LANES
    spec = pl.BlockSpec((1, T_TOK, LANES), lambda b, p: (b, 0, p))
    return pl.pallas_call(
        _na_kernel,
        grid=(batch, n_pairs),
        in_specs=[spec, spec, spec,
                  pl.BlockSpec((3, 1, 2 * NA_NQ, NA_NK), lambda b, p: (0, p, 0, 0))],
        out_specs=spec,
        out_shape=jax.ShapeDtypeStruct((batch, T_TOK, NA_WIDTH), BF16),
        compiler_params=_cparams(2),
        name="na_attn",
    )(q, k, v, bias)


def _na_bias_table(rpb):
    j = np.arange(NA_RB)[:, None, None, None]
    c = np.arange(GRID_W)[None, :, None, None]
    kk = np.arange(NA_KR)[None, None, :, None]
    kc = np.arange(GRID_W)[None, None, None, :]
    cs = np.clip(c - WIN_C // 2, 0, GRID_W - WIN_C)
    col_ok = (kc >= cs) & (kc < cs + WIN_C)
    dc = np.clip(kc - c + WIN_C - 1, 0, 2 * WIN_C - 2)[0, :, 0, :]
    col_sel = (dc[..., None] == np.arange(2 * WIN_C - 1)).astype(np.float32)
    shape = (NA_RB, GRID_W, NA_KR, GRID_W)
    oks, row_sels = [], []
    for off, rs_rel in ((0, 0 * j), (-(WIN_R // 2), j), (-(NA_KR - NA_RB), NA_KR - WIN_R + 0 * j)):
        row_ok = (kk >= rs_rel) & (kk < rs_rel + WIN_R)
        dr = np.clip(off + kk - j + WIN_R - 1, 0, 2 * WIN_R - 2)[:, 0, :, 0]
        row_sels.append((dr[..., None] == np.arange(2 * WIN_R - 1)).astype(np.float32))
        oks.append(np.broadcast_to(row_ok & col_ok, shape))
    vals = jnp.einsum("pjkr,hrc,qmc->phjqkm", np.stack(row_sels), rpb.astype(F32), col_sel,
                      precision=lax.Precision.HIGHEST)
    table = jnp.where(np.stack(oks)[:, None], vals, NEG_BIG)
    return table.reshape(3, NA_HEADS // 2, 2 * NA_NQ, NA_NK)


def _hy_pre_kernel(u0_ref, u1_ref, uv_ref, w0_ref, w1_ref, wv_ref, b0_ref, b1_ref, bv_ref,
                   zl_ref, xl_ref, zc_ref, xc_ref):
    row = lax.broadcasted_iota(jnp.int32, (TM, LANES), 0)
    zero_row = jnp.zeros((1, LANES), F32)

    def conv(u_ref, w_ref, b_ref, tile):
        s = tile * TM
        cur = u_ref[0, s:s + TM, :]
        seq_start = tile in (0, NT_LAT)
        seq_end = tile in (NT_LAT - 1, NT_LAT)
        prev_row = zero_row if seq_start else u_ref[0, s - 1:s, :]
        next_row = zero_row if seq_end else u_ref[0, s + TM:s + TM + 1, :]
        before = jnp.where(row == 0, prev_row, pltpu.roll(cur, 1, 0))
        after = jnp.where(row == TM - 1, next_row, pltpu.roll(cur, TM - 1, 0))
        return before * w_ref[0:1, :] + cur * w_ref[1:2, :] + after * w_ref[2:3, :] + b_ref[...]

    for tile in range(NT_ALL):
        x0 = conv(u0_ref, w0_ref, b0_ref, tile)
        z = conv(uv_ref, wv_ref, bv_ref, tile) * conv(u1_ref, w1_ref, b1_ref, tile)
        if tile < NT_LAT:
            for half in range(TM // FFT_N2):
                base = (tile * (TM // FFT_N2) + half) * FFT_PITCH
                zl_ref[0, base:base + FFT_N2, :] = z[half * FFT_N2:(half + 1) * FFT_N2]
                xl_ref[0, base:base + FFT_N2, :] = x0[half * FFT_N2:(half + 1) * FFT_N2]
                pad = jnp.zeros((FFT_PITCH - FFT_N2, LANES), F32)
                zl_ref[0, base + FFT_N2:base + FFT_PITCH, :] = pad
                xl_ref[0, base + FFT_N2:base + FFT_PITCH, :] = pad
        else:
            zc_ref[0] = z
            xc_ref[0] = x0


def _hy_pre_call(u, conv_w, conv_b):
    batch = u.shape[0]
    nb = HY_WIDTH // LANES
    u_specs = [pl.BlockSpec((1, T_TOK, LANES), lambda b, cb, g=g: (b, 0, g * nb + cb)) for g in range(3)]
    w_specs = [pl.BlockSpec((3, LANES), lambda b, cb, g=g: (0, g * nb + cb)) for g in range(3)]
    b_specs = [pl.BlockSpec((1, LANES), lambda b, cb, g=g: (0, g * nb + cb)) for g in range(3)]
    lat = pl.BlockSpec((1, FFT_N1H * FFT_PITCH, LANES), lambda b, cb: (b, 0, cb))
    ctx = pl.BlockSpec((1, CTX_LEN, LANES), lambda b, cb: (b, 0, cb))
    return pl.pallas_call(
        _hy_pre_kernel,
        grid=(batch, nb),
        in_specs=u_specs + w_specs + b_specs,
        out_specs=[lat, lat, ctx, ctx],
        out_shape=[jax.ShapeDtypeStruct((batch, FFT_N1H * FFT_PITCH, HY_WIDTH), F32)] * 2
        + [jax.ShapeDtypeStruct((batch, CTX_LEN, HY_WIDTH), F32)] * 2,
        compiler_params=_cparams(2),
        name="hy_pre",
    )(u, u, u, conv_w, conv_w, conv_w, conv_b, conv_b, conv_b)


def _complex_block(re, im):
    return np.block([[re, -im], [im, re]])


@functools.lru_cache(maxsize=None)
def _fft_constants():
    k1 = np.arange(FFT_N1)
    n1 = np.arange(FFT_N1H)
    f1 = np.exp(-2j * np.pi * np.outer(k1, n1) / FFT_N1)
    m1 = _complex_block(f1.real, f1.imag)
    f1_full = np.exp(-2j * np.pi * np.outer(k1, k1) / FFT_N1)
    m1r = np.concatenate([f1_full.real, f1_full.imag], axis=0)
    n2 = np.arange(FFT_N2)
    tw = np.exp(-2j * np.pi * np.outer(k1, n2) / FFT_N)
    tw = np.stack([tw.real, tw.imag])[..., None] * np.ones((1, 1, 1, LANES))
    f2 = np.exp(-2j * np.pi * np.outer(n2, n2) / FFT_N2)
    m2 = _complex_block(f2.real, f2.imag)
    m2i = _complex_block(f2.real, -f2.imag)
    c1 = np.exp(2j * np.pi * np.outer(n1, k1) / FFT_N1) / FFT_N
    m3 = _complex_block(c1.real, c1.imag)
    kc = np.arange(FFT_NC)
    nc = np.arange(CTX_LEN)
    ang = 2 * np.pi * np.outer(kc, nc) / FFT_NC
    mcf = np.concatenate([np.cos(ang), -np.sin(ang)], axis=0)
    ang_full = 2 * np.pi * np.outer(kc, kc) / FFT_NC
    mcf_full = np.concatenate([np.cos(ang_full), -np.sin(ang_full)], axis=0)
    mci = np.concatenate([np.cos(ang.T), -np.sin(ang.T)], axis=1) / FFT_NC

    def hi_lo(m):
        hi = m.astype(ml_dtypes.bfloat16)
        lo = (m - hi.astype(np.float64)).astype(ml_dtypes.bfloat16)
        return hi, lo

    return dict(m1=hi_lo(m1), m1r=hi_lo(m1r), m2=hi_lo(m2), m2i=hi_lo(m2i), m3=hi_lo(m3),
                mcf=hi_lo(mcf), mcf_full=hi_lo(mcf_full), mci=hi_lo(mci), tw=tw.astype(np.float32))


def _fft_s1_flat_kernel(z_ref, mhi_ref, mlo_ref, a_ref):
    d = z_ref[0].reshape(-1, FFT_W)
    a_ref[0] = _dot3(mhi_ref[...], mlo_ref[...], d)


def _slab_rows(n2, n_slabs):
    return pl.ds(n2, n_slabs, stride=FFT_PITCH)


def _fft_s1_kernel(z_ref, m_ref, a_ref):
    def step(i, carry):
        cols = []
        for dn in range(2):
            rows = _slab_rows(2 * i + dn, FFT_N1H)
            cols.append(jnp.concatenate([z_ref[0, 0, rows, :], z_ref[0, 1, rows, :]], axis=0))
        r = _bdot(m_ref[...], jnp.concatenate(cols, axis=1).astype(BF16))
        for dn in range(2):
            rows = _slab_rows(2 * i + dn, FFT_N1)
            a_ref[0, 0, rows, :] = r[:FFT_N1, dn * LANES:(dn + 1) * LANES]
            a_ref[0, 1, rows, :] = r[FFT_N1:, dn * LANES:(dn + 1) * LANES]
        return carry

    lax.fori_loop(0, FFT_N2 // 2, step, 0, unroll=4)
    for pad_row in range(FFT_N2, FFT_PITCH):
        for part in range(2):
            a_ref[0, part, _slab_rows(pad_row, FFT_N1), :] = jnp.zeros((FFT_N1, LANES), F32)


def _fft_s2_kernel(a_ref, tw_ref, g_ref, fwd_ref, inv_ref, o_ref):
    reps = HY_WIDTH // LANES
    pad = jnp.zeros((FFT_PITCH - FFT_N2, HY_WIDTH), F32)
    for kk in range(FFT_KB):
        twr = jnp.concatenate([tw_ref[0, kk]] * reps, axis=1)
        twi = jnp.concatenate([tw_ref[1, kk]] * reps, axis=1)
        ar = a_ref[0, 0, kk, 0:FFT_N2, :]
        ai = a_ref[0, 1, kk, 0:FFT_N2, :]
        d = jnp.concatenate([ar * twr - ai * twi, ar * twi + ai * twr], axis=0)
        x = _bdot(fwd_ref[...], d.astype(BF16))
        xr, xi = x[:FFT_N2], x[FFT_N2:]
        gr, gi = g_ref[0, kk], g_ref[1, kk]
        y = jnp.concatenate([xr * gr - xi * gi, xr * gi + xi * gr], axis=0)
        b = _bdot(inv_ref[...], y.astype(BF16))
        br, bi = b[:FFT_N2], b[FFT_N2:]
        o_ref[0, 0, kk, 0:FFT_N2, :] = br * twr + bi * twi
        o_ref[0, 1, kk, 0:FFT_N2, :] = bi * twr - br * twi
        o_ref[0, 0, kk, FFT_N2:FFT_PITCH, :] = pad
        o_ref[0, 1, kk, FFT_N2:FFT_PITCH, :] = pad


def _fft_s3_kernel(b_ref, m_ref, z_ref, x0_ref, bias_ref, y_ref):
    def step(i, carry):
        cols = []
        for dn in range(2):
            rows = _slab_rows(2 * i + dn, FFT_N1)
            cols.append(jnp.concatenate([b_ref[0, 0, rows, :], b_ref[0, 1, rows, :]], axis=0))
        y = _bdot(m_ref[...], jnp.concatenate(cols, axis=1).astype(BF16))
        for dn in range(2):
            rows = _slab_rows(2 * i + dn, FFT_N1H)
            for s in range(2):
                conv = y[s * FFT_N1H:(s + 1) * FFT_N1H, dn * LANES:(dn + 1) * LANES]
                y_ref[0, s, rows, :] = (conv + z_ref[0, s, rows, :] * bias_ref[...]) * x0_ref[0, s, rows, :]
        return carry

    lax.fori_loop(0, FFT_N2 // 2, step, 0, unroll=4)
    for pad_row in range(FFT_N2, FFT_PITCH):
        for s in range(2):
            y_ref[0, s, _slab_rows(pad_row, FFT_N1H), :] = jnp.zeros((FFT_N1H, LANES), F32)


def _hy_long_conv_call(z_lat, x0_lat, g_spec, d_bias):
    batch = z_lat.shape[0]
    pairs = batch // 2
    cst = _fft_constants()
    n_cb = HY_WIDTH // LANES
    rows_z = FFT_N1H * FFT_PITCH
    rows_a = FFT_N1 * FFT_PITCH
    zv = z_lat.reshape(pairs, 2, rows_z, HY_WIDTH)
    xv = x0_lat.reshape(pairs, 2, rows_z, HY_WIDTH)
    z_spec = pl.BlockSpec((1, 2, rows_z, LANES), lambda p, cb: (p, 0, 0, cb))
    a_spec = pl.BlockSpec((1, 2, rows_a, LANES), lambda p, cb: (p, 0, 0, cb))
    a = pl.pallas_call(
        _fft_s1_kernel,
        grid=(pairs, n_cb),
        in_specs=[z_spec, _full_spec((2 * FFT_N1, 2 * FFT_N1H))],
        out_specs=a_spec,
        out_shape=jax.ShapeDtypeStruct((pairs, 2, rows_a, HY_WIDTH), F32),
        compiler_params=_cparams(2),
        name="hy_fft_s1",
    )(zv, cst["m1"][0])
    a = a.reshape(pairs, 2, FFT_N1, FFT_PITCH, HY_WIDTH)
    blk = pl.BlockSpec((1, 2, FFT_KB, FFT_PITCH, HY_WIDTH), lambda kb, p: (p, 0, kb, 0, 0))
    sq = (2 * FFT_N2, 2 * FFT_N2)
    b = pl.pallas_call(
        _fft_s2_kernel,
        grid=(FFT_N1 // FFT_KB, pairs),
        in_specs=[blk,
                  pl.BlockSpec((2, FFT_KB, FFT_N2, LANES), lambda kb, p: (0, kb, 0, 0)),
                  pl.BlockSpec((2, FFT_KB, FFT_N2, HY_WIDTH), lambda kb, p: (0, kb, 0, 0)),
                  _full_spec(sq), _full_spec(sq)],
        out_specs=blk,
        out_shape=jax.ShapeDtypeStruct((pairs, 2, FFT_N1, FFT_PITCH, HY_WIDTH), F32),
        compiler_params=_cparams(2),
        name="hy_fft_s2",
    )(a, cst["tw"], g_spec, cst["m2"][0], cst["m2i"][0])
    b = b.reshape(pairs, 2, rows_a, HY_WIDTH)
    y = pl.pallas_call(
        _fft_s3_kernel,
        grid=(pairs, n_cb),
        in_specs=[a_spec, _full_spec((2 * FFT_N1H, 2 * FFT_N1)),
                  z_spec, z_spec, pl.BlockSpec((1, LANES), lambda p, cb: (0, cb))],
        out_specs=z_spec,
        out_shape=jax.ShapeDtypeStruct((pairs, 2, rows_z, HY_WIDTH), F32),
        compiler_params=_cparams(2),
        name="hy_fft_s3",
    )(b, cst["m3"][0], zv, xv, d_bias.reshape(1, HY_WIDTH))
    return y.reshape(batch, rows_z, HY_WIDTH)


def _hy_ctx_kernel(z_ref, x0_ref, g_ref, fwd_ref, inv_ref, bias_ref, y_ref):
    z = z_ref[0]
    x = _bdot(fwd_ref[...], z.astype(BF16))
    xr, xi = x[:FFT_NC], x[FFT_NC:]
    gr, gi = g_ref[0], g_ref[1]
    y = jnp.concatenate([xr * gr - xi * gi, xr * gi + xi * gr], axis=0)
    conv = _bdot(inv_ref[...], y.astype(BF16))
    y_ref[0] = ((conv + z * bias_ref[...]) * x0_ref[0]).astype(BF16)


def _hy_ctx_conv_call(z_ctx, x0_ctx, g_spec, d_bias):
    batch = z_ctx.shape[0]
    cst = _fft_constants()
    tok = pl.BlockSpec((1, CTX_LEN, HY_WIDTH), lambda b: (b, 0, 0))

    def full(shape):
        zeros = (0,) * len(shape)
        return pl.BlockSpec(shape, lambda b: zeros)

    return pl.pallas_call(
        _hy_ctx_kernel,
        grid=(batch,),
        in_specs=[tok, tok, full((2, FFT_NC, HY_WIDTH)),
                  full((2 * FFT_NC, CTX_LEN)), full((CTX_LEN, 2 * FFT_NC)), full((1, HY_WIDTH))],
        out_specs=tok,
        out_shape=jax.ShapeDtypeStruct((batch, CTX_LEN, HY_WIDTH), BF16),
        compiler_params=_cparams(1),
        name="hy_ctx",
    )(z_ctx, x0_ctx, g_spec, cst["mcf"][0], cst["mci"][0], d_bias.reshape(1, HY_WIDTH))


def _hyena_filter(length, fw1, fb1, fw2, fb2, fw3, fb3, fw4, freq):
    t = jnp.linspace(0.0, 1.0, length, dtype=F32)[:, None]
    w = 2.0 * math.pi * jnp.arange(length, dtype=F32)[:, None] / length
    bands = jnp.linspace(1e-4, HY_BANDS - 1, HY_BANDS, dtype=F32)
    emb = jnp.concatenate([t, jnp.cos(bands * w), -jnp.sin(bands * w)], axis=-1)
    emb = jnp.concatenate([emb, emb[::-1]], axis=0)
    hp = lax.Precision.HIGHEST
    h = jnp.sin(freq * (jnp.dot(emb, fw1, precision=hp) + fb1))
    h = jnp.sin(freq * (jnp.dot(h, fw2, precision=hp) + fb2))
    h = jnp.sin(freq * (jnp.dot(h, fw3, precision=hp) + fb3))
    max_decay = math.log(HY_TARGET) / HY_FAST_PCT
    min_decay = math.log(HY_TARGET) / HY_SLOW_PCT
    deltas = jnp.linspace(min_decay, max_decay, HY_WIDTH, dtype=F32)
    h_fwd = jnp.dot(h[:length], fw4[:, :HY_WIDTH], precision=hp).astype(F32) * jnp.exp(-t * jnp.abs(deltas))
    h_bwd_rev = (jnp.dot(h[length:], fw4[:, HY_WIDTH:], precision=hp).astype(F32)
                 * jnp.exp(-t[::-1] * jnp.abs(deltas)))
    return jnp.concatenate([h_fwd[:1] + h_bwd_rev[length - 1:], h_fwd[1:],
                            jnp.zeros((1, HY_WIDTH), F32), h_bwd_rev[:length - 1]], axis=0)


def _odd_in_kernel(x_ref, mod_ref, gain_ref, w_ref, cos_ref, sin_ref, q_ref, k_ref, v_ref, sg_ref):
    h = _modulated(x_ref[0], gain_ref[...], mod_ref, 3, 4).astype(BF16)
    cos = cos_ref[...]
    sin = sin_ref[...]

    def rope_store(dst_ref, col0, scale):
        p = _bdot(h, w_ref[:, col0:col0 + RET_QK])
        for c in range(RET_QK // LANES):
            t = p[:, c * LANES:(c + 1) * LANES]
            half = (c % 2) * LANES
            r = t * cos[:, half:half + LANES] + pltpu.roll(t, LANES // 2, 1) * sin[:, half:half + LANES]
            dst_ref[0, :, c * LANES:(c + 1) * LANES] = (r * scale).astype(BF16)

    rope_store(q_ref, 0, 1.0)
    rope_store(k_ref, RET_QK, RET_KEY_DIM ** -0.5)
    v_ref[0] = _bdot(h, w_ref[:, 2 * RET_QK:2 * RET_QK + RET_V]).astype(BF16)
    sg_ref[0] = _silu(_bdot(h, w_ref[:, 2 * RET_QK + RET_V:])).astype(BF16)


def _odd_in_call(xt, mod_i, gain, w, cos_t, sin_t):
    batch = xt.shape[0]
    rope_spec = pl.BlockSpec((TM, RET_KEY_DIM), lambda b, j: (j, 0))
    return pl.pallas_call(
        _odd_in_kernel,
        grid=(batch, NT_ALL),
        in_specs=[_tok_spec(D_MODEL), _mod_spec(batch), _full_spec((1, D_MODEL)),
                  _full_spec((D_MODEL, w.shape[1])), rope_spec, rope_spec],
        out_specs=[_tok_spec(RET_QK), _tok_spec(RET_QK), _tok_spec(RET_V), _tok_spec(RET_V)],
        out_shape=[jax.ShapeDtypeStruct((batch, T_TOK, RET_QK), BF16)] * 2
        + [jax.ShapeDtypeStruct((batch, T_TOK, RET_V), BF16)] * 2,
        compiler_params=_cparams(2),
        name="odd_in",
    )(xt, mod_i, gain, w, cos_t, sin_t)


def _rope_tables():
    t = np.arange(SEQ)
    n_freq = RET_KEY_DIM // 4
    inv = ROPE_BASE ** (-jnp.arange(n_freq, dtype=F32) / n_freq)
    ang_r = jnp.asarray(t // GRID_W, F32)[:, None] * inv
    ang_c = jnp.asarray(t % GRID_W, F32)[:, None] * inv
    cr, sr, cc, sc = jnp.cos(ang_r), jnp.sin(ang_r), jnp.cos(ang_c), jnp.sin(ang_c)
    cos_l = jnp.concatenate([cr, cr, cc, cc], axis=-1)
    sin_l = jnp.concatenate([-sr, sr, -sc, sc], axis=-1)
    cos_t = jnp.concatenate([cos_l, jnp.ones((CTX_LEN, RET_KEY_DIM), F32)], axis=0)
    sin_t = jnp.concatenate([sin_l, jnp.zeros((CTX_LEN, RET_KEY_DIM), F32)], axis=0)
    return cos_t, sin_t


def _ret_kernel(lg_ref, q_ref, k_ref, v_ref, sg_ref, y_ref, o_acc, state_f, state_b):
    head = pl.program_id(1)
    n_c = RET_BLOCK
    n_lat = SEQ // n_c
    assert CTX_LEN == n_c and n_lat % 2 == 0
    ii = lax.broadcasted_iota(jnp.int32, (n_c, n_c), 0).astype(F32)
    jj = lax.broadcasted_iota(jnp.int32, (n_c, n_c), 1).astype(F32)
    row_k = lax.broadcasted_iota(jnp.int32, (n_c, RET_KEY_DIM), 0).astype(F32)
    row_v = lax.broadcasted_iota(jnp.int32, (n_c, RET_VAL_DIM), 0).astype(F32)

    def decays(backward):
        lg = lg_ref[1 if backward else 0, head]
        if backward:
            diff = jj - ii
            xi = jnp.exp(lg * (n_c - row_v))
            zeta = jnp.exp(lg * row_k)
        else:
            diff = ii - jj
            xi = jnp.exp(lg * (row_v + 1.0))
            zeta = jnp.exp(lg * (n_c - 1.0 - row_k))
        dmask = jnp.where(diff >= 0, jnp.exp(lg * jnp.maximum(diff, 0.0)), 0.0)
        g_chunk = jnp.exp(lg * n_c + jnp.zeros((1, RET_VAL_DIM), F32))
        return dmask, xi, zeta, g_chunk

    def advance(chunk, state, consts, finalize):
        dmask, xi, zeta, g_chunk = consts
        r = chunk * n_c if isinstance(chunk, int) else pl.multiple_of(chunk * n_c, n_c)
        qc = q_ref[0, pl.ds(r, n_c), :]
        kc = k_ref[0, pl.ds(r, n_c), :]
        vc = v_ref[0, pl.ds(r, n_c), :]
        st = state[...]
        inner = lax.dot_general(qc, kc, _NT_DIMS, preferred_element_type=F32) * dmask
        o = _bdot(inner.astype(BF16), vc) + _bdot(qc, st.astype(BF16)) * xi
        kz = (kc.astype(F32) * zeta).astype(BF16)
        state[...] = st * g_chunk + lax.dot_general(kz, vc, _TN_DIMS, preferred_element_type=F32)
        if finalize:
            tot = o_acc[pl.ds(r, n_c), :] + o
            mu = jnp.mean(tot, axis=-1, keepdims=True)
            cen = tot - mu
            var = jnp.mean(cen * cen, axis=-1, keepdims=True)
            yn = cen * lax.rsqrt(var + GN_EPS)
            y_ref[0, pl.ds(r, n_c), :] = (sg_ref[0, pl.ds(r, n_c), :].astype(F32) * yn).astype(BF16)
        else:
            o_acc[pl.ds(r, n_c), :] = o

    consts_f = decays(False)
    consts_b = decays(True)
    state_f[...] = jnp.zeros_like(state_f)
    state_b[...] = jnp.zeros_like(state_b)

    def both(chunk_f, chunk_b, finalize):
        advance(chunk_f, state_f, consts_f, finalize)
        advance(chunk_b, state_b, consts_b, finalize)

    advance(n_lat, state_f, consts_f, False)
    advance(n_lat, state_b, consts_b, True)

    def first_half(s, carry):
        both(s, n_lat - 1 - s, False)
        return carry

    def second_half(s, carry):
        both(s, n_lat - 1 - s, True)
        return carry

    lax.fori_loop(0, n_lat // 2, first_half, 0, unroll=2)
    lax.fori_loop(n_lat // 2, n_lat, second_half, 0, unroll=2)


def _ret_call(lg, q, k, v, sg):
    batch = q.shape[0]
    qk_spec = pl.BlockSpec((1, T_TOK, RET_KEY_DIM), lambda b, h: (b, 0, h))
    v_spec = pl.BlockSpec((1, T_TOK, RET_VAL_DIM), lambda b, h: (b, 0, h))
    return pl.pallas_call(
        _ret_kernel,
        grid=(batch, RET_HEADS),
        in_specs=[pl.BlockSpec(memory_space=pltpu.SMEM), qk_spec, qk_spec, v_spec, v_spec],
        out_specs=v_spec,
        out_shape=jax.ShapeDtypeStruct((batch, T_TOK, RET_V), BF16),
        scratch_shapes=[pltpu.VMEM((T_TOK, RET_VAL_DIM), F32),
                        pltpu.VMEM((RET_KEY_DIM, RET_VAL_DIM), F32),
                        pltpu.VMEM((RET_KEY_DIM, RET_VAL_DIM), F32)],
        compiler_params=_cparams(2),
        name="retention",
    )(lg, q, k, v, sg)


def _fft_s2_fwd_kernel(a_ref, tw_ref, fhi_ref, flo_ref, g_ref):
    reps = HY_WIDTH // LANES
    for kk in range(FFT_KB):
        twr = jnp.concatenate([tw_ref[0, kk]] * reps, axis=1)
        twi = jnp.concatenate([tw_ref[1, kk]] * reps, axis=1)
        ar = a_ref[0, kk]
        ai = a_ref[1, kk]
        d = jnp.concatenate([ar * twr - ai * twi, ar * twi + ai * twr], axis=0)
        x = _dot3(fhi_ref[...], flo_ref[...], d)
        g_ref[0, kk] = x[:FFT_N2]
        g_ref[1, kk] = x[FFT_N2:]


def _dft_dense_kernel(x_ref, mhi_ref, mlo_ref, o_ref):
    o_ref[...] = _dot3(mhi_ref[...], mlo_ref[...], x_ref[...])


def _filter_spectra(filter_params):
    cst = _fft_constants()
    n_lane = FFT_N2 * HY_WIDTH
    filt = _hyena_filter(SEQ, *filter_params).reshape(1, 1, FFT_N1, n_lane)
    a = pl.pallas_call(
        _fft_s1_flat_kernel,
        grid=(1, n_lane // FFT_W),
        in_specs=[pl.BlockSpec((1, 1, FFT_N1, FFT_W), lambda p, w: (p, 0, 0, w)),
                  _full_spec((2 * FFT_N1, FFT_N1)), _full_spec((2 * FFT_N1, FFT_N1))],
        out_specs=pl.BlockSpec((1, 2 * FFT_N1, FFT_W), lambda p, w: (p, 0, w)),
        out_shape=jax.ShapeDtypeStruct((1, 2 * FFT_N1, n_lane), F32),
        compiler_params=_cparams(2),
        name="filt_fft_s1",
    )(filt, *cst["m1r"])
    a = a.reshape(2, FFT_N1, FFT_N2, HY_WIDTH)
    blk = pl.BlockSpec((2, FFT_KB, FFT_N2, HY_WIDTH), lambda kb: (0, kb, 0, 0))
    sq = (2 * FFT_N2, 2 * FFT_N2)
    g_lat = pl.pallas_call(
        _fft_s2_fwd_kernel,
        grid=(FFT_N1 // FFT_KB,),
        in_specs=[blk, pl.BlockSpec((2, FFT_KB, FFT_N2, LANES), lambda kb: (0, kb, 0, 0)),
                  pl.BlockSpec(sq, lambda kb: (0, 0)), pl.BlockSpec(sq, lambda kb: (0, 0))],
        out_specs=blk,
        out_shape=jax.ShapeDtypeStruct((2, FFT_N1, FFT_N2, HY_WIDTH), F32),
        compiler_params=_cparams(1),
        name="filt_fft_s2",
    )(a, cst["tw"], *cst["m2"])
    filt_ctx = _hyena_filter(CTX_LEN, *filter_params)
    g_ctx = pl.pallas_call(
        _dft_dense_kernel,
        out_shape=jax.ShapeDtypeStruct((2 * FFT_NC, HY_WIDTH), F32),
        compiler_params=pltpu.CompilerParams(vmem_limit_bytes=VMEM_LIMIT_BYTES),
        name="filt_dft_ctx",
    )(filt_ctx, *cst["mcf_full"])
    return g_lat, g_ctx.reshape(2, FFT_NC, HY_WIDTH)


def _even_mixer(xt, mod_i, gain, w_in, w_out, q_gain, k_gain, rpb, conv_w, conv_b, filter_params, d_bias):
    scale = NA_HEAD_DIM ** -0.5
    qg = (jnp.tile(q_gain, NA_HEADS) * scale).reshape(1, NA_WIDTH)
    kg = jnp.tile(k_gain, NA_HEADS).reshape(1, NA_WIDTH)
    head_of = np.arange(NA_WIDTH) // NA_HEAD_DIM
    bd = jnp.asarray((head_of[:, None] == head_of[None, :]) / NA_HEAD_DIM, BF16)
    q, k, v, u = _even_in_call(xt, mod_i, gain, w_in.astype(BF16), qg, kg, bd)
    a = _na_call(q, k, v, _na_bias_table(rpb))
    z_lat, x0_lat, z_ctx, x0_ctx = _hy_pre_call(u, conv_w, conv_b.reshape(1, -1))
    g_lat, g_ctx = _filter_spectra(filter_params)
    y_lat = _hy_long_conv_call(z_lat, x0_lat, g_lat, d_bias)
    y_ctx = _hy_ctx_conv_call(z_ctx, x0_ctx, g_ctx, d_bias)
    return a, y_lat.reshape(xt.shape[0], FFT_N1H, FFT_PITCH, HY_WIDTH), y_ctx, w_out.astype(BF16)


def _odd_mixer(xt, mod_i, gain, w_in, w_out, logit_f, logit_b, rope):
    q, k, v, sg = _odd_in_call(xt, mod_i, gain, w_in.astype(BF16), *rope)
    lg = jnp.stack([jax.nn.log_sigmoid(logit_f.astype(F32)), jax.nn.log_sigmoid(logit_b.astype(F32))])
    return _ret_call(lg, q, k, v, sg), w_out.astype(BF16)


def kernel(x, c, ctx, c_ctx, w_mod, b_mod, norm_gain, ffn_a_in, ffn_a_out, ffn_b_in, ffn_b_out,
           even_in, even_out, na_q_gain, na_k_gain, na_rpb, hy_conv_w, hy_conv_b,
           hy_fw1, hy_fb1, hy_fw2, hy_fb2, hy_fw3, hy_fb3, hy_fw4, hy_freq, hy_bias,
           ret_in, ret_out, ret_logit_f, ret_logit_b):
    batch = x.shape[0]
    assert x.shape == (batch, SEQ, D_MODEL) and ctx.shape == (batch, CTX_LEN, D_MODEL)
    assert batch % 2 == 0 and batch < MOD_ROWS
    c_rows = jnp.concatenate([c, c_ctx[None], jnp.zeros((MOD_ROWS - batch - 1, D_MODEL), F32)], axis=0)
    mod_all = _mod_call(c_rows, w_mod, b_mod).reshape(DEPTH, MOD_ROWS, N_MOD, D_MODEL)
    rope = _rope_tables()
    for i in range(DEPTH):
        last = i == DEPTH - 1
        mod_i = mod_all[i]
        gains = norm_gain[i].reshape(3, 1, D_MODEL)
        w_a = (ffn_a_in[i].astype(BF16), ffn_a_out[i].astype(BF16))
        w_b = (ffn_b_in[i].astype(BF16), ffn_b_out[i].astype(BF16))
        if i == 0:
            xt = _ffn_call("split", (x, ctx), mod_i, gains[0], *w_a, (0, 1, 2), NT_ALL, batch)
        else:
            xt = _ffn_call("stream", (xt,), mod_i, gains[0], *w_a, (0, 1, 2), NT_ALL, batch)
        if i % 2 == 0:
            e = i // 2
            filter_params = (hy_fw1[e], hy_fb1[e], hy_fw2[e], hy_fb2[e], hy_fw3[e], hy_fb3[e],
                             hy_fw4[e], hy_freq[e])
            source = "even"
            mixed = _even_mixer(xt, mod_i, gains[1], even_in[e], even_out[e], na_q_gain[e], na_k_gain[e],
                                na_rpb[e], hy_conv_w[e], hy_conv_b[e], filter_params, hy_bias[e])
        else:
            o = i // 2
            source = "odd"
            mixed = _odd_mixer(xt, mod_i, gains[1], ret_in[o], ret_out[o], ret_logit_f[o], ret_logit_b[o], rope)
        xt = _ffn_call(source, (xt,) + mixed, mod_i, gains[2], *w_b, (6, 7, 8),
                       NT_LAT if last else NT_ALL, batch)
    return xt
```

```python
import functools
import math

import ml_dtypes
import numpy as np
import jax
import jax.numpy as jnp
from jax import lax
from jax.experimental import pallas as pl
from jax.experimental.pallas import tpu as pltpu

F32 = jnp.float32
BF16 = jnp.bfloat16

D_MODEL = 1024
SEQ = 4096
DEPTH = 4
GRID_W = 64
CTX_LEN = 256
N_MOD = 9
RMS_EPS = 1e-6
GN_EPS = 1e-6
D_FF = 2816
NA_HEADS = 8
NA_HEAD_DIM = 64
NA_WIDTH = NA_HEADS * NA_HEAD_DIM
WIN_R = 8
WIN_C = 16
HY_WIDTH = D_MODEL - NA_WIDTH
HY_BANDS = 8
HY_TARGET = 1e-2
HY_FAST_PCT = 0.3
HY_SLOW_PCT = 1.5
RET_HEADS = 4
RET_KEY_DIM = D_MODEL // RET_HEADS
RET_VAL_DIM = 2 * RET_KEY_DIM
RET_QK = RET_HEADS * RET_KEY_DIM
RET_V = RET_HEADS * RET_VAL_DIM
ROPE_BASE = 10000.0

LANES = 128
VMEM_LIMIT_BYTES = 56 * 2**20
T_TOK = SEQ + CTX_LEN
TM = CTX_LEN
NT_LAT = SEQ // TM
NT_ALL = T_TOK // TM
MOD_ROWS = 16
FF_CHUNKS = ((0, 1536), (1536, 1280))
RET_BLOCK = 256

NA_RB = 4
NA_KR = NA_RB + WIN_R - 1
GRID_H = SEQ // GRID_W
NA_NQ = NA_RB * GRID_W
NA_NK = NA_KR * GRID_W
NEG_BIG = -1e30

FFT_N = 2 * SEQ
FFT_N1 = 64
FFT_N2 = 128
FFT_N1H = FFT_N1 // 2
FFT_W = 2048
FFT_KB = 4
FFT_PITCH = FFT_N2 + 8
FFT_NC = 2 * CTX_LEN


def _cparams(n_axes):
    return pltpu.CompilerParams(dimension_semantics=("arbitrary",) * n_axes,
                                vmem_limit_bytes=VMEM_LIMIT_BYTES)


def _bdot(a, b):
    return jnp.dot(a, b, preferred_element_type=F32)


_NT_DIMS = (((1,), (1,)), ((), ()))
_TN_DIMS = (((0,), (0,)), ((), ()))


def _split_hi_lo(m):
    hi = m.astype(BF16)
    lo = (m - hi.astype(F32)).astype(BF16)
    return hi, lo


def _dot3(m_hi, m_lo, d):
    d_hi, d_lo = _split_hi_lo(d)
    return _bdot(m_hi, d_hi) + _bdot(m_lo, d_hi) + _bdot(m_hi, d_lo)


def _modulated(x, gain, mod_ref, shift_row, scale_row):
    ms = jnp.mean(x * x, axis=-1, keepdims=True)
    y = x * lax.rsqrt(ms + RMS_EPS) * gain
    return (y * (1.0 + mod_ref[0, scale_row:scale_row + 1, :])
            + mod_ref[0, shift_row:shift_row + 1, :])


def _silu(a):
    return a * jax.nn.sigmoid(a)


def _mod_kernel(c_ref, w_ref, b_ref, o_ref):
    s = _silu(c_ref[...])
    o_ref[0] = jnp.dot(s, w_ref[0], precision=lax.Precision.HIGHEST,
                       preferred_element_type=F32) + b_ref[0]


def _mod_call(c_rows, w_mod, b_mod):
    depth, d, n = w_mod.shape
    tn = 1024
    return pl.pallas_call(
        _mod_kernel,
        grid=(depth, n // tn),
        in_specs=[pl.BlockSpec((MOD_ROWS, d), lambda i, j: (0, 0)),
                  pl.BlockSpec((1, d, tn), lambda i, j: (i, 0, j)),
                  pl.BlockSpec((1, 1, tn), lambda i, j: (i, 0, j))],
        out_specs=pl.BlockSpec((1, MOD_ROWS, tn), lambda i, j: (i, 0, j)),
        out_shape=jax.ShapeDtypeStruct((depth, MOD_ROWS, n), F32),
        compiler_params=_cparams(2),
        name="mod",
    )(c_rows, w_mod, b_mod.reshape(depth, 1, n))


def _tok_spec(width):
    return pl.BlockSpec((1, TM, width), lambda b, j: (b, j, 0))


def _mod_spec(batch):
    return pl.BlockSpec((1, N_MOD, D_MODEL), lambda b, j: (jnp.where(j == NT_LAT, batch, b), 0, 0))


def _full_spec(shape):
    zeros = (0,) * len(shape)
    return pl.BlockSpec(shape, lambda b, j: zeros)


_FFN_SOURCES = {"stream": 1, "split": 4, "even": 7, "odd": 3}
TM2 = 2 * TM


def _per_half(fn):
    return jnp.concatenate([fn(half, slice(half * TM, (half + 1) * TM)) for half in range(2)], axis=0)


def _ffn_kernel(*refs, rows, source, n_tiles, n_pairs):
    shift_row, scale_row, gate_row = rows
    n_src = _FFN_SOURCES[source]
    src = refs[:n_src]
    modn0, modn1, modc0, modc1, gain_ref, win_ref, wout_ref, o_ref, h_a, x_a, h_b, x_b = refs[n_src:]
    modn = (modn0, modn1)
    modc = (modc0, modc1)
    step = pl.program_id(0)
    first_tile_next = 2 * jnp.minimum(step, n_pairs - 1)

    def is_ctx(half):
        return (first_tile_next + half) % n_tiles == NT_LAT

    def gated(x_ref, o):
        return _per_half(lambda half, r: x_ref[0, r, :] + modn[half][0, 5:6, :] * o[r])

    def next_rows():
        if source == "stream":
            return src[0][0]
        if source == "split":
            lat, ctx = src[0:2], src[2:4]
            return _per_half(lambda half, r: jnp.where(is_ctx(half), ctx[half][0], lat[half][0]))
        if source == "even":
            x_ref, a_ref, yl0, yl1, yc0, yc1, wo_ref = src
            y_lat, y_ctx = (yl0, yl1), (yc0, yc1)

            def hyena_rows(half, r):
                slabs = [y_lat[half][0, s, 0:FFT_N2, :] for s in range(TM // FFT_N2)]
                return jnp.where(is_ctx(half), y_ctx[half][0], jnp.concatenate(slabs, axis=0)).astype(BF16)

            o = _bdot(a_ref[0], wo_ref[0:NA_WIDTH, :]) + _bdot(_per_half(hyena_rows), wo_ref[NA_WIDTH:, :])
            return gated(x_ref, o)
        x_ref, y_ref, wo_ref = src
        return gated(x_ref, _bdot(y_ref[0], wo_ref[...]))

    def body(h_read, x_read, h_write, x_write):
        x_next = next_rows()
        h_write[...] = _per_half(lambda half, r: _modulated(
            x_next[r], gain_ref[...], modn[half], shift_row, scale_row)).astype(BF16)
        x_write[...] = x_next
        h = h_read[...]
        acc = jnp.zeros((TM2, D_MODEL), F32)
        for start, size in FF_CHUNKS:
            a = _bdot(h, win_ref[:, start:start + size])
            b = _bdot(h, win_ref[:, D_FF + start:D_FF + start + size])
            g = (_silu(a) * b).astype(BF16)
            acc = acc + _bdot(g, wout_ref[start:start + size, :])
        o_ref[0] = _per_half(lambda half, r: x_read[r, :] + (0.5 * modc[half][0, gate_row:gate_row + 1, :]) * acc[r])

    @pl.when(step == 0)
    def _():
        h_b[...] = jnp.zeros_like(h_b)
        x_b[...] = jnp.zeros_like(x_b)

    @pl.when(step % 2 == 0)
    def _():
        body(h_b, x_b, h_a, x_a)

    @pl.when(step % 2 == 1)
    def _():
        body(h_a, x_a, h_b, x_b)


def _ffn_call(source, srcs, mod_i, gain, w_in, w_out, rows, n_tiles, batch):
    n_all = batch * n_tiles
    assert n_all % 2 == 0 and n_tiles in (NT_LAT, NT_ALL)
    n_pairs = n_all // 2

    def next_pair(s):
        return jnp.minimum(s, n_pairs - 1)

    def cur_pair(s):
        return jnp.maximum(s - 1, 0)

    def tile_of(pair_of, half):
        def fn(s):
            t = 2 * pair_of(s) + half
            return t // n_tiles, t % n_tiles
        return fn

    def mod_spec(pair_of, half):
        def index(s):
            b, j = tile_of(pair_of, half)(s)
            return jnp.where(j == NT_LAT, batch, b), 0, 0
        return pl.BlockSpec((1, N_MOD, D_MODEL), index)

    def tok(arr):
        width = arr.shape[-1]
        if n_tiles == NT_ALL:
            return arr.reshape(1, batch * T_TOK, width), pl.BlockSpec(
                (1, TM2, width), lambda s: (0, next_pair(s), 0))
        per_sample = n_tiles // 2
        return arr, pl.BlockSpec((1, TM2, width),
                                 lambda s: (next_pair(s) // per_sample, next_pair(s) % per_sample, 0))

    def lat_tile(block, n_trailing, half):
        def index(s):
            b, j = tile_of(next_pair, half)(s)
            return (b, jnp.minimum(j, NT_LAT - 1)) + (0,) * n_trailing
        return pl.BlockSpec(block, index)

    def ctx_tile(block, half):
        return pl.BlockSpec(block, lambda s: (tile_of(next_pair, half)(s)[0],) + (0,) * (len(block) - 1))

    def full(shape):
        zeros = (0,) * len(shape)
        return pl.BlockSpec(shape, lambda s: zeros)

    halves = (0, 1)
    if source == "stream":
        (xt,) = srcs
        xt, x_spec = tok(xt)
        args, src_specs = (xt,), [x_spec]
    elif source == "split":
        x, ctx = srcs
        args = (x, x, ctx, ctx)
        src_specs = ([lat_tile((1, TM, D_MODEL), 1, h) for h in halves]
                     + [ctx_tile((1, CTX_LEN, D_MODEL), h) for h in halves])
    elif source == "even":
        xt, a, y_lat, y_ctx, wo = srcs
        (xt, x_spec), (a, a_spec) = tok(xt), tok(a)
        args = (xt, a, y_lat, y_lat, y_ctx, y_ctx, wo)
        src_specs = ([x_spec, a_spec]
                     + [lat_tile((1, TM // FFT_N2, FFT_PITCH, HY_WIDTH), 2, h) for h in halves]
                     + [ctx_tile((1, CTX_LEN, HY_WIDTH), h) for h in halves] + [full((D_MODEL, D_MODEL))])
    else:
        xt, y, wo = srcs
        (xt, x_spec), (y, y_spec) = tok(xt), tok(y)
        args = (xt, y, wo)
        src_specs = [x_spec, y_spec, full((RET_V, D_MODEL))]
    out = pl.pallas_call(
        functools.partial(_ffn_kernel, rows=rows, source=source, n_tiles=n_tiles, n_pairs=n_pairs),
        grid=(n_pairs + 1,),
        in_specs=src_specs + [mod_spec(next_pair, 0), mod_spec(next_pair, 1),
                              mod_spec(cur_pair, 0), mod_spec(cur_pair, 1), full((1, D_MODEL)),
                              full((D_MODEL, 2 * D_FF)), full((D_FF, D_MODEL))],
        out_specs=pl.BlockSpec((1, TM2, D_MODEL), lambda s: (0, cur_pair(s), 0)),
        out_shape=jax.ShapeDtypeStruct((1, n_all * TM, D_MODEL), F32),
        scratch_shapes=[pltpu.VMEM((TM2, D_MODEL), BF16), pltpu.VMEM((TM2, D_MODEL), F32),
                        pltpu.VMEM((TM2, D_MODEL), BF16), pltpu.VMEM((TM2, D_MODEL), F32)],
        compiler_params=_cparams(1),
        name="ffn_" + source,
    )(*args, mod_i, mod_i, mod_i, mod_i, gain, w_in, w_out)
    return out.reshape(batch, n_tiles * TM, D_MODEL)


def _even_in_kernel(x_ref, mod_ref, gain_ref, w_ref, qg_ref, kg_ref, bd_ref,
                    q_ref, k_ref, v_ref, u_ref):
    h = _modulated(x_ref[0], gain_ref[...], mod_ref, 3, 4).astype(BF16)

    def head_norm(t, g):
        ms = _bdot((t * t).astype(BF16), bd_ref[...])
        return (t * lax.rsqrt(ms + RMS_EPS) * g).astype(BF16)

    q_ref[0] = head_norm(_bdot(h, w_ref[:, 0:NA_WIDTH]), qg_ref[...])
    k_ref[0] = head_norm(_bdot(h, w_ref[:, NA_WIDTH:2 * NA_WIDTH]), kg_ref[...])
    v_ref[0] = _bdot(h, w_ref[:, 2 * NA_WIDTH:3 * NA_WIDTH]).astype(BF16)
    u_ref[0] = _bdot(h, w_ref[:, 3 * NA_WIDTH:])


def _even_in_call(xt, mod_i, gain, w, q_gain, k_gain, bd):
    batch = xt.shape[0]
    n_in = w.shape[1]
    return pl.pallas_call(
        _even_in_kernel,
        grid=(batch, NT_ALL),
        in_specs=[_tok_spec(D_MODEL), _mod_spec(batch), _full_spec((1, D_MODEL)),
                  _full_spec((D_MODEL, n_in)), _full_spec((1, NA_WIDTH)), _full_spec((1, NA_WIDTH)),
                  _full_spec((NA_WIDTH, NA_WIDTH))],
        out_specs=[_tok_spec(NA_WIDTH), _tok_spec(NA_WIDTH), _tok_spec(NA_WIDTH),
                   _tok_spec(3 * HY_WIDTH)],
        out_shape=[jax.ShapeDtypeStruct((batch, T_TOK, NA_WIDTH), BF16)] * 3
        + [jax.ShapeDtypeStruct((batch, T_TOK, 3 * HY_WIDTH), F32)],
        compiler_params=_cparams(2),
        name="even_in",
    )(xt, mod_i, gain, w, q_gain, k_gain, bd)


def _na_kernel(q_ref, k_ref, v_ref, bias_ref, o_ref):
    lane = lax.broadcasted_iota(jnp.int32, (1, LANES), 1)
    first_head = lane < NA_HEAD_DIM
    k_ctx = k_ref[0, SEQ:T_TOK, :]
    v_ctx = v_ref[0, SEQ:T_TOK, :]

    def stack_heads(q):
        zero = jnp.zeros_like(q)
        return jnp.concatenate([jnp.where(first_head, q, zero), jnp.where(first_head, zero, q)], axis=0)

    def attend(qs, scores_and_values):
        s_list = [lax.dot_general(qs, kk, _NT_DIMS, preferred_element_type=F32) if bias is None
                  else lax.dot_general(qs, kk, _NT_DIMS, preferred_element_type=F32) + bias
                  for kk, _, bias in scores_and_values]
        m = functools.reduce(jnp.maximum, [jnp.max(s, axis=-1, keepdims=True) for s in s_list])
        o = functools.reduce(jnp.add, [
            _bdot(jnp.exp(s - m).astype(BF16),
                  jnp.concatenate([vv, jnp.ones((vv.shape[0], LANES), BF16)], axis=1))
            for s, (_, vv, _) in zip(s_list, scores_and_values)])
        o = o[:, :LANES] / o[:, LANES:]
        n = qs.shape[0] // 2
        return jnp.where(first_head, o[:n], o[n:]).astype(BF16)

    def block(i, carry):
        r0 = i * NA_RB
        u0 = jnp.clip(r0 - WIN_R // 2, 0, GRID_H - NA_KR)
        pattern = jnp.where(i == 0, 0, jnp.where(i == GRID_H // NA_RB - 1, 2, 1))
        q0 = pl.multiple_of(r0 * GRID_W, NA_NQ)
        k0 = pl.multiple_of(u0 * GRID_W, GRID_W)
        qs = stack_heads(q_ref[0, pl.ds(q0, NA_NQ), :])
        k_win = k_ref[0, pl.ds(k0, NA_NK), :]
        v_win = v_ref[0, pl.ds(k0, NA_NK), :]
        o_ref[0, pl.ds(q0, NA_NQ), :] = attend(
            qs, [(k_win, v_win, bias_ref[pattern, 0]), (k_ctx, v_ctx, None)])
        return carry

    lax.fori_loop(0, GRID_H // NA_RB, block, 0, unroll=4)
    o_ref[0, SEQ:T_TOK, :] = attend(stack_heads(q_ref[0, SEQ:T_TOK, :]), [(k_ctx, v_ctx, None)])


def _na_call(q, k, v, bias):
    batch = q.shape[0]
    n_pairs = NA_WIDTH // LANES
    spec = pl.BlockSpec((1, T_TOK, LANES), lambda b, p: (b, 0, p))
    return pl.pallas_call(
        _na_kernel,
        grid=(batch, n_pairs),
        in_specs=[spec, spec, spec,
                  pl.BlockSpec((3, 1, 2 * NA_NQ, NA_NK), lambda b, p: (0, p, 0, 0))],
        out_specs=spec,
        out_shape=jax.ShapeDtypeStruct((batch, T_TOK, NA_WIDTH), BF16),
        compiler_params=_cparams(2),
        name="na_attn",
    )(q, k, v, bias)


def _na_bias_table(rpb):
    j = np.arange(NA_RB)[:, None, None, None]
    c = np.arange(GRID_W)[None, :, None, None]
    kk = np.arange(NA_KR)[None, None, :, None]
    kc = np.arange(GRID_W)[None, None, None, :]
    cs = np.clip(c - WIN_C // 2, 0, GRID_W - WIN_C)
    col_ok = (kc >= cs) & (kc < cs + WIN_C)
    dc = np.clip(kc - c + WIN_C - 1, 0, 2 * WIN_C - 2)[0, :, 0, :]
    col_sel = (dc[..., None] == np.arange(2 * WIN_C - 1)).astype(np.float32)
    shape = (NA_RB, GRID_W, NA_KR, GRID_W)
    oks, row_sels = [], []
    for off, rs_rel in ((0, 0 * j), (-(WIN_R // 2), j), (-(NA_KR - NA_RB), NA_KR - WIN_R + 0 * j)):
        row_ok = (kk >= rs_rel) & (kk < rs_rel + WIN_R)
        dr = np.clip(off + kk - j + WIN_R - 1, 0, 2 * WIN_R - 2)[:, 0, :, 0]
        row_sels.append((dr[..., None] == np.arange(2 * WIN_R - 1)).astype(np.float32))
        oks.append(np.broadcast_to(row_ok & col_ok, shape))
    vals = jnp.einsum("pjkr,hrc,qmc->phjqkm", np.stack(row_sels), rpb.astype(F32), col_sel,
                      precision=lax.Precision.HIGHEST)
    table = jnp.where(np.stack(oks)[:, None], vals, NEG_BIG)
    return table.reshape(3, NA_HEADS // 2, 2 * NA_NQ, NA_NK)


def _hy_pre_kernel(u0_ref, u1_ref, uv_ref, w0_ref, w1_ref, wv_ref, b0_ref, b1_ref, bv_ref,
                   zl_ref, xl_ref, zc_ref, xc_ref):
    row = lax.broadcasted_iota(jnp.int32, (TM, LANES), 0)
    zero_row = jnp.zeros((1, LANES), F32)

    def conv(u_ref, w_ref, b_ref, tile):
        s = tile * TM
        cur = u_ref[0, s:s + TM, :]
        seq_start = tile in (0, NT_LAT)
        seq_end = tile in (NT_LAT - 1, NT_LAT)
        prev_row = zero_row if seq_start else u_ref[0, s - 1:s, :]
        next_row = zero_row if seq_end else u_ref[0, s + TM:s + TM + 1, :]
        before = jnp.where(row == 0, prev_row, pltpu.roll(cur, 1, 0))
        after = jnp.where(row == TM - 1, next_row, pltpu.roll(cur, TM - 1, 0))
        return before * w_ref[0:1, :] + cur * w_ref[1:2, :] + after * w_ref[2:3, :] + b_ref[...]

    for tile in range(NT_ALL):
        x0 = conv(u0_ref, w0_ref, b0_ref, tile)
        z = conv(uv_ref, wv_ref, bv_ref, tile) * conv(u1_ref, w1_ref, b1_ref, tile)
        if tile < NT_LAT:
            for half in range(TM // FFT_N2):
                base = (tile * (TM // FFT_N2) + half) * FFT_PITCH
                zl_ref[0, base:base + FFT_N2, :] = z[half * FFT_N2:(half + 1) * FFT_N2]
                xl_ref[0, base:base + FFT_N2, :] = x0[half * FFT_N2:(half + 1) * FFT_N2]
                pad = jnp.zeros((FFT_PITCH - FFT_N2, LANES), F32)
                zl_ref[0, base + FFT_N2:base + FFT_PITCH, :] = pad
                xl_ref[0, base + FFT_N2:base + FFT_PITCH, :] = pad
        else:
            zc_ref[0] = z
            xc_ref[0] = x0


def _hy_pre_call(u, conv_w, conv_b):
    batch = u.shape[0]
    nb = HY_WIDTH // LANES
    u_specs = [pl.BlockSpec((1, T_TOK, LANES), lambda b, cb, g=g: (b, 0, g * nb + cb)) for g in range(3)]
    w_specs = [pl.BlockSpec((3, LANES), lambda b, cb, g=g: (0, g * nb + cb)) for g in range(3)]
    b_specs = [pl.BlockSpec((1, LANES), lambda b, cb, g=g: (0, g * nb + cb)) for g in range(3)]
    lat = pl.BlockSpec((1, FFT_N1H * FFT_PITCH, LANES), lambda b, cb: (b, 0, cb))
    ctx = pl.BlockSpec((1, CTX_LEN, LANES), lambda b, cb: (b, 0, cb))
    return pl.pallas_call(
        _hy_pre_kernel,
        grid=(batch, nb),
        in_specs=u_specs + w_specs + b_specs,
        out_specs=[lat, lat, ctx, ctx],
        out_shape=[jax.ShapeDtypeStruct((batch, FFT_N1H * FFT_PITCH, HY_WIDTH), F32)] * 2
        + [jax.ShapeDtypeStruct((batch, CTX_LEN, HY_WIDTH), F32)] * 2,
        compiler_params=_cparams(2),
        name="hy_pre",
    )(u, u, u, conv_w, conv_w, conv_w, conv_b, conv_b, conv_b)


def _complex_block(re, im):
    return np.block([[re, -im], [im, re]])


@functools.lru_cache(maxsize=None)
def _fft_constants():
    k1 = np.arange(FFT_N1)
    n1 = np.arange(FFT_N1H)
    f1 = np.exp(-2j * np.pi * np.outer(k1, n1) / FFT_N1)
    m1 = _complex_block(f1.real, f1.imag)
    f1_full = np.exp(-2j * np.pi * np.outer(k1, k1) / FFT_N1)
    m1r = np.concatenate([f1_full.real, f1_full.imag], axis=0)
    n2 = np.arange(FFT_N2)
    tw = np.exp(-2j * np.pi * np.outer(k1, n2) / FFT_N)
    tw = np.stack([tw.real, tw.imag])[..., None] * np.ones((1, 1, 1, LANES))
    f2 = np.exp(-2j * np.pi * np.outer(n2, n2) / FFT_N2)
    m2 = _complex_block(f2.real, f2.imag)
    m2i = _complex_block(f2.real, -f2.imag)
    c1 = np.exp(2j * np.pi * np.outer(n1, k1) / FFT_N1) / FFT_N
    m3 = _complex_block(c1.real, c1.imag)
    kc = np.arange(FFT_NC)
    nc = np.arange(CTX_LEN)
    ang = 2 * np.pi * np.outer(kc, nc) / FFT_NC
    mcf = np.concatenate([np.cos(ang), -np.sin(ang)], axis=0)
    ang_full = 2 * np.pi * np.outer(kc, kc) / FFT_NC
    mcf_full = np.concatenate([np.cos(ang_full), -np.sin(ang_full)], axis=0)
    mci = np.concatenate([np.cos(ang.T), -np.sin(ang.T)], axis=1) / FFT_NC

    def hi_lo(m):
        hi = m.astype(ml_dtypes.bfloat16)
        lo = (m - hi.astype(np.float64)).astype(ml_dtypes.bfloat16)
        return hi, lo

    return dict(m1=hi_lo(m1), m1r=hi_lo(m1r), m2=hi_lo(m2), m2i=hi_lo(m2i), m3=hi_lo(m3),
                mcf=hi_lo(mcf), mcf_full=hi_lo(mcf_full), mci=hi_lo(mci), tw=tw.astype(np.float32))


def _fft_s1_flat_kernel(z_ref, mhi_ref, mlo_ref, a_ref):
    d = z_ref[0].reshape(-1, FFT_W)
    a_ref[0] = _dot3(mhi_ref[...], mlo_ref[...], d)


def _slab_rows(n2, n_slabs):
    return pl.ds(n2, n_slabs, stride=FFT_PITCH)


def _fft_s1_kernel(z_ref, m_ref, a_ref):
    def step(i, carry):
        cols = []
        for dn in range(2):
            rows = _slab_rows(2 * i + dn, FFT_N1H)
            cols.append(jnp.concatenate([z_ref[0, 0, rows, :], z_ref[0, 1, rows, :]], axis=0))
        r = _bdot(m_ref[...], jnp.concatenate(cols, axis=1).astype(BF16))
        for dn in range(2):
            rows = _slab_rows(2 * i + dn, FFT_N1)
            a_ref[0, 0, rows, :] = r[:FFT_N1, dn * LANES:(dn + 1) * LANES]
            a_ref[0, 1, rows, :] = r[FFT_N1:, dn * LANES:(dn + 1) * LANES]
        return carry

    lax.fori_loop(0, FFT_N2 // 2, step, 0, unroll=4)
    for pad_row in range(FFT_N2, FFT_PITCH):
        for part in range(2):
            a_ref[0, part, _slab_rows(pad_row, FFT_N1), :] = jnp.zeros((FFT_N1, LANES), F32)


def _fft_s2_kernel(a_ref, tw_ref, g_ref, fwd_ref, inv_ref, o_ref):
    reps = HY_WIDTH // LANES
    pad = jnp.zeros((FFT_PITCH - FFT_N2, HY_WIDTH), F32)
    for kk in range(FFT_KB):
        twr = jnp.concatenate([tw_ref[0, kk]] * reps, axis=1)
        twi = jnp.concatenate([tw_ref[1, kk]] * reps, axis=1)
        ar = a_ref[0, 0, kk, 0:FFT_N2, :]
        ai = a_ref[0, 1, kk, 0:FFT_N2, :]
        d = jnp.concatenate([ar * twr - ai * twi, ar * twi + ai * twr], axis=0)
        x = _bdot(fwd_ref[...], d.astype(BF16))
        xr, xi = x[:FFT_N2], x[FFT_N2:]
        gr, gi = g_ref[0, kk], g_ref[1, kk]
        y = jnp.concatenate([xr * gr - xi * gi, xr * gi + xi * gr], axis=0)
        b = _bdot(inv_ref[...], y.astype(BF16))
        br, bi = b[:FFT_N2], b[FFT_N2:]
        o_ref[0, 0, kk, 0:FFT_N2, :] = br * twr + bi * twi
        o_ref[0, 1, kk, 0:FFT_N2, :] = bi * twr - br * twi
        o_ref[0, 0, kk, FFT_N2:FFT_PITCH, :] = pad
        o_ref[0, 1, kk, FFT_N2:FFT_PITCH, :] = pad


def _fft_s3_kernel(b_ref, m_ref, z_ref, x0_ref, bias_ref, y_ref):
    def step(i, carry):
        cols = []
        for dn in range(2):
            rows = _slab_rows(2 * i + dn, FFT_N1)
            cols.append(jnp.concatenate([b_ref[0, 0, rows, :], b_ref[0, 1, rows, :]], axis=0))
        y = _bdot(m_ref[...], jnp.concatenate(cols, axis=1).astype(BF16))
        for dn in range(2):
            rows = _slab_rows(2 * i + dn, FFT_N1H)
            for s in range(2):
                conv = y[s * FFT_N1H:(s + 1) * FFT_N1H, dn * LANES:(dn + 1) * LANES]
                y_ref[0, s, rows, :] = (conv + z_ref[0, s, rows, :] * bias_ref[...]) * x0_ref[0, s, rows, :]
        return carry

    lax.fori_loop(0, FFT_N2 // 2, step, 0, unroll=4)
    for pad_row in range(FFT_N2, FFT_PITCH):
        for s in range(2):
            y_ref[0, s, _slab_rows(pad_row, FFT_N1H), :] = jnp.zeros((FFT_N1H, LANES), F32)


def _hy_long_conv_call(z_lat, x0_lat, g_spec, d_bias):
    batch = z_lat.shape[0]
    pairs = batch // 2
    cst = _fft_constants()
    n_cb = HY_WIDTH // LANES
    rows_z = FFT_N1H * FFT_PITCH
    rows_a = FFT_N1 * FFT_PITCH
    zv = z_lat.reshape(pairs, 2, rows_z, HY_WIDTH)
    xv = x0_lat.reshape(pairs, 2, rows_z, HY_WIDTH)
    z_spec = pl.BlockSpec((1, 2, rows_z, LANES), lambda p, cb: (p, 0, 0, cb))
    a_spec = pl.BlockSpec((1, 2, rows_a, LANES), lambda p, cb: (p, 0, 0, cb))
    a = pl.pallas_call(
        _fft_s1_kernel,
        grid=(pairs, n_cb),
        in_specs=[z_spec, _full_spec((2 * FFT_N1, 2 * FFT_N1H))],
        out_specs=a_spec,
        out_shape=jax.ShapeDtypeStruct((pairs, 2, rows_a, HY_WIDTH), F32),
        compiler_params=_cparams(2),
        name="hy_fft_s1",
    )(zv, cst["m1"][0])
    a = a.reshape(pairs, 2, FFT_N1, FFT_PITCH, HY_WIDTH)
    blk = pl.BlockSpec((1, 2, FFT_KB, FFT_PITCH, HY_WIDTH), lambda kb, p: (p, 0, kb, 0, 0))
    sq = (2 * FFT_N2, 2 * FFT_N2)
    b = pl.pallas_call(
        _fft_s2_kernel,
        grid=(FFT_N1 // FFT_KB, pairs),
        in_specs=[blk,
                  pl.BlockSpec((2, FFT_KB, FFT_N2, LANES), lambda kb, p: (0, kb, 0, 0)),
                  pl.BlockSpec((2, FFT_KB, FFT_N2, HY_WIDTH), lambda kb, p: (0, kb, 0, 0)),
                  _full_spec(sq), _full_spec(sq)],
        out_specs=blk,
        out_shape=jax.ShapeDtypeStruct((pairs, 2, FFT_N1, FFT_PITCH, HY_WIDTH), F32),
        compiler_params=_cparams(2),
        name="hy_fft_s2",
    )(a, cst["tw"], g_spec, cst["m2"][0], cst["m2i"][0])
    b = b.reshape(pairs, 2, rows_a, HY_WIDTH)
    y = pl.pallas_call(
        _fft_s3_kernel,
        grid=(pairs, n_cb),
        in_specs=[a_spec, _full_spec((2 * FFT_N1H, 2 * FFT_N1)),
                  z_spec, z_spec, pl.BlockSpec((1, LANES), lambda p, cb: (0, cb))],
        out_specs=z_spec,
        out_shape=jax.ShapeDtypeStruct((pairs, 2, rows_z, HY_WIDTH), F32),
        compiler_params=_cparams(2),
        name="hy_fft_s3",
    )(b, cst["m3"][0], zv, xv, d_bias.reshape(1, HY_WIDTH))
    return y.reshape(batch, rows_z, HY_WIDTH)


def _hy_ctx_kernel(z_ref, x0_ref, g_ref, fwd_ref, inv_ref, bias_ref, y_ref):
    z = z_ref[0]
    x = _bdot(fwd_ref[...], z.astype(BF16))
    xr, xi = x[:FFT_NC], x[FFT_NC:]
    gr, gi = g_ref[0], g_ref[1]
    y = jnp.concatenate([xr * gr - xi * gi, xr * gi + xi * gr], axis=0)
    conv = _bdot(inv_ref[...], y.astype(BF16))
    y_ref[0] = ((conv + z * bias_ref[...]) * x0_ref[0]).astype(BF16)


def _hy_ctx_conv_call(z_ctx, x0_ctx, g_spec, d_bias):
    batch = z_ctx.shape[0]
    cst = _fft_constants()
    tok = pl.BlockSpec((1, CTX_LEN, HY_WIDTH), lambda b: (b, 0, 0))

    def full(shape):
        zeros = (0,) * len(shape)
        return pl.BlockSpec(shape, lambda b: zeros)

    return pl.pallas_call(
        _hy_ctx_kernel,
        grid=(batch,),
        in_specs=[tok, tok, full((2, FFT_NC, HY_WIDTH)),
                  full((2 * FFT_NC, CTX_LEN)), full((CTX_LEN, 2 * FFT_NC)), full((1, HY_WIDTH))],
        out_specs=tok,
        out_shape=jax.ShapeDtypeStruct((batch, CTX_LEN, HY_WIDTH), BF16),
        compiler_params=_cparams(1),
        name="hy_ctx",
    )(z_ctx, x0_ctx, g_spec, cst["mcf"][0], cst["mci"][0], d_bias.reshape(1, HY_WIDTH))


def _hyena_filter(length, fw1, fb1, fw2, fb2, fw3, fb3, fw4, freq):
    t = jnp.linspace(0.0, 1.0, length, dtype=F32)[:, None]
    w = 2.0 * math.pi * jnp.arange(length, dtype=F32)[:, None] / length
    bands = jnp.linspace(1e-4, HY_BANDS - 1, HY_BANDS, dtype=F32)
    emb = jnp.concatenate([t, jnp.cos(bands * w), -jnp.sin(bands * w)], axis=-1)
    emb = jnp.concatenate([emb, emb[::-1]], axis=0)
    hp = lax.Precision.HIGHEST
    h = jnp.sin(freq * (jnp.dot(emb, fw1, precision=hp) + fb1))
    h = jnp.sin(freq * (jnp.dot(h, fw2, precision=hp) + fb2))
    h = jnp.sin(freq * (jnp.dot(h, fw3, precision=hp) + fb3))
    max_decay = math.log(HY_TARGET) / HY_FAST_PCT
    min_decay = math.log(HY_TARGET) / HY_SLOW_PCT
    deltas = jnp.linspace(min_decay, max_decay, HY_WIDTH, dtype=F32)
    h_fwd = jnp.dot(h[:length], fw4[:, :HY_WIDTH], precision=hp).astype(F32) * jnp.exp(-t * jnp.abs(deltas))
    h_bwd_rev = (jnp.dot(h[length:], fw4[:, HY_WIDTH:], precision=hp).astype(F32)
                 * jnp.exp(-t[::-1] * jnp.abs(deltas)))
    return jnp.concatenate([h_fwd[:1] + h_bwd_rev[length - 1:], h_fwd[1:],
                            jnp.zeros((1, HY_WIDTH), F32), h_bwd_rev[:length - 1]], axis=0)


def _odd_in_kernel(x_ref, mod_ref, gain_ref, w_ref, cos_ref, sin_ref, q_ref, k_ref, v_ref, sg_ref):
    h = _modulated(x_ref[0], gain_ref[...], mod_ref, 3, 4).astype(BF16)
    cos = cos_ref[...]
    sin = sin_ref[...]

    def rope_store(dst_ref, col0, scale):
        p = _bdot(h, w_ref[:, col0:col0 + RET_QK])
        for c in range(RET_QK // LANES):
            t = p[:, c * LANES:(c + 1) * LANES]
            half = (c % 2) * LANES
            r = t * cos[:, half:half + LANES] + pltpu.roll(t, LANES // 2, 1) * sin[:, half:half + LANES]
            dst_ref[0, :, c * LANES:(c + 1) * LANES] = (r * scale).astype(BF16)

    rope_store(q_ref, 0, 1.0)
    rope_store(k_ref, RET_QK, RET_KEY_DIM ** -0.5)
    v_ref[0] = _bdot(h, w_ref[:, 2 * RET_QK:2 * RET_QK + RET_V]).astype(BF16)
    sg_ref[0] = _silu(_bdot(h, w_ref[:, 2 * RET_QK + RET_V:])).astype(BF16)


def _odd_in_call(xt, mod_i, gain, w, cos_t, sin_t):
    batch = xt.shape[0]
    rope_spec = pl.BlockSpec((TM, RET_KEY_DIM), lambda b, j: (j, 0))
    return pl.pallas_call(
        _odd_in_kernel,
        grid=(batch, NT_ALL),
        in_specs=[_tok_spec(D_MODEL), _mod_spec(batch), _full_spec((1, D_MODEL)),
                  _full_spec((D_MODEL, w.shape[1])), rope_spec, rope_spec],
        out_specs=[_tok_spec(RET_QK), _tok_spec(RET_QK), _tok_spec(RET_V), _tok_spec(RET_V)],
        out_shape=[jax.ShapeDtypeStruct((batch, T_TOK, RET_QK), BF16)] * 2
        + [jax.ShapeDtypeStruct((batch, T_TOK, RET_V), BF16)] * 2,
        compiler_params=_cparams(2),
        name="odd_in",
    )(xt, mod_i, gain, w, cos_t, sin_t)


def _rope_tables():
    t = np.arange(SEQ)
    n_freq = RET_KEY_DIM // 4
    inv = ROPE_BASE ** (-jnp.arange(n_freq, dtype=F32) / n_freq)
    ang_r = jnp.asarray(t // GRID_W, F32)[:, None] * inv
    ang_c = jnp.asarray(t % GRID_W, F32)[:, None] * inv
    cr, sr, cc, sc = jnp.cos(ang_r), jnp.sin(ang_r), jnp.cos(ang_c), jnp.sin(ang_c)
    cos_l = jnp.concatenate([cr, cr, cc, cc], axis=-1)
    sin_l = jnp.concatenate([-sr, sr, -sc, sc], axis=-1)
    cos_t = jnp.concatenate([cos_l, jnp.ones((CTX_LEN, RET_KEY_DIM), F32)], axis=0)
    sin_t = jnp.concatenate([sin_l, jnp.zeros((CTX_LEN, RET_KEY_DIM), F32)], axis=0)
    return cos_t, sin_t


def _ret_kernel(lg_ref, q_ref, k_ref, v_ref, sg_ref, y_ref, o_acc, state_f, state_b):
    head = pl.program_id(1)
    n_c = RET_BLOCK
    n_lat = SEQ // n_c
    assert CTX_LEN == n_c and n_lat % 2 == 0
    ii = lax.broadcasted_iota(jnp.int32, (n_c, n_c), 0).astype(F32)
    jj = lax.broadcasted_iota(jnp.int32, (n_c, n_c), 1).astype(F32)
    row_k = lax.broadcasted_iota(jnp.int32, (n_c, RET_KEY_DIM), 0).astype(F32)
    row_v = lax.broadcasted_iota(jnp.int32, (n_c, RET_VAL_DIM), 0).astype(F32)

    def decays(backward):
        lg = lg_ref[1 if backward else 0, head]
        if backward:
            diff = jj - ii
            xi = jnp.exp(lg * (n_c - row_v))
            zeta = jnp.exp(lg * row_k)
        else:
            diff = ii - jj
            xi = jnp.exp(lg * (row_v + 1.0))
            zeta = jnp.exp(lg * (n_c - 1.0 - row_k))
        dmask = jnp.where(diff >= 0, jnp.exp(lg * jnp.maximum(diff, 0.0)), 0.0)
        g_chunk = jnp.exp(lg * n_c + jnp.zeros((1, RET_VAL_DIM), F32))
        return dmask, xi, zeta, g_chunk

    def advance(chunk, state, consts, finalize):
        dmask, xi, zeta, g_chunk = consts
        r = chunk * n_c if isinstance(chunk, int) else pl.multiple_of(chunk * n_c, n_c)
        qc = q_ref[0, pl.ds(r, n_c), :]
        kc = k_ref[0, pl.ds(r, n_c), :]
        vc = v_ref[0, pl.ds(r, n_c), :]
        st = state[...]
        inner = lax.dot_general(qc, kc, _NT_DIMS, preferred_element_type=F32) * dmask
        o = _bdot(inner.astype(BF16), vc) + _bdot(qc, st.astype(BF16)) * xi
        kz = (kc.astype(F32) * zeta).astype(BF16)
        state[...] = st * g_chunk + lax.dot_general(kz, vc, _TN_DIMS, preferred_element_type=F32)
        if finalize:
            tot = o_acc[pl.ds(r, n_c), :] + o
            mu = jnp.mean(tot, axis=-1, keepdims=True)
            cen = tot - mu
            var = jnp.mean(cen * cen, axis=-1, keepdims=True)
            yn = cen * lax.rsqrt(var + GN_EPS)
            y_ref[0, pl.ds(r, n_c), :] = (sg_ref[0, pl.ds(r, n_c), :].astype(F32) * yn).astype(BF16)
        else:
            o_acc[pl.ds(r, n_c), :] = o

    consts_f = decays(False)
    consts_b = decays(True)
    state_f[...] = jnp.zeros_like(state_f)
    state_b[...] = jnp.zeros_like(state_b)

    def both(chunk_f, chunk_b, finalize):
        advance(chunk_f, state_f, consts_f, finalize)
        advance(chunk_b, state_b, consts_b, finalize)

    advance(n_lat, state_f, consts_f, False)
    advance(n_lat, state_b, consts_b, True)

    def first_half(s, carry):
        both(s, n_lat - 1 - s, False)
        return carry

    def second_half(s, carry):
        both(s, n_lat - 1 - s, True)
        return carry

    lax.fori_loop(0, n_lat // 2, first_half, 0, unroll=2)
    lax.fori_loop(n_lat // 2, n_lat, second_half, 0, unroll=2)


def _ret_call(lg, q, k, v, sg):
    batch = q.shape[0]
    qk_spec = pl.BlockSpec((1, T_TOK, RET_KEY_DIM), lambda b, h: (b, 0, h))
    v_spec = pl.BlockSpec((1, T_TOK, RET_VAL_DIM), lambda b, h: (b, 0, h))
    return pl.pallas_call(
        _ret_kernel,
        grid=(batch, RET_HEADS),
        in_specs=[pl.BlockSpec(memory_space=pltpu.SMEM), qk_spec, qk_spec, v_spec, v_spec],
        out_specs=v_spec,
        out_shape=jax.ShapeDtypeStruct((batch, T_TOK, RET_V), BF16),
        scratch_shapes=[pltpu.VMEM((T_TOK, RET_VAL_DIM), F32),
                        pltpu.VMEM((RET_KEY_DIM, RET_VAL_DIM), F32),
                        pltpu.VMEM((RET_KEY_DIM, RET_VAL_DIM), F32)],
        compiler_params=_cparams(2),
        name="retention",
    )(lg, q, k, v, sg)


def _fft_s2_fwd_kernel(a_ref, tw_ref, fhi_ref, flo_ref, g_ref):
    reps = HY_WIDTH // LANES
    for kk in range(FFT_KB):
        twr = jnp.concatenate([tw_ref[0, kk]] * reps, axis=1)
        twi = jnp.concatenate([tw_ref[1, kk]] * reps, axis=1)
        ar = a_ref[0, kk]
        ai = a_ref[1, kk]
        d = jnp.concatenate([ar * twr - ai * twi, ar * twi + ai * twr], axis=0)
        x = _dot3(fhi_ref[...], flo_ref[...], d)
        g_ref[0, kk] = x[:FFT_N2]
        g_ref[1, kk] = x[FFT_N2:]


def _dft_dense_kernel(x_ref, mhi_ref, mlo_ref, o_ref):
    o_ref[...] = _dot3(mhi_ref[...], mlo_ref[...], x_ref[...])


def _filter_spectra(filter_params):
    cst = _fft_constants()
    n_lane = FFT_N2 * HY_WIDTH
    filt = _hyena_filter(SEQ, *filter_params).reshape(1, 1, FFT_N1, n_lane)
    a = pl.pallas_call(
        _fft_s1_flat_kernel,
        grid=(1, n_lane // FFT_W),
        in_specs=[pl.BlockSpec((1, 1, FFT_N1, FFT_W), lambda p, w: (p, 0, 0, w)),
                  _full_spec((2 * FFT_N1, FFT_N1)), _full_spec((2 * FFT_N1, FFT_N1))],
        out_specs=pl.BlockSpec((1, 2 * FFT_N1, FFT_W), lambda p, w: (p, 0, w)),
        out_shape=jax.ShapeDtypeStruct((1, 2 * FFT_N1, n_lane), F32),
        compiler_params=_cparams(2),
        name="filt_fft_s1",
    )(filt, *cst["m1r"])
    a = a.reshape(2, FFT_N1, FFT_N2, HY_WIDTH)
    blk = pl.BlockSpec((2, FFT_KB, FFT_N2, HY_WIDTH), lambda kb: (0, kb, 0, 0))
    sq = (2 * FFT_N2, 2 * FFT_N2)
    g_lat = pl.pallas_call(
        _fft_s2_fwd_kernel,
        grid=(FFT_N1 // FFT_KB,),
        in_specs=[blk, pl.BlockSpec((2, FFT_KB, FFT_N2, LANES), lambda kb: (0, kb, 0, 0)),
                  pl.BlockSpec(sq, lambda kb: (0, 0)), pl.BlockSpec(sq, lambda kb: (0, 0))],
        out_specs=blk,
        out_shape=jax.ShapeDtypeStruct((2, FFT_N1, FFT_N2, HY_WIDTH), F32),
        compiler_params=_cparams(1),
        name="filt_fft_s2",
    )(a, cst["tw"], *cst["m2"])
    filt_ctx = _hyena_filter(CTX_LEN, *filter_params)
    g_ctx = pl.pallas_call(
        _dft_dense_kernel,
        out_shape=jax.ShapeDtypeStruct((2 * FFT_NC, HY_WIDTH), F32),
        compiler_params=pltpu.CompilerParams(vmem_limit_bytes=VMEM_LIMIT_BYTES),
        name="filt_dft_ctx",
    )(filt_ctx, *cst["mcf_full"])
    return g_lat, g_ctx.reshape(2, FFT_NC, HY_WIDTH)


def _even_mixer(xt, mod_i, gain, w_in, w_out, q_gain, k_gain, rpb, conv_w, conv_b, filter_params, d_bias):
    scale = NA_HEAD_DIM ** -0.5
    qg = (jnp.tile(q_gain, NA_HEADS) * scale).reshape(1, NA_WIDTH)
    kg = jnp.tile(k_gain, NA_HEADS).reshape(1, NA_WIDTH)
    head_of = np.arange(NA_WIDTH) // NA_HEAD_DIM
    bd = jnp.asarray((head_of[:, None] == head_of[None, :]) / NA_HEAD_DIM, BF16)
    q, k, v, u = _even_in_call(xt, mod_i, gain, w_in.astype(BF16), qg, kg, bd)
    a = _na_call(q, k, v, _na_bias_table(rpb))
    z_lat, x0_lat, z_ctx, x0_ctx = _hy_pre_call(u, conv_w, conv_b.reshape(1, -1))
    g_lat, g_ctx = _filter_spectra(filter_params)
    y_lat = _hy_long_conv_call(z_lat, x0_lat, g_lat, d_bias)
    y_ctx = _hy_ctx_conv_call(z_ctx, x0_ctx, g_ctx, d_bias)
    return a, y_lat.reshape(xt.shape[0], FFT_N1H, FFT_PITCH, HY_WIDTH), y_ctx, w_out.astype(BF16)


def _odd_mixer(xt, mod_i, gain, w_in, w_out, logit_f, logit_b, rope):
    q, k, v, sg = _odd_in_call(xt, mod_i, gain, w_in.astype(BF16), *rope)
    lg = jnp.stack([jax.nn.log_sigmoid(logit_f.astype(F32)), jax.nn.log_sigmoid(logit_b.astype(F32))])
    return _ret_call(lg, q, k, v, sg), w_out.astype(BF16)


def kernel(x, c, ctx, c_ctx, w_mod, b_mod, norm_gain, ffn_a_in, ffn_a_out, ffn_b_in, ffn_b_out,
           even_in, even_out, na_q_gain, na_k_gain, na_rpb, hy_conv_w, hy_conv_b,
           hy_fw1, hy_fb1, hy_fw2, hy_fb2, hy_fw3, hy_fb3, hy_fw4, hy_freq, hy_bias,
           ret_in, ret_out, ret_logit_f, ret_logit_b):
    batch = x.shape[0]
    assert x.shape == (batch, SEQ, D_MODEL) and ctx.shape == (batch, CTX_LEN, D_MODEL)
    assert batch % 2 == 0 and batch < MOD_ROWS
    c_rows = jnp.concatenate([c, c_ctx[None], jnp.zeros((MOD_ROWS - batch - 1, D_MODEL), F32)], axis=0)
    mod_all = _mod_call(c_rows, w_mod, b_mod).reshape(DEPTH, MOD_ROWS, N_MOD, D_MODEL)
    rope = _rope_tables()
    for i in range(DEPTH):
        last = i == DEPTH - 1
        mod_i = mod_all[i]
        gains = norm_gain[i].reshape(3, 1, D_MODEL)
        w_a = (ffn_a_in[i].astype(BF16), ffn_a_out[i].astype(BF16))
        w_b = (ffn_b_in[i].astype(BF16), ffn_b_out[i].astype(BF16))
        if i == 0:
            xt = _ffn_call("split", (x, ctx), mod_i, gains[0], *w_a, (0, 1, 2), NT_ALL, batch)
        else:
            xt = _ffn_call("stream", (xt,), mod_i, gains[0], *w_a, (0, 1, 2), NT_ALL, batch)
        if i % 2 == 0:
            e = i // 2
            filter_params = (hy_fw1[e], hy_fb1[e], hy_fw2[e], hy_fb2[e], hy_fw3[e], hy_fb3[e],
                             hy_fw4[e], hy_freq[e])
            source = "even"
            mixed = _even_mixer(xt, mod_i, gains[1], even_in[e], even_out[e], na_q_gain[e], na_k_gain[e],
                                na_rpb[e], hy_conv_w[e], hy_conv_b[e], filter_params, hy_bias[e])
        else:
            o = i // 2
            source = "odd"
            mixed = _odd_mixer(xt, mod_i, gains[1], ret_in[o], ret_out[o], ret_logit_f[o], ret_logit_b[o], rope)
        xt = _ffn_call(source, (xt,) + mixed, mod_i, gains[2], *w_b, (6, 7, 8),
                       NT_LAT if last else NT_ALL, batch)
    return xt
```

```python
import functools
import math

import ml_dtypes
import numpy as np
import jax
import jax.numpy as jnp
from jax import lax
from jax.experimental import pallas as pl
from jax.experimental.pallas import tpu as pltpu

F32 = jnp.float32
BF16 = jnp.bfloat16

D_MODEL = 1024
SEQ = 4096
DEPTH = 4
GRID_W = 64
CTX_LEN = 256
N_MOD = 9
RMS_EPS = 1e-6
GN_EPS = 1e-6
D_FF = 2816
NA_HEADS = 8
NA_HEAD_DIM = 64
NA_WIDTH = NA_HEADS * NA_HEAD_DIM
WIN_R = 8
WIN_C = 16
HY_WIDTH = D_MODEL - NA_WIDTH
HY_BANDS = 8
HY_TARGET = 1e-2
HY_FAST_PCT = 0.3
HY_SLOW_PCT = 1.5
RET_HEADS = 4
RET_KEY_DIM = D_MODEL // RET_HEADS
RET_VAL_DIM = 2 * RET_KEY_DIM
RET_QK = RET_HEADS * RET_KEY_DIM
RET_V = RET_HEADS * RET_VAL_DIM
ROPE_BASE = 10000.0

LANES = 128
VMEM_LIMIT_BYTES = 56 * 2**20
T_TOK = SEQ + CTX_LEN
TM = CTX_LEN
NT_LAT = SEQ // TM
NT_ALL = T_TOK // TM
MOD_ROWS = 16
FF_CHUNKS = ((0, 1536), (1536, 1280))
RET_BLOCK = 256

NA_RB = 4
NA_KR = NA_RB + WIN_R - 1
GRID_H = SEQ // GRID_W
NA_NQ = NA_RB * GRID_W
NA_NK = NA_KR * GRID_W
NEG_BIG = -1e30

FFT_N = 2 * SEQ
FFT_N1 = 64
FFT_N2 = 128
FFT_N1H = FFT_N1 // 2
FFT_W = 2048
FFT_KB = 4
FFT_PITCH = FFT_N2 + 8
FFT_NC = 2 * CTX_LEN


def _cparams(n_axes):
    return pltpu.CompilerParams(dimension_semantics=("arbitrary",) * n_axes,
                                vmem_limit_bytes=VMEM_LIMIT_BYTES)


def _bdot(a, b):
    return jnp.dot(a, b, preferred_element_type=F32)


_NT_DIMS = (((1,), (1,)), ((), ()))
_TN_DIMS = (((0,), (0,)), ((), ()))


def _split_hi_lo(m):
    hi = m.astype(BF16)
    lo = (m - hi.astype(F32)).astype(BF16)
    return hi, lo


def _dot3(m_hi, m_lo, d):
    d_hi, d_lo = _split_hi_lo(d)
    return _bdot(m_hi, d_hi) + _bdot(m_lo, d_hi) + _bdot(m_hi, d_lo)


def _modulated(x, gain, mod_ref, shift_row, scale_row):
    ms = jnp.mean(x * x, axis=-1, keepdims=True)
    y = x * lax.rsqrt(ms + RMS_EPS) * gain
    return (y * (1.0 + mod_ref[0, scale_row:scale_row + 1, :])
            + mod_ref[0, shift_row:shift_row + 1, :])


def _silu(a):
    return a * jax.nn.sigmoid(a)


def _mod_kernel(c_ref, w_ref, b_ref, o_ref):
    s = _silu(c_ref[...])
    o_ref[0] = jnp.dot(s, w_ref[0], precision=lax.Precision.HIGHEST,
                       preferred_element_type=F32) + b_ref[0]


def _mod_call(c_rows, w_mod, b_mod):
    depth, d, n = w_mod.shape
    tn = 1024
    return pl.pallas_call(
        _mod_kernel,
        grid=(depth, n // tn),
        in_specs=[pl.BlockSpec((MOD_ROWS, d), lambda i, j: (0, 0)),
                  pl.BlockSpec((1, d, tn), lambda i, j: (i, 0, j)),
                  pl.BlockSpec((1, 1, tn), lambda i, j: (i, 0, j))],
        out_specs=pl.BlockSpec((1, MOD_ROWS, tn), lambda i, j: (i, 0, j)),
        out_shape=jax.ShapeDtypeStruct((depth, MOD_ROWS, n), F32),
        compiler_params=_cparams(2),
        name="mod",
    )(c_rows, w_mod, b_mod.reshape(depth, 1, n))


def _tok_spec(width):
    return pl.BlockSpec((1, TM, width), lambda b, j: (b, j, 0))


def _mod_spec(batch):
    return pl.BlockSpec((1, N_MOD, D_MODEL), lambda b, j: (jnp.where(j == NT_LAT, batch, b), 0, 0))


def _full_spec(shape):
    zeros = (0,) * len(shape)
    return pl.BlockSpec(shape, lambda b, j: zeros)


_FFN_SOURCES = {"stream": 1, "split": 4, "even": 7, "odd": 3}
TM2 = 2 * TM


def _per_half(fn):
    return jnp.concatenate([fn(half, slice(half * TM, (half + 1) * TM)) for half in range(2)], axis=0)


def _ffn_kernel(*refs, rows, source, n_tiles, n_pairs):
    shift_row, scale_row, gate_row = rows
    n_src = _FFN_SOURCES[source]
    src = refs[:n_src]
    modn0, modn1, modc0, modc1, gain_ref, win_ref, wout_ref, o_ref, h_a, x_a, h_b, x_b = refs[n_src:]
    modn = (modn0, modn1)
    modc = (modc0, modc1)
    step = pl.program_id(0)
    first_tile_next = 2 * jnp.minimum(step, n_pairs - 1)

    def is_ctx(half):
        return (first_tile_next + half) % n_tiles == NT_LAT

    def gated(x_ref, o):
        return _per_half(lambda half, r: x_ref[0, r, :] + modn[half][0, 5:6, :] * o[r])

    def next_rows():
        if source == "stream":
            return src[0][0]
        if source == "split":
            lat, ctx = src[0:2], src[2:4]
            return _per_half(lambda half, r: jnp.where(is_ctx(half), ctx[half][0], lat[half][0]))
        if source == "even":
            x_ref, a_ref, yl0, yl1, yc0, yc1, wo_ref = src
            y_lat, y_ctx = (yl0, yl1), (yc0, yc1)

            def hyena_rows(half, r):
                slabs = [y_lat[half][0, s, 0:FFT_N2, :] for s in range(TM // FFT_N2)]
                return jnp.where(is_ctx(half), y_ctx[half][0], jnp.concatenate(slabs, axis=0)).astype(BF16)

            o = _bdot(a_ref[0], wo_ref[0:NA_WIDTH, :]) + _bdot(_per_half(hyena_rows), wo_ref[NA_WIDTH:, :])
            return gated(x_ref, o)
        x_ref, y_ref, wo_ref = src
        return gated(x_ref, _bdot(y_ref[0], wo_ref[...]))

    def body(h_read, x_read, h_write, x_write):
        x_next = next_rows()
        h_write[...] = _per_half(lambda half, r: _modulated(
            x_next[r], gain_ref[...], modn[half], shift_row, scale_row)).astype(BF16)
        x_write[...] = x_next
        h = h_read[...]
        acc = jnp.zeros((TM2, D_MODEL), F32)
        for start, size in FF_CHUNKS:
            a = _bdot(h, win_ref[:, start:start + size])
            b = _bdot(h, win_ref[:, D_FF + start:D_FF + start + size])
            g = (_silu(a) * b).astype(BF16)
            acc = acc + _bdot(g, wout_ref[start:start + size, :])
        o_ref[0] = _per_half(lambda half, r: x_read[r, :] + (0.5 * modc[half][0, gate_row:gate_row + 1, :]) * acc[r])

    @pl.when(step == 0)
    def _():
        h_b[...] = jnp.zeros_like(h_b)
        x_b[...] = jnp.zeros_like(x_b)

    @pl.when(step % 2 == 0)
    def _():
        body(h_b, x_b, h_a, x_a)

    @pl.when(step % 2 == 1)
    def _():
        body(h_a, x_a, h_b, x_b)


def _ffn_call(source, srcs, mod_i, gain, w_in, w_out, rows, n_tiles, batch):
    n_all = batch * n_tiles
    assert n_all % 2 == 0 and n_tiles in (NT_LAT, NT_ALL)
    n_pairs = n_all // 2

    def next_pair(s):
        return jnp.minimum(s, n_pairs - 1)

    def cur_pair(s):
        return jnp.maximum(s - 1, 0)

    def tile_of(pair_of, half):
        def fn(s):
            t = 2 * pair_of(s) + half
            return t // n_tiles, t % n_tiles
        return fn

    def mod_spec(pair_of, half):
        def index(s):
            b, j = tile_of(pair_of, half)(s)
            return jnp.where(j == NT_LAT, batch, b), 0, 0
        return pl.BlockSpec((1, N_MOD, D_MODEL), index)

    def tok(arr):
        width = arr.shape[-1]
        if n_tiles == NT_ALL:
            return arr.reshape(1, batch * T_TOK, width), pl.BlockSpec(
                (1, TM2, width), lambda s: (0, next_pair(s), 0))
        per_sample = n_tiles // 2
        return arr, pl.BlockSpec((1, TM2, width),
                                 lambda s: (next_pair(s) // per_sample, next_pair(s) % per_sample, 0))

    def lat_tile(block, n_trailing, half):
        def index(s):
            b, j = tile_of(next_pair, half)(s)
            return (b, jnp.minimum(j, NT_LAT - 1)) + (0,) * n_trailing
        return pl.BlockSpec(block, index)

    def ctx_tile(block, half):
        return pl.BlockSpec(block, lambda s: (tile_of(next_pair, half)(s)[0],) + (0,) * (len(block) - 1))

    def full(shape):
        zeros = (0,) * len(shape)
        return pl.BlockSpec(shape, lambda s: zeros)

    halves = (0, 1)
    if source == "stream":
        (xt,) = srcs
        xt, x_spec = tok(xt)
        args, src_specs = (xt,), [x_spec]
    elif source == "split":
        x, ctx = srcs
        args = (x, x, ctx, ctx)
        src_specs = ([lat_tile((1, TM, D_MODEL), 1, h) for h in halves]
                     + [ctx_tile((1, CTX_LEN, D_MODEL), h) for h in halves])
    elif source == "even":
        xt, a, y_lat, y_ctx, wo = srcs
        (xt, x_spec), (a, a_spec) = tok(xt), tok(a)
        args = (xt, a, y_lat, y_lat, y_ctx, y_ctx, wo)
        src_specs = ([x_spec, a_spec]
                     + [lat_tile((1, TM // FFT_N2, FFT_PITCH, HY_WIDTH), 2, h) for h in halves]
                     + [ctx_tile((1, CTX_LEN, HY_WIDTH), h) for h in halves] + [full((D_MODEL, D_MODEL))])
    else:
        xt, y, wo = srcs
        (xt, x_spec), (y, y_spec) = tok(xt), tok(y)
        args = (xt, y, wo)
        src_specs = [x_spec, y_spec, full((RET_V, D_MODEL))]
    out = pl.pallas_call(
        functools.partial(_ffn_kernel, rows=rows, source=source, n_tiles=n_tiles, n_pairs=n_pairs),
        grid=(n_pairs + 1,),
        in_specs=src_specs + [mod_spec(next_pair, 0), mod_spec(next_pair, 1),
                              mod_spec(cur_pair, 0), mod_spec(cur_pair, 1), full((1, D_MODEL)),
                              full((D_MODEL, 2 * D_FF)), full((D_FF, D_MODEL))],
        out_specs=pl.BlockSpec((1, TM2, D_MODEL), lambda s: (0, cur_pair(s), 0)),
        out_shape=jax.ShapeDtypeStruct((1, n_all * TM, D_MODEL), F32),
        scratch_shapes=[pltpu.VMEM((TM2, D_MODEL), BF16), pltpu.VMEM((TM2, D_MODEL), F32),
                        pltpu.VMEM((TM2, D_MODEL), BF16), pltpu.VMEM((TM2, D_MODEL), F32)],
        compiler_params=_cparams(1),
        name="ffn_" + source,
    )(*args, mod_i, mod_i, mod_i, mod_i, gain, w_in, w_out)
    return out.reshape(batch, n_tiles * TM, D_MODEL)


HALO = 8


def _even_in_kernel(x_ref, xp_ref, xn_ref, mod_ref, gain_ref, w_ref, qg_ref, kg_ref, bd_ref, cw_ref, cb_ref,
                    q_ref, k_ref, v_ref, zl_ref, xl_ref, zc_ref, xc_ref):
    tile = pl.program_id(1)
    h = _modulated(x_ref[0], gain_ref[...], mod_ref, 3, 4).astype(BF16)

    def head_norm(t, g):
        ms = _bdot((t * t).astype(BF16), bd_ref[...])
        return (t * lax.rsqrt(ms + RMS_EPS) * g).astype(BF16)

    q_ref[0] = head_norm(_bdot(h, w_ref[:, 0:NA_WIDTH]), qg_ref[...])
    k_ref[0] = head_norm(_bdot(h, w_ref[:, NA_WIDTH:2 * NA_WIDTH]), kg_ref[...])
    v_ref[0] = _bdot(h, w_ref[:, 2 * NA_WIDTH:3 * NA_WIDTH]).astype(BF16)

    halo = jnp.concatenate([xp_ref[0], xn_ref[0]], axis=0)
    h_halo = _modulated(halo, gain_ref[...], mod_ref, 3, 4).astype(BF16)
    seq_start = jnp.logical_or(tile == 0, tile == NT_LAT)
    seq_end = jnp.logical_or(tile == NT_LAT - 1, tile == NT_LAT)
    row = lax.broadcasted_iota(jnp.int32, (TM, HY_WIDTH), 0)

    def conv_group(g):
        c0 = g * HY_WIDTH
        w_g = w_ref[:, 3 * NA_WIDTH + c0:3 * NA_WIDTH + c0 + HY_WIDTH]
        u = _bdot(h, w_g)
        u_halo = _bdot(h_halo, w_g)
        prev_row = jnp.where(seq_start, 0.0, u_halo[HALO - 1:HALO])
        next_row = jnp.where(seq_end, 0.0, u_halo[HALO:HALO + 1])
        before = jnp.where(row == 0, prev_row, pltpu.roll(u, 1, 0))
        after = jnp.where(row == TM - 1, next_row, pltpu.roll(u, TM - 1, 0))
        taps = cw_ref[:, c0:c0 + HY_WIDTH]
        return before * taps[0:1] + u * taps[1:2] + after * taps[2:3] + cb_ref[:, c0:c0 + HY_WIDTH]

    x0 = conv_group(0)
    z = conv_group(2) * conv_group(1)

    @pl.when(tile < NT_LAT)
    def _():
        pad = jnp.zeros((FFT_PITCH - FFT_N2, HY_WIDTH), F32)
        for s in range(TM // FFT_N2):
            zl_ref[0, s, 0:FFT_N2, :] = z[s * FFT_N2:(s + 1) * FFT_N2]
            xl_ref[0, s, 0:FFT_N2, :] = x0[s * FFT_N2:(s + 1) * FFT_N2]
            zl_ref[0, s, FFT_N2:FFT_PITCH, :] = pad
            xl_ref[0, s, FFT_N2:FFT_PITCH, :] = pad

    @pl.when(tile == NT_LAT)
    def _():
        zc_ref[0] = z
        xc_ref[0] = x0


def _even_in_call(xt, mod_i, gain, w, q_gain, k_gain, bd, conv_w, conv_b):
    batch = xt.shape[0]
    n_in = w.shape[1]
    per_tile = TM // HALO
    last_halo = T_TOK // HALO - 1
    slabs = pl.BlockSpec((1, TM // FFT_N2, FFT_PITCH, HY_WIDTH),
                         lambda b, j: (b, jnp.minimum(j, NT_LAT - 1), 0, 0))
    ctx = pl.BlockSpec((1, CTX_LEN, HY_WIDTH), lambda b, j: (b, 0, 0))
    return pl.pallas_call(
        _even_in_kernel,
        grid=(batch, NT_ALL),
        in_specs=[_tok_spec(D_MODEL),
                  pl.BlockSpec((1, HALO, D_MODEL), lambda b, j: (b, jnp.maximum(j * per_tile - 1, 0), 0)),
                  pl.BlockSpec((1, HALO, D_MODEL),
                               lambda b, j: (b, jnp.minimum((j + 1) * per_tile, last_halo), 0)),
                  _mod_spec(batch), _full_spec((1, D_MODEL)),
                  _full_spec((D_MODEL, n_in)), _full_spec((1, NA_WIDTH)), _full_spec((1, NA_WIDTH)),
                  _full_spec((NA_WIDTH, NA_WIDTH)), _full_spec((3, 3 * HY_WIDTH)), _full_spec((1, 3 * HY_WIDTH))],
        out_specs=[_tok_spec(NA_WIDTH), _tok_spec(NA_WIDTH), _tok_spec(NA_WIDTH), slabs, slabs, ctx, ctx],
        out_shape=[jax.ShapeDtypeStruct((batch, T_TOK, NA_WIDTH), BF16)] * 3
        + [jax.ShapeDtypeStruct((batch, FFT_N1H, FFT_PITCH, HY_WIDTH), F32)] * 2
        + [jax.ShapeDtypeStruct((batch, CTX_LEN, HY_WIDTH), F32)] * 2,
        compiler_params=_cparams(2),
        name="even_in",
    )(xt, xt, xt, mod_i, gain, w, q_gain, k_gain, bd, conv_w, conv_b)


def _na_kernel(q_ref, k_ref, v_ref, bias_ref, o_ref):
    lane = lax.broadcasted_iota(jnp.int32, (1, LANES), 1)
    first_head = lane < NA_HEAD_DIM
    k_ctx = k_ref[0, SEQ:T_TOK, :]
    v_ctx = v_ref[0, SEQ:T_TOK, :]

    def stack_heads(q):
        zero = jnp.zeros_like(q)
        return jnp.concatenate([jnp.where(first_head, q, zero), jnp.where(first_head, zero, q)], axis=0)

    def attend(qs, scores_and_values):
        s_list = [lax.dot_general(qs, kk, _NT_DIMS, preferred_element_type=F32) if bias is None
                  else lax.dot_general(qs, kk, _NT_DIMS, preferred_element_type=F32) + bias
                  for kk, _, bias in scores_and_values]
        m = functools.reduce(jnp.maximum, [jnp.max(s, axis=-1, keepdims=True) for s in s_list])
        o = functools.reduce(jnp.add, [
            _bdot(jnp.exp(s - m).astype(BF16),
                  jnp.concatenate([vv, jnp.ones((vv.shape[0], LANES), BF16)], axis=1))
            for s, (_, vv, _) in zip(s_list, scores_and_values)])
        o = o[:, :LANES] / o[:, LANES:]
        n = qs.shape[0] // 2
        return jnp.where(first_head, o[:n], o[n:]).astype(BF16)

    def block(i, carry):
        r0 = i * NA_RB
        u0 = jnp.clip(r0 - WIN_R // 2, 0, GRID_H - NA_KR)
        pattern = jnp.where(i == 0, 0, jnp.where(i == GRID_H // NA_RB - 1, 2, 1))
        q0 = pl.multiple_of(r0 * GRID_W, NA_NQ)
        k0 = pl.multiple_of(u0 * GRID_W, GRID_W)
        qs = stack_heads(q_ref[0, pl.ds(q0, NA_NQ), :])
        k_win = k_ref[0, pl.ds(k0, NA_NK), :]
        v_win = v_ref[0, pl.ds(k0, NA_NK), :]
        o_ref[0, pl.ds(q0, NA_NQ), :] = attend(
            qs, [(k_win, v_win, bias_ref[pattern, 0]), (k_ctx, v_ctx, None)])
        return carry

    lax.fori_loop(0, GRID_H // NA_RB, block, 0, unroll=4)
    o_ref[0, SEQ:T_TOK, :] = attend(stack_heads(q_ref[0, SEQ:T_TOK, :]), [(k_ctx, v_ctx, None)])


def _na_call(q, k, v, bias):
    batch = q.shape[0]
    n_pairs = NA_WIDTH // LANES
    spec = pl.BlockSpec((1, T_TOK, LANES), lambda b, p: (b, 0, p))
    return pl.pallas_call(
        _na_kernel,
        grid=(batch, n_pairs),
        in_specs=[spec, spec, spec,
                  pl.BlockSpec((3, 1, 2 * NA_NQ, NA_NK), lambda b, p: (0, p, 0, 0))],
        out_specs=spec,
        out_shape=jax.ShapeDtypeStruct((batch, T_TOK, NA_WIDTH), BF16),
        compiler_params=_cparams(2),
        name="na_attn",
    )(q, k, v, bias)


def _na_bias_table(rpb):
    j = np.arange(NA_RB)[:, None, None, None]
    c = np.arange(GRID_W)[None, :, None, None]
    kk = np.arange(NA_KR)[None, None, :, None]
    kc = np.arange(GRID_W)[None, None, None, :]
    cs = np.clip(c - WIN_C // 2, 0, GRID_W - WIN_C)
    col_ok = (kc >= cs) & (kc < cs + WIN_C)
    dc = np.clip(kc - c + WIN_C - 1, 0, 2 * WIN_C - 2)[0, :, 0, :]
    col_sel = (dc[..., None] == np.arange(2 * WIN_C - 1)).astype(np.float32)
    shape = (NA_RB, GRID_W, NA_KR, GRID_W)
    oks, row_sels = [], []
    for off, rs_rel in ((0, 0 * j), (-(WIN_R // 2), j), (-(NA_KR - NA_RB), NA_KR - WIN_R + 0 * j)):
        row_ok = (kk >= rs_rel) & (kk < rs_rel + WIN_R)
        dr = np.clip(off + kk - j + WIN_R - 1, 0, 2 * WIN_R - 2)[:, 0, :, 0]
        row_sels.append((dr[..., None] == np.arange(2 * WIN_R - 1)).astype(np.float32))
        oks.append(np.broadcast_to(row_ok & col_ok, shape))
    vals = jnp.einsum("pjkr,hrc,qmc->phjqkm", np.stack(row_sels), rpb.astype(F32), col_sel,
                      precision=lax.Precision.HIGHEST)
    table = jnp.where(np.stack(oks)[:, None], vals, NEG_BIG)
    return table.reshape(3, NA_HEADS // 2, 2 * NA_NQ, NA_NK)


def _complex_block(re, im):
    return np.block([[re, -im], [im, re]])


@functools.lru_cache(maxsize=None)
def _fft_constants():
    k1 = np.arange(FFT_N1)
    n1 = np.arange(FFT_N1H)
    f1 = np.exp(-2j * np.pi * np.outer(k1, n1) / FFT_N1)
    m1 = _complex_block(f1.real, f1.imag)
    f1_full = np.exp(-2j * np.pi * np.outer(k1, k1) / FFT_N1)
    m1r = np.concatenate([f1_full.real, f1_full.imag], axis=0)
    n2 = np.arange(FFT_N2)
    tw = np.exp(-2j * np.pi * np.outer(k1, n2) / FFT_N)
    tw = np.stack([tw.real, tw.imag])[..., None] * np.ones((1, 1, 1, LANES))
    f2 = np.exp(-2j * np.pi * np.outer(n2, n2) / FFT_N2)
    m2 = _complex_block(f2.real, f2.imag)
    m2i = _complex_block(f2.real, -f2.imag)
    c1 = np.exp(2j * np.pi * np.outer(n1, k1) / FFT_N1) / FFT_N
    m3 = _complex_block(c1.real, c1.imag)
    kc = np.arange(FFT_NC)
    nc = np.arange(CTX_LEN)
    ang = 2 * np.pi * np.outer(kc, nc) / FFT_NC
    mcf = np.concatenate([np.cos(ang), -np.sin(ang)], axis=0)
    ang_full = 2 * np.pi * np.outer(kc, kc) / FFT_NC
    mcf_full = np.concatenate([np.cos(ang_full), -np.sin(ang_full)], axis=0)
    mci = np.concatenate([np.cos(ang.T), -np.sin(ang.T)], axis=1) / FFT_NC

    def hi_lo(m):
        hi = m.astype(ml_dtypes.bfloat16)
        lo = (m - hi.astype(np.float64)).astype(ml_dtypes.bfloat16)
        return hi, lo

    return dict(m1=hi_lo(m1), m1r=hi_lo(m1r), m2=hi_lo(m2), m2i=hi_lo(m2i), m3=hi_lo(m3),
                mcf=hi_lo(mcf), mcf_full=hi_lo(mcf_full), mci=hi_lo(mci), tw=tw.astype(np.float32))


def _fft_s1_flat_kernel(z_ref, mhi_ref, mlo_ref, a_ref):
    d = z_ref[0].reshape(-1, FFT_W)
    a_ref[0] = _dot3(mhi_ref[...], mlo_ref[...], d)


def _slab_rows(n2, n_slabs):
    return pl.ds(n2, n_slabs, stride=FFT_PITCH)


def _fft_s1_kernel(z_ref, m_ref, a_ref):
    def step(i, carry):
        cols = []
        for dn in range(2):
            rows = _slab_rows(2 * i + dn, FFT_N1H)
            cols.append(jnp.concatenate([z_ref[0, 0, rows, :], z_ref[0, 1, rows, :]], axis=0))
        r = _bdot(m_ref[...], jnp.concatenate(cols, axis=1).astype(BF16))
        for dn in range(2):
            rows = _slab_rows(2 * i + dn, FFT_N1)
            a_ref[0, 0, rows, :] = r[:FFT_N1, dn * LANES:(dn + 1) * LANES]
            a_ref[0, 1, rows, :] = r[FFT_N1:, dn * LANES:(dn + 1) * LANES]
        return carry

    lax.fori_loop(0, FFT_N2 // 2, step, 0, unroll=4)
    for pad_row in range(FFT_N2, FFT_PITCH):
        for part in range(2):
            a_ref[0, part, _slab_rows(pad_row, FFT_N1), :] = jnp.zeros((FFT_N1, LANES), F32)


def _fft_s2_kernel(a_ref, tw_ref, g_ref, fwd_ref, inv_ref, o_ref):
    reps = HY_WIDTH // LANES
    pad = jnp.zeros((FFT_PITCH - FFT_N2, HY_WIDTH), F32)
    for kk in range(FFT_KB):
        twr = jnp.concatenate([tw_ref[0, kk]] * reps, axis=1)
        twi = jnp.concatenate([tw_ref[1, kk]] * reps, axis=1)
        ar = a_ref[0, 0, kk, 0:FFT_N2, :]
        ai = a_ref[0, 1, kk, 0:FFT_N2, :]
        d = jnp.concatenate([ar * twr - ai * twi, ar * twi + ai * twr], axis=0)
        x = _bdot(fwd_ref[...], d.astype(BF16))
        xr, xi = x[:FFT_N2], x[FFT_N2:]
        gr, gi = g_ref[0, kk], g_ref[1, kk]
        y = jnp.concatenate([xr * gr - xi * gi, xr * gi + xi * gr], axis=0)
        b = _bdot(inv_ref[...], y.astype(BF16))
        br, bi = b[:FFT_N2], b[FFT_N2:]
        o_ref[0, 0, kk, 0:FFT_N2, :] = br * twr + bi * twi
        o_ref[0, 1, kk, 0:FFT_N2, :] = bi * twr - br * twi
        o_ref[0, 0, kk, FFT_N2:FFT_PITCH, :] = pad
        o_ref[0, 1, kk, FFT_N2:FFT_PITCH, :] = pad


def _fft_s3_kernel(b_ref, m_ref, z_ref, x0_ref, bias_ref, y_ref):
    def step(i, carry):
        cols = []
        for dn in range(2):
            rows = _slab_rows(2 * i + dn, FFT_N1)
            cols.append(jnp.concatenate([b_ref[0, 0, rows, :], b_ref[0, 1, rows, :]], axis=0))
        y = _bdot(m_ref[...], jnp.concatenate(cols, axis=1).astype(BF16))
        for dn in range(2):
            rows = _slab_rows(2 * i + dn, FFT_N1H)
            for s in range(2):
                conv = y[s * FFT_N1H:(s + 1) * FFT_N1H, dn * LANES:(dn + 1) * LANES]
                y_ref[0, s, rows, :] = (conv + z_ref[0, s, rows, :] * bias_ref[...]) * x0_ref[0, s, rows, :]
        return carry

    lax.fori_loop(0, FFT_N2 // 2, step, 0, unroll=4)
    for pad_row in range(FFT_N2, FFT_PITCH):
        for s in range(2):
            y_ref[0, s, _slab_rows(pad_row, FFT_N1H), :] = jnp.zeros((FFT_N1H, LANES), F32)


def _hy_long_conv_call(z_lat, x0_lat, g_spec, d_bias):
    batch = z_lat.shape[0]
    pairs = batch // 2
    cst = _fft_constants()
    n_cb = HY_WIDTH // LANES
    rows_z = FFT_N1H * FFT_PITCH
    rows_a = FFT_N1 * FFT_PITCH
    zv = z_lat.reshape(pairs, 2, rows_z, HY_WIDTH)
    xv = x0_lat.reshape(pairs, 2, rows_z, HY_WIDTH)
    z_spec = pl.BlockSpec((1, 2, rows_z, LANES), lambda p, cb: (p, 0, 0, cb))
    a_spec = pl.BlockSpec((1, 2, rows_a, LANES), lambda p, cb: (p, 0, 0, cb))
    a = pl.pallas_call(
        _fft_s1_kernel,
        grid=(pairs, n_cb),
        in_specs=[z_spec, _full_spec((2 * FFT_N1, 2 * FFT_N1H))],
        out_specs=a_spec,
        out_shape=jax.ShapeDtypeStruct((pairs, 2, rows_a, HY_WIDTH), F32),
        compiler_params=_cparams(2),
        name="hy_fft_s1",
    )(zv, cst["m1"][0])
    a = a.reshape(pairs, 2, FFT_N1, FFT_PITCH, HY_WIDTH)
    blk = pl.BlockSpec((1, 2, FFT_KB, FFT_PITCH, HY_WIDTH), lambda kb, p: (p, 0, kb, 0, 0))
    sq = (2 * FFT_N2, 2 * FFT_N2)
    b = pl.pallas_call(
        _fft_s2_kernel,
        grid=(FFT_N1 // FFT_KB, pairs),
        in_specs=[blk,
                  pl.BlockSpec((2, FFT_KB, FFT_N2, LANES), lambda kb, p: (0, kb, 0, 0)),
                  pl.BlockSpec((2, FFT_KB, FFT_N2, HY_WIDTH), lambda kb, p: (0, kb, 0, 0)),
                  _full_spec(sq), _full_spec(sq)],
        out_specs=blk,
        out_shape=jax.ShapeDtypeStruct((pairs, 2, FFT_N1, FFT_PITCH, HY_WIDTH), F32),
        compiler_params=_cparams(2),
        name="hy_fft_s2",
    )(a, cst["tw"], g_spec, cst["m2"][0], cst["m2i"][0])
    b = b.reshape(pairs, 2, rows_a, HY_WIDTH)
    y = pl.pallas_call(
        _fft_s3_kernel,
        grid=(pairs, n_cb),
        in_specs=[a_spec, _full_spec((2 * FFT_N1H, 2 * FFT_N1)),
                  z_spec, z_spec, pl.BlockSpec((1, LANES), lambda p, cb: (0, cb))],
        out_specs=z_spec,
        out_shape=jax.ShapeDtypeStruct((pairs, 2, rows_z, HY_WIDTH), F32),
        compiler_params=_cparams(2),
        name="hy_fft_s3",
    )(b, cst["m3"][0], zv, xv, d_bias.reshape(1, HY_WIDTH))
    return y.reshape(batch, rows_z, HY_WIDTH)


def _hy_ctx_kernel(z_ref, x0_ref, g_ref, fwd_ref, inv_ref, bias_ref, y_ref):
    z = z_ref[0]
    x = _bdot(fwd_ref[...], z.astype(BF16))
    xr, xi = x[:FFT_NC], x[FFT_NC:]
    gr, gi = g_ref[0], g_ref[1]
    y = jnp.concatenate([xr * gr - xi * gi, xr * gi + xi * gr], axis=0)
    conv = _bdot(inv_ref[...], y.astype(BF16))
    y_ref[0] = ((conv + z * bias_ref[...]) * x0_ref[0]).astype(BF16)


def _hy_ctx_conv_call(z_ctx, x0_ctx, g_spec, d_bias):
    batch = z_ctx.shape[0]
    cst = _fft_constants()
    tok = pl.BlockSpec((1, CTX_LEN, HY_WIDTH), lambda b: (b, 0, 0))

    def full(shape):
        zeros = (0,) * len(shape)
        return pl.BlockSpec(shape, lambda b: zeros)

    return pl.pallas_call(
        _hy_ctx_kernel,
        grid=(batch,),
        in_specs=[tok, tok, full((2, FFT_NC, HY_WIDTH)),
                  full((2 * FFT_NC, CTX_LEN)), full((CTX_LEN, 2 * FFT_NC)), full((1, HY_WIDTH))],
        out_specs=tok,
        out_shape=jax.ShapeDtypeStruct((batch, CTX_LEN, HY_WIDTH), BF16),
        compiler_params=_cparams(1),
        name="hy_ctx",
    )(z_ctx, x0_ctx, g_spec, cst["mcf"][0], cst["mci"][0], d_bias.reshape(1, HY_WIDTH))


def _hyena_filter(length, fw1, fb1, fw2, fb2, fw3, fb3, fw4, freq):
    t = jnp.linspace(0.0, 1.0, length, dtype=F32)[:, None]
    w = 2.0 * math.pi * jnp.arange(length, dtype=F32)[:, None] / length
    bands = jnp.linspace(1e-4, HY_BANDS - 1, HY_BANDS, dtype=F32)
    emb = jnp.concatenate([t, jnp.cos(bands * w), -jnp.sin(bands * w)], axis=-1)
    emb = jnp.concatenate([emb, emb[::-1]], axis=0)
    hp = lax.Precision.HIGHEST
    h = jnp.sin(freq * (jnp.dot(emb, fw1, precision=hp) + fb1))
    h = jnp.sin(freq * (jnp.dot(h, fw2, precision=hp) + fb2))
    h = jnp.sin(freq * (jnp.dot(h, fw3, precision=hp) + fb3))
    max_decay = math.log(HY_TARGET) / HY_FAST_PCT
    min_decay = math.log(HY_TARGET) / HY_SLOW_PCT
    deltas = jnp.linspace(min_decay, max_decay, HY_WIDTH, dtype=F32)
    h_fwd = jnp.dot(h[:length], fw4[:, :HY_WIDTH], precision=hp).astype(F32) * jnp.exp(-t * jnp.abs(deltas))
    h_bwd_rev = (jnp.dot(h[length:], fw4[:, HY_WIDTH:], precision=hp).astype(F32)
                 * jnp.exp(-t[::-1] * jnp.abs(deltas)))
    return jnp.concatenate([h_fwd[:1] + h_bwd_rev[length - 1:], h_fwd[1:],
                            jnp.zeros((1, HY_WIDTH), F32), h_bwd_rev[:length - 1]], axis=0)


def _odd_in_kernel(x_ref, mod_ref, gain_ref, w_ref, cos_ref, sin_ref, q_ref, k_ref, v_ref, sg_ref):
    h = _modulated(x_ref[0], gain_ref[...], mod_ref, 3, 4).astype(BF16)
    cos = cos_ref[...]
    sin = sin_ref[...]

    def rope_store(dst_ref, col0, scale):
        p = _bdot(h, w_ref[:, col0:col0 + RET_QK])
        for c in range(RET_QK // LANES):
            t = p[:, c * LANES:(c + 1) * LANES]
            half = (c % 2) * LANES
            r = t * cos[:, half:half + LANES] + pltpu.roll(t, LANES // 2, 1) * sin[:, half:half + LANES]
            dst_ref[0, :, c * LANES:(c + 1) * LANES] = (r * scale).astype(BF16)

    rope_store(q_ref, 0, 1.0)
    rope_store(k_ref, RET_QK, RET_KEY_DIM ** -0.5)
    v_ref[0] = _bdot(h, w_ref[:, 2 * RET_QK:2 * RET_QK + RET_V]).astype(BF16)
    sg_ref[0] = _silu(_bdot(h, w_ref[:, 2 * RET_QK + RET_V:])).astype(BF16)


def _odd_in_call(xt, mod_i, gain, w, cos_t, sin_t):
    batch = xt.shape[0]
    rope_spec = pl.BlockSpec((TM, RET_KEY_DIM), lambda b, j: (j, 0))
    return pl.pallas_call(
        _odd_in_kernel,
        grid=(batch, NT_ALL),
        in_specs=[_tok_spec(D_MODEL), _mod_spec(batch), _full_spec((1, D_MODEL)),
                  _full_spec((D_MODEL, w.shape[1])), rope_spec, rope_spec],
        out_specs=[_tok_spec(RET_QK), _tok_spec(RET_QK), _tok_spec(RET_V), _tok_spec(RET_V)],
        out_shape=[jax.ShapeDtypeStruct((batch, T_TOK, RET_QK), BF16)] * 2
        + [jax.ShapeDtypeStruct((batch, T_TOK, RET_V), BF16)] * 2,
        compiler_params=_cparams(2),
        name="odd_in",
    )(xt, mod_i, gain, w, cos_t, sin_t)


def _rope_tables():
    t = np.arange(SEQ)
    n_freq = RET_KEY_DIM // 4
    inv = ROPE_BASE ** (-jnp.arange(n_freq, dtype=F32) / n_freq)
    ang_r = jnp.asarray(t // GRID_W, F32)[:, None] * inv
    ang_c = jnp.asarray(t % GRID_W, F32)[:, None] * inv
    cr, sr, cc, sc = jnp.cos(ang_r), jnp.sin(ang_r), jnp.cos(ang_c), jnp.sin(ang_c)
    cos_l = jnp.concatenate([cr, cr, cc, cc], axis=-1)
    sin_l = jnp.concatenate([-sr, sr, -sc, sc], axis=-1)
    cos_t = jnp.concatenate([cos_l, jnp.ones((CTX_LEN, RET_KEY_DIM), F32)], axis=0)
    sin_t = jnp.concatenate([sin_l, jnp.zeros((CTX_LEN, RET_KEY_DIM), F32)], axis=0)
    return cos_t, sin_t


def _ret_kernel(lg_ref, q_ref, k_ref, v_ref, sg_ref, y_ref, o_acc, state_f, state_b):
    head = pl.program_id(1)
    n_c = RET_BLOCK
    n_lat = SEQ // n_c
    assert CTX_LEN == n_c and n_lat % 2 == 0
    ii = lax.broadcasted_iota(jnp.int32, (n_c, n_c), 0).astype(F32)
    jj = lax.broadcasted_iota(jnp.int32, (n_c, n_c), 1).astype(F32)
    row_k = lax.broadcasted_iota(jnp.int32, (n_c, RET_KEY_DIM), 0).astype(F32)
    row_v = lax.broadcasted_iota(jnp.int32, (n_c, RET_VAL_DIM), 0).astype(F32)

    def decays(backward):
        lg = lg_ref[1 if backward else 0, head]
        if backward:
            diff = jj - ii
            xi = jnp.exp(lg * (n_c - row_v))
            zeta = jnp.exp(lg * row_k)
        else:
            diff = ii - jj
            xi = jnp.exp(lg * (row_v + 1.0))
            zeta = jnp.exp(lg * (n_c - 1.0 - row_k))
        dmask = jnp.where(diff >= 0, jnp.exp(lg * jnp.maximum(diff, 0.0)), 0.0)
        g_chunk = jnp.exp(lg * n_c + jnp.zeros((1, RET_VAL_DIM), F32))
        return dmask, xi, zeta, g_chunk

    def advance(chunk, state, consts, finalize):
        dmask, xi, zeta, g_chunk = consts
        r = chunk * n_c if isinstance(chunk, int) else pl.multiple_of(chunk * n_c, n_c)
        qc = q_ref[0, pl.ds(r, n_c), :]
        kc = k_ref[0, pl.ds(r, n_c), :]
        vc = v_ref[0, pl.ds(r, n_c), :]
        st = state[...]
        inner = lax.dot_general(qc, kc, _NT_DIMS, preferred_element_type=F32) * dmask
        o = _bdot(inner.astype(BF16), vc) + _bdot(qc, st.astype(BF16)) * xi
        kz = (kc.astype(F32) * zeta).astype(BF16)
        state[...] = st * g_chunk + lax.dot_general(kz, vc, _TN_DIMS, preferred_element_type=F32)
        if finalize:
            tot = o_acc[pl.ds(r, n_c), :] + o
            mu = jnp.mean(tot, axis=-1, keepdims=True)
            cen = tot - mu
            var = jnp.mean(cen * cen, axis=-1, keepdims=True)
            yn = cen * lax.rsqrt(var + GN_EPS)
            y_ref[0, pl.ds(r, n_c), :] = (sg_ref[0, pl.ds(r, n_c), :].astype(F32) * yn).astype(BF16)
        else:
            o_acc[pl.ds(r, n_c), :] = o

    consts_f = decays(False)
    consts_b = decays(True)
    state_f[...] = jnp.zeros_like(state_f)
    state_b[...] = jnp.zeros_like(state_b)

    def both(chunk_f, chunk_b, finalize):
        advance(chunk_f, state_f, consts_f, finalize)
        advance(chunk_b, state_b, consts_b, finalize)

    advance(n_lat, state_f, consts_f, False)
    advance(n_lat, state_b, consts_b, True)

    def first_half(s, carry):
        both(s, n_lat - 1 - s, False)
        return carry

    def second_half(s, carry):
        both(s, n_lat - 1 - s, True)
        return carry

    lax.fori_loop(0, n_lat // 2, first_half, 0, unroll=2)
    lax.fori_loop(n_lat // 2, n_lat, second_half, 0, unroll=2)


def _ret_call(lg, q, k, v, sg):
    batch = q.shape[0]
    qk_spec = pl.BlockSpec((1, T_TOK, RET_KEY_DIM), lambda b, h: (b, 0, h))
    v_spec = pl.BlockSpec((1, T_TOK, RET_VAL_DIM), lambda b, h: (b, 0, h))
    return pl.pallas_call(
        _ret_kernel,
        grid=(batch, RET_HEADS),
        in_specs=[pl.BlockSpec(memory_space=pltpu.SMEM), qk_spec, qk_spec, v_spec, v_spec],
        out_specs=v_spec,
        out_shape=jax.ShapeDtypeStruct((batch, T_TOK, RET_V), BF16),
        scratch_shapes=[pltpu.VMEM((T_TOK, RET_VAL_DIM), F32),
                        pltpu.VMEM((RET_KEY_DIM, RET_VAL_DIM), F32),
                        pltpu.VMEM((RET_KEY_DIM, RET_VAL_DIM), F32)],
        compiler_params=_cparams(2),
        name="retention",
    )(lg, q, k, v, sg)


def _fft_s2_fwd_kernel(a_ref, tw_ref, fhi_ref, flo_ref, g_ref):
    reps = HY_WIDTH // LANES
    for kk in range(FFT_KB):
        twr = jnp.concatenate([tw_ref[0, kk]] * reps, axis=1)
        twi = jnp.concatenate([tw_ref[1, kk]] * reps, axis=1)
        ar = a_ref[0, kk]
        ai = a_ref[1, kk]
        d = jnp.concatenate([ar * twr - ai * twi, ar * twi + ai * twr], axis=0)
        x = _dot3(fhi_ref[...], flo_ref[...], d)
        g_ref[0, kk] = x[:FFT_N2]
        g_ref[1, kk] = x[FFT_N2:]


def _dft_dense_kernel(x_ref, mhi_ref, mlo_ref, o_ref):
    o_ref[...] = _dot3(mhi_ref[...], mlo_ref[...], x_ref[...])


def _filter_spectra(filter_params):
    cst = _fft_constants()
    n_lane = FFT_N2 * HY_WIDTH
    filt = _hyena_filter(SEQ, *filter_params).reshape(1, 1, FFT_N1, n_lane)
    a = pl.pallas_call(
        _fft_s1_flat_kernel,
        grid=(1, n_lane // FFT_W),
        in_specs=[pl.BlockSpec((1, 1, FFT_N1, FFT_W), lambda p, w: (p, 0, 0, w)),
                  _full_spec((2 * FFT_N1, FFT_N1)), _full_spec((2 * FFT_N1, FFT_N1))],
        out_specs=pl.BlockSpec((1, 2 * FFT_N1, FFT_W), lambda p, w: (p, 0, w)),
        out_shape=jax.ShapeDtypeStruct((1, 2 * FFT_N1, n_lane), F32),
        compiler_params=_cparams(2),
        name="filt_fft_s1",
    )(filt, *cst["m1r"])
    a = a.reshape(2, FFT_N1, FFT_N2, HY_WIDTH)
    blk = pl.BlockSpec((2, FFT_KB, FFT_N2, HY_WIDTH), lambda kb: (0, kb, 0, 0))
    sq = (2 * FFT_N2, 2 * FFT_N2)
    g_lat = pl.pallas_call(
        _fft_s2_fwd_kernel,
        grid=(FFT_N1 // FFT_KB,),
        in_specs=[blk, pl.BlockSpec((2, FFT_KB, FFT_N2, LANES), lambda kb: (0, kb, 0, 0)),
                  pl.BlockSpec(sq, lambda kb: (0, 0)), pl.BlockSpec(sq, lambda kb: (0, 0))],
        out_specs=blk,
        out_shape=jax.ShapeDtypeStruct((2, FFT_N1, FFT_N2, HY_WIDTH), F32),
        compiler_params=_cparams(1),
        name="filt_fft_s2",
    )(a, cst["tw"], *cst["m2"])
    filt_ctx = _hyena_filter(CTX_LEN, *filter_params)
    g_ctx = pl.pallas_call(
        _dft_dense_kernel,
        out_shape=jax.ShapeDtypeStruct((2 * FFT_NC, HY_WIDTH), F32),
        compiler_params=pltpu.CompilerParams(vmem_limit_bytes=VMEM_LIMIT_BYTES),
        name="filt_dft_ctx",
    )(filt_ctx, *cst["mcf_full"])
    return g_lat, g_ctx.reshape(2, FFT_NC, HY_WIDTH)


def _even_mixer(xt, mod_i, gain, w_in, w_out, q_gain, k_gain, rpb, conv_w, conv_b, filter_params, d_bias):
    scale = NA_HEAD_DIM ** -0.5
    qg = (jnp.tile(q_gain, NA_HEADS) * scale).reshape(1, NA_WIDTH)
    kg = jnp.tile(k_gain, NA_HEADS).reshape(1, NA_WIDTH)
    head_of = np.arange(NA_WIDTH) // NA_HEAD_DIM
    bd = jnp.asarray((head_of[:, None] == head_of[None, :]) / NA_HEAD_DIM, BF16)
    q, k, v, z_lat, x0_lat, z_ctx, x0_ctx = _even_in_call(
        xt, mod_i, gain, w_in.astype(BF16), qg, kg, bd, conv_w, conv_b.reshape(1, -1))
    a = _na_call(q, k, v, _na_bias_table(rpb))
    g_lat, g_ctx = _filter_spectra(filter_params)
    y_lat = _hy_long_conv_call(z_lat, x0_lat, g_lat, d_bias)
    y_ctx = _hy_ctx_conv_call(z_ctx, x0_ctx, g_ctx, d_bias)
    return a, y_lat.reshape(xt.shape[0], FFT_N1H, FFT_PITCH, HY_WIDTH), y_ctx, w_out.astype(BF16)


def _odd_mixer(xt, mod_i, gain, w_in, w_out, logit_f, logit_b, rope):
    q, k, v, sg = _odd_in_call(xt, mod_i, gain, w_in.astype(BF16), *rope)
    lg = jnp.stack([jax.nn.log_sigmoid(logit_f.astype(F32)), jax.nn.log_sigmoid(logit_b.astype(F32))])
    return _ret_call(lg, q, k, v, sg), w_out.astype(BF16)


def kernel(x, c, ctx, c_ctx, w_mod, b_mod, norm_gain, ffn_a_in, ffn_a_out, ffn_b_in, ffn_b_out,
           even_in, even_out, na_q_gain, na_k_gain, na_rpb, hy_conv_w, hy_conv_b,
           hy_fw1, hy_fb1, hy_fw2, hy_fb2, hy_fw3, hy_fb3, hy_fw4, hy_freq, hy_bias,
           ret_in, ret_out, ret_logit_f, ret_logit_b):
    batch = x.shape[0]
    assert x.shape == (batch, SEQ, D_MODEL) and ctx.shape == (batch, CTX_LEN, D_MODEL)
    assert batch % 2 == 0 and batch < MOD_ROWS
    c_rows = jnp.concatenate([c, c_ctx[None], jnp.zeros((MOD_ROWS - batch - 1, D_MODEL), F32)], axis=0)
    mod_all = _mod_call(c_rows, w_mod, b_mod).reshape(DEPTH, MOD_ROWS, N_MOD, D_MODEL)
    rope = _rope_tables()
    for i in range(DEPTH):
        last = i == DEPTH - 1
        mod_i = mod_all[i]
        gains = norm_gain[i].reshape(3, 1, D_MODEL)
        w_a = (ffn_a_in[i].astype(BF16), ffn_a_out[i].astype(BF16))
        w_b = (ffn_b_in[i].astype(BF16), ffn_b_out[i].astype(BF16))
        if i == 0:
            xt = _ffn_call("split", (x, ctx), mod_i, gains[0], *w_a, (0, 1, 2), NT_ALL, batch)
        else:
            xt = _ffn_call("stream", (xt,), mod_i, gains[0], *w_a, (0, 1, 2), NT_ALL, batch)
        if i % 2 == 0:
            e = i // 2
            filter_params = (hy_fw1[e], hy_fb1[e], hy_fw2[e], hy_fb2[e], hy_fw3[e], hy_fb3[e],
                             hy_fw4[e], hy_freq[e])
            source = "even"
            mixed = _even_mixer(xt, mod_i, gains[1], even_in[e], even_out[e], na_q_gain[e], na_k_gain[e],
                                na_rpb[e], hy_conv_w[e], hy_conv_b[e], filter_params, hy_bias[e])
        else:
            o = i // 2
            source = "odd"
            mixed = _odd_mixer(xt, mod_i, gains[1], ret_in[o], ret_out[o], ret_logit_f[o], ret_logit_b[o], rope)
        xt = _ffn_call(source, (xt,) + mixed, mod_i, gains[2], *w_b, (6, 7, 8),
                       NT_LAT if last else NT_ALL, batch)
    return xt
```

```python
import functools
import math

import ml_dtypes
import numpy as np
import jax
import jax.numpy as jnp
from jax import lax
from jax.experimental import pallas as pl
from jax.experimental.pallas import tpu as pltpu

F32 = jnp.float32
BF16 = jnp.bfloat16

D_MODEL = 1024
SEQ = 4096
DEPTH = 4
GRID_W = 64
CTX_LEN = 256
N_MOD = 9
RMS_EPS = 1e-6
GN_EPS = 1e-6
D_FF = 2816
NA_HEADS = 8
NA_HEAD_DIM = 64
NA_WIDTH = NA_HEADS * NA_HEAD_DIM
WIN_R = 8
WIN_C = 16
HY_WIDTH = D_MODEL - NA_WIDTH
HY_BANDS = 8
HY_TARGET = 1e-2
HY_FAST_PCT = 0.3
HY_SLOW_PCT = 1.5
RET_HEADS = 4
RET_KEY_DIM = D_MODEL // RET_HEADS
RET_VAL_DIM = 2 * RET_KEY_DIM
RET_QK = RET_HEADS * RET_KEY_DIM
RET_V = RET_HEADS * RET_VAL_DIM
ROPE_BASE = 10000.0

LANES = 128
BF16_ROWS = 16
VMEM_LIMIT_BYTES = 56 * 2**20
T_TOK = SEQ + CTX_LEN
TM = CTX_LEN
NT_LAT = SEQ // TM
NT_ALL = T_TOK // TM
MOD_ROWS = 16
FF_CHUNKS = ((0, 1536), (1536, 1280))
RET_BLOCK = 256

NA_RB = 4
NA_KR = NA_RB + WIN_R - 1
GRID_H = SEQ // GRID_W
NA_NQ = NA_RB * GRID_W
NA_NK = NA_KR * GRID_W
NEG_BIG = -1e30

FFT_N = 2 * SEQ
FFT_N1 = 64
FFT_N2 = 128
FFT_N1H = FFT_N1 // 2
FFT_KB = 4
FFT_PITCH = FFT_N2 + 8
FFT_NC = 2 * CTX_LEN


def _cparams(n_axes):
    return pltpu.CompilerParams(dimension_semantics=("arbitrary",) * n_axes,
                                vmem_limit_bytes=VMEM_LIMIT_BYTES)


def _bdot(a, b):
    return jnp.dot(a, b, preferred_element_type=F32)


_NT_DIMS = (((1,), (1,)), ((), ()))
_TN_DIMS = (((0,), (0,)), ((), ()))


def _split_hi_lo(m):
    hi = m.astype(BF16)
    lo = (m - hi.astype(F32)).astype(BF16)
    return hi, lo


def _dot3(m_hi, m_lo, d):
    d_hi, d_lo = _split_hi_lo(d)
    return _bdot(m_hi, d_hi) + _bdot(m_lo, d_hi) + _bdot(m_hi, d_lo)


def _modulated(x, gain, mod_ref, shift_row, scale_row):
    ms = jnp.mean(x * x, axis=-1, keepdims=True)
    y = x * lax.rsqrt(ms + RMS_EPS) * gain
    return (y * (1.0 + mod_ref[0, scale_row:scale_row + 1, :])
            + mod_ref[0, shift_row:shift_row + 1, :])


def _silu(a):
    return a * jax.nn.sigmoid(a)


def _mod_kernel(c_ref, w_ref, b_ref, o_ref):
    s = _silu(c_ref[...])
    o_ref[0] = jnp.dot(s, w_ref[0], precision=lax.Precision.HIGHEST,
                       preferred_element_type=F32) + b_ref[0]


def _mod_call(c_rows, w_mod, b_mod):
    depth, d, n = w_mod.shape
    tn = 1024
    return pl.pallas_call(
        _mod_kernel,
        grid=(depth, n // tn),
        in_specs=[pl.BlockSpec((MOD_ROWS, d), lambda i, j: (0, 0)),
                  pl.BlockSpec((1, d, tn), lambda i, j: (i, 0, j)),
                  pl.BlockSpec((1, 1, tn), lambda i, j: (i, 0, j))],
        out_specs=pl.BlockSpec((1, MOD_ROWS, tn), lambda i, j: (i, 0, j)),
        out_shape=jax.ShapeDtypeStruct((depth, MOD_ROWS, n), F32),
        compiler_params=_cparams(2),
        name="mod",
    )(c_rows, w_mod, b_mod.reshape(depth, 1, n))


def _tok_spec(width):
    return pl.BlockSpec((1, TM, width), lambda b, j: (b, j, 0))


def _mod_spec(batch):
    return pl.BlockSpec((1, N_MOD, D_MODEL), lambda b, j: (jnp.where(j == NT_LAT, batch, b), 0, 0))


def _full_spec(shape):
    zeros = (0,) * len(shape)
    return pl.BlockSpec(shape, lambda b, j: zeros)


_FFN_SOURCES = {"stream": 1, "split": 4, "even": 7, "odd": 3}
TM2 = 2 * TM


def _per_half(fn):
    return jnp.concatenate([fn(half, slice(half * TM, (half + 1) * TM)) for half in range(2)], axis=0)


def _ffn_kernel(*refs, rows, source, n_tiles, n_pairs):
    shift_row, scale_row, gate_row = rows
    n_src = _FFN_SOURCES[source]
    src = refs[:n_src]
    modn0, modn1, modc0, modc1, gain_ref, win_ref, wout_ref, o_ref, h_a, x_a, h_b, x_b = refs[n_src:]
    modn = (modn0, modn1)
    modc = (modc0, modc1)
    step = pl.program_id(0)
    first_tile_next = 2 * jnp.minimum(step, n_pairs - 1)

    def is_ctx(half):
        return (first_tile_next + half) % n_tiles == NT_LAT

    def gated(x_ref, o):
        return _per_half(lambda half, r: x_ref[0, r, :] + modn[half][0, 5:6, :] * o[r])

    def next_rows():
        if source == "stream":
            return src[0][0]
        if source == "split":
            lat, ctx = src[0:2], src[2:4]
            return _per_half(lambda half, r: jnp.where(is_ctx(half), ctx[half][0], lat[half][0]))
        if source == "even":
            x_ref, a_ref, yl0, yl1, yc0, yc1, wo_ref = src
            y_lat, y_ctx = (yl0, yl1), (yc0, yc1)

            def hyena_rows(half, r):
                slabs = [y_lat[half][0, s, 0:FFT_N2, :] for s in range(TM // FFT_N2)]
                return jnp.where(is_ctx(half), y_ctx[half][0], jnp.concatenate(slabs, axis=0)).astype(BF16)

            o = _bdot(a_ref[0], wo_ref[0:NA_WIDTH, :]) + _bdot(_per_half(hyena_rows), wo_ref[NA_WIDTH:, :])
            return gated(x_ref, o)
        x_ref, y_ref, wo_ref = src
        return gated(x_ref, _bdot(y_ref[0], wo_ref[...]))

    def body(h_read, x_read, h_write, x_write):
        x_next = next_rows()
        h_write[...] = _per_half(lambda half, r: _modulated(
            x_next[r], gain_ref[...], modn[half], shift_row, scale_row)).astype(BF16)
        x_write[...] = x_next
        h = h_read[...]
        acc = jnp.zeros((TM2, D_MODEL), F32)
        for start, size in FF_CHUNKS:
            a = _bdot(h, win_ref[:, start:start + size])
            b = _bdot(h, win_ref[:, D_FF + start:D_FF + start + size])
            g = (_silu(a) * b).astype(BF16)
            acc = acc + _bdot(g, wout_ref[start:start + size, :])
        o_ref[0] = _per_half(lambda half, r: x_read[r, :] + (0.5 * modc[half][0, gate_row:gate_row + 1, :]) * acc[r])

    @pl.when(step == 0)
    def _():
        h_b[...] = jnp.zeros_like(h_b)
        x_b[...] = jnp.zeros_like(x_b)

    @pl.when(step % 2 == 0)
    def _():
        body(h_b, x_b, h_a, x_a)

    @pl.when(step % 2 == 1)
    def _():
        body(h_a, x_a, h_b, x_b)


def _ffn_call(source, srcs, mod_i, gain, w_in, w_out, rows, n_tiles, batch):
    n_all = batch * n_tiles
    assert n_all % 2 == 0 and n_tiles in (NT_LAT, NT_ALL)
    n_pairs = n_all // 2

    def next_pair(s):
        return jnp.minimum(s, n_pairs - 1)

    def cur_pair(s):
        return jnp.maximum(s - 1, 0)

    def tile_of(pair_of, half):
        def fn(s):
            t = 2 * pair_of(s) + half
            return t // n_tiles, t % n_tiles
        return fn

    def mod_spec(pair_of, half):
        def index(s):
            b, j = tile_of(pair_of, half)(s)
            return jnp.where(j == NT_LAT, batch, b), 0, 0
        return pl.BlockSpec((1, N_MOD, D_MODEL), index)

    def tok(arr):
        width = arr.shape[-1]
        if n_tiles == NT_ALL:
            return arr.reshape(1, batch * T_TOK, width), pl.BlockSpec(
                (1, TM2, width), lambda s: (0, next_pair(s), 0))
        per_sample = n_tiles // 2
        return arr, pl.BlockSpec((1, TM2, width),
                                 lambda s: (next_pair(s) // per_sample, next_pair(s) % per_sample, 0))

    def lat_tile(block, n_trailing, half):
        def index(s):
            b, j = tile_of(next_pair, half)(s)
            return (b, jnp.minimum(j, NT_LAT - 1)) + (0,) * n_trailing
        return pl.BlockSpec(block, index)

    def ctx_tile(block, half):
        return pl.BlockSpec(block, lambda s: (tile_of(next_pair, half)(s)[0],) + (0,) * (len(block) - 1))

    def full(shape):
        zeros = (0,) * len(shape)
        return pl.BlockSpec(shape, lambda s: zeros)

    halves = (0, 1)
    if source == "stream":
        (xt,) = srcs
        xt, x_spec = tok(xt)
        args, src_specs = (xt,), [x_spec]
    elif source == "split":
        x, ctx = srcs
        args = (x, x, ctx, ctx)
        src_specs = ([lat_tile((1, TM, D_MODEL), 1, h) for h in halves]
                     + [ctx_tile((1, CTX_LEN, D_MODEL), h) for h in halves])
    elif source == "even":
        xt, a, y_lat, y_ctx, wo = srcs
        (xt, x_spec), (a, a_spec) = tok(xt), tok(a)
        args = (xt, a, y_lat, y_lat, y_ctx, y_ctx, wo)
        src_specs = ([x_spec, a_spec]
                     + [lat_tile((1, TM // FFT_N2, FFT_PITCH, HY_WIDTH), 2, h) for h in halves]
                     + [ctx_tile((1, CTX_LEN, HY_WIDTH), h) for h in halves] + [full((D_MODEL, D_MODEL))])
    else:
        xt, y, wo = srcs
        (xt, x_spec), (y, y_spec) = tok(xt), tok(y)
        args = (xt, y, wo)
        src_specs = [x_spec, y_spec, full((RET_V, D_MODEL))]
    out = pl.pallas_call(
        functools.partial(_ffn_kernel, rows=rows, source=source, n_tiles=n_tiles, n_pairs=n_pairs),
        grid=(n_pairs + 1,),
        in_specs=src_specs + [mod_spec(next_pair, 0), mod_spec(next_pair, 1),
                              mod_spec(cur_pair, 0), mod_spec(cur_pair, 1), full((1, D_MODEL)),
                              full((D_MODEL, 2 * D_FF)), full((D_FF, D_MODEL))],
        out_specs=pl.BlockSpec((1, TM2, D_MODEL), lambda s: (0, cur_pair(s), 0)),
        out_shape=jax.ShapeDtypeStruct((1, n_all * TM, D_MODEL), F32),
        scratch_shapes=[pltpu.VMEM((TM2, D_MODEL), BF16), pltpu.VMEM((TM2, D_MODEL), F32),
                        pltpu.VMEM((TM2, D_MODEL), BF16), pltpu.VMEM((TM2, D_MODEL), F32)],
        compiler_params=_cparams(1),
        name="ffn_" + source,
    )(*args, mod_i, mod_i, mod_i, mod_i, gain, w_in, w_out)
    return out.reshape(batch, n_tiles * TM, D_MODEL)


def _even_in_kernel(x_ref, mod_ref, gain_ref, w_ref, qg_ref, kg_ref, bd_ref,
                    q_ref, k_ref, v_ref, u_ref):
    h = _modulated(x_ref[0], gain_ref[...], mod_ref, 3, 4).astype(BF16)

    def head_norm(t, g):
        ms = _bdot((t * t).astype(BF16), bd_ref[...])
        return (t * lax.rsqrt(ms + RMS_EPS) * g).astype(BF16)

    q_ref[0] = head_norm(_bdot(h, w_ref[:, 0:NA_WIDTH]), qg_ref[...])
    k_ref[0] = head_norm(_bdot(h, w_ref[:, NA_WIDTH:2 * NA_WIDTH]), kg_ref[...])
    v_ref[0] = _bdot(h, w_ref[:, 2 * NA_WIDTH:3 * NA_WIDTH]).astype(BF16)
    u_ref[0] = _bdot(h, w_ref[:, 3 * NA_WIDTH:]).astype(BF16)


def _even_in_call(xt, mod_i, gain, w, q_gain, k_gain, bd):
    batch = xt.shape[0]
    n_in = w.shape[1]
    return pl.pallas_call(
        _even_in_kernel,
        grid=(batch, NT_ALL),
        in_specs=[_tok_spec(D_MODEL), _mod_spec(batch), _full_spec((1, D_MODEL)),
                  _full_spec((D_MODEL, n_in)), _full_spec((1, NA_WIDTH)), _full_spec((1, NA_WIDTH)),
                  _full_spec((NA_WIDTH, NA_WIDTH))],
        out_specs=[_tok_spec(NA_WIDTH), _tok_spec(NA_WIDTH), _tok_spec(NA_WIDTH),
                   _tok_spec(3 * HY_WIDTH)],
        out_shape=[jax.ShapeDtypeStruct((batch, T_TOK, NA_WIDTH), BF16)] * 3
        + [jax.ShapeDtypeStruct((batch, T_TOK, 3 * HY_WIDTH), BF16)],
        compiler_params=_cparams(2),
        name="even_in",
    )(xt, mod_i, gain, w, q_gain, k_gain, bd)


def _na_kernel(q_ref, k_ref, v_ref, bias_ref, o_ref):
    lane = lax.broadcasted_iota(jnp.int32, (1, LANES), 1)
    first_head = lane < NA_HEAD_DIM
    k_ctx = k_ref[0, SEQ:T_TOK, :]
    v_ctx = v_ref[0, SEQ:T_TOK, :]

    def stack_heads(q):
        zero = jnp.zeros_like(q)
        return jnp.concatenate([jnp.where(first_head, q, zero), jnp.where(first_head, zero, q)], axis=0)

    def attend(qs, scores_and_values):
        s_list = [lax.dot_general(qs, kk, _NT_DIMS, preferred_element_type=F32) if bias is None
                  else lax.dot_general(qs, kk, _NT_DIMS, preferred_element_type=F32) + bias
                  for kk, _, bias in scores_and_values]
        m = functools.reduce(jnp.maximum, [jnp.max(s, axis=-1, keepdims=True) for s in s_list])
        o = functools.reduce(jnp.add, [
            _bdot(jnp.exp(s - m).astype(BF16),
                  jnp.concatenate([vv, jnp.ones((vv.shape[0], LANES), BF16)], axis=1))
            for s, (_, vv, _) in zip(s_list, scores_and_values)])
        o = o[:, :LANES] / o[:, LANES:]
        n = qs.shape[0] // 2
        return jnp.where(first_head, o[:n], o[n:]).astype(BF16)

    def block(i, carry):
        r0 = i * NA_RB
        u0 = jnp.clip(r0 - WIN_R // 2, 0, GRID_H - NA_KR)
        pattern = jnp.where(i == 0, 0, jnp.where(i == GRID_H // NA_RB - 1, 2, 1))
        q0 = pl.multiple_of(r0 * GRID_W, NA_NQ)
        k0 = pl.multiple_of(u0 * GRID_W, GRID_W)
        qs = stack_heads(q_ref[0, pl.ds(q0, NA_NQ), :])
        k_win = k_ref[0, pl.ds(k0, NA_NK), :]
        v_win = v_ref[0, pl.ds(k0, NA_NK), :]
        o_ref[0, pl.ds(q0, NA_NQ), :] = attend(
            qs, [(k_win, v_win, bias_ref[pattern, 0]), (k_ctx, v_ctx, None)])
        return carry

    lax.fori_loop(0, GRID_H // NA_RB, block, 0, unroll=4)
    o_ref[0, SEQ:T_TOK, :] = attend(stack_heads(q_ref[0, SEQ:T_TOK, :]), [(k_ctx, v_ctx, None)])


def _na_call(q, k, v, bias):
    batch = q.shape[0]
    n_pairs = NA_WIDTH // LANES
    spec = pl.BlockSpec((1, T_TOK, LANES), lambda b, p: (b, 0, p))
    return pl.pallas_call(
        _na_kernel,
        grid=(batch, n_pairs),
        in_specs=[spec, spec, spec,
                  pl.BlockSpec((3, 1, 2 * NA_NQ, NA_NK), lambda b, p: (0, p, 0, 0))],
        out_specs=spec,
        out_shape=jax.ShapeDtypeStruct((batch, T_TOK, NA_WIDTH), BF16),
        compiler_params=_cparams(2),
        name="na_attn",
    )(q, k, v, bias)


def _na_bias_table(rpb):
    j = np.arange(NA_RB)[:, None, None, None]
    c = np.arange(GRID_W)[None, :, None, None]
    kk = np.arange(NA_KR)[None, None, :, None]
    kc = np.arange(GRID_W)[None, None, None, :]
    cs = np.clip(c - WIN_C // 2, 0, GRID_W - WIN_C)
    col_ok = (kc >= cs) & (kc < cs + WIN_C)
    dc = np.clip(kc - c + WIN_C - 1, 0, 2 * WIN_C - 2)[0, :, 0, :]
    col_sel = (dc[..., None] == np.arange(2 * WIN_C - 1)).astype(np.float32)
    shape = (NA_RB, GRID_W, NA_KR, GRID_W)
    oks, row_sels = [], []
    for off, rs_rel in ((0, 0 * j), (-(WIN_R // 2), j), (-(NA_KR - NA_RB), NA_KR - WIN_R + 0 * j)):
        row_ok = (kk >= rs_rel) & (kk < rs_rel + WIN_R)
        dr = np.clip(off + kk - j + WIN_R - 1, 0, 2 * WIN_R - 2)[:, 0, :, 0]
        row_sels.append((dr[..., None] == np.arange(2 * WIN_R - 1)).astype(np.float32))
        oks.append(np.broadcast_to(row_ok & col_ok, shape))
    vals = jnp.einsum("pjkr,hrc,qmc->phjqkm", np.stack(row_sels), rpb.astype(F32), col_sel,
                      precision=lax.Precision.HIGHEST)
    table = jnp.where(np.stack(oks)[:, None], vals, NEG_BIG)
    return table.reshape(3, NA_HEADS // 2, 2 * NA_NQ, NA_NK)


def _hy_pre_kernel(u0_ref, u1_ref, uv_ref, w0_ref, w1_ref, wv_ref, b0_ref, b1_ref, bv_ref,
                   zl_ref, xl_ref, zc_ref, xc_ref):
    row = lax.broadcasted_iota(jnp.int32, (TM, LANES), 0)
    zero_row = jnp.zeros((1, LANES), F32)

    def conv(u_ref, w_ref, b_ref, tile):
        s = tile * TM
        cur = u_ref[0, s:s + TM, :].astype(F32)
        seq_start = tile in (0, NT_LAT)
        seq_end = tile in (NT_LAT - 1, NT_LAT)
        prev_row = zero_row if seq_start else u_ref[0, s - BF16_ROWS:s, :].astype(F32)[BF16_ROWS - 1:]
        next_row = zero_row if seq_end else u_ref[0, s + TM:s + TM + BF16_ROWS, :].astype(F32)[:1]
        before = jnp.where(row == 0, prev_row, pltpu.roll(cur, 1, 0))
        after = jnp.where(row == TM - 1, next_row, pltpu.roll(cur, TM - 1, 0))
        return before * w_ref[0:1, :] + cur * w_ref[1:2, :] + after * w_ref[2:3, :] + b_ref[...]

    for tile in range(NT_ALL):
        x0 = conv(u0_ref, w0_ref, b0_ref, tile)
        z = conv(uv_ref, wv_ref, bv_ref, tile) * conv(u1_ref, w1_ref, b1_ref, tile)
        if tile < NT_LAT:
            for half in range(TM // FFT_N2):
                base = (tile * (TM // FFT_N2) + half) * FFT_PITCH
                zl_ref[0, base:base + FFT_N2, :] = z[half * FFT_N2:(half + 1) * FFT_N2]
                xl_ref[0, base:base + FFT_N2, :] = x0[half * FFT_N2:(half + 1) * FFT_N2]
                pad = jnp.zeros((FFT_PITCH - FFT_N2, LANES), F32)
                zl_ref[0, base + FFT_N2:base + FFT_PITCH, :] = pad
                xl_ref[0, base + FFT_N2:base + FFT_PITCH, :] = pad
        else:
            zc_ref[0] = z
            xc_ref[0] = x0


def _hy_pre_call(u, conv_w, conv_b):
    batch = u.shape[0]
    nb = HY_WIDTH // LANES
    u_specs = [pl.BlockSpec((1, T_TOK, LANES), lambda b, cb, g=g: (b, 0, g * nb + cb)) for g in range(3)]
    w_specs = [pl.BlockSpec((3, LANES), lambda b, cb, g=g: (0, g * nb + cb)) for g in range(3)]
    b_specs = [pl.BlockSpec((1, LANES), lambda b, cb, g=g: (0, g * nb + cb)) for g in range(3)]
    lat = pl.BlockSpec((1, FFT_N1H * FFT_PITCH, LANES), lambda b, cb: (b, 0, cb))
    ctx = pl.BlockSpec((1, CTX_LEN, LANES), lambda b, cb: (b, 0, cb))
    return pl.pallas_call(
        _hy_pre_kernel,
        grid=(batch, nb),
        in_specs=u_specs + w_specs + b_specs,
        out_specs=[lat, lat, ctx, ctx],
        out_shape=[jax.ShapeDtypeStruct((batch, FFT_N1H * FFT_PITCH, HY_WIDTH), F32)] * 2
        + [jax.ShapeDtypeStruct((batch, CTX_LEN, HY_WIDTH), F32)] * 2,
        compiler_params=_cparams(2),
        name="hy_pre",
    )(u, u, u, conv_w, conv_w, conv_w, conv_b, conv_b, conv_b)


def _complex_block(re, im):
    return np.block([[re, -im], [im, re]])


@functools.lru_cache(maxsize=None)
def _fft_constants():
    k1 = np.arange(FFT_N1)
    n1 = np.arange(FFT_N1H)
    f1 = np.exp(-2j * np.pi * np.outer(k1, n1) / FFT_N1)
    m1 = _complex_block(f1.real, f1.imag)
    f1_full = np.exp(-2j * np.pi * np.outer(k1, k1) / FFT_N1)
    m1r = np.concatenate([f1_full.real, f1_full.imag], axis=0)
    n2 = np.arange(FFT_N2)
    tw = np.exp(-2j * np.pi * np.outer(k1, n2) / FFT_N)
    tw = np.stack([tw.real, tw.imag])[..., None] * np.ones((1, 1, 1, LANES))
    f2 = np.exp(-2j * np.pi * np.outer(n2, n2) / FFT_N2)
    m2 = _complex_block(f2.real, f2.imag)
    m2i = _complex_block(f2.real, -f2.imag)
    c1 = np.exp(2j * np.pi * np.outer(n1, k1) / FFT_N1) / FFT_N
    m3 = _complex_block(c1.real, c1.imag)
    kc = np.arange(FFT_NC)
    nc = np.arange(CTX_LEN)
    ang = 2 * np.pi * np.outer(kc, nc) / FFT_NC
    mcf = np.concatenate([np.cos(ang), -np.sin(ang)], axis=0)
    ang_full = 2 * np.pi * np.outer(kc, kc) / FFT_NC
    mcf_full = np.concatenate([np.cos(ang_full), -np.sin(ang_full)], axis=0)
    mci = np.concatenate([np.cos(ang.T), -np.sin(ang.T)], axis=1) / FFT_NC

    def hi_lo(m):
        hi = m.astype(ml_dtypes.bfloat16)
        lo = (m - hi.astype(np.float64)).astype(ml_dtypes.bfloat16)
        return hi, lo

    return dict(m1=hi_lo(m1), m1r=hi_lo(m1r), m2=hi_lo(m2), m2i=hi_lo(m2i), m3=hi_lo(m3),
                mcf=hi_lo(mcf), mcf_full=hi_lo(mcf_full), mci=hi_lo(mci), tw=tw.astype(np.float32))


def _slab_rows(n2, n_slabs):
    return pl.ds(n2, n_slabs, stride=FFT_PITCH)


def _fft_s1_kernel(z_ref, *rest):
    *m_refs, a_ref = rest

    def step(i, carry):
        cols = []
        for dn in range(2):
            rows = _slab_rows(2 * i + dn, FFT_N1H)
            cols.append(jnp.concatenate([z_ref[0, 0, rows, :], z_ref[0, 1, rows, :]], axis=0))
        d = jnp.concatenate(cols, axis=1)
        if len(m_refs) == 2:
            r = _dot3(m_refs[0][...], m_refs[1][...], d)
        else:
            r = _bdot(m_refs[0][...], d.astype(BF16))
        for dn in range(2):
            rows = _slab_rows(2 * i + dn, FFT_N1)
            a_ref[0, 0, rows, :] = r[:FFT_N1, dn * LANES:(dn + 1) * LANES]
            a_ref[0, 1, rows, :] = r[FFT_N1:, dn * LANES:(dn + 1) * LANES]
        return carry

    lax.fori_loop(0, FFT_N2 // 2, step, 0, unroll=4)
    for pad_row in range(FFT_N2, FFT_PITCH):
        for part in range(2):
            a_ref[0, part, _slab_rows(pad_row, FFT_N1), :] = jnp.zeros((FFT_N1, LANES), F32)


def _fft_s2_kernel(a_ref, tw_ref, g_ref, fwd_ref, inv_ref, o_ref):
    reps = HY_WIDTH // LANES
    pad = jnp.zeros((FFT_PITCH - FFT_N2, HY_WIDTH), F32)
    for kk in range(FFT_KB):
        twr = jnp.concatenate([tw_ref[0, kk]] * reps, axis=1)
        twi = jnp.concatenate([tw_ref[1, kk]] * reps, axis=1)
        ar = a_ref[0, 0, kk, 0:FFT_N2, :]
        ai = a_ref[0, 1, kk, 0:FFT_N2, :]
        d = jnp.concatenate([ar * twr - ai * twi, ar * twi + ai * twr], axis=0)
        x = _bdot(fwd_ref[...], d.astype(BF16))
        xr, xi = x[:FFT_N2], x[FFT_N2:]
        gr, gi = g_ref[0, kk], g_ref[1, kk]
        y = jnp.concatenate([xr * gr - xi * gi, xr * gi + xi * gr], axis=0)
        b = _bdot(inv_ref[...], y.astype(BF16))
        br, bi = b[:FFT_N2], b[FFT_N2:]
        o_ref[0, 0, kk, 0:FFT_N2, :] = br * twr + bi * twi
        o_ref[0, 1, kk, 0:FFT_N2, :] = bi * twr - br * twi
        o_ref[0, 0, kk, FFT_N2:FFT_PITCH, :] = pad
        o_ref[0, 1, kk, FFT_N2:FFT_PITCH, :] = pad


def _fft_s3_kernel(b_ref, m_ref, z_ref, x0_ref, bias_ref, y_ref):
    def step(i, carry):
        cols = []
        for dn in range(2):
            rows = _slab_rows(2 * i + dn, FFT_N1)
            cols.append(jnp.concatenate([b_ref[0, 0, rows, :], b_ref[0, 1, rows, :]], axis=0))
        y = _bdot(m_ref[...], jnp.concatenate(cols, axis=1).astype(BF16))
        for dn in range(2):
            rows = _slab_rows(2 * i + dn, FFT_N1H)
            for s in range(2):
                conv = y[s * FFT_N1H:(s + 1) * FFT_N1H, dn * LANES:(dn + 1) * LANES]
                y_ref[0, s, rows, :] = (conv + z_ref[0, s, rows, :] * bias_ref[...]) * x0_ref[0, s, rows, :]
        return carry

    lax.fori_loop(0, FFT_N2 // 2, step, 0, unroll=4)
    for pad_row in range(FFT_N2, FFT_PITCH):
        for s in range(2):
            y_ref[0, s, _slab_rows(pad_row, FFT_N1H), :] = jnp.zeros((FFT_N1H, LANES), F32)


def _hy_long_conv_call(z_lat, x0_lat, g_spec, d_bias):
    batch = z_lat.shape[0]
    pairs = batch // 2
    cst = _fft_constants()
    n_cb = HY_WIDTH // LANES
    rows_z = FFT_N1H * FFT_PITCH
    rows_a = FFT_N1 * FFT_PITCH
    zv = z_lat.reshape(pairs, 2, rows_z, HY_WIDTH)
    xv = x0_lat.reshape(pairs, 2, rows_z, HY_WIDTH)
    z_spec = pl.BlockSpec((1, 2, rows_z, LANES), lambda p, cb: (p, 0, 0, cb))
    a_spec = pl.BlockSpec((1, 2, rows_a, LANES), lambda p, cb: (p, 0, 0, cb))
    a = pl.pallas_call(
        _fft_s1_kernel,
        grid=(pairs, n_cb),
        in_specs=[z_spec, _full_spec((2 * FFT_N1, 2 * FFT_N1H))],
        out_specs=a_spec,
        out_shape=jax.ShapeDtypeStruct((pairs, 2, rows_a, HY_WIDTH), F32),
        compiler_params=_cparams(2),
        name="hy_fft_s1",
    )(zv, cst["m1"][0])
    a = a.reshape(pairs, 2, FFT_N1, FFT_PITCH, HY_WIDTH)
    blk = pl.BlockSpec((1, 2, FFT_KB, FFT_PITCH, HY_WIDTH), lambda kb, p: (p, 0, kb, 0, 0))
    sq = (2 * FFT_N2, 2 * FFT_N2)
    b = pl.pallas_call(
        _fft_s2_kernel,
        grid=(FFT_N1 // FFT_KB, pairs),
        in_specs=[blk,
                  pl.BlockSpec((2, FFT_KB, FFT_N2, LANES), lambda kb, p: (0, kb, 0, 0)),
                  pl.BlockSpec((2, FFT_KB, FFT_N2, HY_WIDTH), lambda kb, p: (0, kb, 0, 0)),
                  _full_spec(sq), _full_spec(sq)],
        out_specs=blk,
        out_shape=jax.ShapeDtypeStruct((pairs, 2, FFT_N1, FFT_PITCH, HY_WIDTH), F32),
        compiler_params=_cparams(2),
        name="hy_fft_s2",
    )(a, cst["tw"], g_spec, cst["m2"][0], cst["m2i"][0])
    b = b.reshape(pairs, 2, rows_a, HY_WIDTH)
    y = pl.pallas_call(
        _fft_s3_kernel,
        grid=(pairs, n_cb),
        in_specs=[a_spec, _full_spec((2 * FFT_N1H, 2 * FFT_N1)),
                  z_spec, z_spec, pl.BlockSpec((1, LANES), lambda p, cb: (0, cb))],
        out_specs=z_spec,
        out_shape=jax.ShapeDtypeStruct((pairs, 2, rows_z, HY_WIDTH), F32),
        compiler_params=_cparams(2),
        name="hy_fft_s3",
    )(b, cst["m3"][0], zv, xv, d_bias.reshape(1, HY_WIDTH))
    return y.reshape(batch, rows_z, HY_WIDTH)


def _hy_ctx_kernel(z_ref, x0_ref, g_ref, fwd_ref, inv_ref, bias_ref, y_ref):
    z = z_ref[0]
    x = _bdot(fwd_ref[...], z.astype(BF16))
    xr, xi = x[:FFT_NC], x[FFT_NC:]
    gr, gi = g_ref[0], g_ref[1]
    y = jnp.concatenate([xr * gr - xi * gi, xr * gi + xi * gr], axis=0)
    conv = _bdot(inv_ref[...], y.astype(BF16))
    y_ref[0] = ((conv + z * bias_ref[...]) * x0_ref[0]).astype(BF16)


def _hy_ctx_conv_call(z_ctx, x0_ctx, g_spec, d_bias):
    batch = z_ctx.shape[0]
    cst = _fft_constants()
    tok = pl.BlockSpec((1, CTX_LEN, HY_WIDTH), lambda b: (b, 0, 0))

    def full(shape):
        zeros = (0,) * len(shape)
        return pl.BlockSpec(shape, lambda b: zeros)

    return pl.pallas_call(
        _hy_ctx_kernel,
        grid=(batch,),
        in_specs=[tok, tok, full((2, FFT_NC, HY_WIDTH)),
                  full((2 * FFT_NC, CTX_LEN)), full((CTX_LEN, 2 * FFT_NC)), full((1, HY_WIDTH))],
        out_specs=tok,
        out_shape=jax.ShapeDtypeStruct((batch, CTX_LEN, HY_WIDTH), BF16),
        compiler_params=_cparams(1),
        name="hy_ctx",
    )(z_ctx, x0_ctx, g_spec, cst["mcf"][0], cst["mci"][0], d_bias.reshape(1, HY_WIDTH))


def _hyena_filter_halves(length, fw1, fb1, fw2, fb2, fw3, fb3, fw4, freq):
    t = jnp.linspace(0.0, 1.0, length, dtype=F32)[:, None]
    w = 2.0 * math.pi * jnp.arange(length, dtype=F32)[:, None] / length
    bands = jnp.linspace(1e-4, HY_BANDS - 1, HY_BANDS, dtype=F32)
    emb = jnp.concatenate([t, jnp.cos(bands * w), -jnp.sin(bands * w)], axis=-1)

    def second_half_order(a):
        return jnp.concatenate([a[:1], a[:0:-1]], axis=0)

    hp = lax.Precision.HIGHEST
    h = jnp.concatenate([emb, second_half_order(emb)], axis=0)
    h = jnp.sin(freq * (jnp.dot(h, fw1, precision=hp) + fb1))
    h = jnp.sin(freq * (jnp.dot(h, fw2, precision=hp) + fb2))
    h = jnp.sin(freq * (jnp.dot(h, fw3, precision=hp) + fb3))
    max_decay = math.log(HY_TARGET) / HY_FAST_PCT
    min_decay = math.log(HY_TARGET) / HY_SLOW_PCT
    deltas = jnp.abs(jnp.linspace(min_decay, max_decay, HY_WIDTH, dtype=F32))
    causal = jnp.dot(h[:length], fw4[:, :HY_WIDTH], precision=hp).astype(F32) * jnp.exp(-t * deltas)
    anti = (jnp.dot(h[length:], fw4[:, HY_WIDTH:], precision=hp).astype(F32)
            * jnp.exp(-second_half_order(t) * deltas))
    first_row = jnp.arange(length)[:, None] == 0
    return causal + jnp.where(first_row, anti[:1], 0.0), jnp.where(first_row, 0.0, anti)


def _odd_in_kernel(x_ref, mod_ref, gain_ref, w_ref, cos_ref, sin_ref, q_ref, k_ref, v_ref, sg_ref):
    h = _modulated(x_ref[0], gain_ref[...], mod_ref, 3, 4).astype(BF16)
    cos = cos_ref[...]
    sin = sin_ref[...]

    def rope_store(dst_ref, col0, scale):
        p = _bdot(h, w_ref[:, col0:col0 + RET_QK])
        for c in range(RET_QK // LANES):
            t = p[:, c * LANES:(c + 1) * LANES]
            half = (c % 2) * LANES
            r = t * cos[:, half:half + LANES] + pltpu.roll(t, LANES // 2, 1) * sin[:, half:half + LANES]
            dst_ref[0, :, c * LANES:(c + 1) * LANES] = (r * scale).astype(BF16)

    rope_store(q_ref, 0, 1.0)
    rope_store(k_ref, RET_QK, RET_KEY_DIM ** -0.5)
    v_ref[0] = _bdot(h, w_ref[:, 2 * RET_QK:2 * RET_QK + RET_V]).astype(BF16)
    sg_ref[0] = _silu(_bdot(h, w_ref[:, 2 * RET_QK + RET_V:])).astype(BF16)


def _odd_in_call(xt, mod_i, gain, w, cos_t, sin_t):
    batch = xt.shape[0]
    rope_spec = pl.BlockSpec((TM, RET_KEY_DIM), lambda b, j: (j, 0))
    return pl.pallas_call(
        _odd_in_kernel,
        grid=(batch, NT_ALL),
        in_specs=[_tok_spec(D_MODEL), _mod_spec(batch), _full_spec((1, D_MODEL)),
                  _full_spec((D_MODEL, w.shape[1])), rope_spec, rope_spec],
        out_specs=[_tok_spec(RET_QK), _tok_spec(RET_QK), _tok_spec(RET_V), _tok_spec(RET_V)],
        out_shape=[jax.ShapeDtypeStruct((batch, T_TOK, RET_QK), BF16)] * 2
        + [jax.ShapeDtypeStruct((batch, T_TOK, RET_V), BF16)] * 2,
        compiler_params=_cparams(2),
        name="odd_in",
    )(xt, mod_i, gain, w, cos_t, sin_t)


def _rope_tables():
    t = np.arange(SEQ)
    n_freq = RET_KEY_DIM // 4
    inv = ROPE_BASE ** (-jnp.arange(n_freq, dtype=F32) / n_freq)
    ang_r = jnp.asarray(t // GRID_W, F32)[:, None] * inv
    ang_c = jnp.asarray(t % GRID_W, F32)[:, None] * inv
    cr, sr, cc, sc = jnp.cos(ang_r), jnp.sin(ang_r), jnp.cos(ang_c), jnp.sin(ang_c)
    cos_l = jnp.concatenate([cr, cr, cc, cc], axis=-1)
    sin_l = jnp.concatenate([-sr, sr, -sc, sc], axis=-1)
    cos_t = jnp.concatenate([cos_l, jnp.ones((CTX_LEN, RET_KEY_DIM), F32)], axis=0)
    sin_t = jnp.concatenate([sin_l, jnp.zeros((CTX_LEN, RET_KEY_DIM), F32)], axis=0)
    return cos_t, sin_t


def _ret_kernel(lg_ref, q_ref, k_ref, v_ref, sg_ref, y_ref, o_acc, state_f, state_b):
    head = pl.program_id(1)
    n_c = RET_BLOCK
    n_lat = SEQ // n_c
    assert CTX_LEN == n_c and n_lat % 2 == 0
    ii = lax.broadcasted_iota(jnp.int32, (n_c, n_c), 0).astype(F32)
    jj = lax.broadcasted_iota(jnp.int32, (n_c, n_c), 1).astype(F32)
    row_k = lax.broadcasted_iota(jnp.int32, (n_c, RET_KEY_DIM), 0).astype(F32)
    row_v = lax.broadcasted_iota(jnp.int32, (n_c, RET_VAL_DIM), 0).astype(F32)

    def decays(backward):
        lg = lg_ref[1 if backward else 0, head]
        if backward:
            diff = jj - ii
            xi = jnp.exp(lg * (n_c - row_v))
            zeta = jnp.exp(lg * row_k)
        else:
            diff = ii - jj
            xi = jnp.exp(lg * (row_v + 1.0))
            zeta = jnp.exp(lg * (n_c - 1.0 - row_k))
        dmask = jnp.where(diff >= 0, jnp.exp(lg * jnp.maximum(diff, 0.0)), 0.0)
        g_chunk = jnp.exp(lg * n_c + jnp.zeros((1, RET_VAL_DIM), F32))
        return dmask, xi, zeta, g_chunk

    def advance(chunk, state, consts, finalize):
        dmask, xi, zeta, g_chunk = consts
        r = chunk * n_c if isinstance(chunk, int) else pl.multiple_of(chunk * n_c, n_c)
        qc = q_ref[0, pl.ds(r, n_c), :]
        kc = k_ref[0, pl.ds(r, n_c), :]
        vc = v_ref[0, pl.ds(r, n_c), :]
        st = state[...]
        inner = lax.dot_general(qc, kc, _NT_DIMS, preferred_element_type=F32) * dmask
        o = _bdot(inner.astype(BF16), vc) + _bdot(qc, st.astype(BF16)) * xi
        kz = (kc.astype(F32) * zeta).astype(BF16)
        state[...] = st * g_chunk + lax.dot_general(kz, vc, _TN_DIMS, preferred_element_type=F32)
        if finalize:
            tot = o_acc[pl.ds(r, n_c), :] + o
            mu = jnp.mean(tot, axis=-1, keepdims=True)
            cen = tot - mu
            var = jnp.mean(cen * cen, axis=-1, keepdims=True)
            yn = cen * lax.rsqrt(var + GN_EPS)
            y_ref[0, pl.ds(r, n_c), :] = (sg_ref[0, pl.ds(r, n_c), :].astype(F32) * yn).astype(BF16)
        else:
            o_acc[pl.ds(r, n_c), :] = o

    consts_f = decays(False)
    consts_b = decays(True)
    state_f[...] = jnp.zeros_like(state_f)
    state_b[...] = jnp.zeros_like(state_b)

    def both(chunk_f, chunk_b, finalize):
        advance(chunk_f, state_f, consts_f, finalize)
        advance(chunk_b, state_b, consts_b, finalize)

    advance(n_lat, state_f, consts_f, False)
    advance(n_lat, state_b, consts_b, True)

    def first_half(s, carry):
        both(s, n_lat - 1 - s, False)
        return carry

    def second_half(s, carry):
        both(s, n_lat - 1 - s, True)
        return carry

    lax.fori_loop(0, n_lat // 2, first_half, 0, unroll=2)
    lax.fori_loop(n_lat // 2, n_lat, second_half, 0, unroll=2)


def _ret_call(lg, q, k, v, sg):
    batch = q.shape[0]
    qk_spec = pl.BlockSpec((1, T_TOK, RET_KEY_DIM), lambda b, h: (b, 0, h))
    v_spec = pl.BlockSpec((1, T_TOK, RET_VAL_DIM), lambda b, h: (b, 0, h))
    return pl.pallas_call(
        _ret_kernel,
        grid=(batch, RET_HEADS),
        in_specs=[pl.BlockSpec(memory_space=pltpu.SMEM), qk_spec, qk_spec, v_spec, v_spec],
        out_specs=v_spec,
        out_shape=jax.ShapeDtypeStruct((batch, T_TOK, RET_V), BF16),
        scratch_shapes=[pltpu.VMEM((T_TOK, RET_VAL_DIM), F32),
                        pltpu.VMEM((RET_KEY_DIM, RET_VAL_DIM), F32),
                        pltpu.VMEM((RET_KEY_DIM, RET_VAL_DIM), F32)],
        compiler_params=_cparams(2),
        name="retention",
    )(lg, q, k, v, sg)


def _fft_s2_fwd_kernel(a_ref, tw_ref, fhi_ref, flo_ref, g_ref):
    reps = HY_WIDTH // LANES
    for kk in range(FFT_KB):
        twr = jnp.concatenate([tw_ref[0, kk]] * reps, axis=1)
        twi = jnp.concatenate([tw_ref[1, kk]] * reps, axis=1)
        ar = a_ref[0, kk, 0:FFT_N2, :]
        ai = a_ref[1, kk, 0:FFT_N2, :]
        d = jnp.concatenate([ar * twr - ai * twi, ar * twi + ai * twr], axis=0)
        x = _dot3(fhi_ref[...], flo_ref[...], d)
        g_ref[0, kk] = x[:FFT_N2]
        g_ref[1, kk] = x[FFT_N2:]


def _dft_dense_kernel(x_ref, mhi_ref, mlo_ref, o_ref):
    o_ref[...] = _dot3(mhi_ref[...], mlo_ref[...], x_ref[...])


def _filter_spectra(filter_params):
    cst = _fft_constants()
    rows_z = FFT_N1H * FFT_PITCH
    rows_a = FFT_N1 * FFT_PITCH

    def slabs(half):
        half = half.reshape(FFT_N1H, FFT_N2, HY_WIDTH)
        return jnp.pad(half, ((0, 0), (0, FFT_PITCH - FFT_N2), (0, 0))).reshape(rows_z, HY_WIDTH)

    filt = jnp.stack([slabs(half) for half in _hyena_filter_halves(SEQ, *filter_params)])[None]
    a = pl.pallas_call(
        _fft_s1_kernel,
        grid=(1, HY_WIDTH // LANES),
        in_specs=[pl.BlockSpec((1, 2, rows_z, LANES), lambda p, cb: (p, 0, 0, cb)),
                  _full_spec((2 * FFT_N1, FFT_N1)), _full_spec((2 * FFT_N1, FFT_N1))],
        out_specs=pl.BlockSpec((1, 2, rows_a, LANES), lambda p, cb: (p, 0, 0, cb)),
        out_shape=jax.ShapeDtypeStruct((1, 2, rows_a, HY_WIDTH), F32),
        compiler_params=_cparams(2),
        name="filt_fft_s1",
    )(filt, *cst["m1r"])
    a = a.reshape(2, FFT_N1, FFT_PITCH, HY_WIDTH)
    sq = (2 * FFT_N2, 2 * FFT_N2)
    g_lat = pl.pallas_call(
        _fft_s2_fwd_kernel,
        grid=(FFT_N1 // FFT_KB,),
        in_specs=[pl.BlockSpec((2, FFT_KB, FFT_PITCH, HY_WIDTH), lambda kb: (0, kb, 0, 0)),
                  pl.BlockSpec((2, FFT_KB, FFT_N2, LANES), lambda kb: (0, kb, 0, 0)),
                  pl.BlockSpec(sq, lambda kb: (0, 0)), pl.BlockSpec(sq, lambda kb: (0, 0))],
        out_specs=pl.BlockSpec((2, FFT_KB, FFT_N2, HY_WIDTH), lambda kb: (0, kb, 0, 0)),
        out_shape=jax.ShapeDtypeStruct((2, FFT_N1, FFT_N2, HY_WIDTH), F32),
        compiler_params=_cparams(1),
        name="filt_fft_s2",
    )(a, cst["tw"], *cst["m2"])
    filt_ctx = jnp.concatenate(_hyena_filter_halves(CTX_LEN, *filter_params), axis=0)
    g_ctx = pl.pallas_call(
        _dft_dense_kernel,
        out_shape=jax.ShapeDtypeStruct((2 * FFT_NC, HY_WIDTH), F32),
        compiler_params=pltpu.CompilerParams(vmem_limit_bytes=VMEM_LIMIT_BYTES),
        name="filt_dft_ctx",
    )(filt_ctx, *cst["mcf_full"])
    return g_lat, g_ctx.reshape(2, FFT_NC, HY_WIDTH)


def _even_mixer(xt, mod_i, gain, w_in, w_out, q_gain, k_gain, rpb, conv_w, conv_b, filter_params, d_bias):
    scale = NA_HEAD_DIM ** -0.5
    qg = (jnp.tile(q_gain, NA_HEADS) * scale).reshape(1, NA_WIDTH)
    kg = jnp.tile(k_gain, NA_HEADS).reshape(1, NA_WIDTH)
    head_of = np.arange(NA_WIDTH) // NA_HEAD_DIM
    bd = jnp.asarray((head_of[:, None] == head_of[None, :]) / NA_HEAD_DIM, BF16)
    q, k, v, u = _even_in_call(xt, mod_i, gain, w_in.astype(BF16), qg, kg, bd)
    a = _na_call(q, k, v, _na_bias_table(rpb))
    z_lat, x0_lat, z_ctx, x0_ctx = _hy_pre_call(u, conv_w, conv_b.reshape(1, -1))
    g_lat, g_ctx = _filter_spectra(filter_params)
    y_lat = _hy_long_conv_call(z_lat, x0_lat, g_lat, d_bias)
    y_ctx = _hy_ctx_conv_call(z_ctx, x0_ctx, g_ctx, d_bias)
    return a, y_lat.reshape(xt.shape[0], FFT_N1H, FFT_PITCH, HY_WIDTH), y_ctx, w_out.astype(BF16)


def _odd_mixer(xt, mod_i, gain, w_in, w_out, logit_f, logit_b, rope):
    q, k, v, sg = _odd_in_call(xt, mod_i, gain, w_in.astype(BF16), *rope)
    lg = jnp.stack([jax.nn.log_sigmoid(logit_f.astype(F32)), jax.nn.log_sigmoid(logit_b.astype(F32))])
    return _ret_call(lg, q, k, v, sg), w_out.astype(BF16)


def kernel(x, c, ctx, c_ctx, w_mod, b_mod, norm_gain, ffn_a_in, ffn_a_out, ffn_b_in, ffn_b_out,
           even_in, even_out, na_q_gain, na_k_gain, na_rpb, hy_conv_w, hy_conv_b,
           hy_fw1, hy_fb1, hy_fw2, hy_fb2, hy_fw3, hy_fb3, hy_fw4, hy_freq, hy_bias,
           ret_in, ret_out, ret_logit_f, ret_logit_b):
    batch = x.shape[0]
    assert x.shape == (batch, SEQ, D_MODEL) and ctx.shape == (batch, CTX_LEN, D_MODEL)
    assert batch % 2 == 0 and batch < MOD_ROWS
    c_rows = jnp.concatenate([c, c_ctx[None], jnp.zeros((MOD_ROWS - batch - 1, D_MODEL), F32)], axis=0)
    mod_all = _mod_call(c_rows, w_mod, b_mod).reshape(DEPTH, MOD_ROWS, N_MOD, D_MODEL)
    rope = _rope_tables()
    for i in range(DEPTH):
        last = i == DEPTH - 1
        mod_i = mod_all[i]
        gains = norm_gain[i].reshape(3, 1, D_MODEL)
        w_a = (ffn_a_in[i].astype(BF16), ffn_a_out[i].astype(BF16))
        w_b = (ffn_b_in[i].astype(BF16), ffn_b_out[i].astype(BF16))
        if i == 0:
            xt = _ffn_call("split", (x, ctx), mod_i, gains[0], *w_a, (0, 1, 2), NT_ALL, batch)
        else:
            xt = _ffn_call("stream", (xt,), mod_i, gains[0], *w_a, (0, 1, 2), NT_ALL, batch)
        if i % 2 == 0:
            e = i // 2
            filter_params = (hy_fw1[e], hy_fb1[e], hy_fw2[e], hy_fb2[e], hy_fw3[e], hy_fb3[e],
                             hy_fw4[e], hy_freq[e])
            source = "even"
            mixed = _even_mixer(xt, mod_i, gains[1], even_in[e], even_out[e], na_q_gain[e], na_k_gain[e],
                                na_rpb[e], hy_conv_w[e], hy_conv_b[e], filter_params, hy_bias[e])
        else:
            o = i // 2
            source = "odd"
            mixed = _odd_mixer(xt, mod_i, gains[1], ret_in[o], ret_out[o], ret_logit_f[o], ret_logit_b[o], rope)
        xt = _ffn_call(source, (xt,) + mixed, mod_i, gains[2], *w_b, (6, 7, 8),
                       NT_LAT if last else NT_ALL, batch)
    return xt
```

```python
import functools
import math

import ml_dtypes
import numpy as np
import jax
import jax.numpy as jnp
from jax import lax
from jax.experimental import pallas as pl
from jax.experimental.pallas import tpu as pltpu

F32 = jnp.float32
BF16 = jnp.bfloat16

D_MODEL = 1024
SEQ = 4096
DEPTH = 4
GRID_W = 64
CTX_LEN = 256
N_MOD = 9
RMS_EPS = 1e-6
GN_EPS = 1e-6
D_FF = 2816
NA_HEADS = 8
NA_HEAD_DIM = 64
NA_WIDTH = NA_HEADS * NA_HEAD_DIM
WIN_R = 8
WIN_C = 16
HY_WIDTH = D_MODEL - NA_WIDTH
HY_BANDS = 8
HY_TARGET = 1e-2
HY_FAST_PCT = 0.3
HY_SLOW_PCT = 1.5
RET_HEADS = 4
RET_KEY_DIM = D_MODEL // RET_HEADS
RET_VAL_DIM = 2 * RET_KEY_DIM
RET_QK = RET_HEADS * RET_KEY_DIM
RET_V = RET_HEADS * RET_VAL_DIM
ROPE_BASE = 10000.0

LANES = 128
VMEM_LIMIT_BYTES = 56 * 2**20
T_TOK = SEQ + CTX_LEN
TM = CTX_LEN
NT_LAT = SEQ // TM
NT_ALL = T_TOK // TM
MOD_ROWS = 16
FF_CHUNKS = ((0, 1536), (1536, 1280))
RET_BLOCK = 256

NA_RB = 4
NA_KR = NA_RB + WIN_R - 1
GRID_H = SEQ // GRID_W
NA_NQ = NA_RB * GRID_W
NA_NK = NA_KR * GRID_W
NEG_BIG = -1e30

FFT_N = 2 * SEQ
FFT_N1 = 64
FFT_N2 = 128
FFT_N1H = FFT_N1 // 2
FFT_KB = 4
FFT_PITCH = FFT_N2 + 8
FFT_NC = 2 * CTX_LEN


def _cparams(n_axes):
    return pltpu.CompilerParams(dimension_semantics=("arbitrary",) * n_axes,
                                vmem_limit_bytes=VMEM_LIMIT_BYTES)


def _bdot(a, b):
    return jnp.dot(a, b, preferred_element_type=F32)


_NT_DIMS = (((1,), (1,)), ((), ()))
_TN_DIMS = (((0,), (0,)), ((), ()))


def _split_hi_lo(m):
    hi = m.astype(BF16)
    lo = (m - hi.astype(F32)).astype(BF16)
    return hi, lo


def _dot3(m_hi, m_lo, d):
    d_hi, d_lo = _split_hi_lo(d)
    return _bdot(m_hi, d_hi) + _bdot(m_lo, d_hi) + _bdot(m_hi, d_lo)


def _modulated(x, gain, mod_ref, shift_row, scale_row):
    ms = jnp.mean(x * x, axis=-1, keepdims=True)
    y = x * lax.rsqrt(ms + RMS_EPS) * gain
    return (y * (1.0 + mod_ref[0, scale_row:scale_row + 1, :])
            + mod_ref[0, shift_row:shift_row + 1, :])


def _silu(a):
    return a * jax.nn.sigmoid(a)


def _mod_kernel(c_ref, w_ref, b_ref, o_ref):
    s = _silu(c_ref[...])
    o_ref[0] = jnp.dot(s, w_ref[0], precision=lax.Precision.HIGHEST,
                       preferred_element_type=F32) + b_ref[0]


def _mod_call(c_rows, w_mod, b_mod):
    depth, d, n = w_mod.shape
    tn = 1024
    return pl.pallas_call(
        _mod_kernel,
        grid=(depth, n // tn),
        in_specs=[pl.BlockSpec((MOD_ROWS, d), lambda i, j: (0, 0)),
                  pl.BlockSpec((1, d, tn), lambda i, j: (i, 0, j)),
                  pl.BlockSpec((1, 1, tn), lambda i, j: (i, 0, j))],
        out_specs=pl.BlockSpec((1, MOD_ROWS, tn), lambda i, j: (i, 0, j)),
        out_shape=jax.ShapeDtypeStruct((depth, MOD_ROWS, n), F32),
        compiler_params=_cparams(2),
        name="mod",
    )(c_rows, w_mod, b_mod.reshape(depth, 1, n))


def _tok_spec(width):
    return pl.BlockSpec((1, TM, width), lambda b, j: (b, j, 0))


def _mod_spec(batch):
    return pl.BlockSpec((1, N_MOD, D_MODEL), lambda b, j: (jnp.where(j == NT_LAT, batch, b), 0, 0))


def _full_spec(shape):
    zeros = (0,) * len(shape)
    return pl.BlockSpec(shape, lambda b, j: zeros)


_FFN_SOURCES = {"stream": 1, "split": 4, "even": 7, "odd": 3}
TM2 = 2 * TM


def _per_half(fn):
    return jnp.concatenate([fn(half, slice(half * TM, (half + 1) * TM)) for half in range(2)], axis=0)


W_STAGE = 256


def _stage_weight(src, dst, stage, sem, axis):
    n_chunks = src.shape[axis] // W_STAGE

    def window(ref, c):
        chunk = pl.ds(c * W_STAGE, W_STAGE)
        return ref.at[chunk, :] if axis == 0 else ref.at[:, chunk]

    def copy(c):
        return pltpu.make_async_copy(window(src, c), stage.at[c % 2], sem.at[c % 2])

    copy(0).start()
    for c in range(n_chunks):
        if c + 1 < n_chunks:
            copy(c + 1).start()
        copy(c).wait()
        window(dst, c)[...] = stage[c % 2].astype(BF16)


def _ffn_kernel(*refs, rows, source, n_tiles, n_pairs, layer):
    shift_row, scale_row, gate_row = rows
    n_src = _FFN_SOURCES[source]
    src = refs[:n_src]
    (modn0, modn1, modc0, modc1, gain_ref, win_hbm, wout_hbm, o_ref, h_a, x_a, h_b, x_b,
     win_ref, wout_ref, stage_in, stage_out, sem) = refs[n_src:]
    modn = (modn0, modn1)
    modc = (modc0, modc1)
    step = pl.program_id(0)
    first_tile_next = 2 * jnp.minimum(step, n_pairs - 1)

    def is_ctx(half):
        return (first_tile_next + half) % n_tiles == NT_LAT

    def gated(x_ref, o):
        return _per_half(lambda half, r: x_ref[0, r, :] + modn[half][0, 5:6, :] * o[r])

    def next_rows():
        if source == "stream":
            return src[0][0]
        if source == "split":
            lat, ctx = src[0:2], src[2:4]
            return _per_half(lambda half, r: jnp.where(is_ctx(half), ctx[half][0], lat[half][0]))
        if source == "even":
            x_ref, a_ref, yl0, yl1, yc0, yc1, wo_ref = src
            y_lat, y_ctx = (yl0, yl1), (yc0, yc1)

            def hyena_rows(half, r):
                slabs = [y_lat[half][0, s, 0:FFT_N2, :] for s in range(TM // FFT_N2)]
                return jnp.where(is_ctx(half), y_ctx[half][0], jnp.concatenate(slabs, axis=0)).astype(BF16)

            o = _bdot(a_ref[0], wo_ref[0:NA_WIDTH, :]) + _bdot(_per_half(hyena_rows), wo_ref[NA_WIDTH:, :])
            return gated(x_ref, o)
        x_ref, y_ref, wo_ref = src
        return gated(x_ref, _bdot(y_ref[0], wo_ref[...]))

    def body(h_read, x_read, h_write, x_write):
        x_next = next_rows()
        h_write[...] = _per_half(lambda half, r: _modulated(
            x_next[r], gain_ref[...], modn[half], shift_row, scale_row)).astype(BF16)
        x_write[...] = x_next
        h = h_read[...]
        acc = jnp.zeros((TM2, D_MODEL), F32)
        for start, size in FF_CHUNKS:
            a = _bdot(h, win_ref[:, start:start + size])
            b = _bdot(h, win_ref[:, D_FF + start:D_FF + start + size])
            g = (_silu(a) * b).astype(BF16)
            acc = acc + _bdot(g, wout_ref[start:start + size, :])
        o_ref[0] = _per_half(lambda half, r: x_read[r, :] + (0.5 * modc[half][0, gate_row:gate_row + 1, :]) * acc[r])

    @pl.when(step == 0)
    def _():
        h_b[...] = jnp.zeros_like(h_b)
        x_b[...] = jnp.zeros_like(x_b)
        _stage_weight(win_hbm.at[layer], win_ref, stage_in, sem, axis=1)
        _stage_weight(wout_hbm.at[layer], wout_ref, stage_out, sem, axis=0)

    @pl.when(step % 2 == 0)
    def _():
        body(h_b, x_b, h_a, x_a)

    @pl.when(step % 2 == 1)
    def _():
        body(h_a, x_a, h_b, x_b)


def _ffn_call(source, srcs, mod_i, gain, w_in, w_out, layer, rows, n_tiles, batch):
    n_all = batch * n_tiles
    assert n_all % 2 == 0 and n_tiles in (NT_LAT, NT_ALL)
    n_pairs = n_all // 2

    def next_pair(s):
        return jnp.minimum(s, n_pairs - 1)

    def cur_pair(s):
        return jnp.maximum(s - 1, 0)

    def tile_of(pair_of, half):
        def fn(s):
            t = 2 * pair_of(s) + half
            return t // n_tiles, t % n_tiles
        return fn

    def mod_spec(pair_of, half):
        def index(s):
            b, j = tile_of(pair_of, half)(s)
            return jnp.where(j == NT_LAT, batch, b), 0, 0
        return pl.BlockSpec((1, N_MOD, D_MODEL), index)

    def tok(arr):
        width = arr.shape[-1]
        if n_tiles == NT_ALL:
            return arr.reshape(1, batch * T_TOK, width), pl.BlockSpec(
                (1, TM2, width), lambda s: (0, next_pair(s), 0))
        per_sample = n_tiles // 2
        return arr, pl.BlockSpec((1, TM2, width),
                                 lambda s: (next_pair(s) // per_sample, next_pair(s) % per_sample, 0))

    def lat_tile(block, n_trailing, half):
        def index(s):
            b, j = tile_of(next_pair, half)(s)
            return (b, jnp.minimum(j, NT_LAT - 1)) + (0,) * n_trailing
        return pl.BlockSpec(block, index)

    def ctx_tile(block, half):
        return pl.BlockSpec(block, lambda s: (tile_of(next_pair, half)(s)[0],) + (0,) * (len(block) - 1))

    def full(shape):
        zeros = (0,) * len(shape)
        return pl.BlockSpec(shape, lambda s: zeros)

    halves = (0, 1)
    if source == "stream":
        (xt,) = srcs
        xt, x_spec = tok(xt)
        args, src_specs = (xt,), [x_spec]
    elif source == "split":
        x, ctx = srcs
        args = (x, x, ctx, ctx)
        src_specs = ([lat_tile((1, TM, D_MODEL), 1, h) for h in halves]
                     + [ctx_tile((1, CTX_LEN, D_MODEL), h) for h in halves])
    elif source == "even":
        xt, a, y_lat, y_ctx, wo = srcs
        (xt, x_spec), (a, a_spec) = tok(xt), tok(a)
        args = (xt, a, y_lat, y_lat, y_ctx, y_ctx, wo)
        src_specs = ([x_spec, a_spec]
                     + [lat_tile((1, TM // FFT_N2, FFT_PITCH, HY_WIDTH), 2, h) for h in halves]
                     + [ctx_tile((1, CTX_LEN, HY_WIDTH), h) for h in halves] + [full((D_MODEL, D_MODEL))])
    else:
        xt, y, wo = srcs
        (xt, x_spec), (y, y_spec) = tok(xt), tok(y)
        args = (xt, y, wo)
        src_specs = [x_spec, y_spec, full((RET_V, D_MODEL))]
    out = pl.pallas_call(
        functools.partial(_ffn_kernel, rows=rows, source=source, n_tiles=n_tiles, n_pairs=n_pairs,
                          layer=layer),
        grid=(n_pairs + 1,),
        in_specs=src_specs + [mod_spec(next_pair, 0), mod_spec(next_pair, 1),
                              mod_spec(cur_pair, 0), mod_spec(cur_pair, 1), full((1, D_MODEL)),
                              pl.BlockSpec(memory_space=pl.ANY), pl.BlockSpec(memory_space=pl.ANY)],
        out_specs=pl.BlockSpec((1, TM2, D_MODEL), lambda s: (0, cur_pair(s), 0)),
        out_shape=jax.ShapeDtypeStruct((1, n_all * TM, D_MODEL), F32),
        scratch_shapes=[pltpu.VMEM((TM2, D_MODEL), BF16), pltpu.VMEM((TM2, D_MODEL), F32),
                        pltpu.VMEM((TM2, D_MODEL), BF16), pltpu.VMEM((TM2, D_MODEL), F32),
                        pltpu.VMEM((D_MODEL, 2 * D_FF), BF16), pltpu.VMEM((D_FF, D_MODEL), BF16),
                        pltpu.VMEM((2, D_MODEL, W_STAGE), F32), pltpu.VMEM((2, W_STAGE, D_MODEL), F32),
                        pltpu.SemaphoreType.DMA((2,))],
        compiler_params=_cparams(1),
        name="ffn_" + source,
    )(*args, mod_i, mod_i, mod_i, mod_i, gain, w_in, w_out)
    return out.reshape(batch, n_tiles * TM, D_MODEL)


def _even_in_kernel(x_ref, mod_ref, gain_ref, w_ref, qg_ref, kg_ref, bd_ref,
                    q_ref, k_ref, v_ref, u_ref):
    h = _modulated(x_ref[0], gain_ref[...], mod_ref, 3, 4).astype(BF16)

    def head_norm(t, g):
        ms = _bdot((t * t).astype(BF16), bd_ref[...])
        return (t * lax.rsqrt(ms + RMS_EPS) * g).astype(BF16)

    q_ref[0] = head_norm(_bdot(h, w_ref[:, 0:NA_WIDTH]), qg_ref[...])
    k_ref[0] = head_norm(_bdot(h, w_ref[:, NA_WIDTH:2 * NA_WIDTH]), kg_ref[...])
    v_ref[0] = _bdot(h, w_ref[:, 2 * NA_WIDTH:3 * NA_WIDTH]).astype(BF16)
    u_ref[0] = _bdot(h, w_ref[:, 3 * NA_WIDTH:])


def _even_in_call(xt, mod_i, gain, w, q_gain, k_gain, bd):
    batch = xt.shape[0]
    n_in = w.shape[1]
    return pl.pallas_call(
        _even_in_kernel,
        grid=(batch, NT_ALL),
        in_specs=[_tok_spec(D_MODEL), _mod_spec(batch), _full_spec((1, D_MODEL)),
                  _full_spec((D_MODEL, n_in)), _full_spec((1, NA_WIDTH)), _full_spec((1, NA_WIDTH)),
                  _full_spec((NA_WIDTH, NA_WIDTH))],
        out_specs=[_tok_spec(NA_WIDTH), _tok_spec(NA_WIDTH), _tok_spec(NA_WIDTH),
                   _tok_spec(3 * HY_WIDTH)],
        out_shape=[jax.ShapeDtypeStruct((batch, T_TOK, NA_WIDTH), BF16)] * 3
        + [jax.ShapeDtypeStruct((batch, T_TOK, 3 * HY_WIDTH), F32)],
        compiler_params=_cparams(2),
        name="even_in",
    )(xt, mod_i, gain, w, q_gain, k_gain, bd)


def _na_kernel(q_ref, k_ref, v_ref, bias_ref, o_ref):
    lane = lax.broadcasted_iota(jnp.int32, (1, LANES), 1)
    first_head = lane < NA_HEAD_DIM
    k_ctx = k_ref[0, SEQ:T_TOK, :]
    v_ctx = v_ref[0, SEQ:T_TOK, :]

    def stack_heads(q):
        zero = jnp.zeros_like(q)
        return jnp.concatenate([jnp.where(first_head, q, zero), jnp.where(first_head, zero, q)], axis=0)

    def attend(qs, scores_and_values):
        s_list = [lax.dot_general(qs, kk, _NT_DIMS, preferred_element_type=F32) if bias is None
                  else lax.dot_general(qs, kk, _NT_DIMS, preferred_element_type=F32) + bias
                  for kk, _, bias in scores_and_values]
        m = functools.reduce(jnp.maximum, [jnp.max(s, axis=-1, keepdims=True) for s in s_list])
        o = functools.reduce(jnp.add, [
            _bdot(jnp.exp(s - m).astype(BF16),
                  jnp.concatenate([vv, jnp.ones((vv.shape[0], LANES), BF16)], axis=1))
            for s, (_, vv, _) in zip(s_list, scores_and_values)])
        o = o[:, :LANES] / o[:, LANES:]
        n = qs.shape[0] // 2
        return jnp.where(first_head, o[:n], o[n:]).astype(BF16)

    def block(i, carry):
        r0 = i * NA_RB
        u0 = jnp.clip(r0 - WIN_R // 2, 0, GRID_H - NA_KR)
        pattern = jnp.where(i == 0, 0, jnp.where(i == GRID_H // NA_RB - 1, 2, 1))
        q0 = pl.multiple_of(r0 * GRID_W, NA_NQ)
        k0 = pl.multiple_of(u0 * GRID_W, GRID_W)
        qs = stack_heads(q_ref[0, pl.ds(q0, NA_NQ), :])
        k_win = k_ref[0, pl.ds(k0, NA_NK), :]
        v_win = v_ref[0, pl.ds(k0, NA_NK), :]
        o_ref[0, pl.ds(q0, NA_NQ), :] = attend(
            qs, [(k_win, v_win, bias_ref[pattern, 0]), (k_ctx, v_ctx, None)])
        return carry

    lax.fori_loop(0, GRID_H // NA_RB, block, 0, unroll=4)
    o_ref[0, SEQ:T_TOK, :] = attend(stack_heads(q_ref[0, SEQ:T_TOK, :]), [(k_ctx, v_ctx, None)])


def _na_call(q, k, v, bias):
    batch = q.shape[0]
    n_pairs = NA_WIDTH // LANES
    spec = pl.BlockSpec((1, T_TOK, LANES), lambda b, p: (b, 0, p))
    return pl.pallas_call(
        _na_kernel,
        grid=(batch, n_pairs),
        in_specs=[spec, spec, spec,
                  pl.BlockSpec((3, 1, 2 * NA_NQ, NA_NK), lambda b, p: (0, p, 0, 0))],
        out_specs=spec,
        out_shape=jax.ShapeDtypeStruct((batch, T_TOK, NA_WIDTH), BF16),
        compiler_params=_cparams(2),
        name="na_attn",
    )(q, k, v, bias)


def _na_bias_table(rpb):
    j = np.arange(NA_RB)[:, None, None, None]
    c = np.arange(GRID_W)[None, :, None, None]
    kk = np.arange(NA_KR)[None, None, :, None]
    kc = np.arange(GRID_W)[None, None, None, :]
    cs = np.clip(c - WIN_C // 2, 0, GRID_W - WIN_C)
    col_ok = (kc >= cs) & (kc < cs + WIN_C)
    dc = np.clip(kc - c + WIN_C - 1, 0, 2 * WIN_C - 2)[0, :, 0, :]
    col_sel = (dc[..., None] == np.arange(2 * WIN_C - 1)).astype(np.float32)
    shape = (NA_RB, GRID_W, NA_KR, GRID_W)
    oks, row_sels = [], []
    for off, rs_rel in ((0, 0 * j), (-(WIN_R // 2), j), (-(NA_KR - NA_RB), NA_KR - WIN_R + 0 * j)):
        row_ok = (kk >= rs_rel) & (kk < rs_rel + WIN_R)
        dr = np.clip(off + kk - j + WIN_R - 1, 0, 2 * WIN_R - 2)[:, 0, :, 0]
        row_sels.append((dr[..., None] == np.arange(2 * WIN_R - 1)).astype(np.float32))
        oks.append(np.broadcast_to(row_ok & col_ok, shape))
    vals = jnp.einsum("pjkr,hrc,qmc->phjqkm", np.stack(row_sels), rpb.astype(F32), col_sel,
                      precision=lax.Precision.HIGHEST)
    table = jnp.where(np.stack(oks)[:, None], vals, NEG_BIG)
    return table.reshape(3, NA_HEADS // 2, 2 * NA_NQ, NA_NK)


def _hy_pre_kernel(u0_ref, u1_ref, uv_ref, w0_ref, w1_ref, wv_ref, b0_ref, b1_ref, bv_ref,
                   zl_ref, xl_ref, zc_ref, xc_ref):
    row = lax.broadcasted_iota(jnp.int32, (TM, LANES), 0)
    zero_row = jnp.zeros((1, LANES), F32)

    def conv(u_ref, w_ref, b_ref, tile):
        s = tile * TM
        cur = u_ref[0, s:s + TM, :]
        seq_start = tile in (0, NT_LAT)
        seq_end = tile in (NT_LAT - 1, NT_LAT)
        prev_row = zero_row if seq_start else u_ref[0, s - 1:s, :]
        next_row = zero_row if seq_end else u_ref[0, s + TM:s + TM + 1, :]
        before = jnp.where(row == 0, prev_row, pltpu.roll(cur, 1, 0))
        after = jnp.where(row == TM - 1, next_row, pltpu.roll(cur, TM - 1, 0))
        return before * w_ref[0:1, :] + cur * w_ref[1:2, :] + after * w_ref[2:3, :] + b_ref[...]

    for tile in range(NT_ALL):
        x0 = conv(u0_ref, w0_ref, b0_ref, tile)
        z = conv(uv_ref, wv_ref, bv_ref, tile) * conv(u1_ref, w1_ref, b1_ref, tile)
        if tile < NT_LAT:
            for half in range(TM // FFT_N2):
                base = (tile * (TM // FFT_N2) + half) * FFT_PITCH
                zl_ref[0, base:base + FFT_N2, :] = z[half * FFT_N2:(half + 1) * FFT_N2]
                xl_ref[0, base:base + FFT_N2, :] = x0[half * FFT_N2:(half + 1) * FFT_N2]
                pad = jnp.zeros((FFT_PITCH - FFT_N2, LANES), F32)
                zl_ref[0, base + FFT_N2:base + FFT_PITCH, :] = pad
                xl_ref[0, base + FFT_N2:base + FFT_PITCH, :] = pad
        else:
            zc_ref[0] = z
            xc_ref[0] = x0


def _hy_pre_call(u, conv_w, conv_b):
    batch = u.shape[0]
    nb = HY_WIDTH // LANES
    u_specs = [pl.BlockSpec((1, T_TOK, LANES), lambda b, cb, g=g: (b, 0, g * nb + cb)) for g in range(3)]
    w_specs = [pl.BlockSpec((3, LANES), lambda b, cb, g=g: (0, g * nb + cb)) for g in range(3)]
    b_specs = [pl.BlockSpec((1, LANES), lambda b, cb, g=g: (0, g * nb + cb)) for g in range(3)]
    lat = pl.BlockSpec((1, FFT_N1H * FFT_PITCH, LANES), lambda b, cb: (b, 0, cb))
    ctx = pl.BlockSpec((1, CTX_LEN, LANES), lambda b, cb: (b, 0, cb))
    return pl.pallas_call(
        _hy_pre_kernel,
        grid=(batch, nb),
        in_specs=u_specs + w_specs + b_specs,
        out_specs=[lat, lat, ctx, ctx],
        out_shape=[jax.ShapeDtypeStruct((batch, FFT_N1H * FFT_PITCH, HY_WIDTH), F32)] * 2
        + [jax.ShapeDtypeStruct((batch, CTX_LEN, HY_WIDTH), F32)] * 2,
        compiler_params=_cparams(2),
        name="hy_pre",
    )(u, u, u, conv_w, conv_w, conv_w, conv_b, conv_b, conv_b)


def _complex_block(re, im):
    return np.block([[re, -im], [im, re]])


@functools.lru_cache(maxsize=None)
def _fft_constants():
    k1 = np.arange(FFT_N1)
    n1 = np.arange(FFT_N1H)
    f1 = np.exp(-2j * np.pi * np.outer(k1, n1) / FFT_N1)
    m1 = _complex_block(f1.real, f1.imag)
    f1_full = np.exp(-2j * np.pi * np.outer(k1, k1) / FFT_N1)
    m1r = np.concatenate([f1_full.real, f1_full.imag], axis=0)
    n2 = np.arange(FFT_N2)
    tw = np.exp(-2j * np.pi * np.outer(k1, n2) / FFT_N)
    tw = np.stack([tw.real, tw.imag])[..., None] * np.ones((1, 1, 1, LANES))
    f2 = np.exp(-2j * np.pi * np.outer(n2, n2) / FFT_N2)
    m2 = _complex_block(f2.real, f2.imag)
    m2i = _complex_block(f2.real, -f2.imag)
    c1 = np.exp(2j * np.pi * np.outer(n1, k1) / FFT_N1) / FFT_N
    m3 = _complex_block(c1.real, c1.imag)
    kc = np.arange(FFT_NC)
    nc = np.arange(CTX_LEN)
    ang = 2 * np.pi * np.outer(kc, nc) / FFT_NC
    mcf = np.concatenate([np.cos(ang), -np.sin(ang)], axis=0)
    ang_full = 2 * np.pi * np.outer(kc, kc) / FFT_NC
    mcf_full = np.concatenate([np.cos(ang_full), -np.sin(ang_full)], axis=0)
    mci = np.concatenate([np.cos(ang.T), -np.sin(ang.T)], axis=1) / FFT_NC

    def hi_lo(m):
        hi = m.astype(ml_dtypes.bfloat16)
        lo = (m - hi.astype(np.float64)).astype(ml_dtypes.bfloat16)
        return hi, lo

    return dict(m1=hi_lo(m1), m1r=hi_lo(m1r), m2=hi_lo(m2), m2i=hi_lo(m2i), m3=hi_lo(m3),
                mcf=hi_lo(mcf), mcf_full=hi_lo(mcf_full), mci=hi_lo(mci), tw=tw.astype(np.float32))


def _slab_rows(n2, n_slabs):
    return pl.ds(n2, n_slabs, stride=FFT_PITCH)


def _fft_s1_kernel(z_ref, *rest):
    *m_refs, a_ref = rest

    def step(i, carry):
        cols = []
        for dn in range(2):
            rows = _slab_rows(2 * i + dn, FFT_N1H)
            cols.append(jnp.concatenate([z_ref[0, 0, rows, :], z_ref[0, 1, rows, :]], axis=0))
        d = jnp.concatenate(cols, axis=1)
        if len(m_refs) == 2:
            r = _dot3(m_refs[0][...], m_refs[1][...], d)
        else:
            r = _bdot(m_refs[0][...], d.astype(BF16))
        for dn in range(2):
            rows = _slab_rows(2 * i + dn, FFT_N1)
            a_ref[0, 0, rows, :] = r[:FFT_N1, dn * LANES:(dn + 1) * LANES]
            a_ref[0, 1, rows, :] = r[FFT_N1:, dn * LANES:(dn + 1) * LANES]
        return carry

    lax.fori_loop(0, FFT_N2 // 2, step, 0, unroll=4)
    for pad_row in range(FFT_N2, FFT_PITCH):
        for part in range(2):
            a_ref[0, part, _slab_rows(pad_row, FFT_N1), :] = jnp.zeros((FFT_N1, LANES), F32)


def _fft_s2_kernel(a_ref, tw_ref, g_ref, fwd_ref, inv_ref, o_ref):
    reps = HY_WIDTH // LANES
    pad = jnp.zeros((FFT_PITCH - FFT_N2, HY_WIDTH), F32)
    for kk in range(FFT_KB):
        twr = jnp.concatenate([tw_ref[0, kk]] * reps, axis=1)
        twi = jnp.concatenate([tw_ref[1, kk]] * reps, axis=1)
        ar = a_ref[0, 0, kk, 0:FFT_N2, :]
        ai = a_ref[0, 1, kk, 0:FFT_N2, :]
        d = jnp.concatenate([ar * twr - ai * twi, ar * twi + ai * twr], axis=0)
        x = _bdot(fwd_ref[...], d.astype(BF16))
        xr, xi = x[:FFT_N2], x[FFT_N2:]
        gr, gi = g_ref[0, kk], g_ref[1, kk]
        y = jnp.concatenate([xr * gr - xi * gi, xr * gi + xi * gr], axis=0)
        b = _bdot(inv_ref[...], y.astype(BF16))
        br, bi = b[:FFT_N2], b[FFT_N2:]
        o_ref[0, 0, kk, 0:FFT_N2, :] = br * twr + bi * twi
        o_ref[0, 1, kk, 0:FFT_N2, :] = bi * twr - br * twi
        o_ref[0, 0, kk, FFT_N2:FFT_PITCH, :] = pad
        o_ref[0, 1, kk, FFT_N2:FFT_PITCH, :] = pad


def _fft_s3_kernel(b_ref, m_ref, z_ref, x0_ref, bias_ref, y_ref):
    def step(i, carry):
        cols = []
        for dn in range(2):
            rows = _slab_rows(2 * i + dn, FFT_N1)
            cols.append(jnp.concatenate([b_ref[0, 0, rows, :], b_ref[0, 1, rows, :]], axis=0))
        y = _bdot(m_ref[...], jnp.concatenate(cols, axis=1).astype(BF16))
        for dn in range(2):
            rows = _slab_rows(2 * i + dn, FFT_N1H)
            for s in range(2):
                conv = y[s * FFT_N1H:(s + 1) * FFT_N1H, dn * LANES:(dn + 1) * LANES]
                y_ref[0, s, rows, :] = (conv + z_ref[0, s, rows, :] * bias_ref[...]) * x0_ref[0, s, rows, :]
        return carry

    lax.fori_loop(0, FFT_N2 // 2, step, 0, unroll=4)
    for pad_row in range(FFT_N2, FFT_PITCH):
        for s in range(2):
            y_ref[0, s, _slab_rows(pad_row, FFT_N1H), :] = jnp.zeros((FFT_N1H, LANES), F32)


def _hy_long_conv_call(z_lat, x0_lat, g_spec, d_bias):
    batch = z_lat.shape[0]
    pairs = batch // 2
    cst = _fft_constants()
    n_cb = HY_WIDTH // LANES
    rows_z = FFT_N1H * FFT_PITCH
    rows_a = FFT_N1 * FFT_PITCH
    zv = z_lat.reshape(pairs, 2, rows_z, HY_WIDTH)
    xv = x0_lat.reshape(pairs, 2, rows_z, HY_WIDTH)
    z_spec = pl.BlockSpec((1, 2, rows_z, LANES), lambda p, cb: (p, 0, 0, cb))
    a_spec = pl.BlockSpec((1, 2, rows_a, LANES), lambda p, cb: (p, 0, 0, cb))
    a = pl.pallas_call(
        _fft_s1_kernel,
        grid=(pairs, n_cb),
        in_specs=[z_spec, _full_spec((2 * FFT_N1, 2 * FFT_N1H))],
        out_specs=a_spec,
        out_shape=jax.ShapeDtypeStruct((pairs, 2, rows_a, HY_WIDTH), F32),
        compiler_params=_cparams(2),
        name="hy_fft_s1",
    )(zv, cst["m1"][0])
    a = a.reshape(pairs, 2, FFT_N1, FFT_PITCH, HY_WIDTH)
    blk = pl.BlockSpec((1, 2, FFT_KB, FFT_PITCH, HY_WIDTH), lambda kb, p: (p, 0, kb, 0, 0))
    sq = (2 * FFT_N2, 2 * FFT_N2)
    b = pl.pallas_call(
        _fft_s2_kernel,
        grid=(FFT_N1 // FFT_KB, pairs),
        in_specs=[blk,
                  pl.BlockSpec((2, FFT_KB, FFT_N2, LANES), lambda kb, p: (0, kb, 0, 0)),
                  pl.BlockSpec((2, FFT_KB, FFT_N2, HY_WIDTH), lambda kb, p: (0, kb, 0, 0)),
                  _full_spec(sq), _full_spec(sq)],
        out_specs=blk,
        out_shape=jax.ShapeDtypeStruct((pairs, 2, FFT_N1, FFT_PITCH, HY_WIDTH), F32),
        compiler_params=_cparams(2),
        name="hy_fft_s2",
    )(a, cst["tw"], g_spec, cst["m2"][0], cst["m2i"][0])
    b = b.reshape(pairs, 2, rows_a, HY_WIDTH)
    y = pl.pallas_call(
        _fft_s3_kernel,
        grid=(pairs, n_cb),
        in_specs=[a_spec, _full_spec((2 * FFT_N1H, 2 * FFT_N1)),
                  z_spec, z_spec, pl.BlockSpec((1, LANES), lambda p, cb: (0, cb))],
        out_specs=z_spec,
        out_shape=jax.ShapeDtypeStruct((pairs, 2, rows_z, HY_WIDTH), F32),
        compiler_params=_cparams(2),
        name="hy_fft_s3",
    )(b, cst["m3"][0], zv, xv, d_bias.reshape(1, HY_WIDTH))
    return y.reshape(batch, rows_z, HY_WIDTH)


def _hy_ctx_kernel(z_ref, x0_ref, g_ref, fwd_ref, inv_ref, bias_ref, y_ref):
    z = z_ref[0]
    x = _bdot(fwd_ref[...], z.astype(BF16))
    xr, xi = x[:FFT_NC], x[FFT_NC:]
    gr, gi = g_ref[0], g_ref[1]
    y = jnp.concatenate([xr * gr - xi * gi, xr * gi + xi * gr], axis=0)
    conv = _bdot(inv_ref[...], y.astype(BF16))
    y_ref[0] = ((conv + z * bias_ref[...]) * x0_ref[0]).astype(BF16)


def _hy_ctx_conv_call(z_ctx, x0_ctx, g_spec, d_bias):
    batch = z_ctx.shape[0]
    cst = _fft_constants()
    tok = pl.BlockSpec((1, CTX_LEN, HY_WIDTH), lambda b: (b, 0, 0))

    def full(shape):
        zeros = (0,) * len(shape)
        return pl.BlockSpec(shape, lambda b: zeros)

    return pl.pallas_call(
        _hy_ctx_kernel,
        grid=(batch,),
        in_specs=[tok, tok, full((2, FFT_NC, HY_WIDTH)),
                  full((2 * FFT_NC, CTX_LEN)), full((CTX_LEN, 2 * FFT_NC)), full((1, HY_WIDTH))],
        out_specs=tok,
        out_shape=jax.ShapeDtypeStruct((batch, CTX_LEN, HY_WIDTH), BF16),
        compiler_params=_cparams(1),
        name="hy_ctx",
    )(z_ctx, x0_ctx, g_spec, cst["mcf"][0], cst["mci"][0], d_bias.reshape(1, HY_WIDTH))


def _hyena_filter_halves(length, fw1, fb1, fw2, fb2, fw3, fb3, fw4, freq):
    t = jnp.linspace(0.0, 1.0, length, dtype=F32)[:, None]
    w = 2.0 * math.pi * jnp.arange(length, dtype=F32)[:, None] / length
    bands = jnp.linspace(1e-4, HY_BANDS - 1, HY_BANDS, dtype=F32)
    emb = jnp.concatenate([t, jnp.cos(bands * w), -jnp.sin(bands * w)], axis=-1)

    def second_half_order(a):
        return jnp.concatenate([a[:1], a[:0:-1]], axis=0)

    hp = lax.Precision.HIGHEST
    h = jnp.concatenate([emb, second_half_order(emb)], axis=0)
    h = jnp.sin(freq * (jnp.dot(h, fw1, precision=hp) + fb1))
    h = jnp.sin(freq * (jnp.dot(h, fw2, precision=hp) + fb2))
    h = jnp.sin(freq * (jnp.dot(h, fw3, precision=hp) + fb3))
    max_decay = math.log(HY_TARGET) / HY_FAST_PCT
    min_decay = math.log(HY_TARGET) / HY_SLOW_PCT
    deltas = jnp.abs(jnp.linspace(min_decay, max_decay, HY_WIDTH, dtype=F32))
    causal = jnp.dot(h[:length], fw4[:, :HY_WIDTH], precision=hp).astype(F32) * jnp.exp(-t * deltas)
    anti = (jnp.dot(h[length:], fw4[:, HY_WIDTH:], precision=hp).astype(F32)
            * jnp.exp(-second_half_order(t) * deltas))
    first_row = jnp.arange(length)[:, None] == 0
    return causal + jnp.where(first_row, anti[:1], 0.0), jnp.where(first_row, 0.0, anti)


def _odd_in_kernel(x_ref, mod_ref, gain_ref, w_ref, cos_ref, sin_ref, q_ref, k_ref, v_ref, sg_ref):
    h = _modulated(x_ref[0], gain_ref[...], mod_ref, 3, 4).astype(BF16)
    cos = cos_ref[...]
    sin = sin_ref[...]

    def rope_store(dst_ref, col0, scale):
        p = _bdot(h, w_ref[:, col0:col0 + RET_QK])
        for c in range(RET_QK // LANES):
            t = p[:, c * LANES:(c + 1) * LANES]
            half = (c % 2) * LANES
            r = t * cos[:, half:half + LANES] + pltpu.roll(t, LANES // 2, 1) * sin[:, half:half + LANES]
            dst_ref[0, :, c * LANES:(c + 1) * LANES] = (r * scale).astype(BF16)

    rope_store(q_ref, 0, 1.0)
    rope_store(k_ref, RET_QK, RET_KEY_DIM ** -0.5)
    v_ref[0] = _bdot(h, w_ref[:, 2 * RET_QK:2 * RET_QK + RET_V]).astype(BF16)
    sg_ref[0] = _silu(_bdot(h, w_ref[:, 2 * RET_QK + RET_V:])).astype(BF16)


def _odd_in_call(xt, mod_i, gain, w, cos_t, sin_t):
    batch = xt.shape[0]
    rope_spec = pl.BlockSpec((TM, RET_KEY_DIM), lambda b, j: (j, 0))
    return pl.pallas_call(
        _odd_in_kernel,
        grid=(batch, NT_ALL),
        in_specs=[_tok_spec(D_MODEL), _mod_spec(batch), _full_spec((1, D_MODEL)),
                  _full_spec((D_MODEL, w.shape[1])), rope_spec, rope_spec],
        out_specs=[_tok_spec(RET_QK), _tok_spec(RET_QK), _tok_spec(RET_V), _tok_spec(RET_V)],
        out_shape=[jax.ShapeDtypeStruct((batch, T_TOK, RET_QK), BF16)] * 2
        + [jax.ShapeDtypeStruct((batch, T_TOK, RET_V), BF16)] * 2,
        compiler_params=_cparams(2),
        name="odd_in",
    )(xt, mod_i, gain, w, cos_t, sin_t)


def _rope_tables():
    t = np.arange(SEQ)
    n_freq = RET_KEY_DIM // 4
    inv = ROPE_BASE ** (-jnp.arange(n_freq, dtype=F32) / n_freq)
    ang_r = jnp.asarray(t // GRID_W, F32)[:, None] * inv
    ang_c = jnp.asarray(t % GRID_W, F32)[:, None] * inv
    cr, sr, cc, sc = jnp.cos(ang_r), jnp.sin(ang_r), jnp.cos(ang_c), jnp.sin(ang_c)
    cos_l = jnp.concatenate([cr, cr, cc, cc], axis=-1)
    sin_l = jnp.concatenate([-sr, sr, -sc, sc], axis=-1)
    cos_t = jnp.concatenate([cos_l, jnp.ones((CTX_LEN, RET_KEY_DIM), F32)], axis=0)
    sin_t = jnp.concatenate([sin_l, jnp.zeros((CTX_LEN, RET_KEY_DIM), F32)], axis=0)
    return cos_t, sin_t


def _ret_kernel(lg_ref, q_ref, k_ref, v_ref, sg_ref, y_ref, o_acc, state_f, state_b):
    head = pl.program_id(1)
    n_c = RET_BLOCK
    n_lat = SEQ // n_c
    assert CTX_LEN == n_c and n_lat % 2 == 0
    ii = lax.broadcasted_iota(jnp.int32, (n_c, n_c), 0).astype(F32)
    jj = lax.broadcasted_iota(jnp.int32, (n_c, n_c), 1).astype(F32)
    row_k = lax.broadcasted_iota(jnp.int32, (n_c, RET_KEY_DIM), 0).astype(F32)
    row_v = lax.broadcasted_iota(jnp.int32, (n_c, RET_VAL_DIM), 0).astype(F32)

    def decays(backward):
        lg = lg_ref[1 if backward else 0, head]
        if backward:
            diff = jj - ii
            xi = jnp.exp(lg * (n_c - row_v))
            zeta = jnp.exp(lg * row_k)
        else:
            diff = ii - jj
            xi = jnp.exp(lg * (row_v + 1.0))
            zeta = jnp.exp(lg * (n_c - 1.0 - row_k))
        dmask = jnp.where(diff >= 0, jnp.exp(lg * jnp.maximum(diff, 0.0)), 0.0)
        g_chunk = jnp.exp(lg * n_c + jnp.zeros((1, RET_VAL_DIM), F32))
        return dmask, xi, zeta, g_chunk

    def advance(chunk, state, consts, finalize):
        dmask, xi, zeta, g_chunk = consts
        r = chunk * n_c if isinstance(chunk, int) else pl.multiple_of(chunk * n_c, n_c)
        qc = q_ref[0, pl.ds(r, n_c), :]
        kc = k_ref[0, pl.ds(r, n_c), :]
        vc = v_ref[0, pl.ds(r, n_c), :]
        st = state[...]
        inner = lax.dot_general(qc, kc, _NT_DIMS, preferred_element_type=F32) * dmask
        o = _bdot(inner.astype(BF16), vc) + _bdot(qc, st.astype(BF16)) * xi
        kz = (kc.astype(F32) * zeta).astype(BF16)
        state[...] = st * g_chunk + lax.dot_general(kz, vc, _TN_DIMS, preferred_element_type=F32)
        if finalize:
            tot = o_acc[pl.ds(r, n_c), :] + o
            mu = jnp.mean(tot, axis=-1, keepdims=True)
            cen = tot - mu
            var = jnp.mean(cen * cen, axis=-1, keepdims=True)
            yn = cen * lax.rsqrt(var + GN_EPS)
            y_ref[0, pl.ds(r, n_c), :] = (sg_ref[0, pl.ds(r, n_c), :].astype(F32) * yn).astype(BF16)
        else:
            o_acc[pl.ds(r, n_c), :] = o

    consts_f = decays(False)
    consts_b = decays(True)
    state_f[...] = jnp.zeros_like(state_f)
    state_b[...] = jnp.zeros_like(state_b)

    def both(chunk_f, chunk_b, finalize):
        advance(chunk_f, state_f, consts_f, finalize)
        advance(chunk_b, state_b, consts_b, finalize)

    advance(n_lat, state_f, consts_f, False)
    advance(n_lat, state_b, consts_b, True)

    def first_half(s, carry):
        both(s, n_lat - 1 - s, False)
        return carry

    def second_half(s, carry):
        both(s, n_lat - 1 - s, True)
        return carry

    lax.fori_loop(0, n_lat // 2, first_half, 0, unroll=2)
    lax.fori_loop(n_lat // 2, n_lat, second_half, 0, unroll=2)


def _ret_call(lg, q, k, v, sg):
    batch = q.shape[0]
    qk_spec = pl.BlockSpec((1, T_TOK, RET_KEY_DIM), lambda b, h: (b, 0, h))
    v_spec = pl.BlockSpec((1, T_TOK, RET_VAL_DIM), lambda b, h: (b, 0, h))
    return pl.pallas_call(
        _ret_kernel,
        grid=(batch, RET_HEADS),
        in_specs=[pl.BlockSpec(memory_space=pltpu.SMEM), qk_spec, qk_spec, v_spec, v_spec],
        out_specs=v_spec,
        out_shape=jax.ShapeDtypeStruct((batch, T_TOK, RET_V), BF16),
        scratch_shapes=[pltpu.VMEM((T_TOK, RET_VAL_DIM), F32),
                        pltpu.VMEM((RET_KEY_DIM, RET_VAL_DIM), F32),
                        pltpu.VMEM((RET_KEY_DIM, RET_VAL_DIM), F32)],
        compiler_params=_cparams(2),
        name="retention",
    )(lg, q, k, v, sg)


def _fft_s2_fwd_kernel(a_ref, tw_ref, fhi_ref, flo_ref, g_ref):
    reps = HY_WIDTH // LANES
    for kk in range(FFT_KB):
        twr = jnp.concatenate([tw_ref[0, kk]] * reps, axis=1)
        twi = jnp.concatenate([tw_ref[1, kk]] * reps, axis=1)
        ar = a_ref[0, kk, 0:FFT_N2, :]
        ai = a_ref[1, kk, 0:FFT_N2, :]
        d = jnp.concatenate([ar * twr - ai * twi, ar * twi + ai * twr], axis=0)
        x = _dot3(fhi_ref[...], flo_ref[...], d)
        g_ref[0, kk] = x[:FFT_N2]
        g_ref[1, kk] = x[FFT_N2:]


def _dft_dense_kernel(x_ref, mhi_ref, mlo_ref, o_ref):
    o_ref[...] = _dot3(mhi_ref[...], mlo_ref[...], x_ref[...])


def _filter_spectra(filter_params):
    cst = _fft_constants()
    rows_z = FFT_N1H * FFT_PITCH
    rows_a = FFT_N1 * FFT_PITCH

    def slabs(half):
        half = half.reshape(FFT_N1H, FFT_N2, HY_WIDTH)
        return jnp.pad(half, ((0, 0), (0, FFT_PITCH - FFT_N2), (0, 0))).reshape(rows_z, HY_WIDTH)

    filt = jnp.stack([slabs(half) for half in _hyena_filter_halves(SEQ, *filter_params)])[None]
    a = pl.pallas_call(
        _fft_s1_kernel,
        grid=(1, HY_WIDTH // LANES),
        in_specs=[pl.BlockSpec((1, 2, rows_z, LANES), lambda p, cb: (p, 0, 0, cb)),
                  _full_spec((2 * FFT_N1, FFT_N1)), _full_spec((2 * FFT_N1, FFT_N1))],
        out_specs=pl.BlockSpec((1, 2, rows_a, LANES), lambda p, cb: (p, 0, 0, cb)),
        out_shape=jax.ShapeDtypeStruct((1, 2, rows_a, HY_WIDTH), F32),
        compiler_params=_cparams(2),
        name="filt_fft_s1",
    )(filt, *cst["m1r"])
    a = a.reshape(2, FFT_N1, FFT_PITCH, HY_WIDTH)
    sq = (2 * FFT_N2, 2 * FFT_N2)
    g_lat = pl.pallas_call(
        _fft_s2_fwd_kernel,
        grid=(FFT_N1 // FFT_KB,),
        in_specs=[pl.BlockSpec((2, FFT_KB, FFT_PITCH, HY_WIDTH), lambda kb: (0, kb, 0, 0)),
                  pl.BlockSpec((2, FFT_KB, FFT_N2, LANES), lambda kb: (0, kb, 0, 0)),
                  pl.BlockSpec(sq, lambda kb: (0, 0)), pl.BlockSpec(sq, lambda kb: (0, 0))],
        out_specs=pl.BlockSpec((2, FFT_KB, FFT_N2, HY_WIDTH), lambda kb: (0, kb, 0, 0)),
        out_shape=jax.ShapeDtypeStruct((2, FFT_N1, FFT_N2, HY_WIDTH), F32),
        compiler_params=_cparams(1),
        name="filt_fft_s2",
    )(a, cst["tw"], *cst["m2"])
    filt_ctx = jnp.concatenate(_hyena_filter_halves(CTX_LEN, *filter_params), axis=0)
    g_ctx = pl.pallas_call(
        _dft_dense_kernel,
        out_shape=jax.ShapeDtypeStruct((2 * FFT_NC, HY_WIDTH), F32),
        compiler_params=pltpu.CompilerParams(vmem_limit_bytes=VMEM_LIMIT_BYTES),
        name="filt_dft_ctx",
    )(filt_ctx, *cst["mcf_full"])
    return g_lat, g_ctx.reshape(2, FFT_NC, HY_WIDTH)


def _even_mixer(xt, mod_i, gain, w_in, w_out, q_gain, k_gain, rpb, conv_w, conv_b, filter_params, d_bias):
    scale = NA_HEAD_DIM ** -0.5
    qg = (jnp.tile(q_gain, NA_HEADS) * scale).reshape(1, NA_WIDTH)
    kg = jnp.tile(k_gain, NA_HEADS).reshape(1, NA_WIDTH)
    head_of = np.arange(NA_WIDTH) // NA_HEAD_DIM
    bd = jnp.asarray((head_of[:, None] == head_of[None, :]) / NA_HEAD_DIM, BF16)
    q, k, v, u = _even_in_call(xt, mod_i, gain, w_in.astype(BF16), qg, kg, bd)
    a = _na_call(q, k, v, _na_bias_table(rpb))
    z_lat, x0_lat, z_ctx, x0_ctx = _hy_pre_call(u, conv_w, conv_b.reshape(1, -1))
    g_lat, g_ctx = _filter_spectra(filter_params)
    y_lat = _hy_long_conv_call(z_lat, x0_lat, g_lat, d_bias)
    y_ctx = _hy_ctx_conv_call(z_ctx, x0_ctx, g_ctx, d_bias)
    return a, y_lat.reshape(xt.shape[0], FFT_N1H, FFT_PITCH, HY_WIDTH), y_ctx, w_out.astype(BF16)


def _odd_mixer(xt, mod_i, gain, w_in, w_out, logit_f, logit_b, rope):
    q, k, v, sg = _odd_in_call(xt, mod_i, gain, w_in.astype(BF16), *rope)
    lg = jnp.stack([jax.nn.log_sigmoid(logit_f.astype(F32)), jax.nn.log_sigmoid(logit_b.astype(F32))])
    return _ret_call(lg, q, k, v, sg), w_out.astype(BF16)


def kernel(x, c, ctx, c_ctx, w_mod, b_mod, norm_gain, ffn_a_in, ffn_a_out, ffn_b_in, ffn_b_out,
           even_in, even_out, na_q_gain, na_k_gain, na_rpb, hy_conv_w, hy_conv_b,
           hy_fw1, hy_fb1, hy_fw2, hy_fb2, hy_fw3, hy_fb3, hy_fw4, hy_freq, hy_bias,
           ret_in, ret_out, ret_logit_f, ret_logit_b):
    batch = x.shape[0]
    assert x.shape == (batch, SEQ, D_MODEL) and ctx.shape == (batch, CTX_LEN, D_MODEL)
    assert batch % 2 == 0 and batch < MOD_ROWS
    c_rows = jnp.concatenate([c, c_ctx[None], jnp.zeros((MOD_ROWS - batch - 1, D_MODEL), F32)], axis=0)
    mod_all = _mod_call(c_rows, w_mod, b_mod).reshape(DEPTH, MOD_ROWS, N_MOD, D_MODEL)
    rope = _rope_tables()
    for i in range(DEPTH):
        last = i == DEPTH - 1
        mod_i = mod_all[i]
        gains = norm_gain[i].reshape(3, 1, D_MODEL)
        if i == 0:
            xt = _ffn_call("split", (x, ctx), mod_i, gains[0], ffn_a_in, ffn_a_out, i, (0, 1, 2), NT_ALL, batch)
        else:
            xt = _ffn_call("stream", (xt,), mod_i, gains[0], ffn_a_in, ffn_a_out, i, (0, 1, 2), NT_ALL, batch)
        if i % 2 == 0:
            e = i // 2
            filter_params = (hy_fw1[e], hy_fb1[e], hy_fw2[e], hy_fb2[e], hy_fw3[e], hy_fb3[e],
                             hy_fw4[e], hy_freq[e])
            source = "even"
            mixed = _even_mixer(xt, mod_i, gains[1], even_in[e], even_out[e], na_q_gain[e], na_k_gain[e],
                                na_rpb[e], hy_conv_w[e], hy_conv_b[e], filter_params, hy_bias[e])
        else:
            o = i // 2
            source = "odd"
            mixed = _odd_mixer(xt, mod_i, gains[1], ret_in[o], ret_out[o], ret_logit_f[o], ret_logit_b[o], rope)
        xt = _ffn_call(source, (xt,) + mixed, mod_i, gains[2], ffn_b_in, ffn_b_out, i, (6, 7, 8),
                       NT_LAT if last else NT_ALL, batch)
    return xt
```

```python
import functools
import math

import ml_dtypes
import numpy as np
import jax
import jax.numpy as jnp
from jax import lax
from jax.experimental import pallas as pl
from jax.experimental.pallas import tpu as pltpu

F32 = jnp.float32
BF16 = jnp.bfloat16

D_MODEL = 1024
SEQ = 4096
DEPTH = 4
GRID_W = 64
CTX_LEN = 256
N_MOD = 9
RMS_EPS = 1e-6
GN_EPS = 1e-6
D_FF = 2816
NA_HEADS = 8
NA_HEAD_DIM = 64
NA_WIDTH = NA_HEADS * NA_HEAD_DIM
WIN_R = 8
WIN_C = 16
HY_WIDTH = D_MODEL - NA_WIDTH
HY_BANDS = 8
HY_TARGET = 1e-2
HY_FAST_PCT = 0.3
HY_SLOW_PCT = 1.5
RET_HEADS = 4
RET_KEY_DIM = D_MODEL // RET_HEADS
RET_VAL_DIM = 2 * RET_KEY_DIM
RET_QK = RET_HEADS * RET_KEY_DIM
RET_V = RET_HEADS * RET_VAL_DIM
ROPE_BASE = 10000.0

LANES = 128
VMEM_LIMIT_BYTES = 56 * 2**20
T_TOK = SEQ + CTX_LEN
TM = CTX_LEN
NT_LAT = SEQ // TM
NT_ALL = T_TOK // TM
MOD_ROWS = 16
FF_CHUNKS = ((0, 1536), (1536, 1280))
RET_BLOCK = 256

NA_RB = 4
NA_KR = NA_RB + WIN_R - 1
GRID_H = SEQ // GRID_W
NA_NQ = NA_RB * GRID_W
NA_NK = NA_KR * GRID_W
NEG_BIG = -1e30

FFT_N = 2 * SEQ
FFT_N1 = 64
FFT_N2 = 128
FFT_N1H = FFT_N1 // 2
FFT_KB = 4
FFT_PITCH = FFT_N2 + 8
FFT_NC = 2 * CTX_LEN


def _cparams(n_axes):
    return pltpu.CompilerParams(dimension_semantics=("arbitrary",) * n_axes,
                                vmem_limit_bytes=VMEM_LIMIT_BYTES)


def _bdot(a, b):
    return jnp.dot(a, b, preferred_element_type=F32)


_NT_DIMS = (((1,), (1,)), ((), ()))
_TN_DIMS = (((0,), (0,)), ((), ()))


def _split_hi_lo(m):
    hi = m.astype(BF16)
    lo = (m - hi.astype(F32)).astype(BF16)
    return hi, lo


def _dot3(m_hi, m_lo, d):
    d_hi, d_lo = _split_hi_lo(d)
    return _bdot(m_hi, d_hi) + _bdot(m_lo, d_hi) + _bdot(m_hi, d_lo)


def _modulated(x, gain, mod_ref, shift_row, scale_row):
    ms = jnp.mean(x * x, axis=-1, keepdims=True)
    y = x * lax.rsqrt(ms + RMS_EPS) * gain
    return (y * (1.0 + mod_ref[0, scale_row:scale_row + 1, :])
            + mod_ref[0, shift_row:shift_row + 1, :])


def _silu(a):
    return a * jax.nn.sigmoid(a)


def _mod_kernel(c_ref, w_ref, b_ref, o_ref):
    s = _silu(c_ref[...])
    o_ref[0] = jnp.dot(s, w_ref[0], precision=lax.Precision.HIGHEST,
                       preferred_element_type=F32) + b_ref[0]


def _mod_call(c_rows, w_mod, b_mod):
    depth, d, n = w_mod.shape
    tn = 1024
    return pl.pallas_call(
        _mod_kernel,
        grid=(depth, n // tn),
        in_specs=[pl.BlockSpec((MOD_ROWS, d), lambda i, j: (0, 0)),
                  pl.BlockSpec((1, d, tn), lambda i, j: (i, 0, j)),
                  pl.BlockSpec((1, 1, tn), lambda i, j: (i, 0, j))],
        out_specs=pl.BlockSpec((1, MOD_ROWS, tn), lambda i, j: (i, 0, j)),
        out_shape=jax.ShapeDtypeStruct((depth, MOD_ROWS, n), F32),
        compiler_params=_cparams(2),
        name="mod",
    )(c_rows, w_mod, b_mod.reshape(depth, 1, n))


def _tok_spec(width):
    return pl.BlockSpec((1, TM, width), lambda b, j: (b, j, 0))


def _mod_spec(batch):
    return pl.BlockSpec((1, N_MOD, D_MODEL), lambda b, j: (jnp.where(j == NT_LAT, batch, b), 0, 0))


def _full_spec(shape):
    zeros = (0,) * len(shape)
    return pl.BlockSpec(shape, lambda b, j: zeros)


_FFN_SOURCES = {"stream": 1, "split": 4, "even": 7, "odd": 3}
TM2 = 2 * TM


def _per_half(fn):
    return jnp.concatenate([fn(half, slice(half * TM, (half + 1) * TM)) for half in range(2)], axis=0)


W_STAGE = 256


def _stage_weight(src, dst, stage, sem, axis):
    n_chunks = src.shape[axis] // W_STAGE

    def window(ref, c):
        chunk = pl.ds(c * W_STAGE, W_STAGE)
        return ref.at[chunk, :] if axis == 0 else ref.at[:, chunk]

    def copy(c):
        return pltpu.make_async_copy(window(src, c), stage.at[c % 2], sem.at[c % 2])

    copy(0).start()
    for c in range(n_chunks):
        if c + 1 < n_chunks:
            copy(c + 1).start()
        copy(c).wait()
        window(dst, c)[...] = stage[c % 2].astype(BF16)


def _ffn_kernel(*refs, rows, source, n_tiles, n_pairs, layer):
    shift_row, scale_row, gate_row = rows
    n_src = _FFN_SOURCES[source]
    src = refs[:n_src]
    (modn0, modn1, modc0, modc1, gain_ref, win_hbm, wout_hbm, o_ref, h_a, x_a, h_b, x_b,
     win_ref, wout_ref, stage_in, stage_out, sem) = refs[n_src:]
    modn = (modn0, modn1)
    modc = (modc0, modc1)
    step = pl.program_id(0)
    first_tile_next = 2 * jnp.minimum(step, n_pairs - 1)

    def is_ctx(half):
        return (first_tile_next + half) % n_tiles == NT_LAT

    def gated(x_ref, o):
        return _per_half(lambda half, r: x_ref[0, r, :] + modn[half][0, 5:6, :] * o[r])

    def next_rows():
        if source == "stream":
            return src[0][0]
        if source == "split":
            lat, ctx = src[0:2], src[2:4]
            return _per_half(lambda half, r: jnp.where(is_ctx(half), ctx[half][0], lat[half][0]))
        if source == "even":
            x_ref, a_ref, yl0, yl1, yc0, yc1, wo_ref = src
            y_lat, y_ctx = (yl0, yl1), (yc0, yc1)

            def hyena_rows(half, r):
                slabs = [y_lat[half][0, s, 0:FFT_N2, :] for s in range(TM // FFT_N2)]
                return jnp.where(is_ctx(half), y_ctx[half][0], jnp.concatenate(slabs, axis=0)).astype(BF16)

            o = _bdot(a_ref[0], wo_ref[0:NA_WIDTH, :]) + _bdot(_per_half(hyena_rows), wo_ref[NA_WIDTH:, :])
            return gated(x_ref, o)
        x_ref, y_ref, wo_ref = src
        return gated(x_ref, _bdot(y_ref[0], wo_ref[...]))

    def prepare(h_write, x_write):
        x_next = next_rows()
        h_write[...] = _per_half(lambda half, r: _modulated(
            x_next[r], gain_ref[...], modn[half], shift_row, scale_row)).astype(BF16)
        x_write[...] = x_next

    def body(h_read, x_read, h_write, x_write):
        prepare(h_write, x_write)
        h = h_read[...]
        acc = jnp.zeros((TM2, D_MODEL), F32)
        for start, size in FF_CHUNKS:
            a = _bdot(h, win_ref[:, start:start + size])
            b = _bdot(h, win_ref[:, D_FF + start:D_FF + start + size])
            g = (_silu(a) * b).astype(BF16)
            acc = acc + _bdot(g, wout_ref[start:start + size, :])
        o_ref[0] = _per_half(lambda half, r: x_read[r, :] + (0.5 * modc[half][0, gate_row:gate_row + 1, :]) * acc[r])

    @pl.when(step == 0)
    def _():
        _stage_weight(win_hbm.at[layer], win_ref, stage_in, sem, axis=1)
        _stage_weight(wout_hbm.at[layer], wout_ref, stage_out, sem, axis=0)
        prepare(h_a, x_a)

    @pl.when(jnp.logical_and(step > 0, step % 2 == 0))
    def _():
        body(h_b, x_b, h_a, x_a)

    @pl.when(step % 2 == 1)
    def _():
        body(h_a, x_a, h_b, x_b)


def _ffn_call(source, srcs, mod_i, gain, w_in, w_out, layer, rows, n_tiles, batch):
    n_all = batch * n_tiles
    assert n_all % 2 == 0 and n_tiles in (NT_LAT, NT_ALL)
    n_pairs = n_all // 2

    def next_pair(s):
        return jnp.minimum(s, n_pairs - 1)

    def cur_pair(s):
        return jnp.maximum(s - 1, 0)

    def tile_of(pair_of, half):
        def fn(s):
            t = 2 * pair_of(s) + half
            return t // n_tiles, t % n_tiles
        return fn

    def mod_spec(pair_of, half):
        def index(s):
            b, j = tile_of(pair_of, half)(s)
            return jnp.where(j == NT_LAT, batch, b), 0, 0
        return pl.BlockSpec((1, N_MOD, D_MODEL), index)

    def tok(arr):
        width = arr.shape[-1]
        if n_tiles == NT_ALL:
            return arr.reshape(1, batch * T_TOK, width), pl.BlockSpec(
                (1, TM2, width), lambda s: (0, next_pair(s), 0))
        per_sample = n_tiles // 2
        return arr, pl.BlockSpec((1, TM2, width),
                                 lambda s: (next_pair(s) // per_sample, next_pair(s) % per_sample, 0))

    def lat_tile(block, n_trailing, half):
        def index(s):
            b, j = tile_of(next_pair, half)(s)
            return (b, jnp.minimum(j, NT_LAT - 1)) + (0,) * n_trailing
        return pl.BlockSpec(block, index)

    def ctx_tile(block, half):
        return pl.BlockSpec(block, lambda s: (tile_of(next_pair, half)(s)[0],) + (0,) * (len(block) - 1))

    def full(shape):
        zeros = (0,) * len(shape)
        return pl.BlockSpec(shape, lambda s: zeros)

    halves = (0, 1)
    if source == "stream":
        (xt,) = srcs
        xt, x_spec = tok(xt)
        args, src_specs = (xt,), [x_spec]
    elif source == "split":
        x, ctx = srcs
        args = (x, x, ctx, ctx)
        src_specs = ([lat_tile((1, TM, D_MODEL), 1, h) for h in halves]
                     + [ctx_tile((1, CTX_LEN, D_MODEL), h) for h in halves])
    elif source == "even":
        xt, a, y_lat, y_ctx, wo = srcs
        (xt, x_spec), (a, a_spec) = tok(xt), tok(a)
        args = (xt, a, y_lat, y_lat, y_ctx, y_ctx, wo)
        src_specs = ([x_spec, a_spec]
                     + [lat_tile((1, TM // FFT_N2, FFT_PITCH, HY_WIDTH), 2, h) for h in halves]
                     + [ctx_tile((1, CTX_LEN, HY_WIDTH), h) for h in halves] + [full((D_MODEL, D_MODEL))])
    else:
        xt, y, wo = srcs
        (xt, x_spec), (y, y_spec) = tok(xt), tok(y)
        args = (xt, y, wo)
        src_specs = [x_spec, y_spec, full((RET_V, D_MODEL))]
    out = pl.pallas_call(
        functools.partial(_ffn_kernel, rows=rows, source=source, n_tiles=n_tiles, n_pairs=n_pairs,
                          layer=layer),
        grid=(n_pairs + 1,),
        in_specs=src_specs + [mod_spec(next_pair, 0), mod_spec(next_pair, 1),
                              mod_spec(cur_pair, 0), mod_spec(cur_pair, 1), full((1, D_MODEL)),
                              pl.BlockSpec(memory_space=pl.ANY), pl.BlockSpec(memory_space=pl.ANY)],
        out_specs=pl.BlockSpec((1, TM2, D_MODEL), lambda s: (0, cur_pair(s), 0)),
        out_shape=jax.ShapeDtypeStruct((1, n_all * TM, D_MODEL), F32),
        scratch_shapes=[pltpu.VMEM((TM2, D_MODEL), BF16), pltpu.VMEM((TM2, D_MODEL), F32),
                        pltpu.VMEM((TM2, D_MODEL), BF16), pltpu.VMEM((TM2, D_MODEL), F32),
                        pltpu.VMEM((D_MODEL, 2 * D_FF), BF16), pltpu.VMEM((D_FF, D_MODEL), BF16),
                        pltpu.VMEM((2, D_MODEL, W_STAGE), F32), pltpu.VMEM((2, W_STAGE, D_MODEL), F32),
                        pltpu.SemaphoreType.DMA((2,))],
        compiler_params=_cparams(1),
        name="ffn_" + source,
    )(*args, mod_i, mod_i, mod_i, mod_i, gain, w_in, w_out)
    return out.reshape(batch, n_tiles * TM, D_MODEL)


def _stage_in_proj_weight(w_hbm, layer, w_ref, stage, sem):
    @pl.when(jnp.logical_and(pl.program_id(0) == 0, pl.program_id(1) == 0))
    def _():
        _stage_weight(w_hbm.at[layer], w_ref, stage, sem, axis=1)


def _in_proj_weight_scratch(n_in):
    return [pltpu.VMEM((D_MODEL, n_in), BF16), pltpu.VMEM((2, D_MODEL, W_STAGE), F32),
            pltpu.SemaphoreType.DMA((2,))]


def _even_in_kernel(x_ref, mod_ref, gain_ref, w_hbm, qg_ref, kg_ref, bd_ref,
                    q_ref, k_ref, v_ref, u_ref, w_ref, stage, sem, *, layer):
    _stage_in_proj_weight(w_hbm, layer, w_ref, stage, sem)
    h = _modulated(x_ref[0], gain_ref[...], mod_ref, 3, 4).astype(BF16)

    def head_norm(t, g):
        ms = _bdot((t * t).astype(BF16), bd_ref[...])
        return (t * lax.rsqrt(ms + RMS_EPS) * g).astype(BF16)

    q_ref[0] = head_norm(_bdot(h, w_ref[:, 0:NA_WIDTH]), qg_ref[...])
    k_ref[0] = head_norm(_bdot(h, w_ref[:, NA_WIDTH:2 * NA_WIDTH]), kg_ref[...])
    v_ref[0] = _bdot(h, w_ref[:, 2 * NA_WIDTH:3 * NA_WIDTH]).astype(BF16)
    u_ref[0] = _bdot(h, w_ref[:, 3 * NA_WIDTH:])


def _even_in_call(xt, mod_i, gain, w, layer, q_gain, k_gain, bd):
    batch = xt.shape[0]
    n_in = w.shape[2]
    return pl.pallas_call(
        functools.partial(_even_in_kernel, layer=layer),
        grid=(batch, NT_ALL),
        in_specs=[_tok_spec(D_MODEL), _mod_spec(batch), _full_spec((1, D_MODEL)),
                  pl.BlockSpec(memory_space=pl.ANY), _full_spec((1, NA_WIDTH)), _full_spec((1, NA_WIDTH)),
                  _full_spec((NA_WIDTH, NA_WIDTH))],
        scratch_shapes=_in_proj_weight_scratch(n_in),
        out_specs=[_tok_spec(NA_WIDTH), _tok_spec(NA_WIDTH), _tok_spec(NA_WIDTH),
                   _tok_spec(3 * HY_WIDTH)],
        out_shape=[jax.ShapeDtypeStruct((batch, T_TOK, NA_WIDTH), BF16)] * 3
        + [jax.ShapeDtypeStruct((batch, T_TOK, 3 * HY_WIDTH), F32)],
        compiler_params=_cparams(2),
        name="even_in",
    )(xt, mod_i, gain, w, q_gain, k_gain, bd)


def _na_kernel(q_ref, k_ref, v_ref, bias_ref, o_ref):
    lane = lax.broadcasted_iota(jnp.int32, (1, LANES), 1)
    first_head = lane < NA_HEAD_DIM
    k_ctx = k_ref[0, SEQ:T_TOK, :]
    v_ctx = v_ref[0, SEQ:T_TOK, :]

    def stack_heads(q):
        zero = jnp.zeros_like(q)
        return jnp.concatenate([jnp.where(first_head, q, zero), jnp.where(first_head, zero, q)], axis=0)

    def attend(qs, scores_and_values):
        s_list = [lax.dot_general(qs, kk, _NT_DIMS, preferred_element_type=F32) if bias is None
                  else lax.dot_general(qs, kk, _NT_DIMS, preferred_element_type=F32) + bias
                  for kk, _, bias in scores_and_values]
        m = functools.reduce(jnp.maximum, [jnp.max(s, axis=-1, keepdims=True) for s in s_list])
        o = functools.reduce(jnp.add, [
            _bdot(jnp.exp(s - m).astype(BF16),
                  jnp.concatenate([vv, jnp.ones((vv.shape[0], LANES), BF16)], axis=1))
            for s, (_, vv, _) in zip(s_list, scores_and_values)])
        o = o[:, :LANES] / o[:, LANES:]
        n = qs.shape[0] // 2
        return jnp.where(first_head, o[:n], o[n:]).astype(BF16)

    def block(i, carry):
        r0 = i * NA_RB
        u0 = jnp.clip(r0 - WIN_R // 2, 0, GRID_H - NA_KR)
        pattern = jnp.where(i == 0, 0, jnp.where(i == GRID_H // NA_RB - 1, 2, 1))
        q0 = pl.multiple_of(r0 * GRID_W, NA_NQ)
        k0 = pl.multiple_of(u0 * GRID_W, GRID_W)
        qs = stack_heads(q_ref[0, pl.ds(q0, NA_NQ), :])
        k_win = k_ref[0, pl.ds(k0, NA_NK), :]
        v_win = v_ref[0, pl.ds(k0, NA_NK), :]
        o_ref[0, pl.ds(q0, NA_NQ), :] = attend(
            qs, [(k_win, v_win, bias_ref[pattern, 0]), (k_ctx, v_ctx, None)])
        return carry

    lax.fori_loop(0, GRID_H // NA_RB, block, 0, unroll=4)
    o_ref[0, SEQ:T_TOK, :] = attend(stack_heads(q_ref[0, SEQ:T_TOK, :]), [(k_ctx, v_ctx, None)])


def _na_call(q, k, v, bias):
    batch = q.shape[0]
    n_pairs = NA_WIDTH // LANES
    spec = pl.BlockSpec((1, T_TOK, LANES), lambda b, p: (b, 0, p))
    return pl.pallas_call(
        _na_kernel,
        grid=(batch, n_pairs),
        in_specs=[spec, spec, spec,
                  pl.BlockSpec((3, 1, 2 * NA_NQ, NA_NK), lambda b, p: (0, p, 0, 0))],
        out_specs=spec,
        out_shape=jax.ShapeDtypeStruct((batch, T_TOK, NA_WIDTH), BF16),
        compiler_params=_cparams(2),
        name="na_attn",
    )(q, k, v, bias)


def _na_bias_table(rpb):
    j = np.arange(NA_RB)[:, None, None, None]
    c = np.arange(GRID_W)[None, :, None, None]
    kk = np.arange(NA_KR)[None, None, :, None]
    kc = np.arange(GRID_W)[None, None, None, :]
    cs = np.clip(c - WIN_C // 2, 0, GRID_W - WIN_C)
    col_ok = (kc >= cs) & (kc < cs + WIN_C)
    dc = np.clip(kc - c + WIN_C - 1, 0, 2 * WIN_C - 2)[0, :, 0, :]
    col_sel = (dc[..., None] == np.arange(2 * WIN_C - 1)).astype(np.float32)
    shape = (NA_RB, GRID_W, NA_KR, GRID_W)
    oks, row_sels = [], []
    for off, rs_rel in ((0, 0 * j), (-(WIN_R // 2), j), (-(NA_KR - NA_RB), NA_KR - WIN_R + 0 * j)):
        row_ok = (kk >= rs_rel) & (kk < rs_rel + WIN_R)
        dr = np.clip(off + kk - j + WIN_R - 1, 0, 2 * WIN_R - 2)[:, 0, :, 0]
        row_sels.append((dr[..., None] == np.arange(2 * WIN_R - 1)).astype(np.float32))
        oks.append(np.broadcast_to(row_ok & col_ok, shape))
    vals = jnp.einsum("pjkr,hrc,qmc->phjqkm", np.stack(row_sels), rpb.astype(F32), col_sel,
                      precision=lax.Precision.HIGHEST)
    table = jnp.where(np.stack(oks)[:, None], vals, NEG_BIG)
    return table.reshape(3, NA_HEADS // 2, 2 * NA_NQ, NA_NK)


def _hy_pre_kernel(u0_ref, u1_ref, uv_ref, w0_ref, w1_ref, wv_ref, b0_ref, b1_ref, bv_ref,
                   zl_ref, xl_ref, zc_ref, xc_ref):
    row = lax.broadcasted_iota(jnp.int32, (TM, LANES), 0)
    zero_row = jnp.zeros((1, LANES), F32)

    def conv(u_ref, w_ref, b_ref, tile):
        s = tile * TM
        cur = u_ref[0, s:s + TM, :]
        seq_start = tile in (0, NT_LAT)
        seq_end = tile in (NT_LAT - 1, NT_LAT)
        prev_row = zero_row if seq_start else u_ref[0, s - 1:s, :]
        next_row = zero_row if seq_end else u_ref[0, s + TM:s + TM + 1, :]
        before = jnp.where(row == 0, prev_row, pltpu.roll(cur, 1, 0))
        after = jnp.where(row == TM - 1, next_row, pltpu.roll(cur, TM - 1, 0))
        return before * w_ref[0:1, :] + cur * w_ref[1:2, :] + after * w_ref[2:3, :] + b_ref[...]

    for tile in range(NT_ALL):
        x0 = conv(u0_ref, w0_ref, b0_ref, tile)
        z = conv(uv_ref, wv_ref, bv_ref, tile) * conv(u1_ref, w1_ref, b1_ref, tile)
        if tile < NT_LAT:
            for half in range(TM // FFT_N2):
                base = (tile * (TM // FFT_N2) + half) * FFT_PITCH
                zl_ref[0, base:base + FFT_N2, :] = z[half * FFT_N2:(half + 1) * FFT_N2]
                xl_ref[0, base:base + FFT_N2, :] = x0[half * FFT_N2:(half + 1) * FFT_N2]
                pad = jnp.zeros((FFT_PITCH - FFT_N2, LANES), F32)
                zl_ref[0, base + FFT_N2:base + FFT_PITCH, :] = pad
                xl_ref[0, base + FFT_N2:base + FFT_PITCH, :] = pad
        else:
            zc_ref[0] = z
            xc_ref[0] = x0


def _hy_pre_call(u, conv_w, conv_b):
    batch = u.shape[0]
    nb = HY_WIDTH // LANES
    u_specs = [pl.BlockSpec((1, T_TOK, LANES), lambda b, cb, g=g: (b, 0, g * nb + cb)) for g in range(3)]
    w_specs = [pl.BlockSpec((3, LANES), lambda b, cb, g=g: (0, g * nb + cb)) for g in range(3)]
    b_specs = [pl.BlockSpec((1, LANES), lambda b, cb, g=g: (0, g * nb + cb)) for g in range(3)]
    lat = pl.BlockSpec((1, FFT_N1H * FFT_PITCH, LANES), lambda b, cb: (b, 0, cb))
    ctx = pl.BlockSpec((1, CTX_LEN, LANES), lambda b, cb: (b, 0, cb))
    return pl.pallas_call(
        _hy_pre_kernel,
        grid=(batch, nb),
        in_specs=u_specs + w_specs + b_specs,
        out_specs=[lat, lat, ctx, ctx],
        out_shape=[jax.ShapeDtypeStruct((batch, FFT_N1H * FFT_PITCH, HY_WIDTH), F32)] * 2
        + [jax.ShapeDtypeStruct((batch, CTX_LEN, HY_WIDTH), F32)] * 2,
        compiler_params=_cparams(2),
        name="hy_pre",
    )(u, u, u, conv_w, conv_w, conv_w, conv_b, conv_b, conv_b)


def _complex_block(re, im):
    return np.block([[re, -im], [im, re]])


@functools.lru_cache(maxsize=None)
def _fft_constants():
    k1 = np.arange(FFT_N1)
    n1 = np.arange(FFT_N1H)
    f1 = np.exp(-2j * np.pi * np.outer(k1, n1) / FFT_N1)
    m1 = _complex_block(f1.real, f1.imag)
    f1_full = np.exp(-2j * np.pi * np.outer(k1, k1) / FFT_N1)
    m1r = np.concatenate([f1_full.real, f1_full.imag], axis=0)
    n2 = np.arange(FFT_N2)
    tw = np.exp(-2j * np.pi * np.outer(k1, n2) / FFT_N)
    tw = np.stack([tw.real, tw.imag])[..., None] * np.ones((1, 1, 1, LANES))
    f2 = np.exp(-2j * np.pi * np.outer(n2, n2) / FFT_N2)
    m2 = _complex_block(f2.real, f2.imag)
    m2i = _complex_block(f2.real, -f2.imag)
    c1 = np.exp(2j * np.pi * np.outer(n1, k1) / FFT_N1) / FFT_N
    m3 = _complex_block(c1.real, c1.imag)
    kc = np.arange(FFT_NC)
    nc = np.arange(CTX_LEN)
    ang = 2 * np.pi * np.outer(kc, nc) / FFT_NC
    mcf = np.concatenate([np.cos(ang), -np.sin(ang)], axis=0)
    ang_full = 2 * np.pi * np.outer(kc, kc) / FFT_NC
    mcf_full = np.concatenate([np.cos(ang_full), -np.sin(ang_full)], axis=0)
    mci = np.concatenate([np.cos(ang.T), -np.sin(ang.T)], axis=1) / FFT_NC

    def hi_lo(m):
        hi = m.astype(ml_dtypes.bfloat16)
        lo = (m - hi.astype(np.float64)).astype(ml_dtypes.bfloat16)
        return hi, lo

    return dict(m1=hi_lo(m1), m1r=hi_lo(m1r), m2=hi_lo(m2), m2i=hi_lo(m2i), m3=hi_lo(m3),
                mcf=hi_lo(mcf), mcf_full=hi_lo(mcf_full), mci=hi_lo(mci), tw=tw.astype(np.float32))


def _slab_rows(n2, n_slabs):
    return pl.ds(n2, n_slabs, stride=FFT_PITCH)


def _fft_s1_kernel(z_ref, *rest):
    *m_refs, a_ref = rest

    def step(i, carry):
        cols = []
        for dn in range(2):
            rows = _slab_rows(2 * i + dn, FFT_N1H)
            cols.append(jnp.concatenate([z_ref[0, 0, rows, :], z_ref[0, 1, rows, :]], axis=0))
        d = jnp.concatenate(cols, axis=1)
        if len(m_refs) == 2:
            r = _dot3(m_refs[0][...], m_refs[1][...], d)
        else:
            r = _bdot(m_refs[0][...], d.astype(BF16))
        for dn in range(2):
            rows = _slab_rows(2 * i + dn, FFT_N1)
            a_ref[0, 0, rows, :] = r[:FFT_N1, dn * LANES:(dn + 1) * LANES]
            a_ref[0, 1, rows, :] = r[FFT_N1:, dn * LANES:(dn + 1) * LANES]
        return carry

    lax.fori_loop(0, FFT_N2 // 2, step, 0, unroll=4)
    for pad_row in range(FFT_N2, FFT_PITCH):
        for part in range(2):
            a_ref[0, part, _slab_rows(pad_row, FFT_N1), :] = jnp.zeros((FFT_N1, LANES), F32)


def _fft_s2_kernel(a_ref, tw_ref, g_ref, fwd_ref, inv_ref, o_ref):
    reps = HY_WIDTH // LANES
    pad = jnp.zeros((FFT_PITCH - FFT_N2, HY_WIDTH), F32)
    for kk in range(FFT_KB):
        twr = jnp.concatenate([tw_ref[0, kk]] * reps, axis=1)
        twi = jnp.concatenate([tw_ref[1, kk]] * reps, axis=1)
        ar = a_ref[0, 0, kk, 0:FFT_N2, :]
        ai = a_ref[0, 1, kk, 0:FFT_N2, :]
        d = jnp.concatenate([ar * twr - ai * twi, ar * twi + ai * twr], axis=0)
        x = _bdot(fwd_ref[...], d.astype(BF16))
        xr, xi = x[:FFT_N2], x[FFT_N2:]
        gr, gi = g_ref[0, kk], g_ref[1, kk]
        y = jnp.concatenate([xr * gr - xi * gi, xr * gi + xi * gr], axis=0)
        b = _bdot(inv_ref[...], y.astype(BF16))
        br, bi = b[:FFT_N2], b[FFT_N2:]
        o_ref[0, 0, kk, 0:FFT_N2, :] = br * twr + bi * twi
        o_ref[0, 1, kk, 0:FFT_N2, :] = bi * twr - br * twi
        o_ref[0, 0, kk, FFT_N2:FFT_PITCH, :] = pad
        o_ref[0, 1, kk, FFT_N2:FFT_PITCH, :] = pad


def _fft_s3_kernel(b_ref, m_ref, z_ref, x0_ref, bias_ref, y_ref):
    def step(i, carry):
        cols = []
        for dn in range(2):
            rows = _slab_rows(2 * i + dn, FFT_N1)
            cols.append(jnp.concatenate([b_ref[0, 0, rows, :], b_ref[0, 1, rows, :]], axis=0))
        y = _bdot(m_ref[...], jnp.concatenate(cols, axis=1).astype(BF16))
        for dn in range(2):
            rows = _slab_rows(2 * i + dn, FFT_N1H)
            for s in range(2):
                conv = y[s * FFT_N1H:(s + 1) * FFT_N1H, dn * LANES:(dn + 1) * LANES]
                y_ref[0, s, rows, :] = (conv + z_ref[0, s, rows, :] * bias_ref[...]) * x0_ref[0, s, rows, :]
        return carry

    lax.fori_loop(0, FFT_N2 // 2, step, 0, unroll=4)
    for pad_row in range(FFT_N2, FFT_PITCH):
        for s in range(2):
            y_ref[0, s, _slab_rows(pad_row, FFT_N1H), :] = jnp.zeros((FFT_N1H, LANES), F32)


def _hy_long_conv_call(z_lat, x0_lat, g_spec, d_bias):
    batch = z_lat.shape[0]
    pairs = batch // 2
    cst = _fft_constants()
    n_cb = HY_WIDTH // LANES
    rows_z = FFT_N1H * FFT_PITCH
    rows_a = FFT_N1 * FFT_PITCH
    zv = z_lat.reshape(pairs, 2, rows_z, HY_WIDTH)
    xv = x0_lat.reshape(pairs, 2, rows_z, HY_WIDTH)
    z_spec = pl.BlockSpec((1, 2, rows_z, LANES), lambda p, cb: (p, 0, 0, cb))
    a_spec = pl.BlockSpec((1, 2, rows_a, LANES), lambda p, cb: (p, 0, 0, cb))
    a = pl.pallas_call(
        _fft_s1_kernel,
        grid=(pairs, n_cb),
        in_specs=[z_spec, _full_spec((2 * FFT_N1, 2 * FFT_N1H))],
        out_specs=a_spec,
        out_shape=jax.ShapeDtypeStruct((pairs, 2, rows_a, HY_WIDTH), F32),
        compiler_params=_cparams(2),
        name="hy_fft_s1",
    )(zv, cst["m1"][0])
    a = a.reshape(pairs, 2, FFT_N1, FFT_PITCH, HY_WIDTH)
    blk = pl.BlockSpec((1, 2, FFT_KB, FFT_PITCH, HY_WIDTH), lambda kb, p: (p, 0, kb, 0, 0))
    sq = (2 * FFT_N2, 2 * FFT_N2)
    b = pl.pallas_call(
        _fft_s2_kernel,
        grid=(FFT_N1 // FFT_KB, pairs),
        in_specs=[blk,
                  pl.BlockSpec((2, FFT_KB, FFT_N2, LANES), lambda kb, p: (0, kb, 0, 0)),
                  pl.BlockSpec((2, FFT_KB, FFT_N2, HY_WIDTH), lambda kb, p: (0, kb, 0, 0)),
                  _full_spec(sq), _full_spec(sq)],
        out_specs=blk,
        out_shape=jax.ShapeDtypeStruct((pairs, 2, FFT_N1, FFT_PITCH, HY_WIDTH), F32),
        compiler_params=_cparams(2),
        name="hy_fft_s2",
    )(a, cst["tw"], g_spec, cst["m2"][0], cst["m2i"][0])
    b = b.reshape(pairs, 2, rows_a, HY_WIDTH)
    y = pl.pallas_call(
        _fft_s3_kernel,
        grid=(pairs, n_cb),
        in_specs=[a_spec, _full_spec((2 * FFT_N1H, 2 * FFT_N1)),
                  z_spec, z_spec, pl.BlockSpec((1, LANES), lambda p, cb: (0, cb))],
        out_specs=z_spec,
        out_shape=jax.ShapeDtypeStruct((pairs, 2, rows_z, HY_WIDTH), F32),
        compiler_params=_cparams(2),
        name="hy_fft_s3",
    )(b, cst["m3"][0], zv, xv, d_bias.reshape(1, HY_WIDTH))
    return y.reshape(batch, rows_z, HY_WIDTH)


def _hy_ctx_kernel(z_ref, x0_ref, g_ref, fwd_ref, inv_ref, bias_ref, y_ref):
    z = z_ref[0]
    x = _bdot(fwd_ref[...], z.astype(BF16))
    xr, xi = x[:FFT_NC], x[FFT_NC:]
    gr, gi = g_ref[0], g_ref[1]
    y = jnp.concatenate([xr * gr - xi * gi, xr * gi + xi * gr], axis=0)
    conv = _bdot(inv_ref[...], y.astype(BF16))
    y_ref[0] = ((conv + z * bias_ref[...]) * x0_ref[0]).astype(BF16)


def _hy_ctx_conv_call(z_ctx, x0_ctx, g_spec, d_bias):
    batch = z_ctx.shape[0]
    cst = _fft_constants()
    tok = pl.BlockSpec((1, CTX_LEN, HY_WIDTH), lambda b: (b, 0, 0))

    def full(shape):
        zeros = (0,) * len(shape)
        return pl.BlockSpec(shape, lambda b: zeros)

    return pl.pallas_call(
        _hy_ctx_kernel,
        grid=(batch,),
        in_specs=[tok, tok, full((2, FFT_NC, HY_WIDTH)),
                  full((2 * FFT_NC, CTX_LEN)), full((CTX_LEN, 2 * FFT_NC)), full((1, HY_WIDTH))],
        out_specs=tok,
        out_shape=jax.ShapeDtypeStruct((batch, CTX_LEN, HY_WIDTH), BF16),
        compiler_params=_cparams(1),
        name="hy_ctx",
    )(z_ctx, x0_ctx, g_spec, cst["mcf"][0], cst["mci"][0], d_bias.reshape(1, HY_WIDTH))


def _hyena_filter_halves(length, fw1, fb1, fw2, fb2, fw3, fb3, fw4, freq):
    t = jnp.linspace(0.0, 1.0, length, dtype=F32)[:, None]
    w = 2.0 * math.pi * jnp.arange(length, dtype=F32)[:, None] / length
    bands = jnp.linspace(1e-4, HY_BANDS - 1, HY_BANDS, dtype=F32)
    emb = jnp.concatenate([t, jnp.cos(bands * w), -jnp.sin(bands * w)], axis=-1)

    def second_half_order(a):
        return jnp.concatenate([a[:1], a[:0:-1]], axis=0)

    hp = lax.Precision.HIGHEST
    h = jnp.concatenate([emb, second_half_order(emb)], axis=0)
    h = jnp.sin(freq * (jnp.dot(h, fw1, precision=hp) + fb1))
    h = jnp.sin(freq * (jnp.dot(h, fw2, precision=hp) + fb2))
    h = jnp.sin(freq * (jnp.dot(h, fw3, precision=hp) + fb3))
    max_decay = math.log(HY_TARGET) / HY_FAST_PCT
    min_decay = math.log(HY_TARGET) / HY_SLOW_PCT
    deltas = jnp.abs(jnp.linspace(min_decay, max_decay, HY_WIDTH, dtype=F32))
    causal = jnp.dot(h[:length], fw4[:, :HY_WIDTH], precision=hp).astype(F32) * jnp.exp(-t * deltas)
    anti = (jnp.dot(h[length:], fw4[:, HY_WIDTH:], precision=hp).astype(F32)
            * jnp.exp(-second_half_order(t) * deltas))
    first_row = jnp.arange(length)[:, None] == 0
    return causal + jnp.where(first_row, anti[:1], 0.0), jnp.where(first_row, 0.0, anti)


def _odd_in_kernel(x_ref, mod_ref, gain_ref, w_hbm, cos_ref, sin_ref, q_ref, k_ref, v_ref, sg_ref,
                   w_ref, stage, sem, *, layer):
    _stage_in_proj_weight(w_hbm, layer, w_ref, stage, sem)
    h = _modulated(x_ref[0], gain_ref[...], mod_ref, 3, 4).astype(BF16)
    cos = cos_ref[...]
    sin = sin_ref[...]

    def rope_store(dst_ref, col0, scale):
        p = _bdot(h, w_ref[:, col0:col0 + RET_QK])
        for c in range(RET_QK // LANES):
            t = p[:, c * LANES:(c + 1) * LANES]
            half = (c % 2) * LANES
            r = t * cos[:, half:half + LANES] + pltpu.roll(t, LANES // 2, 1) * sin[:, half:half + LANES]
            dst_ref[0, :, c * LANES:(c + 1) * LANES] = (r * scale).astype(BF16)

    rope_store(q_ref, 0, 1.0)
    rope_store(k_ref, RET_QK, RET_KEY_DIM ** -0.5)
    v_ref[0] = _bdot(h, w_ref[:, 2 * RET_QK:2 * RET_QK + RET_V]).astype(BF16)
    sg_ref[0] = _silu(_bdot(h, w_ref[:, 2 * RET_QK + RET_V:])).astype(BF16)


def _odd_in_call(xt, mod_i, gain, w, layer, cos_t, sin_t):
    batch = xt.shape[0]
    rope_spec = pl.BlockSpec((TM, RET_KEY_DIM), lambda b, j: (j, 0))
    return pl.pallas_call(
        functools.partial(_odd_in_kernel, layer=layer),
        grid=(batch, NT_ALL),
        in_specs=[_tok_spec(D_MODEL), _mod_spec(batch), _full_spec((1, D_MODEL)),
                  pl.BlockSpec(memory_space=pl.ANY), rope_spec, rope_spec],
        scratch_shapes=_in_proj_weight_scratch(w.shape[2]),
        out_specs=[_tok_spec(RET_QK), _tok_spec(RET_QK), _tok_spec(RET_V), _tok_spec(RET_V)],
        out_shape=[jax.ShapeDtypeStruct((batch, T_TOK, RET_QK), BF16)] * 2
        + [jax.ShapeDtypeStruct((batch, T_TOK, RET_V), BF16)] * 2,
        compiler_params=_cparams(2),
        name="odd_in",
    )(xt, mod_i, gain, w, cos_t, sin_t)


def _rope_tables():
    t = np.arange(SEQ)
    n_freq = RET_KEY_DIM // 4
    inv = ROPE_BASE ** (-jnp.arange(n_freq, dtype=F32) / n_freq)
    ang_r = jnp.asarray(t // GRID_W, F32)[:, None] * inv
    ang_c = jnp.asarray(t % GRID_W, F32)[:, None] * inv
    cr, sr, cc, sc = jnp.cos(ang_r), jnp.sin(ang_r), jnp.cos(ang_c), jnp.sin(ang_c)
    cos_l = jnp.concatenate([cr, cr, cc, cc], axis=-1)
    sin_l = jnp.concatenate([-sr, sr, -sc, sc], axis=-1)
    cos_t = jnp.concatenate([cos_l, jnp.ones((CTX_LEN, RET_KEY_DIM), F32)], axis=0)
    sin_t = jnp.concatenate([sin_l, jnp.zeros((CTX_LEN, RET_KEY_DIM), F32)], axis=0)
    return cos_t, sin_t


def _ret_kernel(lg_ref, q_ref, k_ref, v_ref, sg_ref, y_ref, o_acc, state_f, state_b):
    head = pl.program_id(1)
    n_c = RET_BLOCK
    n_lat = SEQ // n_c
    assert CTX_LEN == n_c and n_lat % 2 == 0
    ii = lax.broadcasted_iota(jnp.int32, (n_c, n_c), 0).astype(F32)
    jj = lax.broadcasted_iota(jnp.int32, (n_c, n_c), 1).astype(F32)
    row_k = lax.broadcasted_iota(jnp.int32, (n_c, RET_KEY_DIM), 0).astype(F32)
    row_v = lax.broadcasted_iota(jnp.int32, (n_c, RET_VAL_DIM), 0).astype(F32)

    def decays(backward):
        lg = lg_ref[1 if backward else 0, head]
        if backward:
            diff = jj - ii
            xi = jnp.exp(lg * (n_c - row_v))
            zeta = jnp.exp(lg * row_k)
        else:
            diff = ii - jj
            xi = jnp.exp(lg * (row_v + 1.0))
            zeta = jnp.exp(lg * (n_c - 1.0 - row_k))
        dmask = jnp.where(diff >= 0, jnp.exp(lg * jnp.maximum(diff, 0.0)), 0.0)
        g_chunk = jnp.exp(lg * n_c + jnp.zeros((1, RET_VAL_DIM), F32))
        return dmask, xi, zeta, g_chunk

    def advance(chunk, state, consts, finalize):
        dmask, xi, zeta, g_chunk = consts
        r = chunk * n_c if isinstance(chunk, int) else pl.multiple_of(chunk * n_c, n_c)
        qc = q_ref[0, pl.ds(r, n_c), :]
        kc = k_ref[0, pl.ds(r, n_c), :]
        vc = v_ref[0, pl.ds(r, n_c), :]
        st = state[...]
        inner = lax.dot_general(qc, kc, _NT_DIMS, preferred_element_type=F32) * dmask
        o = _bdot(inner.astype(BF16), vc) + _bdot(qc, st.astype(BF16)) * xi
        kz = (kc.astype(F32) * zeta).astype(BF16)
        state[...] = st * g_chunk + lax.dot_general(kz, vc, _TN_DIMS, preferred_element_type=F32)
        if finalize:
            tot = o_acc[pl.ds(r, n_c), :] + o
            mu = jnp.mean(tot, axis=-1, keepdims=True)
            cen = tot - mu
            var = jnp.mean(cen * cen, axis=-1, keepdims=True)
            yn = cen * lax.rsqrt(var + GN_EPS)
            y_ref[0, pl.ds(r, n_c), :] = (sg_ref[0, pl.ds(r, n_c), :].astype(F32) * yn).astype(BF16)
        else:
            o_acc[pl.ds(r, n_c), :] = o

    consts_f = decays(False)
    consts_b = decays(True)
    state_f[...] = jnp.zeros_like(state_f)
    state_b[...] = jnp.zeros_like(state_b)

    def both(chunk_f, chunk_b, finalize):
        advance(chunk_f, state_f, consts_f, finalize)
        advance(chunk_b, state_b, consts_b, finalize)

    advance(n_lat, state_f, consts_f, False)
    advance(n_lat, state_b, consts_b, True)

    def first_half(s, carry):
        both(s, n_lat - 1 - s, False)
        return carry

    def second_half(s, carry):
        both(s, n_lat - 1 - s, True)
        return carry

    lax.fori_loop(0, n_lat // 2, first_half, 0, unroll=2)
    lax.fori_loop(n_lat // 2, n_lat, second_half, 0, unroll=2)


def _ret_call(lg, q, k, v, sg):
    batch = q.shape[0]
    qk_spec = pl.BlockSpec((1, T_TOK, RET_KEY_DIM), lambda b, h: (b, 0, h))
    v_spec = pl.BlockSpec((1, T_TOK, RET_VAL_DIM), lambda b, h: (b, 0, h))
    return pl.pallas_call(
        _ret_kernel,
        grid=(batch, RET_HEADS),
        in_specs=[pl.BlockSpec(memory_space=pltpu.SMEM), qk_spec, qk_spec, v_spec, v_spec],
        out_specs=v_spec,
        out_shape=jax.ShapeDtypeStruct((batch, T_TOK, RET_V), BF16),
        scratch_shapes=[pltpu.VMEM((T_TOK, RET_VAL_DIM), F32),
                        pltpu.VMEM((RET_KEY_DIM, RET_VAL_DIM), F32),
                        pltpu.VMEM((RET_KEY_DIM, RET_VAL_DIM), F32)],
        compiler_params=_cparams(2),
        name="retention",
    )(lg, q, k, v, sg)


def _fft_s2_fwd_kernel(a_ref, tw_ref, fhi_ref, flo_ref, g_ref):
    reps = HY_WIDTH // LANES
    for kk in range(FFT_KB):
        twr = jnp.concatenate([tw_ref[0, kk]] * reps, axis=1)
        twi = jnp.concatenate([tw_ref[1, kk]] * reps, axis=1)
        ar = a_ref[0, kk, 0:FFT_N2, :]
        ai = a_ref[1, kk, 0:FFT_N2, :]
        d = jnp.concatenate([ar * twr - ai * twi, ar * twi + ai * twr], axis=0)
        x = _dot3(fhi_ref[...], flo_ref[...], d)
        g_ref[0, kk] = x[:FFT_N2]
        g_ref[1, kk] = x[FFT_N2:]


def _dft_dense_kernel(x_ref, mhi_ref, mlo_ref, o_ref):
    o_ref[...] = _dot3(mhi_ref[...], mlo_ref[...], x_ref[...])


def _filter_spectra(filter_params):
    cst = _fft_constants()
    rows_z = FFT_N1H * FFT_PITCH
    rows_a = FFT_N1 * FFT_PITCH

    def slabs(half):
        half = half.reshape(FFT_N1H, FFT_N2, HY_WIDTH)
        return jnp.pad(half, ((0, 0), (0, FFT_PITCH - FFT_N2), (0, 0))).reshape(rows_z, HY_WIDTH)

    filt = jnp.stack([slabs(half) for half in _hyena_filter_halves(SEQ, *filter_params)])[None]
    a = pl.pallas_call(
        _fft_s1_kernel,
        grid=(1, HY_WIDTH // LANES),
        in_specs=[pl.BlockSpec((1, 2, rows_z, LANES), lambda p, cb: (p, 0, 0, cb)),
                  _full_spec((2 * FFT_N1, FFT_N1)), _full_spec((2 * FFT_N1, FFT_N1))],
        out_specs=pl.BlockSpec((1, 2, rows_a, LANES), lambda p, cb: (p, 0, 0, cb)),
        out_shape=jax.ShapeDtypeStruct((1, 2, rows_a, HY_WIDTH), F32),
        compiler_params=_cparams(2),
        name="filt_fft_s1",
    )(filt, *cst["m1r"])
    a = a.reshape(2, FFT_N1, FFT_PITCH, HY_WIDTH)
    sq = (2 * FFT_N2, 2 * FFT_N2)
    g_lat = pl.pallas_call(
        _fft_s2_fwd_kernel,
        grid=(FFT_N1 // FFT_KB,),
        in_specs=[pl.BlockSpec((2, FFT_KB, FFT_PITCH, HY_WIDTH), lambda kb: (0, kb, 0, 0)),
                  pl.BlockSpec((2, FFT_KB, FFT_N2, LANES), lambda kb: (0, kb, 0, 0)),
                  pl.BlockSpec(sq, lambda kb: (0, 0)), pl.BlockSpec(sq, lambda kb: (0, 0))],
        out_specs=pl.BlockSpec((2, FFT_KB, FFT_N2, HY_WIDTH), lambda kb: (0, kb, 0, 0)),
        out_shape=jax.ShapeDtypeStruct((2, FFT_N1, FFT_N2, HY_WIDTH), F32),
        compiler_params=_cparams(1),
        name="filt_fft_s2",
    )(a, cst["tw"], *cst["m2"])
    filt_ctx = jnp.concatenate(_hyena_filter_halves(CTX_LEN, *filter_params), axis=0)
    g_ctx = pl.pallas_call(
        _dft_dense_kernel,
        out_shape=jax.ShapeDtypeStruct((2 * FFT_NC, HY_WIDTH), F32),
        compiler_params=pltpu.CompilerParams(vmem_limit_bytes=VMEM_LIMIT_BYTES),
        name="filt_dft_ctx",
    )(filt_ctx, *cst["mcf_full"])
    return g_lat, g_ctx.reshape(2, FFT_NC, HY_WIDTH)


def _even_mixer(xt, mod_i, gain, w_in, layer, w_out, q_gain, k_gain, rpb, conv_w, conv_b, filter_params,
                d_bias):
    scale = NA_HEAD_DIM ** -0.5
    qg = (jnp.tile(q_gain, NA_HEADS) * scale).reshape(1, NA_WIDTH)
    kg = jnp.tile(k_gain, NA_HEADS).reshape(1, NA_WIDTH)
    head_of = np.arange(NA_WIDTH) // NA_HEAD_DIM
    bd = jnp.asarray((head_of[:, None] == head_of[None, :]) / NA_HEAD_DIM, BF16)
    q, k, v, u = _even_in_call(xt, mod_i, gain, w_in, layer, qg, kg, bd)
    a = _na_call(q, k, v, _na_bias_table(rpb))
    z_lat, x0_lat, z_ctx, x0_ctx = _hy_pre_call(u, conv_w, conv_b.reshape(1, -1))
    g_lat, g_ctx = _filter_spectra(filter_params)
    y_lat = _hy_long_conv_call(z_lat, x0_lat, g_lat, d_bias)
    y_ctx = _hy_ctx_conv_call(z_ctx, x0_ctx, g_ctx, d_bias)
    return a, y_lat.reshape(xt.shape[0], FFT_N1H, FFT_PITCH, HY_WIDTH), y_ctx, w_out.astype(BF16)


def _odd_mixer(xt, mod_i, gain, w_in, layer, w_out, logit_f, logit_b, rope):
    q, k, v, sg = _odd_in_call(xt, mod_i, gain, w_in, layer, *rope)
    lg = jnp.stack([jax.nn.log_sigmoid(logit_f.astype(F32)), jax.nn.log_sigmoid(logit_b.astype(F32))])
    return _ret_call(lg, q, k, v, sg), w_out.astype(BF16)


def kernel(x, c, ctx, c_ctx, w_mod, b_mod, norm_gain, ffn_a_in, ffn_a_out, ffn_b_in, ffn_b_out,
           even_in, even_out, na_q_gain, na_k_gain, na_rpb, hy_conv_w, hy_conv_b,
           hy_fw1, hy_fb1, hy_fw2, hy_fb2, hy_fw3, hy_fb3, hy_fw4, hy_freq, hy_bias,
           ret_in, ret_out, ret_logit_f, ret_logit_b):
    batch = x.shape[0]
    assert x.shape == (batch, SEQ, D_MODEL) and ctx.shape == (batch, CTX_LEN, D_MODEL)
    assert batch % 2 == 0 and batch < MOD_ROWS
    c_rows = jnp.concatenate([c, c_ctx[None], jnp.zeros((MOD_ROWS - batch - 1, D_MODEL), F32)], axis=0)
    mod_all = _mod_call(c_rows, w_mod, b_mod).reshape(DEPTH, MOD_ROWS, N_MOD, D_MODEL)
    rope = _rope_tables()
    for i in range(DEPTH):
        last = i == DEPTH - 1
        mod_i = mod_all[i]
        gains = norm_gain[i].reshape(3, 1, D_MODEL)
        if i == 0:
            xt = _ffn_call("split", (x, ctx), mod_i, gains[0], ffn_a_in, ffn_a_out, i, (0, 1, 2), NT_ALL, batch)
        else:
            xt = _ffn_call("stream", (xt,), mod_i, gains[0], ffn_a_in, ffn_a_out, i, (0, 1, 2), NT_ALL, batch)
        if i % 2 == 0:
            e = i // 2
            filter_params = (hy_fw1[e], hy_fb1[e], hy_fw2[e], hy_fb2[e], hy_fw3[e], hy_fb3[e],
                             hy_fw4[e], hy_freq[e])
            source = "even"
            mixed = _even_mixer(xt, mod_i, gains[1], even_in, e, even_out[e], na_q_gain[e], na_k_gain[e],
                                na_rpb[e], hy_conv_w[e], hy_conv_b[e], filter_params, hy_bias[e])
        else:
            o = i // 2
            source = "odd"
            mixed = _odd_mixer(xt, mod_i, gains[1], ret_in, o, ret_out[o], ret_logit_f[o], ret_logit_b[o], rope)
        xt = _ffn_call(source, (xt,) + mixed, mod_i, gains[2], ffn_b_in, ffn_b_out, i, (6, 7, 8),
                       NT_LAT if last else NT_ALL, batch)
    return xt
```

```python
import functools
import math

import ml_dtypes
import numpy as np
import jax
import jax.numpy as jnp
from jax import lax
from jax.experimental import pallas as pl
from jax.experimental.pallas import tpu as pltpu

F32 = jnp.float32
BF16 = jnp.bfloat16

D_MODEL = 1024
SEQ = 4096
DEPTH = 4
GRID_W = 64
CTX_LEN = 256
N_MOD = 9
RMS_EPS = 1e-6
GN_EPS = 1e-6
D_FF = 2816
NA_HEADS = 8
NA_HEAD_DIM = 64
NA_WIDTH = NA_HEADS * NA_HEAD_DIM
WIN_R = 8
WIN_C = 16
HY_WIDTH = D_MODEL - NA_WIDTH
HY_BANDS = 8
HY_TARGET = 1e-2
HY_FAST_PCT = 0.3
HY_SLOW_PCT = 1.5
RET_HEADS = 4
RET_KEY_DIM = D_MODEL // RET_HEADS
RET_VAL_DIM = 2 * RET_KEY_DIM
RET_QK = RET_HEADS * RET_KEY_DIM
RET_V = RET_HEADS * RET_VAL_DIM
ROPE_BASE = 10000.0

LANES = 128
VMEM_LIMIT_BYTES = 56 * 2**20
T_TOK = SEQ + CTX_LEN
TM = CTX_LEN
NT_LAT = SEQ // TM
NT_ALL = T_TOK // TM
MOD_ROWS = 16
FF_CHUNKS = ((0, 1536), (1536, 1280))
RET_BLOCK = 256

NA_RB = 4
NA_KR = NA_RB + WIN_R - 1
GRID_H = SEQ // GRID_W
NA_NQ = NA_RB * GRID_W
NA_NK = NA_KR * GRID_W
NEG_BIG = -1e30

FFT_N = 2 * SEQ
FFT_N1 = 64
FFT_N2 = 128
FFT_N1H = FFT_N1 // 2
FFT_KB = 4
FFT_PITCH = FFT_N2 + 8
FFT_NC = 2 * CTX_LEN


def _cparams(n_axes):
    return pltpu.CompilerParams(dimension_semantics=("arbitrary",) * n_axes,
                                vmem_limit_bytes=VMEM_LIMIT_BYTES)


def _bdot(a, b):
    return jnp.dot(a, b, preferred_element_type=F32)


_NT_DIMS = (((1,), (1,)), ((), ()))
_TN_DIMS = (((0,), (0,)), ((), ()))


def _split_hi_lo(m):
    hi = m.astype(BF16)
    lo = (m - hi.astype(F32)).astype(BF16)
    return hi, lo


def _dot3(m_hi, m_lo, d):
    d_hi, d_lo = _split_hi_lo(d)
    return _bdot(m_hi, d_hi) + _bdot(m_lo, d_hi) + _bdot(m_hi, d_lo)


def _modulated(x, gain, mod_ref, shift_row, scale_row):
    ms = jnp.mean(x * x, axis=-1, keepdims=True)
    y = x * lax.rsqrt(ms + RMS_EPS) * gain
    return (y * (1.0 + mod_ref[0, scale_row:scale_row + 1, :])
            + mod_ref[0, shift_row:shift_row + 1, :])


def _silu(a):
    return a * jax.nn.sigmoid(a)


def _mod_kernel(c_ref, w_ref, b_ref, o_ref):
    s = _silu(c_ref[...])
    o_ref[0] = jnp.dot(s, w_ref[0], precision=lax.Precision.HIGHEST,
                       preferred_element_type=F32) + b_ref[0]


def _mod_call(c_rows, w_mod, b_mod):
    depth, d, n = w_mod.shape
    tn = 1024
    return pl.pallas_call(
        _mod_kernel,
        grid=(depth, n // tn),
        in_specs=[pl.BlockSpec((MOD_ROWS, d), lambda i, j: (0, 0)),
                  pl.BlockSpec((1, d, tn), lambda i, j: (i, 0, j)),
                  pl.BlockSpec((1, 1, tn), lambda i, j: (i, 0, j))],
        out_specs=pl.BlockSpec((1, MOD_ROWS, tn), lambda i, j: (i, 0, j)),
        out_shape=jax.ShapeDtypeStruct((depth, MOD_ROWS, n), F32),
        compiler_params=_cparams(2),
        name="mod",
    )(c_rows, w_mod, b_mod.reshape(depth, 1, n))


def _tok_spec(width):
    return pl.BlockSpec((1, TM, width), lambda b, j: (b, j, 0))


def _mod_spec(batch):
    return pl.BlockSpec((1, N_MOD, D_MODEL), lambda b, j: (jnp.where(j == NT_LAT, batch, b), 0, 0))


def _full_spec(shape):
    zeros = (0,) * len(shape)
    return pl.BlockSpec(shape, lambda b, j: zeros)


_FFN_SOURCES = {"stream": 1, "split": 4, "even": 7, "odd": 3}
TM2 = 2 * TM


def _per_half(fn):
    return jnp.concatenate([fn(half, slice(half * TM, (half + 1) * TM)) for half in range(2)], axis=0)


W_STAGE = 256


def _stage_weight(src, dst, stage, sem, axis):
    n_chunks = src.shape[axis] // W_STAGE

    def window(ref, c):
        chunk = pl.ds(c * W_STAGE, W_STAGE)
        return ref.at[chunk, :] if axis == 0 else ref.at[:, chunk]

    def copy(c):
        return pltpu.make_async_copy(window(src, c), stage.at[c % 2], sem.at[c % 2])

    copy(0).start()
    for c in range(n_chunks):
        if c + 1 < n_chunks:
            copy(c + 1).start()
        copy(c).wait()
        window(dst, c)[...] = stage[c % 2].astype(BF16)


def _ffn_kernel(*refs, rows, source, n_tiles, n_pairs, layer):
    shift_row, scale_row, gate_row = rows
    n_src = _FFN_SOURCES[source]
    src = refs[:n_src]
    (modn0, modn1, modc0, modc1, gain_ref, win_hbm, wout_hbm, o_ref, h_a, x_a, h_b, x_b,
     win_ref, wout_ref, stage_in, stage_out, sem) = refs[n_src:]
    modn = (modn0, modn1)
    modc = (modc0, modc1)
    step = pl.program_id(0)
    first_tile_next = 2 * jnp.minimum(step, n_pairs - 1)

    def is_ctx(half):
        return (first_tile_next + half) % n_tiles == NT_LAT

    def gated(x_ref, o):
        return _per_half(lambda half, r: x_ref[0, r, :] + modn[half][0, 5:6, :] * o[r])

    def next_rows():
        if source == "stream":
            return src[0][0]
        if source == "split":
            lat, ctx = src[0:2], src[2:4]
            return _per_half(lambda half, r: jnp.where(is_ctx(half), ctx[half][0], lat[half][0]))
        if source == "even":
            x_ref, a_ref, yl0, yl1, yc0, yc1, wo_ref = src
            y_lat, y_ctx = (yl0, yl1), (yc0, yc1)

            def hyena_rows(half, r):
                slabs = [y_lat[half][0, s, 0:FFT_N2, :] for s in range(TM // FFT_N2)]
                return jnp.where(is_ctx(half), y_ctx[half][0], jnp.concatenate(slabs, axis=0)).astype(BF16)

            o = _bdot(a_ref[0], wo_ref[0:NA_WIDTH, :]) + _bdot(_per_half(hyena_rows), wo_ref[NA_WIDTH:, :])
            return gated(x_ref, o)
        x_ref, y_ref, wo_ref = src
        return gated(x_ref, _bdot(y_ref[0], wo_ref[...]))

    def prepare(h_write, x_write):
        x_next = next_rows()
        h_write[...] = _per_half(lambda half, r: _modulated(
            x_next[r], gain_ref[...], modn[half], shift_row, scale_row)).astype(BF16)
        x_write[...] = x_next

    def body(h_read, x_read, h_write, x_write):
        prepare(h_write, x_write)
        h = h_read[...]
        acc = jnp.zeros((TM2, D_MODEL), F32)
        for start, size in FF_CHUNKS:
            a = _bdot(h, win_ref[:, start:start + size])
            b = _bdot(h, win_ref[:, D_FF + start:D_FF + start + size])
            g = (_silu(a) * b).astype(BF16)
            acc = acc + _bdot(g, wout_ref[start:start + size, :])
        o_ref[0] = _per_half(lambda half, r: x_read[r, :] + (0.5 * modc[half][0, gate_row:gate_row + 1, :]) * acc[r])

    @pl.when(step == 0)
    def _():
        _stage_weight(win_hbm.at[layer], win_ref, stage_in, sem, axis=1)
        _stage_weight(wout_hbm.at[layer], wout_ref, stage_out, sem, axis=0)
        prepare(h_a, x_a)

    @pl.when(jnp.logical_and(step > 0, step % 2 == 0))
    def _():
        body(h_b, x_b, h_a, x_a)

    @pl.when(step % 2 == 1)
    def _():
        body(h_a, x_a, h_b, x_b)


def _ffn_call(source, srcs, mod_i, gain, w_in, w_out, layer, rows, n_tiles, batch):
    n_all = batch * n_tiles
    assert n_all % 2 == 0 and n_tiles in (NT_LAT, NT_ALL)
    n_pairs = n_all // 2

    def next_pair(s):
        return jnp.minimum(s, n_pairs - 1)

    def cur_pair(s):
        return jnp.maximum(s - 1, 0)

    def tile_of(pair_of, half):
        def fn(s):
            t = 2 * pair_of(s) + half
            return t // n_tiles, t % n_tiles
        return fn

    def mod_spec(pair_of, half):
        def index(s):
            b, j = tile_of(pair_of, half)(s)
            return jnp.where(j == NT_LAT, batch, b), 0, 0
        return pl.BlockSpec((1, N_MOD, D_MODEL), index)

    def tok(arr):
        width = arr.shape[-1]
        if n_tiles == NT_ALL:
            return arr.reshape(1, batch * T_TOK, width), pl.BlockSpec(
                (1, TM2, width), lambda s: (0, next_pair(s), 0))
        per_sample = n_tiles // 2
        return arr, pl.BlockSpec((1, TM2, width),
                                 lambda s: (next_pair(s) // per_sample, next_pair(s) % per_sample, 0))

    def lat_tile(block, n_trailing, half):
        def index(s):
            b, j = tile_of(next_pair, half)(s)
            return (b, jnp.minimum(j, NT_LAT - 1)) + (0,) * n_trailing
        return pl.BlockSpec(block, index)

    def ctx_tile(block, half):
        return pl.BlockSpec(block, lambda s: (tile_of(next_pair, half)(s)[0],) + (0,) * (len(block) - 1))

    def full(shape):
        zeros = (0,) * len(shape)
        return pl.BlockSpec(shape, lambda s: zeros)

    halves = (0, 1)
    if source == "stream":
        (xt,) = srcs
        xt, x_spec = tok(xt)
        args, src_specs = (xt,), [x_spec]
    elif source == "split":
        x, ctx = srcs
        args = (x, x, ctx, ctx)
        src_specs = ([lat_tile((1, TM, D_MODEL), 1, h) for h in halves]
                     + [ctx_tile((1, CTX_LEN, D_MODEL), h) for h in halves])
    elif source == "even":
        xt, a, y_lat, y_ctx, wo = srcs
        (xt, x_spec), (a, a_spec) = tok(xt), tok(a)
        args = (xt, a, y_lat, y_lat, y_ctx, y_ctx, wo)
        src_specs = ([x_spec, a_spec]
                     + [lat_tile((1, TM // FFT_N2, FFT_PITCH, HY_WIDTH), 2, h) for h in halves]
                     + [ctx_tile((1, CTX_LEN, HY_WIDTH), h) for h in halves] + [full((D_MODEL, D_MODEL))])
    else:
        xt, y, wo = srcs
        (xt, x_spec), (y, y_spec) = tok(xt), tok(y)
        args = (xt, y, wo)
        src_specs = [x_spec, y_spec, full((RET_V, D_MODEL))]
    out = pl.pallas_call(
        functools.partial(_ffn_kernel, rows=rows, source=source, n_tiles=n_tiles, n_pairs=n_pairs,
                          layer=layer),
        grid=(n_pairs + 1,),
        in_specs=src_specs + [mod_spec(next_pair, 0), mod_spec(next_pair, 1),
                              mod_spec(cur_pair, 0), mod_spec(cur_pair, 1), full((1, D_MODEL)),
                              pl.BlockSpec(memory_space=pl.ANY), pl.BlockSpec(memory_space=pl.ANY)],
        out_specs=pl.BlockSpec((1, TM2, D_MODEL), lambda s: (0, cur_pair(s), 0)),
        out_shape=jax.ShapeDtypeStruct((1, n_all * TM, D_MODEL), F32),
        scratch_shapes=[pltpu.VMEM((TM2, D_MODEL), BF16), pltpu.VMEM((TM2, D_MODEL), F32),
                        pltpu.VMEM((TM2, D_MODEL), BF16), pltpu.VMEM((TM2, D_MODEL), F32),
                        pltpu.VMEM((D_MODEL, 2 * D_FF), BF16), pltpu.VMEM((D_FF, D_MODEL), BF16),
                        pltpu.VMEM((2, D_MODEL, W_STAGE), F32), pltpu.VMEM((2, W_STAGE, D_MODEL), F32),
                        pltpu.SemaphoreType.DMA((2,))],
        compiler_params=_cparams(1),
        name="ffn_" + source,
    )(*args, mod_i, mod_i, mod_i, mod_i, gain, w_in, w_out)
    return out.reshape(batch, n_tiles * TM, D_MODEL)


def _stage_in_proj_weight(w_hbm, layer, w_ref, stage, sem):
    @pl.when(jnp.logical_and(pl.program_id(0) == 0, pl.program_id(1) == 0))
    def _():
        _stage_weight(w_hbm.at[layer], w_ref, stage, sem, axis=1)


def _in_proj_weight_scratch(n_in):
    return [pltpu.VMEM((D_MODEL, n_in), BF16), pltpu.VMEM((2, D_MODEL, W_STAGE), F32),
            pltpu.SemaphoreType.DMA((2,))]


def _even_in_kernel(x_ref, mod_ref, gain_ref, w_hbm, qg_ref, kg_ref, bd_ref,
                    q_ref, k_ref, v_ref, u_ref, w_ref, stage, sem, *, layer):
    _stage_in_proj_weight(w_hbm, layer, w_ref, stage, sem)
    h = _modulated(x_ref[0], gain_ref[...], mod_ref, 3, 4).astype(BF16)

    def head_norm(t, g):
        ms = _bdot((t * t).astype(BF16), bd_ref[...])
        return (t * lax.rsqrt(ms + RMS_EPS) * g).astype(BF16)

    q_ref[0] = head_norm(_bdot(h, w_ref[:, 0:NA_WIDTH]), qg_ref[...])
    k_ref[0] = head_norm(_bdot(h, w_ref[:, NA_WIDTH:2 * NA_WIDTH]), kg_ref[...])
    v_ref[0] = _bdot(h, w_ref[:, 2 * NA_WIDTH:3 * NA_WIDTH]).astype(BF16)
    u_ref[0] = _bdot(h, w_ref[:, 3 * NA_WIDTH:])


def _even_in_call(xt, mod_i, gain, w, layer, q_gain, k_gain, bd):
    batch = xt.shape[0]
    n_in = w.shape[2]
    return pl.pallas_call(
        functools.partial(_even_in_kernel, layer=layer),
        grid=(batch, NT_ALL),
        in_specs=[_tok_spec(D_MODEL), _mod_spec(batch), _full_spec((1, D_MODEL)),
                  pl.BlockSpec(memory_space=pl.ANY), _full_spec((1, NA_WIDTH)), _full_spec((1, NA_WIDTH)),
                  _full_spec((NA_WIDTH, NA_WIDTH))],
        scratch_shapes=_in_proj_weight_scratch(n_in),
        out_specs=[_tok_spec(NA_WIDTH), _tok_spec(NA_WIDTH), _tok_spec(NA_WIDTH),
                   _tok_spec(3 * HY_WIDTH)],
        out_shape=[jax.ShapeDtypeStruct((batch, T_TOK, NA_WIDTH), BF16)] * 3
        + [jax.ShapeDtypeStruct((batch, T_TOK, 3 * HY_WIDTH), F32)],
        compiler_params=_cparams(2),
        name="even_in",
    )(xt, mod_i, gain, w, q_gain, k_gain, bd)


def _na_kernel(q_ref, k_ref, v_ref, bias_ref, o_ref):
    lane = lax.broadcasted_iota(jnp.int32, (1, LANES), 1)
    first_head = lane < NA_HEAD_DIM
    k_ctx = k_ref[0, SEQ:T_TOK, :]
    v_ctx = v_ref[0, SEQ:T_TOK, :]

    def stack_heads(q):
        zero = jnp.zeros_like(q)
        return jnp.concatenate([jnp.where(first_head, q, zero), jnp.where(first_head, zero, q)], axis=0)

    def attend(qs, scores_and_values):
        s_list = [lax.dot_general(qs, kk, _NT_DIMS, preferred_element_type=F32) if bias is None
                  else lax.dot_general(qs, kk, _NT_DIMS, preferred_element_type=F32) + bias
                  for kk, _, bias in scores_and_values]
        m = functools.reduce(jnp.maximum, [jnp.max(s, axis=-1, keepdims=True) for s in s_list])
        o = functools.reduce(jnp.add, [
            _bdot(jnp.exp(s - m).astype(BF16),
                  jnp.concatenate([vv, jnp.ones((vv.shape[0], LANES), BF16)], axis=1))
            for s, (_, vv, _) in zip(s_list, scores_and_values)])
        o = o[:, :LANES] / o[:, LANES:]
        n = qs.shape[0] // 2
        return jnp.where(first_head, o[:n], o[n:]).astype(BF16)

    def block(i, carry):
        r0 = i * NA_RB
        u0 = jnp.clip(r0 - WIN_R // 2, 0, GRID_H - NA_KR)
        pattern = jnp.where(i == 0, 0, jnp.where(i == GRID_H // NA_RB - 1, 2, 1))
        q0 = pl.multiple_of(r0 * GRID_W, NA_NQ)
        k0 = pl.multiple_of(u0 * GRID_W, GRID_W)
        qs = stack_heads(q_ref[0, pl.ds(q0, NA_NQ), :])
        k_win = k_ref[0, pl.ds(k0, NA_NK), :]
        v_win = v_ref[0, pl.ds(k0, NA_NK), :]
        o_ref[0, pl.ds(q0, NA_NQ), :] = attend(
            qs, [(k_win, v_win, bias_ref[pattern, 0]), (k_ctx, v_ctx, None)])
        return carry

    lax.fori_loop(0, GRID_H // NA_RB, block, 0, unroll=8)
    o_ref[0, SEQ:T_TOK, :] = attend(stack_heads(q_ref[0, SEQ:T_TOK, :]), [(k_ctx, v_ctx, None)])


def _na_call(q, k, v, bias):
    batch = q.shape[0]
    n_pairs = NA_WIDTH // LANES
    spec = pl.BlockSpec((1, T_TOK, LANES), lambda b, p: (b, 0, p))
    return pl.pallas_call(
        _na_kernel,
        grid=(batch, n_pairs),
        in_specs=[spec, spec, spec,
                  pl.BlockSpec((3, 1, 2 * NA_NQ, NA_NK), lambda b, p: (0, p, 0, 0))],
        out_specs=spec,
        out_shape=jax.ShapeDtypeStruct((batch, T_TOK, NA_WIDTH), BF16),
        compiler_params=_cparams(2),
        name="na_attn",
    )(q, k, v, bias)


def _na_bias_table(rpb):
    j = np.arange(NA_RB)[:, None, None, None]
    c = np.arange(GRID_W)[None, :, None, None]
    kk = np.arange(NA_KR)[None, None, :, None]
    kc = np.arange(GRID_W)[None, None, None, :]
    cs = np.clip(c - WIN_C // 2, 0, GRID_W - WIN_C)
    col_ok = (kc >= cs) & (kc < cs + WIN_C)
    dc = np.clip(kc - c + WIN_C - 1, 0, 2 * WIN_C - 2)[0, :, 0, :]
    col_sel = (dc[..., None] == np.arange(2 * WIN_C - 1)).astype(np.float32)
    shape = (NA_RB, GRID_W, NA_KR, GRID_W)
    oks, row_sels = [], []
    for off, rs_rel in ((0, 0 * j), (-(WIN_R // 2), j), (-(NA_KR - NA_RB), NA_KR - WIN_R + 0 * j)):
        row_ok = (kk >= rs_rel) & (kk < rs_rel + WIN_R)
        dr = np.clip(off + kk - j + WIN_R - 1, 0, 2 * WIN_R - 2)[:, 0, :, 0]
        row_sels.append((dr[..., None] == np.arange(2 * WIN_R - 1)).astype(np.float32))
        oks.append(np.broadcast_to(row_ok & col_ok, shape))
    vals = jnp.einsum("pjkr,hrc,qmc->phjqkm", np.stack(row_sels), rpb.astype(F32), col_sel,
                      precision=lax.Precision.HIGHEST)
    table = jnp.where(np.stack(oks)[:, None], vals, NEG_BIG)
    return table.reshape(3, NA_HEADS // 2, 2 * NA_NQ, NA_NK)


def _hy_pre_kernel(u0_ref, u1_ref, uv_ref, w0_ref, w1_ref, wv_ref, b0_ref, b1_ref, bv_ref,
                   zl_ref, xl_ref, zc_ref, xc_ref):
    row = lax.broadcasted_iota(jnp.int32, (TM, LANES), 0)
    zero_row = jnp.zeros((1, LANES), F32)

    def conv(u_ref, w_ref, b_ref, tile):
        s = tile * TM
        cur = u_ref[0, s:s + TM, :]
        seq_start = tile in (0, NT_LAT)
        seq_end = tile in (NT_LAT - 1, NT_LAT)
        prev_row = zero_row if seq_start else u_ref[0, s - 1:s, :]
        next_row = zero_row if seq_end else u_ref[0, s + TM:s + TM + 1, :]
        before = jnp.where(row == 0, prev_row, pltpu.roll(cur, 1, 0))
        after = jnp.where(row == TM - 1, next_row, pltpu.roll(cur, TM - 1, 0))
        return before * w_ref[0:1, :] + cur * w_ref[1:2, :] + after * w_ref[2:3, :] + b_ref[...]

    for tile in range(NT_ALL):
        x0 = conv(u0_ref, w0_ref, b0_ref, tile)
        z = conv(uv_ref, wv_ref, bv_ref, tile) * conv(u1_ref, w1_ref, b1_ref, tile)
        if tile < NT_LAT:
            for half in range(TM // FFT_N2):
                base = (tile * (TM // FFT_N2) + half) * FFT_PITCH
                zl_ref[0, base:base + FFT_N2, :] = z[half * FFT_N2:(half + 1) * FFT_N2]
                xl_ref[0, base:base + FFT_N2, :] = x0[half * FFT_N2:(half + 1) * FFT_N2]
                pad = jnp.zeros((FFT_PITCH - FFT_N2, LANES), F32)
                zl_ref[0, base + FFT_N2:base + FFT_PITCH, :] = pad
                xl_ref[0, base + FFT_N2:base + FFT_PITCH, :] = pad
        else:
            zc_ref[0] = z
            xc_ref[0] = x0


def _hy_pre_call(u, conv_w, conv_b):
    batch = u.shape[0]
    nb = HY_WIDTH // LANES
    u_specs = [pl.BlockSpec((1, T_TOK, LANES), lambda b, cb, g=g: (b, 0, g * nb + cb)) for g in range(3)]
    w_specs = [pl.BlockSpec((3, LANES), lambda b, cb, g=g: (0, g * nb + cb)) for g in range(3)]
    b_specs = [pl.BlockSpec((1, LANES), lambda b, cb, g=g: (0, g * nb + cb)) for g in range(3)]
    lat = pl.BlockSpec((1, FFT_N1H * FFT_PITCH, LANES), lambda b, cb: (b, 0, cb))
    ctx = pl.BlockSpec((1, CTX_LEN, LANES), lambda b, cb: (b, 0, cb))
    return pl.pallas_call(
        _hy_pre_kernel,
        grid=(batch, nb),
        in_specs=u_specs + w_specs + b_specs,
        out_specs=[lat, lat, ctx, ctx],
        out_shape=[jax.ShapeDtypeStruct((batch, FFT_N1H * FFT_PITCH, HY_WIDTH), F32)] * 2
        + [jax.ShapeDtypeStruct((batch, CTX_LEN, HY_WIDTH), F32)] * 2,
        compiler_params=_cparams(2),
        name="hy_pre",
    )(u, u, u, conv_w, conv_w, conv_w, conv_b, conv_b, conv_b)


def _complex_block(re, im):
    return np.block([[re, -im], [im, re]])


@functools.lru_cache(maxsize=None)
def _fft_constants():
    k1 = np.arange(FFT_N1)
    n1 = np.arange(FFT_N1H)
    f1 = np.exp(-2j * np.pi * np.outer(k1, n1) / FFT_N1)
    m1 = _complex_block(f1.real, f1.imag)
    f1_full = np.exp(-2j * np.pi * np.outer(k1, k1) / FFT_N1)
    m1r = np.concatenate([f1_full.real, f1_full.imag], axis=0)
    n2 = np.arange(FFT_N2)
    tw = np.exp(-2j * np.pi * np.outer(k1, n2) / FFT_N)
    tw = np.stack([tw.real, tw.imag])[..., None] * np.ones((1, 1, 1, LANES))
    f2 = np.exp(-2j * np.pi * np.outer(n2, n2) / FFT_N2)
    m2 = _complex_block(f2.real, f2.imag)
    m2i = _complex_block(f2.real, -f2.imag)
    c1 = np.exp(2j * np.pi * np.outer(n1, k1) / FFT_N1) / FFT_N
    m3 = _complex_block(c1.real, c1.imag)
    kc = np.arange(FFT_NC)
    nc = np.arange(CTX_LEN)
    ang = 2 * np.pi * np.outer(kc, nc) / FFT_NC
    mcf = np.concatenate([np.cos(ang), -np.sin(ang)], axis=0)
    ang_full = 2 * np.pi * np.outer(kc, kc) / FFT_NC
    mcf_full = np.concatenate([np.cos(ang_full), -np.sin(ang_full)], axis=0)
    mci = np.concatenate([np.cos(ang.T), -np.sin(ang.T)], axis=1) / FFT_NC

    def hi_lo(m):
        hi = m.astype(ml_dtypes.bfloat16)
        lo = (m - hi.astype(np.float64)).astype(ml_dtypes.bfloat16)
        return hi, lo

    return dict(m1=hi_lo(m1), m1r=hi_lo(m1r), m2=hi_lo(m2), m2i=hi_lo(m2i), m3=hi_lo(m3),
                mcf=hi_lo(mcf), mcf_full=hi_lo(mcf_full), mci=hi_lo(mci), tw=tw.astype(np.float32))


def _slab_rows(n2, n_slabs):
    return pl.ds(n2, n_slabs, stride=FFT_PITCH)


def _fft_s1_kernel(z_ref, *rest):
    *m_refs, a_ref = rest

    def step(i, carry):
        cols = []
        for dn in range(2):
            rows = _slab_rows(2 * i + dn, FFT_N1H)
            cols.append(jnp.concatenate([z_ref[0, 0, rows, :], z_ref[0, 1, rows, :]], axis=0))
        d = jnp.concatenate(cols, axis=1)
        if len(m_refs) == 2:
            r = _dot3(m_refs[0][...], m_refs[1][...], d)
        else:
            r = _bdot(m_refs[0][...], d.astype(BF16))
        for dn in range(2):
            rows = _slab_rows(2 * i + dn, FFT_N1)
            a_ref[0, 0, rows, :] = r[:FFT_N1, dn * LANES:(dn + 1) * LANES]
            a_ref[0, 1, rows, :] = r[FFT_N1:, dn * LANES:(dn + 1) * LANES]
        return carry

    lax.fori_loop(0, FFT_N2 // 2, step, 0, unroll=4)
    for pad_row in range(FFT_N2, FFT_PITCH):
        for part in range(2):
            a_ref[0, part, _slab_rows(pad_row, FFT_N1), :] = jnp.zeros((FFT_N1, LANES), F32)


def _fft_s2_kernel(a_ref, tw_ref, g_ref, fwd_ref, inv_ref, o_ref):
    reps = HY_WIDTH // LANES
    pad = jnp.zeros((FFT_PITCH - FFT_N2, HY_WIDTH), F32)
    for kk in range(FFT_KB):
        twr = jnp.concatenate([tw_ref[0, kk]] * reps, axis=1)
        twi = jnp.concatenate([tw_ref[1, kk]] * reps, axis=1)
        ar = a_ref[0, 0, kk, 0:FFT_N2, :]
        ai = a_ref[0, 1, kk, 0:FFT_N2, :]
        d = jnp.concatenate([ar * twr - ai * twi, ar * twi + ai * twr], axis=0)
        x = _bdot(fwd_ref[...], d.astype(BF16))
        xr, xi = x[:FFT_N2], x[FFT_N2:]
        gr, gi = g_ref[0, kk], g_ref[1, kk]
        y = jnp.concatenate([xr * gr - xi * gi, xr * gi + xi * gr], axis=0)
        b = _bdot(inv_ref[...], y.astype(BF16))
        br, bi = b[:FFT_N2], b[FFT_N2:]
        o_ref[0, 0, kk, 0:FFT_N2, :] = br * twr + bi * twi
        o_ref[0, 1, kk, 0:FFT_N2, :] = bi * twr - br * twi
        o_ref[0, 0, kk, FFT_N2:FFT_PITCH, :] = pad
        o_ref[0, 1, kk, FFT_N2:FFT_PITCH, :] = pad


def _fft_s3_kernel(b_ref, m_ref, z_ref, x0_ref, bias_ref, y_ref):
    def step(i, carry):
        cols = []
        for dn in range(2):
            rows = _slab_rows(2 * i + dn, FFT_N1)
            cols.append(jnp.concatenate([b_ref[0, 0, rows, :], b_ref[0, 1, rows, :]], axis=0))
        y = _bdot(m_ref[...], jnp.concatenate(cols, axis=1).astype(BF16))
        for dn in range(2):
            rows = _slab_rows(2 * i + dn, FFT_N1H)
            for s in range(2):
                conv = y[s * FFT_N1H:(s + 1) * FFT_N1H, dn * LANES:(dn + 1) * LANES]
                y_ref[0, s, rows, :] = (conv + z_ref[0, s, rows, :] * bias_ref[...]) * x0_ref[0, s, rows, :]
        return carry

    lax.fori_loop(0, FFT_N2 // 2, step, 0, unroll=4)
    for pad_row in range(FFT_N2, FFT_PITCH):
        for s in range(2):
            y_ref[0, s, _slab_rows(pad_row, FFT_N1H), :] = jnp.zeros((FFT_N1H, LANES), F32)


def _hy_long_conv_call(z_lat, x0_lat, g_spec, d_bias):
    batch = z_lat.shape[0]
    pairs = batch // 2
    cst = _fft_constants()
    n_cb = HY_WIDTH // LANES
    rows_z = FFT_N1H * FFT_PITCH
    rows_a = FFT_N1 * FFT_PITCH
    zv = z_lat.reshape(pairs, 2, rows_z, HY_WIDTH)
    xv = x0_lat.reshape(pairs, 2, rows_z, HY_WIDTH)
    z_spec = pl.BlockSpec((1, 2, rows_z, LANES), lambda p, cb: (p, 0, 0, cb))
    a_spec = pl.BlockSpec((1, 2, rows_a, LANES), lambda p, cb: (p, 0, 0, cb))
    a = pl.pallas_call(
        _fft_s1_kernel,
        grid=(pairs, n_cb),
        in_specs=[z_spec, _full_spec((2 * FFT_N1, 2 * FFT_N1H))],
        out_specs=a_spec,
        out_shape=jax.ShapeDtypeStruct((pairs, 2, rows_a, HY_WIDTH), F32),
        compiler_params=_cparams(2),
        name="hy_fft_s1",
    )(zv, cst["m1"][0])
    a = a.reshape(pairs, 2, FFT_N1, FFT_PITCH, HY_WIDTH)
    blk = pl.BlockSpec((1, 2, FFT_KB, FFT_PITCH, HY_WIDTH), lambda kb, p: (p, 0, kb, 0, 0))
    sq = (2 * FFT_N2, 2 * FFT_N2)
    b = pl.pallas_call(
        _fft_s2_kernel,
        grid=(FFT_N1 // FFT_KB, pairs),
        in_specs=[blk,
                  pl.BlockSpec((2, FFT_KB, FFT_N2, LANES), lambda kb, p: (0, kb, 0, 0)),
                  pl.BlockSpec((2, FFT_KB, FFT_N2, HY_WIDTH), lambda kb, p: (0, kb, 0, 0)),
                  _full_spec(sq), _full_spec(sq)],
        out_specs=blk,
        out_shape=jax.ShapeDtypeStruct((pairs, 2, FFT_N1, FFT_PITCH, HY_WIDTH), F32),
        compiler_params=_cparams(2),
        name="hy_fft_s2",
    )(a, cst["tw"], g_spec, cst["m2"][0], cst["m2i"][0])
    b = b.reshape(pairs, 2, rows_a, HY_WIDTH)
    y = pl.pallas_call(
        _fft_s3_kernel,
        grid=(pairs, n_cb),
        in_specs=[a_spec, _full_spec((2 * FFT_N1H, 2 * FFT_N1)),
                  z_spec, z_spec, pl.BlockSpec((1, LANES), lambda p, cb: (0, cb))],
        out_specs=z_spec,
        out_shape=jax.ShapeDtypeStruct((pairs, 2, rows_z, HY_WIDTH), F32),
        compiler_params=_cparams(2),
        name="hy_fft_s3",
    )(b, cst["m3"][0], zv, xv, d_bias.reshape(1, HY_WIDTH))
    return y.reshape(batch, rows_z, HY_WIDTH)


def _hy_ctx_kernel(z_ref, x0_ref, g_ref, fwd_ref, inv_ref, bias_ref, y_ref):
    z = z_ref[0]
    x = _bdot(fwd_ref[...], z.astype(BF16))
    xr, xi = x[:FFT_NC], x[FFT_NC:]
    gr, gi = g_ref[0], g_ref[1]
    y = jnp.concatenate([xr * gr - xi * gi, xr * gi + xi * gr], axis=0)
    conv = _bdot(inv_ref[...], y.astype(BF16))
    y_ref[0] = ((conv + z * bias_ref[...]) * x0_ref[0]).astype(BF16)


def _hy_ctx_conv_call(z_ctx, x0_ctx, g_spec, d_bias):
    batch = z_ctx.shape[0]
    cst = _fft_constants()
    tok = pl.BlockSpec((1, CTX_LEN, HY_WIDTH), lambda b: (b, 0, 0))

    def full(shape):
        zeros = (0,) * len(shape)
        return pl.BlockSpec(shape, lambda b: zeros)

    return pl.pallas_call(
        _hy_ctx_kernel,
        grid=(batch,),
        in_specs=[tok, tok, full((2, FFT_NC, HY_WIDTH)),
                  full((2 * FFT_NC, CTX_LEN)), full((CTX_LEN, 2 * FFT_NC)), full((1, HY_WIDTH))],
        out_specs=tok,
        out_shape=jax.ShapeDtypeStruct((batch, CTX_LEN, HY_WIDTH), BF16),
        compiler_params=_cparams(1),
        name="hy_ctx",
    )(z_ctx, x0_ctx, g_spec, cst["mcf"][0], cst["mci"][0], d_bias.reshape(1, HY_WIDTH))


def _hyena_filter_halves(length, fw1, fb1, fw2, fb2, fw3, fb3, fw4, freq):
    t = jnp.linspace(0.0, 1.0, length, dtype=F32)[:, None]
    w = 2.0 * math.pi * jnp.arange(length, dtype=F32)[:, None] / length
    bands = jnp.linspace(1e-4, HY_BANDS - 1, HY_BANDS, dtype=F32)
    emb = jnp.concatenate([t, jnp.cos(bands * w), -jnp.sin(bands * w)], axis=-1)

    def second_half_order(a):
        return jnp.concatenate([a[:1], a[:0:-1]], axis=0)

    hp = lax.Precision.HIGHEST
    h = jnp.concatenate([emb, second_half_order(emb)], axis=0)
    h = jnp.sin(freq * (jnp.dot(h, fw1, precision=hp) + fb1))
    h = jnp.sin(freq * (jnp.dot(h, fw2, precision=hp) + fb2))
    h = jnp.sin(freq * (jnp.dot(h, fw3, precision=hp) + fb3))
    max_decay = math.log(HY_TARGET) / HY_FAST_PCT
    min_decay = math.log(HY_TARGET) / HY_SLOW_PCT
    deltas = jnp.abs(jnp.linspace(min_decay, max_decay, HY_WIDTH, dtype=F32))
    causal = jnp.dot(h[:length], fw4[:, :HY_WIDTH], precision=hp).astype(F32) * jnp.exp(-t * deltas)
    anti = (jnp.dot(h[length:], fw4[:, HY_WIDTH:], precision=hp).astype(F32)
            * jnp.exp(-second_half_order(t) * deltas))
    first_row = jnp.arange(length)[:, None] == 0
    return causal + jnp.where(first_row, anti[:1], 0.0), jnp.where(first_row, 0.0, anti)


def _odd_in_kernel(x_ref, mod_ref, gain_ref, w_hbm, cos_ref, sin_ref, q_ref, k_ref, v_ref, sg_ref,
                   w_ref, stage, sem, *, layer):
    _stage_in_proj_weight(w_hbm, layer, w_ref, stage, sem)
    h = _modulated(x_ref[0], gain_ref[...], mod_ref, 3, 4).astype(BF16)
    cos = cos_ref[...]
    sin = sin_ref[...]

    def rope_store(dst_ref, col0, scale):
        p = _bdot(h, w_ref[:, col0:col0 + RET_QK])
        for c in range(RET_QK // LANES):
            t = p[:, c * LANES:(c + 1) * LANES]
            half = (c % 2) * LANES
            r = t * cos[:, half:half + LANES] + pltpu.roll(t, LANES // 2, 1) * sin[:, half:half + LANES]
            dst_ref[0, :, c * LANES:(c + 1) * LANES] = (r * scale).astype(BF16)

    rope_store(q_ref, 0, 1.0)
    rope_store(k_ref, RET_QK, RET_KEY_DIM ** -0.5)
    v_ref[0] = _bdot(h, w_ref[:, 2 * RET_QK:2 * RET_QK + RET_V]).astype(BF16)
    sg_ref[0] = _silu(_bdot(h, w_ref[:, 2 * RET_QK + RET_V:])).astype(BF16)


def _odd_in_call(xt, mod_i, gain, w, layer, cos_t, sin_t):
    batch = xt.shape[0]
    rope_spec = pl.BlockSpec((TM, RET_KEY_DIM), lambda b, j: (j, 0))
    return pl.pallas_call(
        functools.partial(_odd_in_kernel, layer=layer),
        grid=(batch, NT_ALL),
        in_specs=[_tok_spec(D_MODEL), _mod_spec(batch), _full_spec((1, D_MODEL)),
                  pl.BlockSpec(memory_space=pl.ANY), rope_spec, rope_spec],
        scratch_shapes=_in_proj_weight_scratch(w.shape[2]),
        out_specs=[_tok_spec(RET_QK), _tok_spec(RET_QK), _tok_spec(RET_V), _tok_spec(RET_V)],
        out_shape=[jax.ShapeDtypeStruct((batch, T_TOK, RET_QK), BF16)] * 2
        + [jax.ShapeDtypeStruct((batch, T_TOK, RET_V), BF16)] * 2,
        compiler_params=_cparams(2),
        name="odd_in",
    )(xt, mod_i, gain, w, cos_t, sin_t)


def _rope_tables():
    t = np.arange(SEQ)
    n_freq = RET_KEY_DIM // 4
    inv = ROPE_BASE ** (-jnp.arange(n_freq, dtype=F32) / n_freq)
    ang_r = jnp.asarray(t // GRID_W, F32)[:, None] * inv
    ang_c = jnp.asarray(t % GRID_W, F32)[:, None] * inv
    cr, sr, cc, sc = jnp.cos(ang_r), jnp.sin(ang_r), jnp.cos(ang_c), jnp.sin(ang_c)
    cos_l = jnp.concatenate([cr, cr, cc, cc], axis=-1)
    sin_l = jnp.concatenate([-sr, sr, -sc, sc], axis=-1)
    cos_t = jnp.concatenate([cos_l, jnp.ones((CTX_LEN, RET_KEY_DIM), F32)], axis=0)
    sin_t = jnp.concatenate([sin_l, jnp.zeros((CTX_LEN, RET_KEY_DIM), F32)], axis=0)
    return cos_t, sin_t


def _ret_kernel(lg_ref, q_ref, k_ref, v_ref, sg_ref, y_ref, o_acc, state_f, state_b):
    head = pl.program_id(1)
    n_c = RET_BLOCK
    n_lat = SEQ // n_c
    assert CTX_LEN == n_c and n_lat % 2 == 0
    ii = lax.broadcasted_iota(jnp.int32, (n_c, n_c), 0).astype(F32)
    jj = lax.broadcasted_iota(jnp.int32, (n_c, n_c), 1).astype(F32)
    row_k = lax.broadcasted_iota(jnp.int32, (n_c, RET_KEY_DIM), 0).astype(F32)
    row_v = lax.broadcasted_iota(jnp.int32, (n_c, RET_VAL_DIM), 0).astype(F32)

    def decays(backward):
        lg = lg_ref[1 if backward else 0, head]
        if backward:
            diff = jj - ii
            xi = jnp.exp(lg * (n_c - row_v))
            zeta = jnp.exp(lg * row_k)
        else:
            diff = ii - jj
            xi = jnp.exp(lg * (row_v + 1.0))
            zeta = jnp.exp(lg * (n_c - 1.0 - row_k))
        dmask = jnp.where(diff >= 0, jnp.exp(lg * jnp.maximum(diff, 0.0)), 0.0)
        g_chunk = jnp.exp(lg * n_c + jnp.zeros((1, RET_VAL_DIM), F32))
        return dmask, xi, zeta, g_chunk

    def advance(chunk, state, consts, finalize):
        dmask, xi, zeta, g_chunk = consts
        r = chunk * n_c if isinstance(chunk, int) else pl.multiple_of(chunk * n_c, n_c)
        qc = q_ref[0, pl.ds(r, n_c), :]
        kc = k_ref[0, pl.ds(r, n_c), :]
        vc = v_ref[0, pl.ds(r, n_c), :]
        st = state[...]
        inner = lax.dot_general(qc, kc, _NT_DIMS, preferred_element_type=F32) * dmask
        o = _bdot(inner.astype(BF16), vc) + _bdot(qc, st.astype(BF16)) * xi
        kz = (kc.astype(F32) * zeta).astype(BF16)
        state[...] = st * g_chunk + lax.dot_general(kz, vc, _TN_DIMS, preferred_element_type=F32)
        if finalize:
            tot = o_acc[pl.ds(r, n_c), :] + o
            mu = jnp.mean(tot, axis=-1, keepdims=True)
            cen = tot - mu
            var = jnp.mean(cen * cen, axis=-1, keepdims=True)
            yn = cen * lax.rsqrt(var + GN_EPS)
            y_ref[0, pl.ds(r, n_c), :] = (sg_ref[0, pl.ds(r, n_c), :].astype(F32) * yn).astype(BF16)
        else:
            o_acc[pl.ds(r, n_c), :] = o

    consts_f = decays(False)
    consts_b = decays(True)
    state_f[...] = jnp.zeros_like(state_f)
    state_b[...] = jnp.zeros_like(state_b)

    def both(chunk_f, chunk_b, finalize):
        advance(chunk_f, state_f, consts_f, finalize)
        advance(chunk_b, state_b, consts_b, finalize)

    advance(n_lat, state_f, consts_f, False)
    advance(n_lat, state_b, consts_b, True)

    def first_half(s, carry):
        both(s, n_lat - 1 - s, False)
        return carry

    def second_half(s, carry):
        both(s, n_lat - 1 - s, True)
        return carry

    lax.fori_loop(0, n_lat // 2, first_half, 0, unroll=2)
    lax.fori_loop(n_lat // 2, n_lat, second_half, 0, unroll=2)


def _ret_call(lg, q, k, v, sg):
    batch = q.shape[0]
    qk_spec = pl.BlockSpec((1, T_TOK, RET_KEY_DIM), lambda b, h: (b, 0, h))
    v_spec = pl.BlockSpec((1, T_TOK, RET_VAL_DIM), lambda b, h: (b, 0, h))
    return pl.pallas_call(
        _ret_kernel,
        grid=(batch, RET_HEADS),
        in_specs=[pl.BlockSpec(memory_space=pltpu.SMEM), qk_spec, qk_spec, v_spec, v_spec],
        out_specs=v_spec,
        out_shape=jax.ShapeDtypeStruct((batch, T_TOK, RET_V), BF16),
        scratch_shapes=[pltpu.VMEM((T_TOK, RET_VAL_DIM), F32),
                        pltpu.VMEM((RET_KEY_DIM, RET_VAL_DIM), F32),
                        pltpu.VMEM((RET_KEY_DIM, RET_VAL_DIM), F32)],
        compiler_params=_cparams(2),
        name="retention",
    )(lg, q, k, v, sg)


def _fft_s2_fwd_kernel(a_ref, tw_ref, fhi_ref, flo_ref, g_ref):
    reps = HY_WIDTH // LANES
    for kk in range(FFT_KB):
        twr = jnp.concatenate([tw_ref[0, kk]] * reps, axis=1)
        twi = jnp.concatenate([tw_ref[1, kk]] * reps, axis=1)
        ar = a_ref[0, kk, 0:FFT_N2, :]
        ai = a_ref[1, kk, 0:FFT_N2, :]
        d = jnp.concatenate([ar * twr - ai * twi, ar * twi + ai * twr], axis=0)
        x = _dot3(fhi_ref[...], flo_ref[...], d)
        g_ref[0, kk] = x[:FFT_N2]
        g_ref[1, kk] = x[FFT_N2:]


def _dft_dense_kernel(x_ref, mhi_ref, mlo_ref, o_ref):
    o_ref[...] = _dot3(mhi_ref[...], mlo_ref[...], x_ref[...])


def _filter_spectra(filter_params):
    cst = _fft_constants()
    rows_z = FFT_N1H * FFT_PITCH
    rows_a = FFT_N1 * FFT_PITCH

    def slabs(half):
        half = half.reshape(FFT_N1H, FFT_N2, HY_WIDTH)
        return jnp.pad(half, ((0, 0), (0, FFT_PITCH - FFT_N2), (0, 0))).reshape(rows_z, HY_WIDTH)

    filt = jnp.stack([slabs(half) for half in _hyena_filter_halves(SEQ, *filter_params)])[None]
    a = pl.pallas_call(
        _fft_s1_kernel,
        grid=(1, HY_WIDTH // LANES),
        in_specs=[pl.BlockSpec((1, 2, rows_z, LANES), lambda p, cb: (p, 0, 0, cb)),
                  _full_spec((2 * FFT_N1, FFT_N1)), _full_spec((2 * FFT_N1, FFT_N1))],
        out_specs=pl.BlockSpec((1, 2, rows_a, LANES), lambda p, cb: (p, 0, 0, cb)),
        out_shape=jax.ShapeDtypeStruct((1, 2, rows_a, HY_WIDTH), F32),
        compiler_params=_cparams(2),
        name="filt_fft_s1",
    )(filt, *cst["m1r"])
    a = a.reshape(2, FFT_N1, FFT_PITCH, HY_WIDTH)
    sq = (2 * FFT_N2, 2 * FFT_N2)
    g_lat = pl.pallas_call(
        _fft_s2_fwd_kernel,
        grid=(FFT_N1 // FFT_KB,),
        in_specs=[pl.BlockSpec((2, FFT_KB, FFT_PITCH, HY_WIDTH), lambda kb: (0, kb, 0, 0)),
                  pl.BlockSpec((2, FFT_KB, FFT_N2, LANES), lambda kb: (0, kb, 0, 0)),
                  pl.BlockSpec(sq, lambda kb: (0, 0)), pl.BlockSpec(sq, lambda kb: (0, 0))],
        out_specs=pl.BlockSpec((2, FFT_KB, FFT_N2, HY_WIDTH), lambda kb: (0, kb, 0, 0)),
        out_shape=jax.ShapeDtypeStruct((2, FFT_N1, FFT_N2, HY_WIDTH), F32),
        compiler_params=_cparams(1),
        name="filt_fft_s2",
    )(a, cst["tw"], *cst["m2"])
    filt_ctx = jnp.concatenate(_hyena_filter_halves(CTX_LEN, *filter_params), axis=0)
    g_ctx = pl.pallas_call(
        _dft_dense_kernel,
        out_shape=jax.ShapeDtypeStruct((2 * FFT_NC, HY_WIDTH), F32),
        compiler_params=pltpu.CompilerParams(vmem_limit_bytes=VMEM_LIMIT_BYTES),
        name="filt_dft_ctx",
    )(filt_ctx, *cst["mcf_full"])
    return g_lat, g_ctx.reshape(2, FFT_NC, HY_WIDTH)


def _even_mixer(xt, mod_i, gain, w_in, layer, w_out, q_gain, k_gain, rpb, conv_w, conv_b, filter_params,
                d_bias):
    scale = NA_HEAD_DIM ** -0.5
    qg = (jnp.tile(q_gain, NA_HEADS) * scale).reshape(1, NA_WIDTH)
    kg = jnp.tile(k_gain, NA_HEADS).reshape(1, NA_WIDTH)
    head_of = np.arange(NA_WIDTH) // NA_HEAD_DIM
    bd = jnp.asarray((head_of[:, None] == head_of[None, :]) / NA_HEAD_DIM, BF16)
    q, k, v, u = _even_in_call(xt, mod_i, gain, w_in, layer, qg, kg, bd)
    a = _na_call(q, k, v, _na_bias_table(rpb))
    z_lat, x0_lat, z_ctx, x0_ctx = _hy_pre_call(u, conv_w, conv_b.reshape(1, -1))
    g_lat, g_ctx = _filter_spectra(filter_params)
    y_lat = _hy_long_conv_call(z_lat, x0_lat, g_lat, d_bias)
    y_ctx = _hy_ctx_conv_call(z_ctx, x0_ctx, g_ctx, d_bias)
    return a, y_lat.reshape(xt.shape[0], FFT_N1H, FFT_PITCH, HY_WIDTH), y_ctx, w_out.astype(BF16)


def _odd_mixer(xt, mod_i, gain, w_in, layer, w_out, logit_f, logit_b, rope):
    q, k, v, sg = _odd_in_call(xt, mod_i, gain, w_in, layer, *rope)
    lg = jnp.stack([jax.nn.log_sigmoid(logit_f.astype(F32)), jax.nn.log_sigmoid(logit_b.astype(F32))])
    return _ret_call(lg, q, k, v, sg), w_out.astype(BF16)


def kernel(x, c, ctx, c_ctx, w_mod, b_mod, norm_gain, ffn_a_in, ffn_a_out, ffn_b_in, ffn_b_out,
           even_in, even_out, na_q_gain, na_k_gain, na_rpb, hy_conv_w, hy_conv_b,
           hy_fw1, hy_fb1, hy_fw2, hy_fb2, hy_fw3, hy_fb3, hy_fw4, hy_freq, hy_bias,
           ret_in, ret_out, ret_logit_f, ret_logit_b):
    batch = x.shape[0]
    assert x.shape == (batch, SEQ, D_MODEL) and ctx.shape == (batch, CTX_LEN, D_MODEL)
    assert batch % 2 == 0 and batch < MOD_ROWS
    c_rows = jnp.concatenate([c, c_ctx[None], jnp.zeros((MOD_ROWS - batch - 1, D_MODEL), F32)], axis=0)
    mod_all = _mod_call(c_rows, w_mod, b_mod).reshape(DEPTH, MOD_ROWS, N_MOD, D_MODEL)
    rope = _rope_tables()
    for i in range(DEPTH):
        last = i == DEPTH - 1
        mod_i = mod_all[i]
        gains = norm_gain[i].reshape(3, 1, D_MODEL)
        if i == 0:
            xt = _ffn_call("split", (x, ctx), mod_i, gains[0], ffn_a_in, ffn_a_out, i, (0, 1, 2), NT_ALL, batch)
        else:
            xt = _ffn_call("stream", (xt,), mod_i, gains[0], ffn_a_in, ffn_a_out, i, (0, 1, 2), NT_ALL, batch)
        if i % 2 == 0:
            e = i // 2
            filter_params = (hy_fw1[e], hy_fb1[e], hy_fw2[e], hy_fb2[e], hy_fw3[e], hy_fb3[e],
                             hy_fw4[e], hy_freq[e])
            source = "even"
            mixed = _even_mixer(xt, mod_i, gains[1], even_in, e, even_out[e], na_q_gain[e], na_k_gain[e],
                                na_rpb[e], hy_conv_w[e], hy_conv_b[e], filter_params, hy_bias[e])
        else:
            o = i // 2
            source = "odd"
            mixed = _odd_mixer(xt, mod_i, gains[1], ret_in, o, ret_out[o], ret_logit_f[o], ret_logit_b[o], rope)
        xt = _ffn_call(source, (xt,) + mixed, mod_i, gains[2], ffn_b_in, ffn_b_out, i, (6, 7, 8),
                       NT_LAT if last else NT_ALL, batch)
    return xt
```

```python
import functools
import math

import ml_dtypes
import numpy as np
import jax
import jax.numpy as jnp
from jax import lax
from jax.experimental import pallas as pl
from jax.experimental.pallas import tpu as pltpu

F32 = jnp.float32
BF16 = jnp.bfloat16

D_MODEL = 1024
SEQ = 4096
DEPTH = 4
GRID_W = 64
CTX_LEN = 256
N_MOD = 9
RMS_EPS = 1e-6
GN_EPS = 1e-6
D_FF = 2816
NA_HEADS = 8
NA_HEAD_DIM = 64
NA_WIDTH = NA_HEADS * NA_HEAD_DIM
WIN_R = 8
WIN_C = 16
HY_WIDTH = D_MODEL - NA_WIDTH
HY_BANDS = 8
HY_TARGET = 1e-2
HY_FAST_PCT = 0.3
HY_SLOW_PCT = 1.5
RET_HEADS = 4
RET_KEY_DIM = D_MODEL // RET_HEADS
RET_VAL_DIM = 2 * RET_KEY_DIM
RET_QK = RET_HEADS * RET_KEY_DIM
RET_V = RET_HEADS * RET_VAL_DIM
ROPE_BASE = 10000.0

LANES = 128
VMEM_LIMIT_BYTES = 56 * 2**20
T_TOK = SEQ + CTX_LEN
TM = CTX_LEN
NT_LAT = SEQ // TM
NT_ALL = T_TOK // TM
MOD_ROWS = 16
FF_CHUNKS = ((0, 1536), (1536, 1280))
RET_BLOCK = 256

NA_RB = 4
NA_KR = NA_RB + WIN_R - 1
GRID_H = SEQ // GRID_W
NA_NQ = NA_RB * GRID_W
NA_NK = NA_KR * GRID_W
NEG_BIG = -1e30

FFT_N = 2 * SEQ
FFT_N1 = 64
FFT_N2 = 128
FFT_N1H = FFT_N1 // 2
FFT_KB = 4
FFT_PITCH = FFT_N2 + 8
FFT_NC = 2 * CTX_LEN


def _cparams(n_axes):
    return pltpu.CompilerParams(dimension_semantics=("arbitrary",) * n_axes,
                                vmem_limit_bytes=VMEM_LIMIT_BYTES)


def _bdot(a, b):
    return jnp.dot(a, b, preferred_element_type=F32)


_NT_DIMS = (((1,), (1,)), ((), ()))
_TN_DIMS = (((0,), (0,)), ((), ()))


def _split_hi_lo(m):
    hi = m.astype(BF16)
    lo = (m - hi.astype(F32)).astype(BF16)
    return hi, lo


def _dot3(m_hi, m_lo, d):
    d_hi, d_lo = _split_hi_lo(d)
    return _bdot(m_hi, d_hi) + _bdot(m_lo, d_hi) + _bdot(m_hi, d_lo)


def _modulated(x, gain, mod_ref, shift_row, scale_row):
    ms = jnp.mean(x * x, axis=-1, keepdims=True)
    y = x * lax.rsqrt(ms + RMS_EPS) * gain
    return (y * (1.0 + mod_ref[0, scale_row:scale_row + 1, :])
            + mod_ref[0, shift_row:shift_row + 1, :])


def _silu(a):
    return a * jax.nn.sigmoid(a)


def _mod_kernel(c_ref, w_ref, b_ref, o_ref):
    s = _silu(c_ref[...])
    o_ref[0] = jnp.dot(s, w_ref[0], precision=lax.Precision.HIGHEST,
                       preferred_element_type=F32) + b_ref[0]


def _mod_call(c_rows, w_mod, b_mod):
    depth, d, n = w_mod.shape
    tn = 1024
    return pl.pallas_call(
        _mod_kernel,
        grid=(depth, n // tn),
        in_specs=[pl.BlockSpec((MOD_ROWS, d), lambda i, j: (0, 0)),
                  pl.BlockSpec((1, d, tn), lambda i, j: (i, 0, j)),
                  pl.BlockSpec((1, 1, tn), lambda i, j: (i, 0, j))],
        out_specs=pl.BlockSpec((1, MOD_ROWS, tn), lambda i, j: (i, 0, j)),
        out_shape=jax.ShapeDtypeStruct((depth, MOD_ROWS, n), F32),
        compiler_params=_cparams(2),
        name="mod",
    )(c_rows, w_mod, b_mod.reshape(depth, 1, n))


def _tok_spec(width):
    return pl.BlockSpec((1, TM, width), lambda b, j: (b, j, 0))


def _mod_spec(batch):
    return pl.BlockSpec((1, N_MOD, D_MODEL), lambda b, j: (jnp.where(j == NT_LAT, batch, b), 0, 0))


def _full_spec(shape):
    zeros = (0,) * len(shape)
    return pl.BlockSpec(shape, lambda b, j: zeros)


_FFN_SOURCES = {"stream": 1, "split": 4, "even": 7, "odd": 3}
TM2 = 2 * TM


def _per_half(fn):
    return jnp.concatenate([fn(half, slice(half * TM, (half + 1) * TM)) for half in range(2)], axis=0)


W_STAGE = 256


def _stage_weight(src, dst, stage, sem, axis):
    n_chunks = src.shape[axis] // W_STAGE

    def window(ref, c):
        chunk = pl.ds(c * W_STAGE, W_STAGE)
        return ref.at[chunk, :] if axis == 0 else ref.at[:, chunk]

    def copy(c):
        return pltpu.make_async_copy(window(src, c), stage.at[c % 2], sem.at[c % 2])

    copy(0).start()
    for c in range(n_chunks):
        if c + 1 < n_chunks:
            copy(c + 1).start()
        copy(c).wait()
        window(dst, c)[...] = stage[c % 2].astype(BF16)


def _ffn_kernel(*refs, rows, source, n_tiles, n_pairs, layer):
    shift_row, scale_row, gate_row = rows
    n_src = _FFN_SOURCES[source]
    src = refs[:n_src]
    (modn0, modn1, modc0, modc1, gain_ref, win_hbm, wout_hbm, o_ref, h_a, x_a, h_b, x_b,
     win_ref, wout_ref, stage_in, stage_out, sem) = refs[n_src:]
    modn = (modn0, modn1)
    modc = (modc0, modc1)
    step = pl.program_id(0)
    first_tile_next = 2 * jnp.minimum(step, n_pairs - 1)

    def is_ctx(half):
        return (first_tile_next + half) % n_tiles == NT_LAT

    def gated(x_ref, o):
        return _per_half(lambda half, r: x_ref[0, r, :] + modn[half][0, 5:6, :] * o[r])

    def next_rows():
        if source == "stream":
            return src[0][0]
        if source == "split":
            lat, ctx = src[0:2], src[2:4]
            return _per_half(lambda half, r: jnp.where(is_ctx(half), ctx[half][0], lat[half][0]))
        if source == "even":
            x_ref, a_ref, yl0, yl1, yc0, yc1, wo_ref = src
            y_lat, y_ctx = (yl0, yl1), (yc0, yc1)

            def hyena_rows(half, r):
                slabs = [y_lat[half][0, s, 0:FFT_N2, :] for s in range(TM // FFT_N2)]
                return jnp.where(is_ctx(half), y_ctx[half][0], jnp.concatenate(slabs, axis=0)).astype(BF16)

            o = _bdot(a_ref[0], wo_ref[0:NA_WIDTH, :]) + _bdot(_per_half(hyena_rows), wo_ref[NA_WIDTH:, :])
            return gated(x_ref, o)
        x_ref, y_ref, wo_ref = src
        return gated(x_ref, _bdot(y_ref[0], wo_ref[...]))

    def prepare(h_write, x_write):
        x_next = next_rows()
        h_write[...] = _per_half(lambda half, r: _modulated(
            x_next[r], gain_ref[...], modn[half], shift_row, scale_row)).astype(BF16)
        x_write[...] = x_next

    def body(h_read, x_read, h_write, x_write):
        prepare(h_write, x_write)
        h = h_read[...]
        acc = jnp.zeros((TM2, D_MODEL), F32)
        for start, size in FF_CHUNKS:
            a = _bdot(h, win_ref[:, start:start + size])
            b = _bdot(h, win_ref[:, D_FF + start:D_FF + start + size])
            g = (_silu(a) * b).astype(BF16)
            acc = acc + _bdot(g, wout_ref[start:start + size, :])
        o_ref[0] = _per_half(lambda half, r: x_read[r, :] + (0.5 * modc[half][0, gate_row:gate_row + 1, :]) * acc[r])

    @pl.when(step == 0)
    def _():
        _stage_weight(win_hbm.at[layer], win_ref, stage_in, sem, axis=1)
        _stage_weight(wout_hbm.at[layer], wout_ref, stage_out, sem, axis=0)
        prepare(h_a, x_a)

    @pl.when(jnp.logical_and(step > 0, step % 2 == 0))
    def _():
        body(h_b, x_b, h_a, x_a)

    @pl.when(step % 2 == 1)
    def _():
        body(h_a, x_a, h_b, x_b)


def _ffn_call(source, srcs, mod_i, gain, w_in, w_out, layer, rows, n_tiles, batch):
    n_all = batch * n_tiles
    assert n_all % 2 == 0 and n_tiles in (NT_LAT, NT_ALL)
    n_pairs = n_all // 2

    def next_pair(s):
        return jnp.minimum(s, n_pairs - 1)

    def cur_pair(s):
        return jnp.maximum(s - 1, 0)

    def tile_of(pair_of, half):
        def fn(s):
            t = 2 * pair_of(s) + half
            return t // n_tiles, t % n_tiles
        return fn

    def mod_spec(pair_of, half):
        def index(s):
            b, j = tile_of(pair_of, half)(s)
            return jnp.where(j == NT_LAT, batch, b), 0, 0
        return pl.BlockSpec((1, N_MOD, D_MODEL), index)

    def tok(arr):
        width = arr.shape[-1]
        if n_tiles == NT_ALL:
            return arr.reshape(1, batch * T_TOK, width), pl.BlockSpec(
                (1, TM2, width), lambda s: (0, next_pair(s), 0))
        per_sample = n_tiles // 2
        return arr, pl.BlockSpec((1, TM2, width),
                                 lambda s: (next_pair(s) // per_sample, next_pair(s) % per_sample, 0))

    def lat_tile(block, n_trailing, half):
        def index(s):
            b, j = tile_of(next_pair, half)(s)
            return (b, jnp.minimum(j, NT_LAT - 1)) + (0,) * n_trailing
        return pl.BlockSpec(block, index)

    def ctx_tile(block, half):
        return pl.BlockSpec(block, lambda s: (tile_of(next_pair, half)(s)[0],) + (0,) * (len(block) - 1))

    def full(shape):
        zeros = (0,) * len(shape)
        return pl.BlockSpec(shape, lambda s: zeros)

    halves = (0, 1)
    if source == "stream":
        (xt,) = srcs
        xt, x_spec = tok(xt)
        args, src_specs = (xt,), [x_spec]
    elif source == "split":
        x, ctx = srcs
        args = (x, x, ctx, ctx)
        src_specs = ([lat_tile((1, TM, D_MODEL), 1, h) for h in halves]
                     + [ctx_tile((1, CTX_LEN, D_MODEL), h) for h in halves])
    elif source == "even":
        xt, a, y_lat, y_ctx, wo = srcs
        (xt, x_spec), (a, a_spec) = tok(xt), tok(a)
        args = (xt, a, y_lat, y_lat, y_ctx, y_ctx, wo)
        src_specs = ([x_spec, a_spec]
                     + [lat_tile((1, TM // FFT_N2, FFT_PITCH, HY_WIDTH), 2, h) for h in halves]
                     + [ctx_tile((1, CTX_LEN, HY_WIDTH), h) for h in halves] + [full((D_MODEL, D_MODEL))])
    else:
        xt, y, wo = srcs
        (xt, x_spec), (y, y_spec) = tok(xt), tok(y)
        args = (xt, y, wo)
        src_specs = [x_spec, y_spec, full((RET_V, D_MODEL))]
    out = pl.pallas_call(
        functools.partial(_ffn_kernel, rows=rows, source=source, n_tiles=n_tiles, n_pairs=n_pairs,
                          layer=layer),
        grid=(n_pairs + 1,),
        in_specs=src_specs + [mod_spec(next_pair, 0), mod_spec(next_pair, 1),
                              mod_spec(cur_pair, 0), mod_spec(cur_pair, 1), full((1, D_MODEL)),
                              pl.BlockSpec(memory_space=pl.ANY), pl.BlockSpec(memory_space=pl.ANY)],
        out_specs=pl.BlockSpec((1, TM2, D_MODEL), lambda s: (0, cur_pair(s), 0)),
        out_shape=jax.ShapeDtypeStruct((1, n_all * TM, D_MODEL), F32),
        scratch_shapes=[pltpu.VMEM((TM2, D_MODEL), BF16), pltpu.VMEM((TM2, D_MODEL), F32),
                        pltpu.VMEM((TM2, D_MODEL), BF16), pltpu.VMEM((TM2, D_MODEL), F32),
                        pltpu.VMEM((D_MODEL, 2 * D_FF), BF16), pltpu.VMEM((D_FF, D_MODEL), BF16),
                        pltpu.VMEM((2, D_MODEL, W_STAGE), F32), pltpu.VMEM((2, W_STAGE, D_MODEL), F32),
                        pltpu.SemaphoreType.DMA((2,))],
        compiler_params=_cparams(1),
        name="ffn_" + source,
    )(*args, mod_i, mod_i, mod_i, mod_i, gain, w_in, w_out)
    return out.reshape(batch, n_tiles * TM, D_MODEL)


def _stage_in_proj_weight(w_hbm, layer, w_ref, stage, sem):
    @pl.when(jnp.logical_and(pl.program_id(0) == 0, pl.program_id(1) == 0))
    def _():
        _stage_weight(w_hbm.at[layer], w_ref, stage, sem, axis=1)


def _in_proj_weight_scratch(n_in):
    return [pltpu.VMEM((D_MODEL, n_in), BF16), pltpu.VMEM((2, D_MODEL, W_STAGE), F32),
            pltpu.SemaphoreType.DMA((2,))]


def _even_in_kernel(x_ref, mod_ref, gain_ref, w_hbm, qg_ref, kg_ref, bd_ref,
                    q_ref, k_ref, v_ref, u_ref, w_ref, stage, sem, *, layer):
    _stage_in_proj_weight(w_hbm, layer, w_ref, stage, sem)
    h = _modulated(x_ref[0], gain_ref[...], mod_ref, 3, 4).astype(BF16)

    def head_norm(t, g):
        ms = _bdot((t * t).astype(BF16), bd_ref[...])
        return (t * lax.rsqrt(ms + RMS_EPS) * g).astype(BF16)

    q_ref[0] = head_norm(_bdot(h, w_ref[:, 0:NA_WIDTH]), qg_ref[...])
    k_ref[0] = head_norm(_bdot(h, w_ref[:, NA_WIDTH:2 * NA_WIDTH]), kg_ref[...])
    v_ref[0] = _bdot(h, w_ref[:, 2 * NA_WIDTH:3 * NA_WIDTH]).astype(BF16)
    u_ref[0] = _bdot(h, w_ref[:, 3 * NA_WIDTH:])


def _even_in_call(xt, mod_i, gain, w, layer, q_gain, k_gain, bd):
    batch = xt.shape[0]
    n_in = w.shape[2]
    return pl.pallas_call(
        functools.partial(_even_in_kernel, layer=layer),
        grid=(batch, NT_ALL),
        in_specs=[_tok_spec(D_MODEL), _mod_spec(batch), _full_spec((1, D_MODEL)),
                  pl.BlockSpec(memory_space=pl.ANY), _full_spec((1, NA_WIDTH)), _full_spec((1, NA_WIDTH)),
                  _full_spec((NA_WIDTH, NA_WIDTH))],
        scratch_shapes=_in_proj_weight_scratch(n_in),
        out_specs=[_tok_spec(NA_WIDTH), _tok_spec(NA_WIDTH), _tok_spec(NA_WIDTH),
                   _tok_spec(3 * HY_WIDTH)],
        out_shape=[jax.ShapeDtypeStruct((batch, T_TOK, NA_WIDTH), BF16)] * 3
        + [jax.ShapeDtypeStruct((batch, T_TOK, 3 * HY_WIDTH), F32)],
        compiler_params=_cparams(2),
        name="even_in",
    )(xt, mod_i, gain, w, q_gain, k_gain, bd)


def _na_kernel(q_ref, k_ref, v_ref, bias_ref, o_ref):
    lane = lax.broadcasted_iota(jnp.int32, (1, LANES), 1)
    first_head = lane < NA_HEAD_DIM
    k_ctx = k_ref[0, SEQ:T_TOK, :]
    v_ctx = v_ref[0, SEQ:T_TOK, :]

    def stack_heads(q):
        zero = jnp.zeros_like(q)
        return jnp.concatenate([jnp.where(first_head, q, zero), jnp.where(first_head, zero, q)], axis=0)

    def attend(qs, scores_and_values):
        s_list = [lax.dot_general(qs, kk, _NT_DIMS, preferred_element_type=F32) if bias is None
                  else lax.dot_general(qs, kk, _NT_DIMS, preferred_element_type=F32) + bias
                  for kk, _, bias in scores_and_values]
        m = functools.reduce(jnp.maximum, [jnp.max(s, axis=-1, keepdims=True) for s in s_list])
        o = functools.reduce(jnp.add, [
            _bdot(jnp.exp(s - m).astype(BF16),
                  jnp.concatenate([vv, jnp.ones((vv.shape[0], LANES), BF16)], axis=1))
            for s, (_, vv, _) in zip(s_list, scores_and_values)])
        o = o[:, :LANES] / o[:, LANES:]
        n = qs.shape[0] // 2
        return jnp.where(first_head, o[:n], o[n:]).astype(BF16)

    def block(i, carry):
        r0 = i * NA_RB
        u0 = jnp.clip(r0 - WIN_R // 2, 0, GRID_H - NA_KR)
        pattern = jnp.where(i == 0, 0, jnp.where(i == GRID_H // NA_RB - 1, 2, 1))
        q0 = pl.multiple_of(r0 * GRID_W, NA_NQ)
        k0 = pl.multiple_of(u0 * GRID_W, GRID_W)
        qs = stack_heads(q_ref[0, pl.ds(q0, NA_NQ), :])
        k_win = k_ref[0, pl.ds(k0, NA_NK), :]
        v_win = v_ref[0, pl.ds(k0, NA_NK), :]
        o_ref[0, pl.ds(q0, NA_NQ), :] = attend(
            qs, [(k_win, v_win, bias_ref[pattern, 0]), (k_ctx, v_ctx, None)])
        return carry

    lax.fori_loop(0, GRID_H // NA_RB, block, 0, unroll=8)
    o_ref[0, SEQ:T_TOK, :] = attend(stack_heads(q_ref[0, SEQ:T_TOK, :]), [(k_ctx, v_ctx, None)])


def _na_call(q, k, v, bias):
    batch = q.shape[0]
    n_pairs = NA_WIDTH // LANES
    spec = pl.BlockSpec((1, T_TOK, LANES), lambda b, p: (b, 0, p))
    return pl.pallas_call(
        _na_kernel,
        grid=(batch, n_pairs),
        in_specs=[spec, spec, spec,
                  pl.BlockSpec((3, 1, 2 * NA_NQ, NA_NK), lambda b, p: (0, p, 0, 0))],
        out_specs=spec,
        out_shape=jax.ShapeDtypeStruct((batch, T_TOK, NA_WIDTH), BF16),
        compiler_params=_cparams(2),
        name="na_attn",
    )(q, k, v, bias)


def _na_bias_table(rpb):
    j = np.arange(NA_RB)[:, None, None, None]
    c = np.arange(GRID_W)[None, :, None, None]
    kk = np.arange(NA_KR)[None, None, :, None]
    kc = np.arange(GRID_W)[None, None, None, :]
    cs = np.clip(c - WIN_C // 2, 0, GRID_W - WIN_C)
    col_ok = (kc >= cs) & (kc < cs + WIN_C)
    dc = np.clip(kc - c + WIN_C - 1, 0, 2 * WIN_C - 2)[0, :, 0, :]
    col_sel = (dc[..., None] == np.arange(2 * WIN_C - 1)).astype(np.float32)
    shape = (NA_RB, GRID_W, NA_KR, GRID_W)
    oks, row_sels = [], []
    for off, rs_rel in ((0, 0 * j), (-(WIN_R // 2), j), (-(NA_KR - NA_RB), NA_KR - WIN_R + 0 * j)):
        row_ok = (kk >= rs_rel) & (kk < rs_rel + WIN_R)
        dr = np.clip(off + kk - j + WIN_R - 1, 0, 2 * WIN_R - 2)[:, 0, :, 0]
        row_sels.append((dr[..., None] == np.arange(2 * WIN_R - 1)).astype(np.float32))
        oks.append(np.broadcast_to(row_ok & col_ok, shape))
    vals = jnp.einsum("pjkr,hrc,qmc->phjqkm", np.stack(row_sels), rpb.astype(F32), col_sel,
                      precision=lax.Precision.HIGHEST)
    table = jnp.where(np.stack(oks)[:, None], vals, NEG_BIG)
    return table.reshape(3, NA_HEADS // 2, 2 * NA_NQ, NA_NK)


def _hy_pre_kernel(u0_ref, u1_ref, uv_ref, w0_ref, w1_ref, wv_ref, b0_ref, b1_ref, bv_ref,
                   zl_ref, xl_ref, zc_ref, xc_ref):
    row = lax.broadcasted_iota(jnp.int32, (TM, LANES), 0)
    zero_row = jnp.zeros((1, LANES), F32)

    def conv(u_ref, w_ref, b_ref, tile):
        s = tile * TM
        cur = u_ref[0, s:s + TM, :]
        seq_start = tile in (0, NT_LAT)
        seq_end = tile in (NT_LAT - 1, NT_LAT)
        prev_row = zero_row if seq_start else u_ref[0, s - 1:s, :]
        next_row = zero_row if seq_end else u_ref[0, s + TM:s + TM + 1, :]
        before = jnp.where(row == 0, prev_row, pltpu.roll(cur, 1, 0))
        after = jnp.where(row == TM - 1, next_row, pltpu.roll(cur, TM - 1, 0))
        return before * w_ref[0:1, :] + cur * w_ref[1:2, :] + after * w_ref[2:3, :] + b_ref[...]

    for tile in range(NT_ALL):
        x0 = conv(u0_ref, w0_ref, b0_ref, tile)
        z = conv(uv_ref, wv_ref, bv_ref, tile) * conv(u1_ref, w1_ref, b1_ref, tile)
        if tile < NT_LAT:
            for half in range(TM // FFT_N2):
                base = (tile * (TM // FFT_N2) + half) * FFT_PITCH
                zl_ref[0, base:base + FFT_N2, :] = z[half * FFT_N2:(half + 1) * FFT_N2]
                xl_ref[0, base:base + FFT_N2, :] = x0[half * FFT_N2:(half + 1) * FFT_N2]
                pad = jnp.zeros((FFT_PITCH - FFT_N2, LANES), F32)
                zl_ref[0, base + FFT_N2:base + FFT_PITCH, :] = pad
                xl_ref[0, base + FFT_N2:base + FFT_PITCH, :] = pad
        else:
            zc_ref[0] = z
            xc_ref[0] = x0


def _hy_pre_call(u, conv_w, conv_b):
    batch = u.shape[0]
    nb = HY_WIDTH // LANES
    u_specs = [pl.BlockSpec((1, T_TOK, LANES), lambda b, cb, g=g: (b, 0, g * nb + cb)) for g in range(3)]
    w_specs = [pl.BlockSpec((3, LANES), lambda b, cb, g=g: (0, g * nb + cb)) for g in range(3)]
    b_specs = [pl.BlockSpec((1, LANES), lambda b, cb, g=g: (0, g * nb + cb)) for g in range(3)]
    lat = pl.BlockSpec((1, FFT_N1H * FFT_PITCH, LANES), lambda b, cb: (b, 0, cb))
    ctx = pl.BlockSpec((1, CTX_LEN, LANES), lambda b, cb: (b, 0, cb))
    return pl.pallas_call(
        _hy_pre_kernel,
        grid=(batch, nb),
        in_specs=u_specs + w_specs + b_specs,
        out_specs=[lat, lat, ctx, ctx],
        out_shape=[jax.ShapeDtypeStruct((batch, FFT_N1H * FFT_PITCH, HY_WIDTH), F32)] * 2
        + [jax.ShapeDtypeStruct((batch, CTX_LEN, HY_WIDTH), F32)] * 2,
        compiler_params=_cparams(2),
        name="hy_pre",
    )(u, u, u, conv_w, conv_w, conv_w, conv_b, conv_b, conv_b)


def _complex_block(re, im):
    return np.block([[re, -im], [im, re]])


@functools.lru_cache(maxsize=None)
def _fft_constants():
    k1 = np.arange(FFT_N1)
    n1 = np.arange(FFT_N1H)
    f1 = np.exp(-2j * np.pi * np.outer(k1, n1) / FFT_N1)
    m1 = _complex_block(f1.real, f1.imag)
    f1_full = np.exp(-2j * np.pi * np.outer(k1, k1) / FFT_N1)
    m1r = np.concatenate([f1_full.real, f1_full.imag], axis=0)
    n2 = np.arange(FFT_N2)
    tw = np.exp(-2j * np.pi * np.outer(k1, n2) / FFT_N)
    tw = np.stack([tw.real, tw.imag])[..., None] * np.ones((1, 1, 1, LANES))
    f2 = np.exp(-2j * np.pi * np.outer(n2, n2) / FFT_N2)
    m2 = _complex_block(f2.real, f2.imag)
    m2i = _complex_block(f2.real, -f2.imag)
    c1 = np.exp(2j * np.pi * np.outer(n1, k1) / FFT_N1) / FFT_N
    m3 = _complex_block(c1.real, c1.imag)
    kc = np.arange(FFT_NC)
    nc = np.arange(CTX_LEN)
    ang = 2 * np.pi * np.outer(kc, nc) / FFT_NC
    mcf = np.concatenate([np.cos(ang), -np.sin(ang)], axis=0)
    ang_full = 2 * np.pi * np.outer(kc, kc) / FFT_NC
    mcf_full = np.concatenate([np.cos(ang_full), -np.sin(ang_full)], axis=0)
    mci = np.concatenate([np.cos(ang.T), -np.sin(ang.T)], axis=1) / FFT_NC

    def hi_lo(m):
        hi = m.astype(ml_dtypes.bfloat16)
        lo = (m - hi.astype(np.float64)).astype(ml_dtypes.bfloat16)
        return hi, lo

    return dict(m1=hi_lo(m1), m1r=hi_lo(m1r), m2=hi_lo(m2), m2i=hi_lo(m2i), m3=hi_lo(m3),
                mcf=hi_lo(mcf), mcf_full=hi_lo(mcf_full), mci=hi_lo(mci), tw=tw.astype(np.float32))


def _slab_rows(n2, n_slabs):
    return pl.ds(n2, n_slabs, stride=FFT_PITCH)


def _fft_s1_kernel(z_ref, *rest):
    *m_refs, a_ref = rest

    def step(i, carry):
        cols = []
        for dn in range(2):
            rows = _slab_rows(2 * i + dn, FFT_N1H)
            cols.append(jnp.concatenate([z_ref[0, 0, rows, :], z_ref[0, 1, rows, :]], axis=0))
        d = jnp.concatenate(cols, axis=1)
        if len(m_refs) == 2:
            r = _dot3(m_refs[0][...], m_refs[1][...], d)
        else:
            r = _bdot(m_refs[0][...], d.astype(BF16))
        for dn in range(2):
            rows = _slab_rows(2 * i + dn, FFT_N1)
            a_ref[0, 0, rows, :] = r[:FFT_N1, dn * LANES:(dn + 1) * LANES]
            a_ref[0, 1, rows, :] = r[FFT_N1:, dn * LANES:(dn + 1) * LANES]
        return carry

    lax.fori_loop(0, FFT_N2 // 2, step, 0, unroll=4)
    for pad_row in range(FFT_N2, FFT_PITCH):
        for part in range(2):
            a_ref[0, part, _slab_rows(pad_row, FFT_N1), :] = jnp.zeros((FFT_N1, LANES), F32)


def _fft_s2_kernel(a_ref, tw_ref, g_ref, fwd_ref, inv_ref, o_ref):
    reps = HY_WIDTH // LANES
    pad = jnp.zeros((FFT_PITCH - FFT_N2, HY_WIDTH), F32)
    for kk in range(FFT_KB):
        twr = jnp.concatenate([tw_ref[0, kk]] * reps, axis=1)
        twi = jnp.concatenate([tw_ref[1, kk]] * reps, axis=1)
        ar = a_ref[0, 0, kk, 0:FFT_N2, :]
        ai = a_ref[0, 1, kk, 0:FFT_N2, :]
        d = jnp.concatenate([ar * twr - ai * twi, ar * twi + ai * twr], axis=0)
        x = _bdot(fwd_ref[...], d.astype(BF16))
        xr, xi = x[:FFT_N2], x[FFT_N2:]
        gr, gi = g_ref[0, kk], g_ref[1, kk]
        y = jnp.concatenate([xr * gr - xi * gi, xr * gi + xi * gr], axis=0)
        b = _bdot(inv_ref[...], y.astype(BF16))
        br, bi = b[:FFT_N2], b[FFT_N2:]
        o_ref[0, 0, kk, 0:FFT_N2, :] = br * twr + bi * twi
        o_ref[0, 1, kk, 0:FFT_N2, :] = bi * twr - br * twi
        o_ref[0, 0, kk, FFT_N2:FFT_PITCH, :] = pad
        o_ref[0, 1, kk, FFT_N2:FFT_PITCH, :] = pad


def _fft_s3_kernel(b_ref, m_ref, z_ref, x0_ref, bias_ref, y_ref):
    def step(i, carry):
        cols = []
        for dn in range(2):
            rows = _slab_rows(2 * i + dn, FFT_N1)
            cols.append(jnp.concatenate([b_ref[0, 0, rows, :], b_ref[0, 1, rows, :]], axis=0))
        y = _bdot(m_ref[...], jnp.concatenate(cols, axis=1).astype(BF16))
        for dn in range(2):
            rows = _slab_rows(2 * i + dn, FFT_N1H)
            for s in range(2):
                conv = y[s * FFT_N1H:(s + 1) * FFT_N1H, dn * LANES:(dn + 1) * LANES]
                y_ref[0, s, rows, :] = (conv + z_ref[0, s, rows, :] * bias_ref[...]) * x0_ref[0, s, rows, :]
        return carry

    lax.fori_loop(0, FFT_N2 // 2, step, 0, unroll=4)
    for pad_row in range(FFT_N2, FFT_PITCH):
        for s in range(2):
            y_ref[0, s, _slab_rows(pad_row, FFT_N1H), :] = jnp.zeros((FFT_N1H, LANES), F32)


def _hy_long_conv_call(z_lat, x0_lat, g_spec, d_bias):
    batch = z_lat.shape[0]
    pairs = batch // 2
    cst = _fft_constants()
    n_cb = HY_WIDTH // LANES
    rows_z = FFT_N1H * FFT_PITCH
    rows_a = FFT_N1 * FFT_PITCH
    zv = z_lat.reshape(pairs, 2, rows_z, HY_WIDTH)
    xv = x0_lat.reshape(pairs, 2, rows_z, HY_WIDTH)
    z_spec = pl.BlockSpec((1, 2, rows_z, LANES), lambda p, cb: (p, 0, 0, cb))
    a_spec = pl.BlockSpec((1, 2, rows_a, LANES), lambda p, cb: (p, 0, 0, cb))
    a = pl.pallas_call(
        _fft_s1_kernel,
        grid=(pairs, n_cb),
        in_specs=[z_spec, _full_spec((2 * FFT_N1, 2 * FFT_N1H))],
        out_specs=a_spec,
        out_shape=jax.ShapeDtypeStruct((pairs, 2, rows_a, HY_WIDTH), F32),
        compiler_params=_cparams(2),
        name="hy_fft_s1",
    )(zv, cst["m1"][0])
    a = a.reshape(pairs, 2, FFT_N1, FFT_PITCH, HY_WIDTH)
    blk = pl.BlockSpec((1, 2, FFT_KB, FFT_PITCH, HY_WIDTH), lambda kb, p: (p, 0, kb, 0, 0))
    sq = (2 * FFT_N2, 2 * FFT_N2)
    b = pl.pallas_call(
        _fft_s2_kernel,
        grid=(FFT_N1 // FFT_KB, pairs),
        in_specs=[blk,
                  pl.BlockSpec((2, FFT_KB, FFT_N2, LANES), lambda kb, p: (0, kb, 0, 0)),
                  pl.BlockSpec((2, FFT_KB, FFT_N2, HY_WIDTH), lambda kb, p: (0, kb, 0, 0)),
                  _full_spec(sq), _full_spec(sq)],
        out_specs=blk,
        out_shape=jax.ShapeDtypeStruct((pairs, 2, FFT_N1, FFT_PITCH, HY_WIDTH), F32),
        compiler_params=_cparams(2),
        name="hy_fft_s2",
    )(a, cst["tw"], g_spec, cst["m2"][0], cst["m2i"][0])
    b = b.reshape(pairs, 2, rows_a, HY_WIDTH)
    y = pl.pallas_call(
        _fft_s3_kernel,
        grid=(pairs, n_cb),
        in_specs=[a_spec, _full_spec((2 * FFT_N1H, 2 * FFT_N1)),
                  z_spec, z_spec, pl.BlockSpec((1, LANES), lambda p, cb: (0, cb))],
        out_specs=z_spec,
        out_shape=jax.ShapeDtypeStruct((pairs, 2, rows_z, HY_WIDTH), F32),
        compiler_params=_cparams(2),
        name="hy_fft_s3",
    )(b, cst["m3"][0], zv, xv, d_bias.reshape(1, HY_WIDTH))
    return y.reshape(batch, rows_z, HY_WIDTH)


def _hy_ctx_kernel(z_ref, x0_ref, g_ref, fwd_ref, inv_ref, bias_ref, y_ref):
    z = z_ref[0]
    x = _bdot(fwd_ref[...], z.astype(BF16))
    xr, xi = x[:FFT_NC], x[FFT_NC:]
    gr, gi = g_ref[0], g_ref[1]
    y = jnp.concatenate([xr * gr - xi * gi, xr * gi + xi * gr], axis=0)
    conv = _bdot(inv_ref[...], y.astype(BF16))
    y_ref[0] = ((conv + z * bias_ref[...]) * x0_ref[0]).astype(BF16)


def _hy_ctx_conv_call(z_ctx, x0_ctx, g_spec, d_bias):
    batch = z_ctx.shape[0]
    cst = _fft_constants()
    tok = pl.BlockSpec((1, CTX_LEN, HY_WIDTH), lambda b: (b, 0, 0))

    def full(shape):
        zeros = (0,) * len(shape)
        return pl.BlockSpec(shape, lambda b: zeros)

    return pl.pallas_call(
        _hy_ctx_kernel,
        grid=(batch,),
        in_specs=[tok, tok, full((2, FFT_NC, HY_WIDTH)),
                  full((2 * FFT_NC, CTX_LEN)), full((CTX_LEN, 2 * FFT_NC)), full((1, HY_WIDTH))],
        out_specs=tok,
        out_shape=jax.ShapeDtypeStruct((batch, CTX_LEN, HY_WIDTH), BF16),
        compiler_params=_cparams(1),
        name="hy_ctx",
    )(z_ctx, x0_ctx, g_spec, cst["mcf"][0], cst["mci"][0], d_bias.reshape(1, HY_WIDTH))


def _hyena_filter_halves(length, fw1, fb1, fw2, fb2, fw3, fb3, fw4, freq):
    t = jnp.linspace(0.0, 1.0, length, dtype=F32)[:, None]
    w = 2.0 * math.pi * jnp.arange(length, dtype=F32)[:, None] / length
    bands = jnp.linspace(1e-4, HY_BANDS - 1, HY_BANDS, dtype=F32)
    emb = jnp.concatenate([t, jnp.cos(bands * w), -jnp.sin(bands * w)], axis=-1)

    def second_half_order(a):
        return jnp.concatenate([a[:1], a[:0:-1]], axis=0)

    hp = lax.Precision.HIGHEST
    h = jnp.concatenate([emb, second_half_order(emb)], axis=0)
    h = jnp.sin(freq * (jnp.dot(h, fw1, precision=hp) + fb1))
    h = jnp.sin(freq * (jnp.dot(h, fw2, precision=hp) + fb2))
    h = jnp.sin(freq * (jnp.dot(h, fw3, precision=hp) + fb3))
    max_decay = math.log(HY_TARGET) / HY_FAST_PCT
    min_decay = math.log(HY_TARGET) / HY_SLOW_PCT
    deltas = jnp.abs(jnp.linspace(min_decay, max_decay, HY_WIDTH, dtype=F32))
    causal = jnp.dot(h[:length], fw4[:, :HY_WIDTH], precision=hp).astype(F32) * jnp.exp(-t * deltas)
    anti = (jnp.dot(h[length:], fw4[:, HY_WIDTH:], precision=hp).astype(F32)
            * jnp.exp(-second_half_order(t) * deltas))
    first_row = jnp.arange(length)[:, None] == 0
    return causal + jnp.where(first_row, anti[:1], 0.0), jnp.where(first_row, 0.0, anti)


def _odd_in_kernel(x_ref, mod_ref, gain_ref, w_hbm, cos_ref, sin_ref, q_ref, k_ref, v_ref, sg_ref,
                   w_ref, stage, sem, *, layer):
    _stage_in_proj_weight(w_hbm, layer, w_ref, stage, sem)
    h = _modulated(x_ref[0], gain_ref[...], mod_ref, 3, 4).astype(BF16)
    cos = cos_ref[...]
    sin = sin_ref[...]

    def rope_store(dst_ref, col0, scale):
        p = _bdot(h, w_ref[:, col0:col0 + RET_QK])
        for c in range(RET_QK // LANES):
            t = p[:, c * LANES:(c + 1) * LANES]
            half = (c % 2) * LANES
            r = t * cos[:, half:half + LANES] + pltpu.roll(t, LANES // 2, 1) * sin[:, half:half + LANES]
            dst_ref[0, :, c * LANES:(c + 1) * LANES] = (r * scale).astype(BF16)

    rope_store(q_ref, 0, 1.0)
    rope_store(k_ref, RET_QK, RET_KEY_DIM ** -0.5)
    v_ref[0] = _bdot(h, w_ref[:, 2 * RET_QK:2 * RET_QK + RET_V]).astype(BF16)
    sg_ref[0] = _silu(_bdot(h, w_ref[:, 2 * RET_QK + RET_V:])).astype(BF16)


def _odd_in_call(xt, mod_i, gain, w, layer, cos_t, sin_t):
    batch = xt.shape[0]
    rope_spec = pl.BlockSpec((TM, RET_KEY_DIM), lambda b, j: (j, 0))
    return pl.pallas_call(
        functools.partial(_odd_in_kernel, layer=layer),
        grid=(batch, NT_ALL),
        in_specs=[_tok_spec(D_MODEL), _mod_spec(batch), _full_spec((1, D_MODEL)),
                  pl.BlockSpec(memory_space=pl.ANY), rope_spec, rope_spec],
        scratch_shapes=_in_proj_weight_scratch(w.shape[2]),
        out_specs=[_tok_spec(RET_QK), _tok_spec(RET_QK), _tok_spec(RET_V), _tok_spec(RET_V)],
        out_shape=[jax.ShapeDtypeStruct((batch, T_TOK, RET_QK), BF16)] * 2
        + [jax.ShapeDtypeStruct((batch, T_TOK, RET_V), BF16)] * 2,
        compiler_params=_cparams(2),
        name="odd_in",
    )(xt, mod_i, gain, w, cos_t, sin_t)


def _rope_tables():
    t = np.arange(SEQ)
    n_freq = RET_KEY_DIM // 4
    inv = ROPE_BASE ** (-jnp.arange(n_freq, dtype=F32) / n_freq)
    ang_r = jnp.asarray(t // GRID_W, F32)[:, None] * inv
    ang_c = jnp.asarray(t % GRID_W, F32)[:, None] * inv
    cr, sr, cc, sc = jnp.cos(ang_r), jnp.sin(ang_r), jnp.cos(ang_c), jnp.sin(ang_c)
    cos_l = jnp.concatenate([cr, cr, cc, cc], axis=-1)
    sin_l = jnp.concatenate([-sr, sr, -sc, sc], axis=-1)
    cos_t = jnp.concatenate([cos_l, jnp.ones((CTX_LEN, RET_KEY_DIM), F32)], axis=0)
    sin_t = jnp.concatenate([sin_l, jnp.zeros((CTX_LEN, RET_KEY_DIM), F32)], axis=0)
    return cos_t, sin_t


def _ret_kernel(lg_ref, q_ref, k_ref, v_ref, sg_ref, y_ref, o_acc, state_f, state_b):
    head = pl.program_id(1)
    n_c = RET_BLOCK
    n_lat = SEQ // n_c
    assert CTX_LEN == n_c and n_lat % 2 == 0
    ii = lax.broadcasted_iota(jnp.int32, (n_c, n_c), 0).astype(F32)
    jj = lax.broadcasted_iota(jnp.int32, (n_c, n_c), 1).astype(F32)
    row_k = lax.broadcasted_iota(jnp.int32, (n_c, RET_KEY_DIM), 0).astype(F32)
    row_v = lax.broadcasted_iota(jnp.int32, (n_c, RET_VAL_DIM), 0).astype(F32)

    def decays(backward):
        lg = lg_ref[1 if backward else 0, head]
        if backward:
            diff = jj - ii
            xi = jnp.exp(lg * (n_c - row_v))
            zeta = jnp.exp(lg * row_k)
        else:
            diff = ii - jj
            xi = jnp.exp(lg * (row_v + 1.0))
            zeta = jnp.exp(lg * (n_c - 1.0 - row_k))
        dmask = jnp.where(diff >= 0, jnp.exp(lg * jnp.maximum(diff, 0.0)), 0.0)
        g_chunk = jnp.exp(lg * n_c + jnp.zeros((1, RET_VAL_DIM), F32))
        return dmask, xi, zeta, g_chunk

    def advance(chunk, state, consts, finalize):
        dmask, xi, zeta, g_chunk = consts
        r = chunk * n_c if isinstance(chunk, int) else pl.multiple_of(chunk * n_c, n_c)
        qc = q_ref[0, pl.ds(r, n_c), :]
        kc = k_ref[0, pl.ds(r, n_c), :]
        vc = v_ref[0, pl.ds(r, n_c), :]
        st = state[...]
        inner = lax.dot_general(qc, kc, _NT_DIMS, preferred_element_type=F32) * dmask
        o = _bdot(inner.astype(BF16), vc) + _bdot(qc, st.astype(BF16)) * xi
        kz = (kc.astype(F32) * zeta).astype(BF16)
        state[...] = st * g_chunk + lax.dot_general(kz, vc, _TN_DIMS, preferred_element_type=F32)
        if finalize:
            tot = o_acc[pl.ds(r, n_c), :] + o
            mu = jnp.mean(tot, axis=-1, keepdims=True)
            cen = tot - mu
            var = jnp.mean(cen * cen, axis=-1, keepdims=True)
            yn = cen * lax.rsqrt(var + GN_EPS)
            y_ref[0, pl.ds(r, n_c), :] = (sg_ref[0, pl.ds(r, n_c), :].astype(F32) * yn).astype(BF16)
        else:
            o_acc[pl.ds(r, n_c), :] = o

    consts_f = decays(False)
    consts_b = decays(True)
    state_f[...] = jnp.zeros_like(state_f)
    state_b[...] = jnp.zeros_like(state_b)

    def both(chunk_f, chunk_b, finalize):
        advance(chunk_f, state_f, consts_f, finalize)
        advance(chunk_b, state_b, consts_b, finalize)

    advance(n_lat, state_f, consts_f, False)
    advance(n_lat, state_b, consts_b, True)

    def first_half(s, carry):
        both(s, n_lat - 1 - s, False)
        return carry

    def second_half(s, carry):
        both(s, n_lat - 1 - s, True)
        return carry

    lax.fori_loop(0, n_lat // 2, first_half, 0, unroll=4)
    lax.fori_loop(n_lat // 2, n_lat, second_half, 0, unroll=4)


def _ret_call(lg, q, k, v, sg):
    batch = q.shape[0]
    qk_spec = pl.BlockSpec((1, T_TOK, RET_KEY_DIM), lambda b, h: (b, 0, h))
    v_spec = pl.BlockSpec((1, T_TOK, RET_VAL_DIM), lambda b, h: (b, 0, h))
    return pl.pallas_call(
        _ret_kernel,
        grid=(batch, RET_HEADS),
        in_specs=[pl.BlockSpec(memory_space=pltpu.SMEM), qk_spec, qk_spec, v_spec, v_spec],
        out_specs=v_spec,
        out_shape=jax.ShapeDtypeStruct((batch, T_TOK, RET_V), BF16),
        scratch_shapes=[pltpu.VMEM((T_TOK, RET_VAL_DIM), F32),
                        pltpu.VMEM((RET_KEY_DIM, RET_VAL_DIM), F32),
                        pltpu.VMEM((RET_KEY_DIM, RET_VAL_DIM), F32)],
        compiler_params=_cparams(2),
        name="retention",
    )(lg, q, k, v, sg)


def _fft_s2_fwd_kernel(a_ref, tw_ref, fhi_ref, flo_ref, g_ref):
    reps = HY_WIDTH // LANES
    for kk in range(FFT_KB):
        twr = jnp.concatenate([tw_ref[0, kk]] * reps, axis=1)
        twi = jnp.concatenate([tw_ref[1, kk]] * reps, axis=1)
        ar = a_ref[0, kk, 0:FFT_N2, :]
        ai = a_ref[1, kk, 0:FFT_N2, :]
        d = jnp.concatenate([ar * twr - ai * twi, ar * twi + ai * twr], axis=0)
        x = _dot3(fhi_ref[...], flo_ref[...], d)
        g_ref[0, kk] = x[:FFT_N2]
        g_ref[1, kk] = x[FFT_N2:]


def _dft_dense_kernel(x_ref, mhi_ref, mlo_ref, o_ref):
    o_ref[...] = _dot3(mhi_ref[...], mlo_ref[...], x_ref[...])


def _filter_spectra(filter_params):
    cst = _fft_constants()
    rows_z = FFT_N1H * FFT_PITCH
    rows_a = FFT_N1 * FFT_PITCH

    def slabs(half):
        half = half.reshape(FFT_N1H, FFT_N2, HY_WIDTH)
        return jnp.pad(half, ((0, 0), (0, FFT_PITCH - FFT_N2), (0, 0))).reshape(rows_z, HY_WIDTH)

    filt = jnp.stack([slabs(half) for half in _hyena_filter_halves(SEQ, *filter_params)])[None]
    a = pl.pallas_call(
        _fft_s1_kernel,
        grid=(1, HY_WIDTH // LANES),
        in_specs=[pl.BlockSpec((1, 2, rows_z, LANES), lambda p, cb: (p, 0, 0, cb)),
                  _full_spec((2 * FFT_N1, FFT_N1)), _full_spec((2 * FFT_N1, FFT_N1))],
        out_specs=pl.BlockSpec((1, 2, rows_a, LANES), lambda p, cb: (p, 0, 0, cb)),
        out_shape=jax.ShapeDtypeStruct((1, 2, rows_a, HY_WIDTH), F32),
        compiler_params=_cparams(2),
        name="filt_fft_s1",
    )(filt, *cst["m1r"])
    a = a.reshape(2, FFT_N1, FFT_PITCH, HY_WIDTH)
    sq = (2 * FFT_N2, 2 * FFT_N2)
    g_lat = pl.pallas_call(
        _fft_s2_fwd_kernel,
        grid=(FFT_N1 // FFT_KB,),
        in_specs=[pl.BlockSpec((2, FFT_KB, FFT_PITCH, HY_WIDTH), lambda kb: (0, kb, 0, 0)),
                  pl.BlockSpec((2, FFT_KB, FFT_N2, LANES), lambda kb: (0, kb, 0, 0)),
                  pl.BlockSpec(sq, lambda kb: (0, 0)), pl.BlockSpec(sq, lambda kb: (0, 0))],
        out_specs=pl.BlockSpec((2, FFT_KB, FFT_N2, HY_WIDTH), lambda kb: (0, kb, 0, 0)),
        out_shape=jax.ShapeDtypeStruct((2, FFT_N1, FFT_N2, HY_WIDTH), F32),
        compiler_params=_cparams(1),
        name="filt_fft_s2",
    )(a, cst["tw"], *cst["m2"])
    filt_ctx = jnp.concatenate(_hyena_filter_halves(CTX_LEN, *filter_params), axis=0)
    g_ctx = pl.pallas_call(
        _dft_dense_kernel,
        out_shape=jax.ShapeDtypeStruct((2 * FFT_NC, HY_WIDTH), F32),
        compiler_params=pltpu.CompilerParams(vmem_limit_bytes=VMEM_LIMIT_BYTES),
        name="filt_dft_ctx",
    )(filt_ctx, *cst["mcf_full"])
    return g_lat, g_ctx.reshape(2, FFT_NC, HY_WIDTH)


def _even_mixer(xt, mod_i, gain, w_in, layer, w_out, q_gain, k_gain, rpb, conv_w, conv_b, filter_params,
                d_bias):
    scale = NA_HEAD_DIM ** -0.5
    qg = (jnp.tile(q_gain, NA_HEADS) * scale).reshape(1, NA_WIDTH)
    kg = jnp.tile(k_gain, NA_HEADS).reshape(1, NA_WIDTH)
    head_of = np.arange(NA_WIDTH) // NA_HEAD_DIM
    bd = jnp.asarray((head_of[:, None] == head_of[None, :]) / NA_HEAD_DIM, BF16)
    q, k, v, u = _even_in_call(xt, mod_i, gain, w_in, layer, qg, kg, bd)
    a = _na_call(q, k, v, _na_bias_table(rpb))
    z_lat, x0_lat, z_ctx, x0_ctx = _hy_pre_call(u, conv_w, conv_b.reshape(1, -1))
    g_lat, g_ctx = _filter_spectra(filter_params)
    y_lat = _hy_long_conv_call(z_lat, x0_lat, g_lat, d_bias)
    y_ctx = _hy_ctx_conv_call(z_ctx, x0_ctx, g_ctx, d_bias)
    return a, y_lat.reshape(xt.shape[0], FFT_N1H, FFT_PITCH, HY_WIDTH), y_ctx, w_out.astype(BF16)


def _odd_mixer(xt, mod_i, gain, w_in, layer, w_out, logit_f, logit_b, rope):
    q, k, v, sg = _odd_in_call(xt, mod_i, gain, w_in, layer, *rope)
    lg = jnp.stack([jax.nn.log_sigmoid(logit_f.astype(F32)), jax.nn.log_sigmoid(logit_b.astype(F32))])
    return _ret_call(lg, q, k, v, sg), w_out.astype(BF16)


def kernel(x, c, ctx, c_ctx, w_mod, b_mod, norm_gain, ffn_a_in, ffn_a_out, ffn_b_in, ffn_b_out,
           even_in, even_out, na_q_gain, na_k_gain, na_rpb, hy_conv_w, hy_conv_b,
           hy_fw1, hy_fb1, hy_fw2, hy_fb2, hy_fw3, hy_fb3, hy_fw4, hy_freq, hy_bias,
           ret_in, ret_out, ret_logit_f, ret_logit_b):
    batch = x.shape[0]
    assert x.shape == (batch, SEQ, D_MODEL) and ctx.shape == (batch, CTX_LEN, D_MODEL)
    assert batch % 2 == 0 and batch < MOD_ROWS
    c_rows = jnp.concatenate([c, c_ctx[None], jnp.zeros((MOD_ROWS - batch - 1, D_MODEL), F32)], axis=0)
    mod_all = _mod_call(c_rows, w_mod, b_mod).reshape(DEPTH, MOD_ROWS, N_MOD, D_MODEL)
    rope = _rope_tables()
    for i in range(DEPTH):
        last = i == DEPTH - 1
        mod_i = mod_all[i]
        gains = norm_gain[i].reshape(3, 1, D_MODEL)
        if i == 0:
            xt = _ffn_call("split", (x, ctx), mod_i, gains[0], ffn_a_in, ffn_a_out, i, (0, 1, 2), NT_ALL, batch)
        else:
            xt = _ffn_call("stream", (xt,), mod_i, gains[0], ffn_a_in, ffn_a_out, i, (0, 1, 2), NT_ALL, batch)
        if i % 2 == 0:
            e = i // 2
            filter_params = (hy_fw1[e], hy_fb1[e], hy_fw2[e], hy_fb2[e], hy_fw3[e], hy_fb3[e],
                             hy_fw4[e], hy_freq[e])
            source = "even"
            mixed = _even_mixer(xt, mod_i, gains[1], even_in, e, even_out[e], na_q_gain[e], na_k_gain[e],
                                na_rpb[e], hy_conv_w[e], hy_conv_b[e], filter_params, hy_bias[e])
        else:
            o = i // 2
            source = "odd"
            mixed = _odd_mixer(xt, mod_i, gains[1], ret_in, o, ret_out[o], ret_logit_f[o], ret_logit_b[o], rope)
        xt = _ffn_call(source, (xt,) + mixed, mod_i, gains[2], ffn_b_in, ffn_b_out, i, (6, 7, 8),
                       NT_LAT if last else NT_ALL, batch)
    return xt
```
